```python
import numpy as np
import jax
import jax.numpy as jnp
from jax import lax

D_MODEL = 2048
BATCH = 4
SEQ = 4096
DEPTH = 4

HEAD_DIM = 128
Q_BLOCK = 128

MLA_HEADS = 8
MLA_Q_RANK = 512
MLA_KV_RANK = 256
MLA_NOPE = 128
MLA_ROPE = 64
MLA_V = 128
ROPE_THETA = 10000.0

NSA_HEADS = 8
NSA_GROUPS = 2
NSA_CMP_LEN = 32
NSA_CMP_STRIDE = 16
NSA_SEL_LEN = 64
NSA_TOPK = 16
NSA_WINDOW = 512
NSA_Q_CHUNK = 32

SB_HEADS = 8

MEM_LEN = 256
MEM_HEADS = 4

N_EXPERTS = 32
TOP_K = 4
D_EXPERT = 512
SWIGLU_LIMIT = 7.0
SWIGLU_ALPHA = 1.702
MOE_BLOCK = 128

N_BRANCH = 3
BRANCH_WIDTH = 1024
DN_ALPHA = (2 * DEPTH) ** 0.25
DN_BETA = (8 * DEPTH) ** -0.25
LN_EPS = 1e-5
RMS_EPS = 1e-6
NEG = -1e30

IN_SIZES = (MLA_Q_RANK, MLA_KV_RANK, MLA_ROPE,
            NSA_HEADS * HEAD_DIM, 3 * 2 * NSA_GROUPS * HEAD_DIM, 3 * NSA_HEADS,
            3 * SB_HEADS * HEAD_DIM, N_BRANCH * D_MODEL)
N_IN = sum(IN_SIZES)
IN_SPLITS = tuple(int(v) for v in np.cumsum(IN_SIZES)[:-1])

kernel_name = "hybrid_mla_nsa_stickbreak_moe_deepnorm"


def layer_norm(x, g, b):
    xf = x.astype(jnp.float32)
    mu = xf.mean(-1, keepdims=True)
    var = jnp.square(xf - mu).mean(-1, keepdims=True)
    y = (xf - mu) * lax.rsqrt(var + LN_EPS) * g.astype(jnp.float32) + b.astype(jnp.float32)
    return y.astype(x.dtype)


def rms_norm(x, g):
    xf = x.astype(jnp.float32)
    y = xf * lax.rsqrt(jnp.mean(xf * xf, -1, keepdims=True) + RMS_EPS) * g.astype(jnp.float32)
    return y.astype(x.dtype)


def masked_softmax(s, mask):
    p = jax.nn.softmax(jnp.where(mask, s, NEG), axis=-1)
    return jnp.where(mask, p, 0.0)


def alibi_slopes(n_heads):
    return np.array([2.0 ** (-8.0 * (h + 1) / n_heads) for h in range(n_heads)], np.float32)


def rope_tables(seq):
    inv = np.asarray(ROPE_THETA ** (-np.arange(0, MLA_ROPE, 2) / MLA_ROPE), np.float32)
    ang = jnp.arange(seq, dtype=jnp.float32)[:, None] * jnp.asarray(inv)[None, :]
    return jnp.cos(ang), jnp.sin(ang)


def apply_rope(x, cos, sin):
    half = x.shape[-1] // 2
    x1, x2 = x[..., :half], x[..., half:]
    c = cos[None, :, None, :].astype(x.dtype)
    s = sin[None, :, None, :].astype(x.dtype)
    return jnp.concatenate([x1 * c - x2 * s, x1 * s + x2 * c], axis=-1)


def causal_softmax_attention(q, k, v, scale):
    B, S, H, _ = q.shape
    nqb = S // Q_BLOCK
    qb = q.reshape(B, nqb, Q_BLOCK, H, q.shape[-1]).swapaxes(0, 1)
    kpos = jnp.arange(S)

    def block(args):
        i, qi = args
        s = jnp.einsum('bqhd,bkhd->bhqk', qi, k, preferred_element_type=jnp.float32) * scale
        tpos = i * Q_BLOCK + jnp.arange(Q_BLOCK)
        p = masked_softmax(s, kpos[None, :] <= tpos[:, None])
        return jnp.einsum('bhqk,bkhd->bqhd', p.astype(v.dtype), v)

    o = lax.map(block, (jnp.arange(nqb), qb))
    return o.swapaxes(0, 1).reshape(B, S, H, v.shape[-1])


def stick_breaking_attention(q, k, v):
    B, S, H, dk = q.shape
    scale = dk ** -0.5
    nqb = S // Q_BLOCK
    qb = q.reshape(B, nqb, Q_BLOCK, H, dk).swapaxes(0, 1)
    kpos = jnp.arange(S)

    def block(args):
        i, qi = args
        z = jnp.einsum('bqhd,bkhd->bhqk', qi, k, preferred_element_type=jnp.float32) * scale
        tpos = i * Q_BLOCK + jnp.arange(Q_BLOCK)
        strict = kpos[None, :] < tpos[:, None]
        log_1m = jnp.where(strict, jax.nn.log_sigmoid(-z), 0.0)
        between = lax.cumsum(log_1m, axis=3, reverse=True) - log_1m
        a = jnp.where(strict, jnp.exp(jax.nn.log_sigmoid(z) + between), 0.0)
        return jnp.einsum('bhqk,bkhd->bqhd', a.astype(v.dtype), v)

    o = lax.map(block, (jnp.arange(nqb), qb))
    return o.swapaxes(0, 1).reshape(B, S, H, dk)


def mla(c_q, c_kv, k_rope, q_norm_g, w_q_up, kv_norm_g, w_kv_up):
    B, S, _ = c_q.shape
    cos, sin = rope_tables(S)
    q = (rms_norm(c_q, q_norm_g) @ w_q_up).reshape(B, S, MLA_HEADS, MLA_NOPE + MLA_ROPE)
    q_rope = apply_rope(q[..., MLA_NOPE:], cos, sin)
    kr = apply_rope(k_rope[:, :, None, :], cos, sin)
    kv = (rms_norm(c_kv, kv_norm_g) @ w_kv_up).reshape(B, S, MLA_HEADS, MLA_NOPE + MLA_V)
    q_full = jnp.concatenate([q[..., :MLA_NOPE], q_rope], axis=-1)
    k_full = jnp.concatenate([kv[..., :MLA_NOPE],
                              jnp.broadcast_to(kr, (B, S, MLA_HEADS, MLA_ROPE))], axis=-1)
    o = causal_softmax_attention(q_full, k_full, kv[..., MLA_NOPE:], (MLA_NOPE + MLA_ROPE) ** -0.5)
    return o.reshape(B, S, MLA_HEADS * MLA_V)


def nsa(q, kv, gate_logits, pe_k, pe_v, w1_k, w1_v, w2_k, w2_v):
    B, S, _ = q.shape
    G, HG, dk = NSA_GROUPS, NSA_HEADS // NSA_GROUPS, HEAD_DIM
    QC, W, L, SL = NSA_Q_CHUNK, NSA_WINDOW, NSA_CMP_LEN, NSA_SEL_LEN
    scale = dk ** -0.5
    dt = q.dtype
    qg = q.reshape(B, S, G, HG, dk)
    kv = kv.reshape(B, S, 3, 2, G, dk)
    gates = jax.nn.sigmoid(gate_logits.reshape(B, S, G, HG, 3))
    slopes = jnp.asarray(alibi_slopes(NSA_HEADS)).reshape(G, HG)

    n_cmp = (S - L) // NSA_CMP_STRIDE + 1
    cmp_idx = np.arange(n_cmp)[:, None] * NSA_CMP_STRIDE + np.arange(L)[None, :]

    def compress(raw, pe, w1, w2):
        blk = raw[:, cmp_idx] + pe[None, None, :, None, :]
        blk = blk.transpose(0, 1, 3, 2, 4).reshape(B, n_cmp, G, L * dk)
        return jax.nn.gelu(blk @ w1) @ w2

    k_cmp = compress(kv[:, :, 0, 0], pe_k, w1_k, w2_k)
    v_cmp = compress(kv[:, :, 0, 1], pe_v, w1_v, w2_v)
    cmp_end = jnp.asarray(cmp_idx[:, -1])
    cmp_mid = jnp.asarray(cmp_idx.mean(1), jnp.float32)

    n_sel = S // SL
    n_top = min(NSA_TOPK, n_sel)
    cs = np.arange(n_cmp) * NSA_CMP_STRIDE
    ss = np.arange(n_sel) * SL
    ov = np.clip(np.minimum(cs[:, None] + L, ss[None, :] + SL) - np.maximum(cs[:, None], ss[None, :]), 0, None) / L
    overlap = jnp.asarray(ov, jnp.float32)
    k_sel = kv[:, :, 1, 0].reshape(B, n_sel, SL, G, dk).transpose(0, 3, 1, 2, 4)
    v_sel = kv[:, :, 1, 1].reshape(B, n_sel, SL, G, dk).transpose(0, 3, 1, 2, 4)
    b_ix = jnp.arange(B)[:, None, None, None]
    g_ix = jnp.arange(G)[None, :, None, None]
    blk_ids = jnp.arange(n_sel)
    tok_in_blk = jnp.arange(SL)

    pad = ((0, 0), (W, 0), (0, 0), (0, 0))
    k_win = jnp.pad(kv[:, :, 2, 0], pad)
    v_win = jnp.pad(kv[:, :, 2, 1], pad)

    def chunk(c):
        t0 = c * QC
        tpos = t0 + jnp.arange(QC)
        qc = lax.dynamic_slice_in_dim(qg, t0, QC, axis=1)

        s_c = jnp.einsum('bqghd,bngd->bghqn', qc, k_cmp, preferred_element_type=jnp.float32) * scale
        s_c = s_c - slopes[None, :, :, None, None] * (tpos[:, None].astype(jnp.float32) - cmp_mid[None, :])
        p_c = masked_softmax(s_c, cmp_end[None, :] <= tpos[:, None])
        o_c = jnp.einsum('bghqn,bngd->bqghd', p_c.astype(dt), v_cmp)

        imp = jnp.einsum('bghqn,nj->bgqj', p_c, overlap)
        cur = tpos // SL
        forced = (blk_ids[None, :] == 0) | (blk_ids[None, :] == cur[:, None]) | (blk_ids[None, :] == cur[:, None] - 1)
        imp = jnp.where(blk_ids[None, :] > cur[:, None], -jnp.inf, jnp.where(forced, jnp.inf, imp))
        sel = lax.top_k(imp, n_top)[1]
        kg = k_sel[b_ix, g_ix, sel]
        vg = v_sel[b_ix, g_ix, sel]
        spos = sel[..., None] * SL + tok_in_blk
        s_s = jnp.einsum('bqghd,bgqkld->bghqkl', qc, kg, preferred_element_type=jnp.float32) * scale
        dist_s = (tpos[None, None, :, None, None] - spos).astype(jnp.float32)[:, :, None]
        s_s = s_s - slopes[None, :, :, None, None, None] * dist_s
        mask_s = (spos <= tpos[None, None, :, None, None])[:, :, None]
        p_s = masked_softmax(s_s.reshape(B, G, HG, QC, n_top * SL), mask_s.reshape(B, G, 1, QC, n_top * SL))
        o_s = jnp.einsum('bghqkl,bgqkld->bqghd', p_s.reshape(B, G, HG, QC, n_top, SL).astype(dt), vg)

        kw = lax.dynamic_slice_in_dim(k_win, t0, QC + W, axis=1)
        vw = lax.dynamic_slice_in_dim(v_win, t0, QC + W, axis=1)
        wpos = t0 - W + jnp.arange(QC + W)
        dw = tpos[:, None] - wpos[None, :]
        s_w = jnp.einsum('bqghd,bkgd->bghqk', qc, kw, preferred_element_type=jnp.float32) * scale
        s_w = s_w - slopes[None, :, :, None, None] * dw.astype(jnp.float32)
        p_w = masked_softmax(s_w, (dw >= 0) & (dw < W) & (wpos[None, :] >= 0))
        o_w = jnp.einsum('bghqk,bkgd->bqghd', p_w.astype(dt), vw)

        gc = lax.dynamic_slice_in_dim(gates, t0, QC, axis=1)
        return gc[..., 0:1] * o_c + gc[..., 1:2] * o_s + gc[..., 2:3] * o_w

    o = lax.map(chunk, jnp.arange(S // QC))
    return o.swapaxes(0, 1).reshape(B, S, NSA_HEADS * dk)


def hybrid_mixer(x, w_in, mla_q_norm, mla_w_q_up, mla_kv_norm, mla_w_kv_up,
                 nsa_pe_k, nsa_pe_v, nsa_w1_k, nsa_w1_v, nsa_w2_k, nsa_w2_v,
                 w_branch, b_merge, w_out):
    B, S, _ = x.shape
    h = x @ w_in
    c_q, c_kv, k_rope, nsa_q, nsa_kv, nsa_gate, sb_qkv, merge = jnp.split(h, IN_SPLITS, axis=-1)
    o_a = mla(c_q, c_kv, k_rope, mla_q_norm, mla_w_q_up, mla_kv_norm, mla_w_kv_up)
    o_b = nsa(nsa_q, nsa_kv, nsa_gate, nsa_pe_k, nsa_pe_v, nsa_w1_k, nsa_w1_v, nsa_w2_k, nsa_w2_v)
    sb = sb_qkv.reshape(B, S, 3, SB_HEADS, HEAD_DIM)
    o_c = stick_breaking_attention(sb[:, :, 0], sb[:, :, 1], sb[:, :, 2]).reshape(B, S, BRANCH_WIDTH)
    g = jax.nn.sigmoid(merge.reshape(B, S, N_BRANCH, D_MODEL) + b_merge)
    y = (g[:, :, 0] * (o_a @ w_branch[0]) + g[:, :, 1] * (o_b @ w_branch[1])
         + g[:, :, 2] * (o_c @ w_branch[2]))
    return y @ w_out


def memory_attention(x, mem, w_q, w_k, w_v, w_o):
    B, S, _ = x.shape
    M = mem.shape[1]
    q = (x @ w_q).reshape(B, S, MEM_HEADS, HEAD_DIM)
    k = (mem @ w_k).reshape(B, M, MEM_HEADS, HEAD_DIM)
    v = (mem @ w_v).reshape(B, M, MEM_HEADS, HEAD_DIM)
    s = jnp.einsum('bshd,bmhd->bhsm', q, k, preferred_element_type=jnp.float32) * HEAD_DIM ** -0.5
    p = jax.nn.softmax(s, axis=-1)
    o = jnp.einsum('bhsm,bmhd->bshd', p.astype(v.dtype), v).reshape(B, S, MEM_HEADS * HEAD_DIM)
    return o @ w_o


def moe(x, w_router, b_router, w_gate, b_gate, w_up, b_up, w_down, b_down):
    B, S, D = x.shape
    T = B * S
    xf = x.reshape(T, D)
    logits = jnp.matmul(xf, w_router, preferred_element_type=jnp.float32) + b_router.astype(jnp.float32)
    top_val, top_idx = lax.top_k(logits, TOP_K)
    top_w = jax.nn.softmax(top_val, axis=-1).astype(x.dtype)
    n_assign = T * TOP_K
    flat_e = top_idx.reshape(-1)
    flat_tok = (jnp.arange(n_assign) // TOP_K).astype(jnp.int32)
    order = jnp.argsort(flat_e)
    e_sorted = flat_e[order]
    counts = jnp.bincount(flat_e, length=N_EXPERTS)
    padded = (counts + MOE_BLOCK - 1) // MOE_BLOCK * MOE_BLOCK
    pad_end = jnp.cumsum(padded)
    pad_start = pad_end - padded
    start = jnp.cumsum(counts) - counts
    dest = pad_start[e_sorted] + jnp.arange(n_assign) - start[e_sorted]
    n_rows = -(-n_assign // MOE_BLOCK) * MOE_BLOCK + N_EXPERTS * MOE_BLOCK
    n_blocks = n_rows // MOE_BLOCK
    row_tok = jnp.full((n_rows,), T, jnp.int32).at[dest].set(flat_tok[order])
    row_w = jnp.zeros((n_rows,), x.dtype).at[dest].set(top_w.reshape(-1)[order])
    blk_e = jnp.minimum(jnp.searchsorted(pad_end, jnp.arange(n_blocks) * MOE_BLOCK, side='right'), N_EXPERTS - 1)
    x_pad = jnp.concatenate([xf, jnp.zeros((1, D), x.dtype)], axis=0)

    def expert_block(args):
        tok, e = args
        xb = x_pad[tok]
        g = jnp.minimum(xb @ w_gate[e] + b_gate[e], SWIGLU_LIMIT)
        u = jnp.clip(xb @ w_up[e] + b_up[e], -SWIGLU_LIMIT, SWIGLU_LIMIT)
        hdn = (u + 1.0) * (g * jax.nn.sigmoid(SWIGLU_ALPHA * g))
        return hdn @ w_down[e] + b_down[e]

    y_rows = lax.map(expert_block, (row_tok.reshape(n_blocks, MOE_BLOCK), blk_e))
    y = jnp.zeros((T + 1, D), x.dtype).at[row_tok].add(y_rows.reshape(n_rows, D) * row_w[:, None])
    return y[:T].reshape(B, S, D)


def setup_inputs(seed: int = 0) -> dict:
    key = jax.random.key(seed)
    ks = iter(jax.random.split(key, 48))
    L, D, F, E = DEPTH, D_MODEL, D_EXPERT, N_EXPERTS

    def nrm(shape, scale):
        return jax.random.normal(next(ks), shape, jnp.float32) * scale

    def gain(shape):
        return 1.0 + nrm(shape, 0.02)

    return {
        "x": nrm((BATCH, SEQ, D), 1.0),
        "mem": nrm((BATCH, MEM_LEN, D), 1.0),
        "w_in": nrm((L, D, N_IN), D ** -0.5),
        "mla_q_norm": gain((L, MLA_Q_RANK)),
        "mla_w_q_up": nrm((L, MLA_Q_RANK, MLA_HEADS * (MLA_NOPE + MLA_ROPE)), MLA_Q_RANK ** -0.5),
        "mla_kv_norm": gain((L, MLA_KV_RANK)),
        "mla_w_kv_up": nrm((L, MLA_KV_RANK, MLA_HEADS * (MLA_NOPE + MLA_V)), MLA_KV_RANK ** -0.5),
        "nsa_pe_k": nrm((L, NSA_CMP_LEN, HEAD_DIM), 0.1),
        "nsa_pe_v": nrm((L, NSA_CMP_LEN, HEAD_DIM), 0.1),
        "nsa_w1_k": nrm((L, NSA_CMP_LEN * HEAD_DIM, HEAD_DIM), (NSA_CMP_LEN * HEAD_DIM) ** -0.5),
        "nsa_w1_v": nrm((L, NSA_CMP_LEN * HEAD_DIM, HEAD_DIM), (NSA_CMP_LEN * HEAD_DIM) ** -0.5),
        "nsa_w2_k": nrm((L, HEAD_DIM, HEAD_DIM), HEAD_DIM ** -0.5),
        "nsa_w2_v": nrm((L, HEAD_DIM, HEAD_DIM), HEAD_DIM ** -0.5),
        "w_branch": nrm((L, N_BRANCH, BRANCH_WIDTH, D), DN_BETA * BRANCH_WIDTH ** -0.5),
        "b_merge": nrm((L, N_BRANCH, D), 0.02),
        "w_out": nrm((L, D, D), DN_BETA * D ** -0.5),
        "ln_mix_g": gain((L, D)),
        "ln_mix_b": nrm((L, D), 0.02),
        "mem_w_q": nrm((L, D, MEM_HEADS * HEAD_DIM), D ** -0.5),
        "mem_w_k": nrm((L, D, MEM_HEADS * HEAD_DIM), D ** -0.5),
        "mem_w_v": nrm((L, D, MEM_HEADS * HEAD_DIM), DN_BETA * D ** -0.5),
        "mem_w_o": nrm((L, MEM_HEADS * HEAD_DIM, D), DN_BETA * (MEM_HEADS * HEAD_DIM) ** -0.5),
        "ln_mem_g": gain((L, D)),
        "ln_mem_b": nrm((L, D), 0.02),
        "moe_w_router": nrm((L, D, E), D ** -0.5),
        "moe_b_router": nrm((L, E), 0.01),
        "moe_w_gate": nrm((L, E, D, F), D ** -0.5),
        "moe_b_gate": nrm((L, E, F), 0.02),
        "moe_w_up": nrm((L, E, D, F), D ** -0.5),
        "moe_b_up": nrm((L, E, F), 0.02),
        "moe_w_down": nrm((L, E, F, D), DN_BETA * F ** -0.5),
        "moe_b_down": nrm((L, E, D), 0.02),
        "ln_moe_g": gain((L, D)),
        "ln_moe_b": nrm((L, D), 0.02),
    }


def reference(x, mem, w_in, mla_q_norm, mla_w_q_up, mla_kv_norm, mla_w_kv_up,
              nsa_pe_k, nsa_pe_v, nsa_w1_k, nsa_w1_v, nsa_w2_k, nsa_w2_v,
              w_branch, b_merge, w_out, ln_mix_g, ln_mix_b,
              mem_w_q, mem_w_k, mem_w_v, mem_w_o, ln_mem_g, ln_mem_b,
              moe_w_router, moe_b_router, moe_w_gate, moe_b_gate, moe_w_up, moe_b_up,
              moe_w_down, moe_b_down, ln_moe_g, ln_moe_b):
    for l in range(DEPTH):
        h = hybrid_mixer(x, w_in[l], mla_q_norm[l], mla_w_q_up[l], mla_kv_norm[l], mla_w_kv_up[l],
                         nsa_pe_k[l], nsa_pe_v[l], nsa_w1_k[l], nsa_w1_v[l], nsa_w2_k[l], nsa_w2_v[l],
                         w_branch[l], b_merge[l], w_out[l])
        x = layer_norm(DN_ALPHA * x + h, ln_mix_g[l], ln_mix_b[l])
        h = memory_attention(x, mem, mem_w_q[l], mem_w_k[l], mem_w_v[l], mem_w_o[l])
        x = layer_norm(DN_ALPHA * x + h, ln_mem_g[l], ln_mem_b[l])
        h = moe(x, moe_w_router[l], moe_b_router[l], moe_w_gate[l], moe_b_gate[l],
                moe_w_up[l], moe_b_up[l], moe_w_down[l], moe_b_down[l])
        x = layer_norm(DN_ALPHA * x + h, ln_moe_g[l], ln_moe_b[l])
    return x
```

```python
import functools

import numpy as np
import jax
import jax.numpy as jnp
from jax import lax
from jax.experimental import pallas as pl
from jax.experimental.pallas import tpu as pltpu

F32 = jnp.float32
BF16 = jnp.bfloat16

D_MODEL = 2048
HEAD_DIM = 128
MLA_HEADS = 8
MLA_Q_RANK = 512
MLA_KV_RANK = 256
MLA_NOPE = 128
MLA_ROPE = 64
ROPE_THETA = 10000.0
NSA_HEADS = 8
NSA_GROUPS = 2
NSA_HG = NSA_HEADS // NSA_GROUPS
NSA_CMP_LEN = 32
NSA_CMP_STRIDE = 16
NSA_SEL_LEN = 64
NSA_TOPK = 16
NSA_WINDOW = 512
SB_HEADS = 8
MEM_HEADS = 4
N_EXPERTS = 32
TOP_K = 4
D_EXPERT = 512
SWIGLU_LIMIT = 7.0
SWIGLU_ALPHA = 1.702
N_BRANCH = 3
BRANCH_WIDTH = 1024
LN_EPS = 1e-5
RMS_EPS = 1e-6
NEG = -1e30
BIG = 1e30

OFF_CQ = 0
OFF_CKV = 512
OFF_KR = 768
OFF_NSA_Q = 832
OFF_NSA_KV = 1856
OFF_NSA_GATE = 3392
OFF_SB = 3416
OFF_MERGE = 6488

VMEM_LIMIT_V7X = 56 * 1024 * 1024
MOE_BLOCK_ROWS = 256


def _cp(sem, vmem=VMEM_LIMIT_V7X):
    return pltpu.CompilerParams(dimension_semantics=sem, vmem_limit_bytes=vmem)


def _dot(a, b):
    return jnp.dot(a, b, preferred_element_type=F32)


def _dot_nt(a, b):
    return lax.dot_general(a, b, (((1,), (1,)), ((), ())), preferred_element_type=F32)


def _layer_norm(z, g, b):
    mu = jnp.mean(z, axis=-1, keepdims=True)
    zc = z - mu
    var = jnp.mean(zc * zc, axis=-1, keepdims=True)
    return zc * lax.rsqrt(var + LN_EPS) * g + b


def _rms_norm(z, g):
    return z * lax.rsqrt(jnp.mean(z * z, axis=-1, keepdims=True) + RMS_EPS) * g


def _proj_heads_kernel(a_ref, w_ref, s_ref, o_ref, abf_ref, *, n_heads_per_tile):
    @pl.when(pl.program_id(1) == 0)
    def _():
        abf_ref[...] = a_ref[...].astype(BF16)

    acc = _dot(abf_ref[...], w_ref[...]) * s_ref[...]
    for c in range(n_heads_per_tile):
        o_ref[c] = acc[:, c * HEAD_DIM:(c + 1) * HEAD_DIM].astype(o_ref.dtype)


def proj_heads(a, w, scale, *, tm, tn):
    m, k = a.shape
    n = w.shape[1]
    hpt = tn // HEAD_DIM
    return pl.pallas_call(
        functools.partial(_proj_heads_kernel, n_heads_per_tile=hpt),
        out_shape=jax.ShapeDtypeStruct((n // HEAD_DIM, m, HEAD_DIM), BF16),
        grid=(m // tm, n // tn),
        in_specs=[
            pl.BlockSpec((tm, k), lambda i, j: (i, 0)),
            pl.BlockSpec((k, tn), lambda i, j: (0, j)),
            pl.BlockSpec((1, tn), lambda i, j: (0, j)),
        ],
        out_specs=pl.BlockSpec((hpt, tm, HEAD_DIM), lambda i, j: (j, i, 0)),
        scratch_shapes=[pltpu.VMEM((tm, k), BF16)],
        compiler_params=_cp(("arbitrary", "arbitrary")),
    )(a, w, scale)


def _mla_in_kernel(x_ref, w_ref, qg_ref, kg_ref, wq_ref, wkv_ref, cos_ref, sin_ref,
                   q_ref, k_ref, v_ref, g_ref):
    xb = x_ref[...].astype(BF16)
    h = _dot(xb, w_ref[...])
    cq = h[:, 0:512]
    ckv = h[:, 512:768]
    kr1 = h[:, 768:896]
    kr2 = h[:, 896:1024]
    g_ref[...] = jax.nn.sigmoid(h[:, 1024:1280])
    cos = cos_ref[...]
    sin = sin_ref[...]
    scale = (MLA_NOPE + MLA_ROPE) ** -0.5
    nq = _rms_norm(cq, qg_ref[...]).astype(BF16)
    q3 = _dot(nq, wq_ref[...])
    for hh in range(MLA_HEADS):
        lo, hi = hh * 128, (hh + 1) * 128
        q_ref[hh, :, 0:128] = (q3[:, lo:hi] * scale).astype(BF16)
        rot = q3[:, 1024 + lo:1024 + hi] * cos + q3[:, 2048 + lo:2048 + hi] * sin
        q_ref[hh, :, 128:256] = (rot * scale).astype(BF16)
    nkv = _rms_norm(ckv, kg_ref[...]).astype(BF16)
    kv = _dot(nkv, wkv_ref[...])
    krot = (kr1 * cos + kr2 * sin).astype(BF16)
    for hh in range(MLA_HEADS):
        lo, hi = hh * 128, (hh + 1) * 128
        k_ref[hh, :, 0:128] = kv[:, lo:hi].astype(BF16)
        k_ref[hh, :, 128:256] = krot
        v_ref[hh] = kv[:, 1024 + lo:1024 + hi].astype(BF16)


def mla_in(x, w_mla, qg, kg, wq3, wkv, cos128, sin128, *, seq, tm):
    t = x.shape[0]
    npos = seq // tm
    full = lambda shape: pl.BlockSpec(shape, lambda i: (0,) * len(shape))
    return pl.pallas_call(
        _mla_in_kernel,
        out_shape=(
            jax.ShapeDtypeStruct((MLA_HEADS, t, 256), BF16),
            jax.ShapeDtypeStruct((MLA_HEADS, t, 256), BF16),
            jax.ShapeDtypeStruct((MLA_HEADS, t, 128), BF16),
            jax.ShapeDtypeStruct((t, 256), F32),
        ),
        grid=(t // tm,),
        in_specs=[
            pl.BlockSpec((tm, D_MODEL), lambda i: (i, 0)),
            full((D_MODEL, 1280)),
            full((1, MLA_Q_RANK)),
            full((1, MLA_KV_RANK)),
            full((MLA_Q_RANK, 3072)),
            full((MLA_KV_RANK, 2048)),
            pl.BlockSpec((tm, 128), lambda i: (i % npos, 0)),
            pl.BlockSpec((tm, 128), lambda i: (i % npos, 0)),
        ],
        out_specs=(
            pl.BlockSpec((MLA_HEADS, tm, 256), lambda i: (0, i, 0)),
            pl.BlockSpec((MLA_HEADS, tm, 256), lambda i: (0, i, 0)),
            pl.BlockSpec((MLA_HEADS, tm, 128), lambda i: (0, i, 0)),
            pl.BlockSpec((tm, 256), lambda i: (i, 0)),
        ),
        compiler_params=_cp(("arbitrary",)),
    )(x, w_mla, qg, kg, wq3, wkv, cos128, sin128)


def _mla_attn_kernel(q_ref, k_ref, v_ref, o_ref, *, tq):
    qi = pl.program_id(2)
    q = q_ref[...]

    def step(kt, carry, diag):
        m, l, acc = carry
        k0 = pl.multiple_of(kt * tq, tq)
        k = k_ref[pl.ds(k0, tq), :]
        v = v_ref[pl.ds(k0, tq), :]
        s = _dot_nt(q, k)
        if diag:
            row = lax.broadcasted_iota(jnp.int32, (tq, tq), 0)
            col = lax.broadcasted_iota(jnp.int32, (tq, tq), 1)
            s = jnp.where(col <= row, s, NEG)
        m_new = jnp.maximum(m, jnp.max(s, axis=1, keepdims=True))
        alpha = jnp.exp(m - m_new)
        p = jnp.exp(s - m_new)
        l = alpha * l + jnp.sum(p, axis=1, keepdims=True)
        acc = alpha * acc + _dot(p.astype(BF16), v)
        return m_new, l, acc

    init = (jnp.full((tq, 1), NEG, F32), jnp.zeros((tq, 1), F32), jnp.zeros((tq, 128), F32))
    carry = lax.fori_loop(0, qi, lambda kt, c: step(kt, c, False), init)
    m, l, acc = step(qi, carry, True)
    o_ref[...] = (acc / l).astype(o_ref.dtype)


def mla_attn(q, k, v, *, batch, seq, tq):
    nq = seq // tq
    q4 = q.reshape(MLA_HEADS, batch, seq, 256)
    k4 = k.reshape(MLA_HEADS, batch, seq, 256)
    v4 = v.reshape(MLA_HEADS, batch, seq, 128)
    return pl.pallas_call(
        functools.partial(_mla_attn_kernel, tq=tq),
        out_shape=jax.ShapeDtypeStruct((batch * seq, MLA_HEADS * 128), BF16),
        grid=(MLA_HEADS, batch, nq),
        in_specs=[
            pl.BlockSpec((None, None, tq, 256), lambda h, b, i: (h, b, i, 0)),
            pl.BlockSpec((None, None, seq, 256), lambda h, b, i: (h, b, 0, 0)),
            pl.BlockSpec((None, None, seq, 128), lambda h, b, i: (h, b, 0, 0)),
        ],
        out_specs=pl.BlockSpec((tq, 128), lambda h, b, i: (b * nq + i, h)),
        compiler_params=_cp(("arbitrary", "arbitrary", "arbitrary")),
    )(q4, k4, v4)


def _sb_attn_kernel(q_ref, k_ref, v_ref, u_ref, o_ref, *, tq):
    qi = pl.program_id(2)
    q = q_ref[...]
    u = u_ref[...]

    def step(kt, carry, diag):
        run, acc = carry
        k0 = pl.multiple_of(kt * tq, tq)
        k = k_ref[pl.ds(k0, tq), :]
        v = v_ref[pl.ds(k0, tq), :]
        z = _dot_nt(q, k)
        l1m = -(jnp.maximum(z, 0.0) + jnp.log(1.0 + jnp.exp(-jnp.abs(z))))
        if diag:
            row = lax.broadcasted_iota(jnp.int32, (tq, tq), 0)
            col = lax.broadcasted_iota(jnp.int32, (tq, tq), 1)
            strict = col < row
            l1m_m = jnp.where(strict, l1m, 0.0)
        else:
            l1m_m = l1m
        hi = l1m_m.astype(BF16)
        lo = (l1m_m - hi.astype(F32)).astype(BF16)
        between = _dot(hi, u) + _dot(lo, u)
        a = jnp.exp(z + l1m + between + run)
        if diag:
            a = jnp.where(strict, a, 0.0)
        acc = acc + _dot(a.astype(BF16), v)
        run = run + between[:, 0:1] + l1m_m[:, 0:1]
        return run, acc

    init = (jnp.zeros((tq, 1), F32), jnp.zeros((tq, 128), F32))
    carry = step(qi, init, True)
    run, acc = lax.fori_loop(0, qi, lambda j, c: step(qi - 1 - j, c, False), carry)
    o_ref[...] = acc.astype(o_ref.dtype)


def sb_attn(hm, u, *, head0, batch, seq, tq):
    nq = seq // tq
    hm4 = hm.reshape(hm.shape[0], batch, seq, HEAD_DIM)
    return pl.pallas_call(
        functools.partial(_sb_attn_kernel, tq=tq),
        out_shape=jax.ShapeDtypeStruct((batch * seq, SB_HEADS * HEAD_DIM), BF16),
        grid=(SB_HEADS, batch, nq),
        in_specs=[
            pl.BlockSpec((None, None, tq, HEAD_DIM), lambda h, b, i: (head0 + h, b, i, 0)),
            pl.BlockSpec((None, None, seq, HEAD_DIM), lambda h, b, i: (head0 + SB_HEADS + h, b, 0, 0)),
            pl.BlockSpec((None, None, seq, HEAD_DIM), lambda h, b, i: (head0 + 2 * SB_HEADS + h, b, 0, 0)),
            pl.BlockSpec((tq, tq), lambda h, b, i: (0, 0)),
        ],
        out_specs=pl.BlockSpec((tq, HEAD_DIM), lambda h, b, i: (b * nq + i, h)),
        compiler_params=_cp(("arbitrary", "arbitrary", "arbitrary")),
    )(hm4, hm4, hm4, u)


def _nsa_cmp_kernel(c_ref, w1_ref, pe_ref, w2_ref, o_ref, *, nc):
    c = c_ref[...]
    half = NSA_CMP_STRIDE * HEAD_DIM
    a1 = _dot(c, w1_ref[0:half, :])
    a2 = _dot(c, w1_ref[half:2 * half, :])
    pc = _dot(pe_ref[...], w1_ref[...])[0:1, :]
    pre = a1 + pltpu.roll(a2, nc - 1, 0) + pc
    act = 0.5 * pre * (1.0 + jnp.tanh(0.7978845608028654 * (pre + 0.044715 * (pre * pre * pre))))
    o_ref[...] = _dot(act.astype(BF16), w2_ref[...]).astype(BF16)


def nsa_compress(hm, w1, pe, w2, *, head0, batch, seq):
    nc = seq // NSA_CMP_STRIDE
    hm4 = hm.reshape(hm.shape[0], batch, nc, NSA_CMP_STRIDE * HEAD_DIM)
    return pl.pallas_call(
        functools.partial(_nsa_cmp_kernel, nc=nc),
        out_shape=jax.ShapeDtypeStruct((4, batch, nc, HEAD_DIM), BF16),
        grid=(4, batch),
        in_specs=[
            pl.BlockSpec((None, None, nc, NSA_CMP_STRIDE * HEAD_DIM), lambda c, b: (head0 + c, b, 0, 0)),
            pl.BlockSpec((None, NSA_CMP_LEN * HEAD_DIM, HEAD_DIM), lambda c, b: (c // 2, 0, 0)),
            pl.BlockSpec((None, 8, NSA_CMP_LEN * HEAD_DIM), lambda c, b: (c // 2, 0, 0)),
            pl.BlockSpec((None, HEAD_DIM, HEAD_DIM), lambda c, b: (c // 2, 0, 0)),
        ],
        out_specs=pl.BlockSpec((None, None, nc, HEAD_DIM), lambda c, b: (c, b, 0, 0)),
        compiler_params=_cp(("arbitrary", "arbitrary")),
    )(hm4, w1, pe, w2)


def _nsa_attn_kernel(q_ref, kc_ref, vc_ref, ks_ref, vs_ref, kw_ref, vw_ref, g_ref, ov_ref, e_ref,
                     o_ref, *, tq, tk, seq, n_sel, n_top):
    grp = pl.program_id(1)
    qi = pl.program_id(2)
    t0 = qi * tq
    rows = NSA_HG * tq
    nc = seq // NSA_CMP_STRIDE
    q = q_ref[...].reshape(rows, HEAD_DIM)

    rid = lax.broadcasted_iota(jnp.int32, (rows, 1), 0)
    hrow = lax.shift_right_logical(rid, int(np.log2(tq)))
    trow = t0 + lax.bitwise_and(rid, tq - 1)
    tf = trow.astype(F32)
    slope = jnp.exp2(-(grp * NSA_HG + hrow + 1).astype(F32))

    def masked_softmax(s, valid):
        sm = jnp.where(valid, s, NEG)
        m = jnp.max(sm, axis=1, keepdims=True)
        e = jnp.where(valid, jnp.exp(sm - m), 0.0)
        d = jnp.sum(e, axis=1, keepdims=True)
        return e * (1.0 / jnp.where(d > 0.0, d, 1.0))

    n_i = lax.broadcasted_iota(jnp.int32, (1, nc), 1)
    mid = n_i.astype(F32) * float(NSA_CMP_STRIDE) + (NSA_CMP_LEN - 1) / 2.0
    end = n_i * NSA_CMP_STRIDE + (NSA_CMP_LEN - 1)
    s_c = _dot_nt(q, kc_ref[...]) - slope * (tf - mid)
    p_c = masked_softmax(s_c, end <= trow)
    o_c = _dot(p_c.astype(BF16), vc_ref[...])

    psum = p_c[0:tq] + p_c[tq:2 * tq] + p_c[2 * tq:3 * tq] + p_c[3 * tq:4 * tq]
    p_hi = psum.astype(BF16)
    p_lo = (psum - p_hi.astype(F32)).astype(BF16)
    ov = ov_ref[...]
    imp = _dot(p_hi, ov) + _dot(p_lo, ov)
    cur = lax.shift_right_logical(t0 + lax.broadcasted_iota(jnp.int32, (tq, 1), 0),
                                  int(np.log2(NSA_SEL_LEN)))
    blk = lax.broadcasted_iota(jnp.int32, (tq, n_sel), 1)
    forced = (blk == 0) | (blk == cur) | (blk == cur - 1)
    key = jnp.where(blk > cur, -BIG, jnp.where(forced, BIG, imp))
    rank = jnp.zeros((tq, n_sel), F32)
    for i in range(n_sel):
        vi = key[:, i:i + 1]
        beats = (vi > key) | ((vi == key) & (blk > i))
        rank = rank + jnp.where(beats, 1.0, 0.0)
    selm = jnp.where((rank < float(n_top)) & (blk <= cur), 1.0, 0.0).astype(BF16)

    def sel_step(kt, carry, diag):
        m, l, acc = carry
        k0 = pl.multiple_of(kt * tk, tk)
        kk = ks_ref[pl.ds(k0, tk), :]
        vv = vs_ref[pl.ds(k0, tk), :]
        mex = _dot(selm, e_ref[kt])
        mex4 = jnp.concatenate([mex] * NSA_HG, axis=0)
        spos = k0 + lax.broadcasted_iota(jnp.int32, (1, tk), 1)
        s = _dot_nt(q, kk) - slope * (tf - spos.astype(F32))
        valid = mex4 > 0.5
        if diag:
            valid = valid & (spos <= trow)
        sm = jnp.where(valid, s, NEG)
        m_new = jnp.maximum(m, jnp.max(sm, axis=1, keepdims=True))
        alpha = jnp.exp(m - m_new)
        p = jnp.where(valid, jnp.exp(sm - m_new), 0.0)
        l = alpha * l + jnp.sum(p, axis=1, keepdims=True)
        acc = alpha * acc + _dot(p.astype(BF16), vv)
        return m_new, l, acc

    kt_last = t0 // tk
    init = (jnp.full((rows, 1), NEG, F32), jnp.zeros((rows, 1), F32), jnp.zeros((rows, HEAD_DIM), F32))
    carry = lax.fori_loop(0, kt_last, lambda kt, c: sel_step(kt, c, False), init)
    _, l_s, acc_s = sel_step(kt_last, carry, True)
    o_s = acc_s * (1.0 / l_s)

    wk = NSA_WINDOW + tq
    ks0 = pl.multiple_of(jnp.maximum(t0 - NSA_WINDOW, 0), tq)
    kw = kw_ref[pl.ds(ks0, wk), :]
    vw = vw_ref[pl.ds(ks0, wk), :]
    wpos = ks0 + lax.broadcasted_iota(jnp.int32, (1, wk), 1)
    dw = trow - wpos
    s_w = _dot_nt(q, kw) - slope * dw.astype(F32)
    p_w = masked_softmax(s_w, (dw >= 0) & (dw < NSA_WINDOW))
    o_w = _dot(p_w.astype(BF16), vw)

    gt = g_ref[...]
    for hg in range(NSA_HG):
        sl = slice(hg * tq, (hg + 1) * tq)
        o = (gt[:, 3 * hg:3 * hg + 1] * o_c[sl] + gt[:, 3 * hg + 1:3 * hg + 2] * o_s[sl]
             + gt[:, 3 * hg + 2:3 * hg + 3] * o_w[sl])
        o_ref[:, hg * HEAD_DIM:(hg + 1) * HEAD_DIM] = o.astype(o_ref.dtype)


def nsa_attn(hm, cmp, gates, ov, e, *, q_head0, kv_head0, batch, seq, tq, tk):
    nq = seq // tq
    nc = seq // NSA_CMP_STRIDE
    n_sel = seq // NSA_SEL_LEN
    n_top = min(NSA_TOPK, n_sel)
    assert tk % tq == 0 and seq % tk == 0 and seq >= NSA_WINDOW + tq and NSA_WINDOW % tq == 0
    hm4 = hm.reshape(hm.shape[0], batch, seq, HEAD_DIM)
    kv_spec = lambda off: pl.BlockSpec((None, None, seq, HEAD_DIM),
                                       lambda b, g, i: (kv_head0 + off + g, b, 0, 0))
    return pl.pallas_call(
        functools.partial(_nsa_attn_kernel, tq=tq, tk=tk, seq=seq, n_sel=n_sel, n_top=n_top),
        out_shape=jax.ShapeDtypeStruct((batch * seq, NSA_HEADS * HEAD_DIM), BF16),
        grid=(batch, NSA_GROUPS, nq),
        in_specs=[
            pl.BlockSpec((NSA_HG, None, tq, HEAD_DIM), lambda b, g, i: (q_head0 // NSA_HG + g, b, i, 0)),
            pl.BlockSpec((None, None, nc, HEAD_DIM), lambda b, g, i: (g, b, 0, 0)),
            pl.BlockSpec((None, None, nc, HEAD_DIM), lambda b, g, i: (2 + g, b, 0, 0)),
            kv_spec(4), kv_spec(6), kv_spec(8), kv_spec(10),
            pl.BlockSpec((tq, 128), lambda b, g, i: (b * nq + i, g)),
            pl.BlockSpec((nc, n_sel), lambda b, g, i: (0, 0)),
            pl.BlockSpec((seq // tk, n_sel, tk), lambda b, g, i: (0, 0, 0)),
        ],
        out_specs=pl.BlockSpec((tq, NSA_HG * HEAD_DIM), lambda b, g, i: (b * nq + i, g)),
        compiler_params=_cp(("arbitrary", "arbitrary", "arbitrary")),
    )(hm4, cmp, cmp, hm4, hm4, hm4, hm4, gates, ov, e)


def _merge_kernel(x_ref, oa_ref, ob_ref, oc_ref, wm_ref, wb_ref, bm_ref, y_ref, xb_ref):
    @pl.when(pl.program_id(1) == 0)
    def _():
        xb_ref[...] = x_ref[...].astype(BF16)

    xb = xb_ref[...]
    acc = None
    for br, o_ref in enumerate((oa_ref, ob_ref, oc_ref)):
        gate = jax.nn.sigmoid(_dot(xb, wm_ref[br]) + bm_ref[br])
        term = gate * _dot(o_ref[...], wb_ref[br])
        acc = term if acc is None else acc + term
    y_ref[...] = acc.astype(y_ref.dtype)


def merge_branches(x, o_a, o_b, o_c, wm, wb, bm, *, tm, tn):
    t = x.shape[0]
    o_spec = pl.BlockSpec((tm, BRANCH_WIDTH), lambda i, j: (i, 0))
    return pl.pallas_call(
        _merge_kernel,
        out_shape=jax.ShapeDtypeStruct((t, D_MODEL), BF16),
        grid=(t // tm, D_MODEL // tn),
        in_specs=[
            pl.BlockSpec((tm, D_MODEL), lambda i, j: (i, 0)),
            o_spec, o_spec, o_spec,
            pl.BlockSpec((N_BRANCH, D_MODEL, tn), lambda i, j: (0, 0, j)),
            pl.BlockSpec((N_BRANCH, BRANCH_WIDTH, tn), lambda i, j: (0, 0, j)),
            pl.BlockSpec((N_BRANCH, 1, tn), lambda i, j: (0, 0, j)),
        ],
        out_specs=pl.BlockSpec((tm, tn), lambda i, j: (i, j)),
        scratch_shapes=[pltpu.VMEM((tm, D_MODEL), BF16)],
        compiler_params=_cp(("arbitrary", "arbitrary")),
    )(x, o_a, o_b, o_c, wm, wb, bm)


def _out_ln_kernel(y_ref, w_ref, x_ref, g_ref, b_ref, o_ref, *, alpha):
    h = _dot(y_ref[...], w_ref[...])
    o_ref[...] = _layer_norm(alpha * x_ref[...] + h, g_ref[...], b_ref[...])


def out_ln(y, w, x, g, b, *, alpha, tm):
    t = x.shape[0]
    return pl.pallas_call(
        functools.partial(_out_ln_kernel, alpha=alpha),
        out_shape=jax.ShapeDtypeStruct((t, D_MODEL), F32),
        grid=(t // tm,),
        in_specs=[
            pl.BlockSpec((tm, D_MODEL), lambda i: (i, 0)),
            pl.BlockSpec((D_MODEL, D_MODEL), lambda i: (0, 0)),
            pl.BlockSpec((tm, D_MODEL), lambda i: (i, 0)),
            pl.BlockSpec((1, D_MODEL), lambda i: (0, 0)),
            pl.BlockSpec((1, D_MODEL), lambda i: (0, 0)),
        ],
        out_specs=pl.BlockSpec((tm, D_MODEL), lambda i: (i, 0)),
        compiler_params=_cp(("arbitrary",)),
    )(y, w, x, g, b)


def _mem_attn_kernel(x_ref, wq_ref, k_ref, v_ref, wo_ref, g_ref, b_ref, o_ref, *, alpha):
    x = x_ref[...]
    q = _dot(x.astype(BF16), wq_ref[...]) * (HEAD_DIM ** -0.5)
    outs = []
    for h in range(MEM_HEADS):
        qh = q[:, h * HEAD_DIM:(h + 1) * HEAD_DIM].astype(BF16)
        s = _dot_nt(qh, k_ref[h])
        m = jnp.max(s, axis=1, keepdims=True)
        e = jnp.exp(s - m)
        p = e * (1.0 / jnp.sum(e, axis=1, keepdims=True))
        outs.append(_dot(p.astype(BF16), v_ref[h]).astype(BF16))
    o = jnp.concatenate(outs, axis=1)
    h_out = _dot(o, wo_ref[...])
    o_ref[...] = _layer_norm(alpha * x + h_out, g_ref[...], b_ref[...])


def mem_attn_ln(x, wq, kv, wo, g, b, *, alpha, seq, mem_len, tm):
    t = x.shape[0]
    per_b = seq // tm
    kv4 = kv.reshape(2 * MEM_HEADS, t // seq, mem_len, HEAD_DIM)
    width = MEM_HEADS * HEAD_DIM
    return pl.pallas_call(
        functools.partial(_mem_attn_kernel, alpha=alpha),
        out_shape=jax.ShapeDtypeStruct((t, D_MODEL), F32),
        grid=(t // tm,),
        in_specs=[
            pl.BlockSpec((tm, D_MODEL), lambda i: (i, 0)),
            pl.BlockSpec((D_MODEL, width), lambda i: (0, 0)),
            pl.BlockSpec((MEM_HEADS, None, mem_len, HEAD_DIM), lambda i: (0, i // per_b, 0, 0)),
            pl.BlockSpec((MEM_HEADS, None, mem_len, HEAD_DIM), lambda i: (1, i // per_b, 0, 0)),
            pl.BlockSpec((width, D_MODEL), lambda i: (0, 0)),
            pl.BlockSpec((1, D_MODEL), lambda i: (0, 0)),
            pl.BlockSpec((1, D_MODEL), lambda i: (0, 0)),
        ],
        out_specs=pl.BlockSpec((tm, D_MODEL), lambda i: (i, 0)),
        compiler_params=_cp(("arbitrary",)),
    )(x, wq, kv4, kv4, wo, g, b)


def _router_kernel(x_ref, w_ref, b_ref, mask_ref, wsel_ref):
    x = x_ref[...]
    xh = x.astype(BF16)
    xl = (x - xh.astype(F32)).astype(BF16)
    logits = _dot(xh, w_ref[0]) + _dot(xh, w_ref[1]) + _dot(xl, w_ref[0]) + b_ref[...]
    tm = logits.shape[0]
    lane = lax.broadcasted_iota(jnp.int32, (tm, N_EXPERTS), 1)
    work = logits
    hots, vals = [], []
    for _ in range(TOP_K):
        m = jnp.max(work, axis=1, keepdims=True)
        idx = jnp.min(jnp.where(work == m, lane, N_EXPERTS), axis=1, keepdims=True)
        hot = lane == idx
        hots.append(hot)
        vals.append(m)
        work = jnp.where(hot, -jnp.inf, work)
    es = [jnp.exp(v - vals[0]) for v in vals]
    inv = 1.0 / (es[0] + es[1] + es[2] + es[3])
    mask = jnp.zeros((tm, N_EXPERTS), F32)
    wsel = jnp.zeros((tm, N_EXPERTS), F32)
    for hot, e in zip(hots, es):
        mask = mask + jnp.where(hot, 1.0, 0.0)
        wsel = wsel + jnp.where(hot, e * inv, 0.0)
    mask_ref[...] = mask
    wsel_ref[...] = wsel


def router(x, w_hl, b, *, tm):
    t = x.shape[0]
    return pl.pallas_call(
        _router_kernel,
        out_shape=(jax.ShapeDtypeStruct((t, N_EXPERTS), F32), jax.ShapeDtypeStruct((t, N_EXPERTS), F32)),
        grid=(t // tm,),
        in_specs=[
            pl.BlockSpec((tm, D_MODEL), lambda i: (i, 0)),
            pl.BlockSpec((2, D_MODEL, N_EXPERTS), lambda i: (0, 0, 0)),
            pl.BlockSpec((1, N_EXPERTS), lambda i: (0, 0)),
        ],
        out_specs=(pl.BlockSpec((tm, N_EXPERTS), lambda i: (i, 0)),
                   pl.BlockSpec((tm, N_EXPERTS), lambda i: (i, 0))),
        compiler_params=_cp(("arbitrary",)),
    )(x, w_hl, b)


def _row_dma(src_hbm, dst_hbm, sem, src_row, dst_row):
    return pltpu.make_async_copy(src_hbm.at[pl.ds(src_row, 1)], dst_hbm.at[pl.ds(dst_row, 1)], sem)


def _scatter_rows_kernel(pos_ref, x_hbm, init_hbm, out_hbm, sem, *, tt):
    del init_hbm
    t0 = pl.program_id(0) * tt

    def issue(t, c):
        for k in range(TOP_K):
            _row_dma(x_hbm, out_hbm, sem, t0 + t, pos_ref[t * TOP_K + k]).start()
        return c

    lax.fori_loop(0, tt, issue, 0)

    def drain(t, c):
        for k in range(TOP_K):
            _row_dma(x_hbm, out_hbm, sem, 0, 0).wait()
        return c

    lax.fori_loop(0, tt, drain, 0)


def scatter_rows(x, pos_flat, n_rows, *, tt):
    t, d = x.shape
    init = jnp.zeros((n_rows, d), x.dtype)
    return pl.pallas_call(
        functools.partial(_scatter_rows_kernel, tt=tt),
        out_shape=jax.ShapeDtypeStruct((n_rows, d), x.dtype),
        grid=(t // tt,),
        in_specs=[
            pl.BlockSpec((tt * TOP_K,), lambda i: (i,), memory_space=pltpu.SMEM),
            pl.BlockSpec(memory_space=pl.ANY),
            pl.BlockSpec(memory_space=pl.ANY),
        ],
        out_specs=pl.BlockSpec(memory_space=pl.ANY),
        scratch_shapes=[pltpu.SemaphoreType.DMA(())],
        input_output_aliases={2: 0},
        compiler_params=pltpu.CompilerParams(dimension_semantics=("arbitrary",), has_side_effects=True),
    )(pos_flat, x, init)


def _gather_rows_kernel(pos_ref, y_hbm, out_hbm, sem, *, tt, n_tok):
    t0 = pl.program_id(0) * tt

    def issue(t, c):
        for k in range(TOP_K):
            _row_dma(y_hbm, out_hbm, sem, pos_ref[t * TOP_K + k], k * n_tok + t0 + t).start()
        return c

    lax.fori_loop(0, tt, issue, 0)

    def drain(t, c):
        for k in range(TOP_K):
            _row_dma(y_hbm, out_hbm, sem, 0, 0).wait()
        return c

    lax.fori_loop(0, tt, drain, 0)


def gather_rows(y_rows, pos_flat, n_tok, *, tt):
    d = y_rows.shape[1]
    return pl.pallas_call(
        functools.partial(_gather_rows_kernel, tt=tt, n_tok=n_tok),
        out_shape=jax.ShapeDtypeStruct((TOP_K * n_tok, d), y_rows.dtype),
        grid=(n_tok // tt,),
        in_specs=[
            pl.BlockSpec((tt * TOP_K,), lambda i: (i,), memory_space=pltpu.SMEM),
            pl.BlockSpec(memory_space=pl.ANY),
        ],
        out_specs=pl.BlockSpec(memory_space=pl.ANY),
        scratch_shapes=[pltpu.SemaphoreType.DMA(())],
        compiler_params=pltpu.CompilerParams(dimension_semantics=("arbitrary",), has_side_effects=True),
    )(pos_flat, y_rows)


def _experts_kernel(be_ref, x_ref, wg_ref, bg_ref, wu_ref, bu_ref, wd_ref, bd_ref, y_ref,
                    wgb_ref, wub_ref, wdb_ref):
    i = pl.program_id(0)
    prev = be_ref[jnp.maximum(i - 1, 0)]

    @pl.when((i == 0) | (be_ref[i] != prev))
    def _():
        wgb_ref[...] = wg_ref[...].astype(BF16)
        wub_ref[...] = wu_ref[...].astype(BF16)
        wdb_ref[...] = wd_ref[...].astype(BF16)

    xb = x_ref[...].astype(BF16)
    g = jnp.minimum(_dot(xb, wgb_ref[...]) + bg_ref[...], SWIGLU_LIMIT)
    u = jnp.clip(_dot(xb, wub_ref[...]) + bu_ref[...], -SWIGLU_LIMIT, SWIGLU_LIMIT)
    hdn = (u + 1.0) * (g * jax.nn.sigmoid(SWIGLU_ALPHA * g))
    y_ref[...] = _dot(hdn.astype(BF16), wdb_ref[...]) + bd_ref[...]


def experts(x_rows, blk_e, wg, bg, wu, bu, wd, bd, *, bm):
    n_rows, d = x_rows.shape
    f = wg.shape[2]
    grid_spec = pltpu.PrefetchScalarGridSpec(
        num_scalar_prefetch=1,
        grid=(n_rows // bm,),
        in_specs=[
            pl.BlockSpec((bm, d), lambda i, be: (i, 0)),
            pl.BlockSpec((None, d, f), lambda i, be: (be[i], 0, 0)),
            pl.BlockSpec((None, 1, f), lambda i, be: (be[i], 0, 0)),
            pl.BlockSpec((None, d, f), lambda i, be: (be[i], 0, 0)),
            pl.BlockSpec((None, 1, f), lambda i, be: (be[i], 0, 0)),
            pl.BlockSpec((None, f, d), lambda i, be: (be[i], 0, 0)),
            pl.BlockSpec((None, 1, d), lambda i, be: (be[i], 0, 0)),
        ],
        out_specs=pl.BlockSpec((bm, d), lambda i, be: (i, 0)),
        scratch_shapes=[pltpu.VMEM((d, f), BF16), pltpu.VMEM((d, f), BF16), pltpu.VMEM((f, d), BF16)],
    )
    return pl.pallas_call(
        _experts_kernel,
        out_shape=jax.ShapeDtypeStruct((n_rows, d), F32),
        grid_spec=grid_spec,
        compiler_params=_cp(("arbitrary",)),
    )(blk_e, x_rows, wg, bg, wu, bu, wd, bd)


def _moe_ln_kernel(x_ref, y_ref, w_ref, g_ref, b_ref, o_ref, *, alpha):
    w = w_ref[...]
    y = w[:, 0:1] * y_ref[0]
    for k in range(1, TOP_K):
        y = y + w[:, k:k + 1] * y_ref[k]
    o_ref[...] = _layer_norm(alpha * x_ref[...] + y, g_ref[...], b_ref[...])


def moe_ln(x, y4, w4p, g, b, *, alpha, tm):
    t = x.shape[0]
    return pl.pallas_call(
        functools.partial(_moe_ln_kernel, alpha=alpha),
        out_shape=jax.ShapeDtypeStruct((t, D_MODEL), F32),
        grid=(t // tm,),
        in_specs=[
            pl.BlockSpec((tm, D_MODEL), lambda i: (i, 0)),
            pl.BlockSpec((TOP_K, tm, D_MODEL), lambda i: (0, i, 0)),
            pl.BlockSpec((tm, 128), lambda i: (i, 0)),
            pl.BlockSpec((1, D_MODEL), lambda i: (0, 0)),
            pl.BlockSpec((1, D_MODEL), lambda i: (0, 0)),
        ],
        out_specs=pl.BlockSpec((tm, D_MODEL), lambda i: (i, 0)),
        compiler_params=_cp(("arbitrary",)),
    )(x, y4, w4p, g, b)


def _rope_tables(seq):
    inv = np.asarray(ROPE_THETA ** (-np.arange(0, MLA_ROPE, 2) / MLA_ROPE), np.float32)
    ang = jnp.arange(seq, dtype=F32)[:, None] * jnp.asarray(inv)[None, :]
    cos, sin = jnp.cos(ang), jnp.sin(ang)
    zeros = jnp.zeros((seq, 128 - MLA_ROPE), F32)
    return (jnp.concatenate([cos, cos, zeros], axis=1), jnp.concatenate([-sin, sin, zeros], axis=1))


def _nsa_constants(seq, tk):
    nc = seq // NSA_CMP_STRIDE
    n_cmp = (seq - NSA_CMP_LEN) // NSA_CMP_STRIDE + 1
    n_sel = seq // NSA_SEL_LEN
    cs = np.arange(nc) * NSA_CMP_STRIDE
    ss = np.arange(n_sel) * NSA_SEL_LEN
    ov = np.clip(np.minimum(cs[:, None] + NSA_CMP_LEN, ss[None, :] + NSA_SEL_LEN)
                 - np.maximum(cs[:, None], ss[None, :]), 0, None) / NSA_CMP_LEN
    ov[n_cmp:] = 0.0
    e = (np.arange(seq)[None, :] // NSA_SEL_LEN == np.arange(n_sel)[:, None]).astype(np.float32)
    e = e.reshape(n_sel, seq // tk, tk).transpose(1, 0, 2)
    return jnp.asarray(ov, BF16), jnp.asarray(e, BF16)


def _pad_cols(w, width):
    return jnp.pad(w, ((0, 0), (0, width - w.shape[1])))


def _swap_halves(w):
    half = w.shape[1] // 2
    return jnp.concatenate([w[:, half:], w[:, :half]], axis=1)


def _layer_weights(w_in, w_q_up, w_kv_up):
    kr = w_in[:, OFF_KR:OFF_NSA_Q]
    gate = w_in[:, OFF_NSA_GATE:OFF_SB]
    per_g = NSA_HG * 3
    w_mla = jnp.concatenate([
        w_in[:, OFF_CQ:OFF_KR],
        _pad_cols(kr, 128), _pad_cols(_swap_halves(kr), 128),
        _pad_cols(gate[:, :per_g], 128), _pad_cols(gate[:, per_g:], 128)], axis=1).astype(BF16)
    wq = w_q_up.reshape(MLA_Q_RANK, MLA_HEADS, MLA_NOPE + MLA_ROPE)
    rope = wq[:, :, MLA_NOPE:]
    rope_sw = jnp.concatenate([rope[:, :, MLA_ROPE // 2:], rope[:, :, :MLA_ROPE // 2]], axis=2)
    pad = ((0, 0), (0, 0), (0, 128 - MLA_ROPE))
    wq3 = jnp.concatenate([
        wq[:, :, :MLA_NOPE].reshape(MLA_Q_RANK, -1),
        jnp.pad(rope, pad).reshape(MLA_Q_RANK, -1),
        jnp.pad(rope_sw, pad).reshape(MLA_Q_RANK, -1)], axis=1).astype(BF16)
    wkv = w_kv_up.reshape(MLA_KV_RANK, MLA_HEADS, 2, 128)
    wkv = jnp.concatenate([wkv[:, :, 0].reshape(MLA_KV_RANK, -1),
                           wkv[:, :, 1].reshape(MLA_KV_RANK, -1)], axis=1).astype(BF16)
    w_heads = w_in[:, OFF_NSA_Q:OFF_NSA_GATE]
    w_heads = jnp.concatenate([w_heads, w_in[:, OFF_SB:OFF_MERGE]], axis=1).astype(BF16)
    wm = w_in[:, OFF_MERGE:].reshape(D_MODEL, N_BRANCH, D_MODEL).transpose(1, 0, 2).astype(BF16)
    return w_mla, wq3, wkv, w_heads, wm


def _forward(x, mem, w_in, mla_q_norm, mla_w_q_up, mla_kv_norm, mla_w_kv_up,
             nsa_pe_k, nsa_pe_v, nsa_w1_k, nsa_w1_v, nsa_w2_k, nsa_w2_v,
             w_branch, b_merge, w_out, ln_mix_g, ln_mix_b,
             mem_w_q, mem_w_k, mem_w_v, mem_w_o, ln_mem_g, ln_mem_b,
             moe_w_router, moe_b_router, moe_w_gate, moe_b_gate, moe_w_up, moe_b_up,
             moe_w_down, moe_b_down, ln_moe_g, ln_moe_b):
    batch, seq, _ = x.shape
    mem_len = mem.shape[1]
    depth = w_in.shape[0]
    t = batch * seq
    alpha = float((2 * depth) ** 0.25)
    bm = MOE_BLOCK_ROWS
    n_rows = t * TOP_K + N_EXPERTS * bm
    n_blocks = n_rows // bm

    tm_in = min(512, seq)
    tq_mla = min(512, seq)
    tq_sb = 256
    tq_nsa, tk_nsa = 128, 512
    tm_ln = 256
    tt_rows = 512

    cos128, sin128 = _rope_tables(seq)
    ov, e_sel = _nsa_constants(seq, tk_nsa)
    u_sb = jnp.asarray(np.arange(tq_sb)[:, None] > np.arange(tq_sb)[None, :], BF16)
    n_qheads = NSA_HEADS
    n_kvheads = 3 * 2 * NSA_GROUPS
    head_scale = np.ones((1, (n_qheads + n_kvheads + 3 * SB_HEADS) * HEAD_DIM), np.float32)
    head_scale[:, :n_qheads * HEAD_DIM] = HEAD_DIM ** -0.5
    sb0 = n_qheads + n_kvheads
    head_scale[:, sb0 * HEAD_DIM:(sb0 + SB_HEADS) * HEAD_DIM] = HEAD_DIM ** -0.5
    head_scale = jnp.asarray(head_scale)
    ones_kv = jnp.ones((1, 2 * MEM_HEADS * HEAD_DIM), F32)

    xf = x.reshape(t, D_MODEL)
    memf = mem.reshape(batch * mem_len, D_MODEL)
    row = lambda v: v.reshape(1, -1)

    for l in range(depth):
        w_mla, wq3, wkv, w_heads, wm = _layer_weights(w_in[l], mla_w_q_up[l], mla_w_kv_up[l])

        q_a, k_a, v_a, gates = mla_in(xf, w_mla, row(mla_q_norm[l]), row(mla_kv_norm[l]), wq3, wkv,
                                      cos128, sin128, seq=seq, tm=tm_ln)
        hm = proj_heads(xf, w_heads, head_scale, tm=tm_in, tn=512)
        o_a = mla_attn(q_a, k_a, v_a, batch=batch, seq=seq, tq=tq_mla)
        w1 = jnp.stack([nsa_w1_k[l], nsa_w1_v[l]]).astype(BF16)
        pe = jnp.stack([nsa_pe_k[l], nsa_pe_v[l]]).reshape(2, 1, -1)
        pe = jnp.broadcast_to(pe, (2, 8, pe.shape[-1])).astype(BF16)
        w2 = jnp.stack([nsa_w2_k[l], nsa_w2_v[l]]).astype(BF16)
        cmp = nsa_compress(hm, w1, pe, w2, head0=n_qheads, batch=batch, seq=seq)
        o_b = nsa_attn(hm, cmp, gates, ov, e_sel, q_head0=0, kv_head0=n_qheads,
                       batch=batch, seq=seq, tq=tq_nsa, tk=tk_nsa)
        o_c = sb_attn(hm, u_sb, head0=sb0, batch=batch, seq=seq, tq=tq_sb)
        y = merge_branches(xf, o_a, o_b, o_c, wm, w_branch[l].astype(BF16),
                           b_merge[l].reshape(N_BRANCH, 1, D_MODEL), tm=tm_in, tn=512)
        xf = out_ln(y, w_out[l].astype(BF16), xf, row(ln_mix_g[l]), row(ln_mix_b[l]), alpha=alpha, tm=tm_ln)

        w_kv_mem = jnp.concatenate([mem_w_k[l], mem_w_v[l]], axis=1).astype(BF16)
        kv_mem = proj_heads(memf, w_kv_mem, ones_kv, tm=min(512, batch * mem_len), tn=512)
        xf = mem_attn_ln(xf, mem_w_q[l].astype(BF16), kv_mem, mem_w_o[l].astype(BF16),
                         row(ln_mem_g[l]), row(ln_mem_b[l]), alpha=alpha, seq=seq, mem_len=mem_len, tm=tm_ln)

        wr = moe_w_router[l]
        wr_hi = wr.astype(BF16)
        wr_lo = (wr - wr_hi.astype(F32)).astype(BF16)
        mask, wsel = router(xf, jnp.stack([wr_hi, wr_lo]), row(moe_b_router[l]), tm=tm_in)
        before = jnp.cumsum(mask, axis=0) - mask
        counts = jnp.sum(mask, axis=0).astype(jnp.int32)
        padded = (counts + bm - 1) // bm * bm
        pad_end = jnp.cumsum(padded)
        pad_start = pad_end - padded
        slot = pad_start[None, :] + before.astype(jnp.int32)
        top_e = lax.top_k(mask, TOP_K)[1]
        pos4 = jnp.take_along_axis(slot, top_e, axis=1).astype(jnp.int32)
        w4 = jnp.take_along_axis(wsel, top_e, axis=1)
        pos_flat = pos4.reshape(-1)
        blk_e = jnp.minimum(jnp.searchsorted(pad_end, jnp.arange(n_blocks) * bm, side='right'),
                            N_EXPERTS - 1).astype(jnp.int32)
        x_rows = scatter_rows(xf, pos_flat, n_rows, tt=tt_rows)
        y_rows = experts(x_rows, blk_e, moe_w_gate[l], moe_b_gate[l].reshape(N_EXPERTS, 1, D_EXPERT),
                         moe_w_up[l], moe_b_up[l].reshape(N_EXPERTS, 1, D_EXPERT),
                         moe_w_down[l], moe_b_down[l].reshape(N_EXPERTS, 1, D_MODEL), bm=bm)
        y4 = gather_rows(y_rows, pos_flat, t, tt=tt_rows).reshape(TOP_K, t, D_MODEL)
        xf = moe_ln(xf, y4, _pad_cols(w4, 128), row(ln_moe_g[l]), row(ln_moe_b[l]), alpha=alpha, tm=tm_ln)

    return xf.reshape(batch, seq, D_MODEL)


def kernel(x, mem, w_in, mla_q_norm, mla_w_q_up, mla_kv_norm, mla_w_kv_up, nsa_pe_k, nsa_pe_v, nsa_w1_k, nsa_w1_v, nsa_w2_k, nsa_w2_v, w_branch, b_merge, w_out, ln_mix_g, ln_mix_b, mem_w_q, mem_w_k, mem_w_v, mem_w_o, ln_mem_g, ln_mem_b, moe_w_router, moe_b_router, moe_w_gate, moe_b_gate, moe_w_up, moe_b_up, moe_w_down, moe_b_down, ln_moe_g, ln_moe_b):
    return _forward(x, mem, w_in, mla_q_norm, mla_w_q_up, mla_kv_norm, mla_w_kv_up,
                    nsa_pe_k, nsa_pe_v, nsa_w1_k, nsa_w1_v, nsa_w2_k, nsa_w2_v,
                    w_branch, b_merge, w_out, ln_mix_g, ln_mix_b,
                    mem_w_q, mem_w_k, mem_w_v, mem_w_o, ln_mem_g, ln_mem_b,
                    moe_w_router, moe_b_router, moe_w_gate, moe_b_gate, moe_w_up, moe_b_up,
                    moe_w_down, moe_b_down, ln_moe_g, ln_moe_b)
```

```python
import functools

import numpy as np
import jax
import jax.numpy as jnp
from jax import lax
from jax.experimental import pallas as pl
from jax.experimental.pallas import tpu as pltpu
from jax.experimental.pallas import tpu_sc as plsc

F32 = jnp.float32
BF16 = jnp.bfloat16

D_MODEL = 2048
HEAD_DIM = 128
MLA_HEADS = 8
MLA_Q_RANK = 512
MLA_KV_RANK = 256
MLA_NOPE = 128
MLA_ROPE = 64
ROPE_THETA = 10000.0
NSA_HEADS = 8
NSA_GROUPS = 2
NSA_HG = NSA_HEADS // NSA_GROUPS
NSA_CMP_LEN = 32
NSA_CMP_STRIDE = 16
NSA_SEL_LEN = 64
NSA_TOPK = 16
NSA_WINDOW = 512
SB_HEADS = 8
MEM_HEADS = 4
N_EXPERTS = 32
TOP_K = 4
D_EXPERT = 512
SWIGLU_LIMIT = 7.0
SWIGLU_ALPHA = 1.702
N_BRANCH = 3
BRANCH_WIDTH = 1024
LN_EPS = 1e-5
RMS_EPS = 1e-6
NEG = -1e30
BIG = 1e30

OFF_CQ = 0
OFF_CKV = 512
OFF_KR = 768
OFF_NSA_Q = 832
OFF_NSA_KV = 1856
OFF_NSA_GATE = 3392
OFF_SB = 3416
OFF_MERGE = 6488

VMEM_LIMIT_V7X = 56 * 1024 * 1024
MOE_BLOCK_ROWS = 256
SC_CORES_V7X = 2
SC_SUBCORES_V7X = 16


def _cp(sem, vmem=VMEM_LIMIT_V7X):
    return pltpu.CompilerParams(dimension_semantics=sem, vmem_limit_bytes=vmem)


def _dot(a, b):
    return jnp.dot(a, b, preferred_element_type=F32)


def _dot_nt(a, b):
    return lax.dot_general(a, b, (((1,), (1,)), ((), ())), preferred_element_type=F32)


def _layer_norm(z, g, b):
    mu = jnp.mean(z, axis=-1, keepdims=True)
    zc = z - mu
    var = jnp.mean(zc * zc, axis=-1, keepdims=True)
    return zc * lax.rsqrt(var + LN_EPS) * g + b


def _rms_norm(z, g):
    return z * lax.rsqrt(jnp.mean(z * z, axis=-1, keepdims=True) + RMS_EPS) * g


def _proj_heads_kernel(a_ref, w_ref, s_ref, o_ref, abf_ref, *, n_heads_per_tile):
    @pl.when(pl.program_id(1) == 0)
    def _():
        abf_ref[...] = a_ref[...].astype(BF16)

    acc = _dot(abf_ref[...], w_ref[...]) * s_ref[...]
    for c in range(n_heads_per_tile):
        o_ref[c] = acc[:, c * HEAD_DIM:(c + 1) * HEAD_DIM].astype(o_ref.dtype)


def proj_heads(a, w, scale, *, tm, tn):
    m, k = a.shape
    n = w.shape[1]
    hpt = tn // HEAD_DIM
    return pl.pallas_call(
        functools.partial(_proj_heads_kernel, n_heads_per_tile=hpt),
        out_shape=jax.ShapeDtypeStruct((n // HEAD_DIM, m, HEAD_DIM), BF16),
        grid=(m // tm, n // tn),
        in_specs=[
            pl.BlockSpec((tm, k), lambda i, j: (i, 0)),
            pl.BlockSpec((k, tn), lambda i, j: (0, j)),
            pl.BlockSpec((1, tn), lambda i, j: (0, j)),
        ],
        out_specs=pl.BlockSpec((hpt, tm, HEAD_DIM), lambda i, j: (j, i, 0)),
        scratch_shapes=[pltpu.VMEM((tm, k), BF16)],
        compiler_params=_cp(("arbitrary", "arbitrary")),
    )(a, w, scale)


def _mla_in_kernel(x_ref, w_ref, qg_ref, kg_ref, wq_ref, wkv_ref, cos_ref, sin_ref,
                   q_ref, k_ref, v_ref, g_ref):
    xb = x_ref[...].astype(BF16)
    h = _dot(xb, w_ref[...])
    cq = h[:, 0:512]
    ckv = h[:, 512:768]
    kr1 = h[:, 768:896]
    kr2 = h[:, 896:1024]
    g_ref[...] = jax.nn.sigmoid(h[:, 1024:1280])
    cos = cos_ref[...]
    sin = sin_ref[...]
    scale = (MLA_NOPE + MLA_ROPE) ** -0.5
    nq = _rms_norm(cq, qg_ref[...]).astype(BF16)
    q3 = _dot(nq, wq_ref[...])
    for hh in range(MLA_HEADS):
        lo, hi = hh * 128, (hh + 1) * 128
        q_ref[hh, :, 0:128] = (q3[:, lo:hi] * scale).astype(BF16)
        rot = q3[:, 1024 + lo:1024 + hi] * cos + q3[:, 2048 + lo:2048 + hi] * sin
        q_ref[hh, :, 128:256] = (rot * scale).astype(BF16)
    nkv = _rms_norm(ckv, kg_ref[...]).astype(BF16)
    kv = _dot(nkv, wkv_ref[...])
    krot = (kr1 * cos + kr2 * sin).astype(BF16)
    for hh in range(MLA_HEADS):
        lo, hi = hh * 128, (hh + 1) * 128
        k_ref[hh, :, 0:128] = kv[:, lo:hi].astype(BF16)
        k_ref[hh, :, 128:256] = krot
        v_ref[hh] = kv[:, 1024 + lo:1024 + hi].astype(BF16)


def mla_in(x, w_mla, qg, kg, wq3, wkv, cos128, sin128, *, seq, tm):
    t = x.shape[0]
    npos = seq // tm
    full = lambda shape: pl.BlockSpec(shape, lambda i: (0,) * len(shape))
    return pl.pallas_call(
        _mla_in_kernel,
        out_shape=(
            jax.ShapeDtypeStruct((MLA_HEADS, t, 256), BF16),
            jax.ShapeDtypeStruct((MLA_HEADS, t, 256), BF16),
            jax.ShapeDtypeStruct((MLA_HEADS, t, 128), BF16),
            jax.ShapeDtypeStruct((t, 256), F32),
        ),
        grid=(t // tm,),
        in_specs=[
            pl.BlockSpec((tm, D_MODEL), lambda i: (i, 0)),
            full((D_MODEL, 1280)),
            full((1, MLA_Q_RANK)),
            full((1, MLA_KV_RANK)),
            full((MLA_Q_RANK, 3072)),
            full((MLA_KV_RANK, 2048)),
            pl.BlockSpec((tm, 128), lambda i: (i % npos, 0)),
            pl.BlockSpec((tm, 128), lambda i: (i % npos, 0)),
        ],
        out_specs=(
            pl.BlockSpec((MLA_HEADS, tm, 256), lambda i: (0, i, 0)),
            pl.BlockSpec((MLA_HEADS, tm, 256), lambda i: (0, i, 0)),
            pl.BlockSpec((MLA_HEADS, tm, 128), lambda i: (0, i, 0)),
            pl.BlockSpec((tm, 256), lambda i: (i, 0)),
        ),
        compiler_params=_cp(("arbitrary",)),
    )(x, w_mla, qg, kg, wq3, wkv, cos128, sin128)


def _mla_attn_kernel(q_ref, k_ref, v_ref, o_ref, *, tq):
    qi = pl.program_id(2)
    q = q_ref[...]

    def step(kt, carry, diag):
        m, l, acc = carry
        k0 = pl.multiple_of(kt * tq, tq)
        k = k_ref[pl.ds(k0, tq), :]
        v = v_ref[pl.ds(k0, tq), :]
        s = _dot_nt(q, k)
        if diag:
            row = lax.broadcasted_iota(jnp.int32, (tq, tq), 0)
            col = lax.broadcasted_iota(jnp.int32, (tq, tq), 1)
            s = jnp.where(col <= row, s, NEG)
        m_new = jnp.maximum(m, jnp.max(s, axis=1, keepdims=True))
        alpha = jnp.exp(m - m_new)
        p = jnp.exp(s - m_new)
        l = alpha * l + jnp.sum(p, axis=1, keepdims=True)
        acc = alpha * acc + _dot(p.astype(BF16), v)
        return m_new, l, acc

    init = (jnp.full((tq, 1), NEG, F32), jnp.zeros((tq, 1), F32), jnp.zeros((tq, 128), F32))
    carry = lax.fori_loop(0, qi, lambda kt, c: step(kt, c, False), init)
    m, l, acc = step(qi, carry, True)
    o_ref[...] = (acc / l).astype(o_ref.dtype)


def mla_attn(q, k, v, *, batch, seq, tq):
    nq = seq // tq
    q4 = q.reshape(MLA_HEADS, batch, seq, 256)
    k4 = k.reshape(MLA_HEADS, batch, seq, 256)
    v4 = v.reshape(MLA_HEADS, batch, seq, 128)
    return pl.pallas_call(
        functools.partial(_mla_attn_kernel, tq=tq),
        out_shape=jax.ShapeDtypeStruct((batch * seq, MLA_HEADS * 128), BF16),
        grid=(MLA_HEADS, batch, nq),
        in_specs=[
            pl.BlockSpec((None, None, tq, 256), lambda h, b, i: (h, b, i, 0)),
            pl.BlockSpec((None, None, seq, 256), lambda h, b, i: (h, b, 0, 0)),
            pl.BlockSpec((None, None, seq, 128), lambda h, b, i: (h, b, 0, 0)),
        ],
        out_specs=pl.BlockSpec((tq, 128), lambda h, b, i: (b * nq + i, h)),
        compiler_params=_cp(("arbitrary", "arbitrary", "arbitrary")),
    )(q4, k4, v4)


def _sb_attn_kernel(q_ref, k_ref, v_ref, u_ref, o_ref, *, tq):
    qi = pl.program_id(2)
    q = q_ref[...]
    u = u_ref[...]

    def step(kt, carry, diag):
        run, acc = carry
        k0 = pl.multiple_of(kt * tq, tq)
        k = k_ref[pl.ds(k0, tq), :]
        v = v_ref[pl.ds(k0, tq), :]
        z = _dot_nt(q, k)
        l1m = -(jnp.maximum(z, 0.0) + jnp.log(1.0 + jnp.exp(-jnp.abs(z))))
        if diag:
            row = lax.broadcasted_iota(jnp.int32, (tq, tq), 0)
            col = lax.broadcasted_iota(jnp.int32, (tq, tq), 1)
            strict = col < row
            l1m_m = jnp.where(strict, l1m, 0.0)
        else:
            l1m_m = l1m
        hi = l1m_m.astype(BF16)
        lo = (l1m_m - hi.astype(F32)).astype(BF16)
        between = _dot(hi, u) + _dot(lo, u)
        a = jnp.exp(z + l1m + between + run)
        if diag:
            a = jnp.where(strict, a, 0.0)
        acc = acc + _dot(a.astype(BF16), v)
        run = run + between[:, 0:1] + l1m_m[:, 0:1]
        return run, acc

    init = (jnp.zeros((tq, 1), F32), jnp.zeros((tq, 128), F32))
    carry = step(qi, init, True)
    run, acc = lax.fori_loop(0, qi, lambda j, c: step(qi - 1 - j, c, False), carry)
    o_ref[...] = acc.astype(o_ref.dtype)


def sb_attn(hm, u, *, head0, batch, seq, tq):
    nq = seq // tq
    hm4 = hm.reshape(hm.shape[0], batch, seq, HEAD_DIM)
    return pl.pallas_call(
        functools.partial(_sb_attn_kernel, tq=tq),
        out_shape=jax.ShapeDtypeStruct((batch * seq, SB_HEADS * HEAD_DIM), BF16),
        grid=(SB_HEADS, batch, nq),
        in_specs=[
            pl.BlockSpec((None, None, tq, HEAD_DIM), lambda h, b, i: (head0 + h, b, i, 0)),
            pl.BlockSpec((None, None, seq, HEAD_DIM), lambda h, b, i: (head0 + SB_HEADS + h, b, 0, 0)),
            pl.BlockSpec((None, None, seq, HEAD_DIM), lambda h, b, i: (head0 + 2 * SB_HEADS + h, b, 0, 0)),
            pl.BlockSpec((tq, tq), lambda h, b, i: (0, 0)),
        ],
        out_specs=pl.BlockSpec((tq, HEAD_DIM), lambda h, b, i: (b * nq + i, h)),
        compiler_params=_cp(("arbitrary", "arbitrary", "arbitrary")),
    )(hm4, hm4, hm4, u)


def _nsa_cmp_kernel(c_ref, w1_ref, pe_ref, w2_ref, o_ref, *, nc):
    c = c_ref[...]
    half = NSA_CMP_STRIDE * HEAD_DIM
    a1 = _dot(c, w1_ref[0:half, :])
    a2 = _dot(c, w1_ref[half:2 * half, :])
    pc = _dot(pe_ref[...], w1_ref[...])[0:1, :]
    pre = a1 + pltpu.roll(a2, nc - 1, 0) + pc
    act = 0.5 * pre * (1.0 + jnp.tanh(0.7978845608028654 * (pre + 0.044715 * (pre * pre * pre))))
    o_ref[...] = _dot(act.astype(BF16), w2_ref[...]).astype(BF16)


def nsa_compress(hm, w1, pe, w2, *, head0, batch, seq):
    nc = seq // NSA_CMP_STRIDE
    hm4 = hm.reshape(hm.shape[0], batch, nc, NSA_CMP_STRIDE * HEAD_DIM)
    return pl.pallas_call(
        functools.partial(_nsa_cmp_kernel, nc=nc),
        out_shape=jax.ShapeDtypeStruct((4, batch, nc, HEAD_DIM), BF16),
        grid=(4, batch),
        in_specs=[
            pl.BlockSpec((None, None, nc, NSA_CMP_STRIDE * HEAD_DIM), lambda c, b: (head0 + c, b, 0, 0)),
            pl.BlockSpec((None, NSA_CMP_LEN * HEAD_DIM, HEAD_DIM), lambda c, b: (c // 2, 0, 0)),
            pl.BlockSpec((None, 8, NSA_CMP_LEN * HEAD_DIM), lambda c, b: (c // 2, 0, 0)),
            pl.BlockSpec((None, HEAD_DIM, HEAD_DIM), lambda c, b: (c // 2, 0, 0)),
        ],
        out_specs=pl.BlockSpec((None, None, nc, HEAD_DIM), lambda c, b: (c, b, 0, 0)),
        compiler_params=_cp(("arbitrary", "arbitrary")),
    )(hm4, w1, pe, w2)


def _nsa_attn_kernel(q_ref, kc_ref, vc_ref, ks_ref, vs_ref, kw_ref, vw_ref, g_ref, ov_ref, e_ref,
                     o_ref, *, tq, tk, seq, n_sel, n_top):
    grp = pl.program_id(1)
    qi = pl.program_id(2)
    t0 = qi * tq
    rows = NSA_HG * tq
    nc = seq // NSA_CMP_STRIDE
    q = q_ref[...].reshape(rows, HEAD_DIM)

    rid = lax.broadcasted_iota(jnp.int32, (rows, 1), 0)
    hrow = lax.shift_right_logical(rid, int(np.log2(tq)))
    trow = t0 + lax.bitwise_and(rid, tq - 1)
    tf = trow.astype(F32)
    slope = jnp.exp2(-(grp * NSA_HG + hrow + 1).astype(F32))

    def masked_softmax(s, valid):
        sm = jnp.where(valid, s, NEG)
        m = jnp.max(sm, axis=1, keepdims=True)
        e = jnp.where(valid, jnp.exp(sm - m), 0.0)
        d = jnp.sum(e, axis=1, keepdims=True)
        return e * (1.0 / jnp.where(d > 0.0, d, 1.0))

    n_i = lax.broadcasted_iota(jnp.int32, (1, nc), 1)
    mid = n_i.astype(F32) * float(NSA_CMP_STRIDE) + (NSA_CMP_LEN - 1) / 2.0
    end = n_i * NSA_CMP_STRIDE + (NSA_CMP_LEN - 1)
    s_c = _dot_nt(q, kc_ref[...]) - slope * (tf - mid)
    p_c = masked_softmax(s_c, end <= trow)
    o_c = _dot(p_c.astype(BF16), vc_ref[...])

    psum = p_c[0:tq] + p_c[tq:2 * tq] + p_c[2 * tq:3 * tq] + p_c[3 * tq:4 * tq]
    p_hi = psum.astype(BF16)
    p_lo = (psum - p_hi.astype(F32)).astype(BF16)
    ov = ov_ref[...]
    imp = _dot(p_hi, ov) + _dot(p_lo, ov)
    cur = lax.shift_right_logical(t0 + lax.broadcasted_iota(jnp.int32, (tq, 1), 0),
                                  int(np.log2(NSA_SEL_LEN)))
    blk = lax.broadcasted_iota(jnp.int32, (tq, n_sel), 1)
    forced = (blk == 0) | (blk == cur) | (blk == cur - 1)
    key = jnp.where(blk > cur, -BIG, jnp.where(forced, BIG, imp))
    rank = jnp.zeros((tq, n_sel), F32)
    for i in range(n_sel):
        vi = key[:, i:i + 1]
        beats = (vi > key) | ((vi == key) & (blk > i))
        rank = rank + jnp.where(beats, 1.0, 0.0)
    selm = jnp.where((rank < float(n_top)) & (blk <= cur), 1.0, 0.0).astype(BF16)

    def sel_step(kt, carry, diag):
        m, l, acc = carry
        k0 = pl.multiple_of(kt * tk, tk)
        kk = ks_ref[pl.ds(k0, tk), :]
        vv = vs_ref[pl.ds(k0, tk), :]
        mex = _dot(selm, e_ref[kt])
        mex4 = jnp.concatenate([mex] * NSA_HG, axis=0)
        spos = k0 + lax.broadcasted_iota(jnp.int32, (1, tk), 1)
        s = _dot_nt(q, kk) - slope * (tf - spos.astype(F32))
        valid = mex4 > 0.5
        if diag:
            valid = valid & (spos <= trow)
        sm = jnp.where(valid, s, NEG)
        m_new = jnp.maximum(m, jnp.max(sm, axis=1, keepdims=True))
        alpha = jnp.exp(m - m_new)
        p = jnp.where(valid, jnp.exp(sm - m_new), 0.0)
        l = alpha * l + jnp.sum(p, axis=1, keepdims=True)
        acc = alpha * acc + _dot(p.astype(BF16), vv)
        return m_new, l, acc

    kt_last = t0 // tk
    init = (jnp.full((rows, 1), NEG, F32), jnp.zeros((rows, 1), F32), jnp.zeros((rows, HEAD_DIM), F32))
    carry = lax.fori_loop(0, kt_last, lambda kt, c: sel_step(kt, c, False), init)
    _, l_s, acc_s = sel_step(kt_last, carry, True)
    o_s = acc_s * (1.0 / l_s)

    wk = NSA_WINDOW + tq
    ks0 = pl.multiple_of(jnp.maximum(t0 - NSA_WINDOW, 0), tq)
    kw = kw_ref[pl.ds(ks0, wk), :]
    vw = vw_ref[pl.ds(ks0, wk), :]
    wpos = ks0 + lax.broadcasted_iota(jnp.int32, (1, wk), 1)
    dw = trow - wpos
    s_w = _dot_nt(q, kw) - slope * dw.astype(F32)
    p_w = masked_softmax(s_w, (dw >= 0) & (dw < NSA_WINDOW))
    o_w = _dot(p_w.astype(BF16), vw)

    gt = g_ref[...]
    for hg in range(NSA_HG):
        sl = slice(hg * tq, (hg + 1) * tq)
        o = (gt[:, 3 * hg:3 * hg + 1] * o_c[sl] + gt[:, 3 * hg + 1:3 * hg + 2] * o_s[sl]
             + gt[:, 3 * hg + 2:3 * hg + 3] * o_w[sl])
        o_ref[:, hg * HEAD_DIM:(hg + 1) * HEAD_DIM] = o.astype(o_ref.dtype)


def nsa_attn(hm, cmp, gates, ov, e, *, q_head0, kv_head0, batch, seq, tq, tk):
    nq = seq // tq
    nc = seq // NSA_CMP_STRIDE
    n_sel = seq // NSA_SEL_LEN
    n_top = min(NSA_TOPK, n_sel)
    assert tk % tq == 0 and seq % tk == 0 and seq >= NSA_WINDOW + tq and NSA_WINDOW % tq == 0
    hm4 = hm.reshape(hm.shape[0], batch, seq, HEAD_DIM)
    kv_spec = lambda off: pl.BlockSpec((None, None, seq, HEAD_DIM),
                                       lambda b, g, i: (kv_head0 + off + g, b, 0, 0))
    return pl.pallas_call(
        functools.partial(_nsa_attn_kernel, tq=tq, tk=tk, seq=seq, n_sel=n_sel, n_top=n_top),
        out_shape=jax.ShapeDtypeStruct((batch * seq, NSA_HEADS * HEAD_DIM), BF16),
        grid=(batch, NSA_GROUPS, nq),
        in_specs=[
            pl.BlockSpec((NSA_HG, None, tq, HEAD_DIM), lambda b, g, i: (q_head0 // NSA_HG + g, b, i, 0)),
            pl.BlockSpec((None, None, nc, HEAD_DIM), lambda b, g, i: (g, b, 0, 0)),
            pl.BlockSpec((None, None, nc, HEAD_DIM), lambda b, g, i: (2 + g, b, 0, 0)),
            kv_spec(4), kv_spec(6), kv_spec(8), kv_spec(10),
            pl.BlockSpec((tq, 128), lambda b, g, i: (b * nq + i, g)),
            pl.BlockSpec((nc, n_sel), lambda b, g, i: (0, 0)),
            pl.BlockSpec((seq // tk, n_sel, tk), lambda b, g, i: (0, 0, 0)),
        ],
        out_specs=pl.BlockSpec((tq, NSA_HG * HEAD_DIM), lambda b, g, i: (b * nq + i, g)),
        compiler_params=_cp(("arbitrary", "arbitrary", "arbitrary")),
    )(hm4, cmp, cmp, hm4, hm4, hm4, hm4, gates, ov, e)


def _merge_kernel(x_ref, oa_ref, ob_ref, oc_ref, wm_ref, wb_ref, bm_ref, y_ref, xb_ref):
    @pl.when(pl.program_id(1) == 0)
    def _():
        xb_ref[...] = x_ref[...].astype(BF16)

    xb = xb_ref[...]
    acc = None
    for br, o_ref in enumerate((oa_ref, ob_ref, oc_ref)):
        gate = jax.nn.sigmoid(_dot(xb, wm_ref[br]) + bm_ref[br])
        term = gate * _dot(o_ref[...], wb_ref[br])
        acc = term if acc is None else acc + term
    y_ref[...] = acc.astype(y_ref.dtype)


def merge_branches(x, o_a, o_b, o_c, wm, wb, bm, *, tm, tn):
    t = x.shape[0]
    o_spec = pl.BlockSpec((tm, BRANCH_WIDTH), lambda i, j: (i, 0))
    return pl.pallas_call(
        _merge_kernel,
        out_shape=jax.ShapeDtypeStruct((t, D_MODEL), BF16),
        grid=(t // tm, D_MODEL // tn),
        in_specs=[
            pl.BlockSpec((tm, D_MODEL), lambda i, j: (i, 0)),
            o_spec, o_spec, o_spec,
            pl.BlockSpec((N_BRANCH, D_MODEL, tn), lambda i, j: (0, 0, j)),
            pl.BlockSpec((N_BRANCH, BRANCH_WIDTH, tn), lambda i, j: (0, 0, j)),
            pl.BlockSpec((N_BRANCH, 1, tn), lambda i, j: (0, 0, j)),
        ],
        out_specs=pl.BlockSpec((tm, tn), lambda i, j: (i, j)),
        scratch_shapes=[pltpu.VMEM((tm, D_MODEL), BF16)],
        compiler_params=_cp(("arbitrary", "arbitrary")),
    )(x, o_a, o_b, o_c, wm, wb, bm)


def _out_ln_kernel(y_ref, w_ref, x_ref, g_ref, b_ref, o_ref, *, alpha):
    h = _dot(y_ref[...], w_ref[...])
    o_ref[...] = _layer_norm(alpha * x_ref[...] + h, g_ref[...], b_ref[...])


def out_ln(y, w, x, g, b, *, alpha, tm):
    t = x.shape[0]
    return pl.pallas_call(
        functools.partial(_out_ln_kernel, alpha=alpha),
        out_shape=jax.ShapeDtypeStruct((t, D_MODEL), F32),
        grid=(t // tm,),
        in_specs=[
            pl.BlockSpec((tm, D_MODEL), lambda i: (i, 0)),
            pl.BlockSpec((D_MODEL, D_MODEL), lambda i: (0, 0)),
            pl.BlockSpec((tm, D_MODEL), lambda i: (i, 0)),
            pl.BlockSpec((1, D_MODEL), lambda i: (0, 0)),
            pl.BlockSpec((1, D_MODEL), lambda i: (0, 0)),
        ],
        out_specs=pl.BlockSpec((tm, D_MODEL), lambda i: (i, 0)),
        compiler_params=_cp(("arbitrary",)),
    )(y, w, x, g, b)


def _mem_attn_kernel(x_ref, wq_ref, k_ref, v_ref, wo_ref, g_ref, b_ref, o_ref, *, alpha):
    x = x_ref[...]
    q = _dot(x.astype(BF16), wq_ref[...]) * (HEAD_DIM ** -0.5)
    outs = []
    for h in range(MEM_HEADS):
        qh = q[:, h * HEAD_DIM:(h + 1) * HEAD_DIM].astype(BF16)
        s = _dot_nt(qh, k_ref[h])
        m = jnp.max(s, axis=1, keepdims=True)
        e = jnp.exp(s - m)
        p = e * (1.0 / jnp.sum(e, axis=1, keepdims=True))
        outs.append(_dot(p.astype(BF16), v_ref[h]).astype(BF16))
    o = jnp.concatenate(outs, axis=1)
    h_out = _dot(o, wo_ref[...])
    o_ref[...] = _layer_norm(alpha * x + h_out, g_ref[...], b_ref[...])


def mem_attn_ln(x, wq, kv, wo, g, b, *, alpha, seq, mem_len, tm):
    t = x.shape[0]
    per_b = seq // tm
    kv4 = kv.reshape(2 * MEM_HEADS, t // seq, mem_len, HEAD_DIM)
    width = MEM_HEADS * HEAD_DIM
    return pl.pallas_call(
        functools.partial(_mem_attn_kernel, alpha=alpha),
        out_shape=jax.ShapeDtypeStruct((t, D_MODEL), F32),
        grid=(t // tm,),
        in_specs=[
            pl.BlockSpec((tm, D_MODEL), lambda i: (i, 0)),
            pl.BlockSpec((D_MODEL, width), lambda i: (0, 0)),
            pl.BlockSpec((MEM_HEADS, None, mem_len, HEAD_DIM), lambda i: (0, i // per_b, 0, 0)),
            pl.BlockSpec((MEM_HEADS, None, mem_len, HEAD_DIM), lambda i: (1, i // per_b, 0, 0)),
            pl.BlockSpec((width, D_MODEL), lambda i: (0, 0)),
            pl.BlockSpec((1, D_MODEL), lambda i: (0, 0)),
            pl.BlockSpec((1, D_MODEL), lambda i: (0, 0)),
        ],
        out_specs=pl.BlockSpec((tm, D_MODEL), lambda i: (i, 0)),
        compiler_params=_cp(("arbitrary",)),
    )(x, wq, kv4, kv4, wo, g, b)


def _router_kernel(x_ref, w_ref, b_ref, mask_ref, wsel_ref):
    x = x_ref[...]
    xh = x.astype(BF16)
    xl = (x - xh.astype(F32)).astype(BF16)
    logits = _dot(xh, w_ref[0]) + _dot(xh, w_ref[1]) + _dot(xl, w_ref[0]) + b_ref[...]
    tm = logits.shape[0]
    lane = lax.broadcasted_iota(jnp.int32, (tm, N_EXPERTS), 1)
    work = logits
    hots, vals = [], []
    for _ in range(TOP_K):
        m = jnp.max(work, axis=1, keepdims=True)
        idx = jnp.min(jnp.where(work == m, lane, N_EXPERTS), axis=1, keepdims=True)
        hot = lane == idx
        hots.append(hot)
        vals.append(m)
        work = jnp.where(hot, -jnp.inf, work)
    es = [jnp.exp(v - vals[0]) for v in vals]
    inv = 1.0 / (es[0] + es[1] + es[2] + es[3])
    mask = jnp.zeros((tm, N_EXPERTS), F32)
    wsel = jnp.zeros((tm, N_EXPERTS), F32)
    for hot, e in zip(hots, es):
        mask = mask + jnp.where(hot, 1.0, 0.0)
        wsel = wsel + jnp.where(hot, e * inv, 0.0)
    mask_ref[...] = mask
    wsel_ref[...] = wsel


def router(x, w_hl, b, *, tm):
    t = x.shape[0]
    return pl.pallas_call(
        _router_kernel,
        out_shape=(jax.ShapeDtypeStruct((t, N_EXPERTS), F32), jax.ShapeDtypeStruct((t, N_EXPERTS), F32)),
        grid=(t // tm,),
        in_specs=[
            pl.BlockSpec((tm, D_MODEL), lambda i: (i, 0)),
            pl.BlockSpec((2, D_MODEL, N_EXPERTS), lambda i: (0, 0, 0)),
            pl.BlockSpec((1, N_EXPERTS), lambda i: (0, 0)),
        ],
        out_specs=(pl.BlockSpec((tm, N_EXPERTS), lambda i: (i, 0)),
                   pl.BlockSpec((tm, N_EXPERTS), lambda i: (i, 0))),
        compiler_params=_cp(("arbitrary",)),
    )(x, w_hl, b)


def sc_gather_rows(table, idx, *, chunk):
    n = idx.shape[0]
    d = table.shape[1]
    workers = SC_CORES_V7X * SC_SUBCORES_V7X
    per_w = n // workers
    assert n % (workers * chunk) == 0 and chunk % 8 == 0 and chunk <= 128
    mesh = plsc.VectorSubcoreMesh(core_axis_name="c", subcore_axis_name="s")

    @functools.partial(
        pl.kernel, mesh=mesh,
        out_type=jax.ShapeDtypeStruct((n, d), table.dtype),
        scratch_types=[pltpu.VMEM((chunk,), jnp.int32), pltpu.VMEM((chunk, d), table.dtype),
                       pltpu.SemaphoreType.DMA],
    )
    def gather(table_hbm, idx_hbm, out_hbm, idx_v, rows_v, sem):
        wid = lax.axis_index("s") * SC_CORES_V7X + lax.axis_index("c")
        base = wid * per_w

        @pl.loop(0, per_w // chunk)
        def _(j):
            off = pl.multiple_of(base + j * chunk, 8)
            pltpu.sync_copy(idx_hbm.at[pl.ds(off, chunk)], idx_v)
            pltpu.async_copy(table_hbm.at[idx_v], rows_v, sem).wait()
            pltpu.sync_copy(rows_v, out_hbm.at[pl.ds(off, chunk)])

    return gather(table, idx)


def _experts_kernel(be_ref, x_ref, wg_ref, bg_ref, wu_ref, bu_ref, wd_ref, bd_ref, y_ref,
                    wgb_ref, wub_ref, wdb_ref):
    i = pl.program_id(0)
    prev = be_ref[jnp.maximum(i - 1, 0)]

    @pl.when((i == 0) | (be_ref[i] != prev))
    def _():
        wgb_ref[...] = wg_ref[...].astype(BF16)
        wub_ref[...] = wu_ref[...].astype(BF16)
        wdb_ref[...] = wd_ref[...].astype(BF16)

    xb = x_ref[...].astype(BF16)
    g = jnp.minimum(_dot(xb, wgb_ref[...]) + bg_ref[...], SWIGLU_LIMIT)
    u = jnp.clip(_dot(xb, wub_ref[...]) + bu_ref[...], -SWIGLU_LIMIT, SWIGLU_LIMIT)
    hdn = (u + 1.0) * (g * jax.nn.sigmoid(SWIGLU_ALPHA * g))
    y_ref[...] = _dot(hdn.astype(BF16), wdb_ref[...]) + bd_ref[...]


def experts(x_rows, blk_e, wg, bg, wu, bu, wd, bd, *, bm):
    n_rows, d = x_rows.shape
    f = wg.shape[2]
    grid_spec = pltpu.PrefetchScalarGridSpec(
        num_scalar_prefetch=1,
        grid=(n_rows // bm,),
        in_specs=[
            pl.BlockSpec((bm, d), lambda i, be: (i, 0)),
            pl.BlockSpec((None, d, f), lambda i, be: (be[i], 0, 0)),
            pl.BlockSpec((None, 1, f), lambda i, be: (be[i], 0, 0)),
            pl.BlockSpec((None, d, f), lambda i, be: (be[i], 0, 0)),
            pl.BlockSpec((None, 1, f), lambda i, be: (be[i], 0, 0)),
            pl.BlockSpec((None, f, d), lambda i, be: (be[i], 0, 0)),
            pl.BlockSpec((None, 1, d), lambda i, be: (be[i], 0, 0)),
        ],
        out_specs=pl.BlockSpec((bm, d), lambda i, be: (i, 0)),
        scratch_shapes=[pltpu.VMEM((d, f), BF16), pltpu.VMEM((d, f), BF16), pltpu.VMEM((f, d), BF16)],
    )
    return pl.pallas_call(
        _experts_kernel,
        out_shape=jax.ShapeDtypeStruct((n_rows, d), F32),
        grid_spec=grid_spec,
        compiler_params=_cp(("arbitrary",)),
    )(blk_e, x_rows, wg, bg, wu, bu, wd, bd)


def _moe_ln_kernel(x_ref, y_ref, w_ref, g_ref, b_ref, o_ref, *, alpha):
    w = w_ref[...]
    y = w[:, 0:1] * y_ref[0]
    for k in range(1, TOP_K):
        y = y + w[:, k:k + 1] * y_ref[k]
    o_ref[...] = _layer_norm(alpha * x_ref[...] + y, g_ref[...], b_ref[...])


def moe_ln(x, y4, w4p, g, b, *, alpha, tm):
    t = x.shape[0]
    return pl.pallas_call(
        functools.partial(_moe_ln_kernel, alpha=alpha),
        out_shape=jax.ShapeDtypeStruct((t, D_MODEL), F32),
        grid=(t // tm,),
        in_specs=[
            pl.BlockSpec((tm, D_MODEL), lambda i: (i, 0)),
            pl.BlockSpec((TOP_K, tm, D_MODEL), lambda i: (0, i, 0)),
            pl.BlockSpec((tm, 128), lambda i: (i, 0)),
            pl.BlockSpec((1, D_MODEL), lambda i: (0, 0)),
            pl.BlockSpec((1, D_MODEL), lambda i: (0, 0)),
        ],
        out_specs=pl.BlockSpec((tm, D_MODEL), lambda i: (i, 0)),
        compiler_params=_cp(("arbitrary",)),
    )(x, y4, w4p, g, b)


def _rope_tables(seq):
    inv = np.asarray(ROPE_THETA ** (-np.arange(0, MLA_ROPE, 2) / MLA_ROPE), np.float32)
    ang = jnp.arange(seq, dtype=F32)[:, None] * jnp.asarray(inv)[None, :]
    cos, sin = jnp.cos(ang), jnp.sin(ang)
    zeros = jnp.zeros((seq, 128 - MLA_ROPE), F32)
    return (jnp.concatenate([cos, cos, zeros], axis=1), jnp.concatenate([-sin, sin, zeros], axis=1))


def _nsa_constants(seq, tk):
    nc = seq // NSA_CMP_STRIDE
    n_cmp = (seq - NSA_CMP_LEN) // NSA_CMP_STRIDE + 1
    n_sel = seq // NSA_SEL_LEN
    cs = np.arange(nc) * NSA_CMP_STRIDE
    ss = np.arange(n_sel) * NSA_SEL_LEN
    ov = np.clip(np.minimum(cs[:, None] + NSA_CMP_LEN, ss[None, :] + NSA_SEL_LEN)
                 - np.maximum(cs[:, None], ss[None, :]), 0, None) / NSA_CMP_LEN
    ov[n_cmp:] = 0.0
    e = (np.arange(seq)[None, :] // NSA_SEL_LEN == np.arange(n_sel)[:, None]).astype(np.float32)
    e = e.reshape(n_sel, seq // tk, tk).transpose(1, 0, 2)
    return jnp.asarray(ov, BF16), jnp.asarray(e, BF16)


def _pad_cols(w, width):
    return jnp.pad(w, ((0, 0), (0, width - w.shape[1])))


def _swap_halves(w):
    half = w.shape[1] // 2
    return jnp.concatenate([w[:, half:], w[:, :half]], axis=1)


def _layer_weights(w_in, w_q_up, w_kv_up):
    kr = w_in[:, OFF_KR:OFF_NSA_Q]
    gate = w_in[:, OFF_NSA_GATE:OFF_SB]
    per_g = NSA_HG * 3
    w_mla = jnp.concatenate([
        w_in[:, OFF_CQ:OFF_KR],
        _pad_cols(kr, 128), _pad_cols(_swap_halves(kr), 128),
        _pad_cols(gate[:, :per_g], 128), _pad_cols(gate[:, per_g:], 128)], axis=1).astype(BF16)
    wq = w_q_up.reshape(MLA_Q_RANK, MLA_HEADS, MLA_NOPE + MLA_ROPE)
    rope = wq[:, :, MLA_NOPE:]
    rope_sw = jnp.concatenate([rope[:, :, MLA_ROPE // 2:], rope[:, :, :MLA_ROPE // 2]], axis=2)
    pad = ((0, 0), (0, 0), (0, 128 - MLA_ROPE))
    wq3 = jnp.concatenate([
        wq[:, :, :MLA_NOPE].reshape(MLA_Q_RANK, -1),
        jnp.pad(rope, pad).reshape(MLA_Q_RANK, -1),
        jnp.pad(rope_sw, pad).reshape(MLA_Q_RANK, -1)], axis=1).astype(BF16)
    wkv = w_kv_up.reshape(MLA_KV_RANK, MLA_HEADS, 2, 128)
    wkv = jnp.concatenate([wkv[:, :, 0].reshape(MLA_KV_RANK, -1),
                           wkv[:, :, 1].reshape(MLA_KV_RANK, -1)], axis=1).astype(BF16)
    w_heads = w_in[:, OFF_NSA_Q:OFF_NSA_GATE]
    w_heads = jnp.concatenate([w_heads, w_in[:, OFF_SB:OFF_MERGE]], axis=1).astype(BF16)
    wm = w_in[:, OFF_MERGE:].reshape(D_MODEL, N_BRANCH, D_MODEL).transpose(1, 0, 2).astype(BF16)
    return w_mla, wq3, wkv, w_heads, wm


def _forward(x, mem, w_in, mla_q_norm, mla_w_q_up, mla_kv_norm, mla_w_kv_up,
             nsa_pe_k, nsa_pe_v, nsa_w1_k, nsa_w1_v, nsa_w2_k, nsa_w2_v,
             w_branch, b_merge, w_out, ln_mix_g, ln_mix_b,
             mem_w_q, mem_w_k, mem_w_v, mem_w_o, ln_mem_g, ln_mem_b,
             moe_w_router, moe_b_router, moe_w_gate, moe_b_gate, moe_w_up, moe_b_up,
             moe_w_down, moe_b_down, ln_moe_g, ln_moe_b):
    batch, seq, _ = x.shape
    mem_len = mem.shape[1]
    depth = w_in.shape[0]
    t = batch * seq
    alpha = float((2 * depth) ** 0.25)
    bm = MOE_BLOCK_ROWS
    n_rows = t * TOP_K + N_EXPERTS * bm
    n_blocks = n_rows // bm

    tm_in = min(512, seq)
    tq_mla = min(512, seq)
    tq_sb = 256
    tq_nsa, tk_nsa = 128, 512
    tm_ln = 256
    sc_chunk = 32

    cos128, sin128 = _rope_tables(seq)
    ov, e_sel = _nsa_constants(seq, tk_nsa)
    u_sb = jnp.asarray(np.arange(tq_sb)[:, None] > np.arange(tq_sb)[None, :], BF16)
    n_qheads = NSA_HEADS
    n_kvheads = 3 * 2 * NSA_GROUPS
    head_scale = np.ones((1, (n_qheads + n_kvheads + 3 * SB_HEADS) * HEAD_DIM), np.float32)
    head_scale[:, :n_qheads * HEAD_DIM] = HEAD_DIM ** -0.5
    sb0 = n_qheads + n_kvheads
    head_scale[:, sb0 * HEAD_DIM:(sb0 + SB_HEADS) * HEAD_DIM] = HEAD_DIM ** -0.5
    head_scale = jnp.asarray(head_scale)
    ones_kv = jnp.ones((1, 2 * MEM_HEADS * HEAD_DIM), F32)

    xf = x.reshape(t, D_MODEL)
    memf = mem.reshape(batch * mem_len, D_MODEL)
    row = lambda v: v.reshape(1, -1)

    for l in range(depth):
        w_mla, wq3, wkv, w_heads, wm = _layer_weights(w_in[l], mla_w_q_up[l], mla_w_kv_up[l])

        q_a, k_a, v_a, gates = mla_in(xf, w_mla, row(mla_q_norm[l]), row(mla_kv_norm[l]), wq3, wkv,
                                      cos128, sin128, seq=seq, tm=tm_ln)
        hm = proj_heads(xf, w_heads, head_scale, tm=tm_in, tn=512)
        o_a = mla_attn(q_a, k_a, v_a, batch=batch, seq=seq, tq=tq_mla)
        w1 = jnp.stack([nsa_w1_k[l], nsa_w1_v[l]]).astype(BF16)
        pe = jnp.stack([nsa_pe_k[l], nsa_pe_v[l]]).reshape(2, 1, -1)
        pe = jnp.broadcast_to(pe, (2, 8, pe.shape[-1])).astype(BF16)
        w2 = jnp.stack([nsa_w2_k[l], nsa_w2_v[l]]).astype(BF16)
        cmp = nsa_compress(hm, w1, pe, w2, head0=n_qheads, batch=batch, seq=seq)
        o_b = nsa_attn(hm, cmp, gates, ov, e_sel, q_head0=0, kv_head0=n_qheads,
                       batch=batch, seq=seq, tq=tq_nsa, tk=tk_nsa)
        o_c = sb_attn(hm, u_sb, head0=sb0, batch=batch, seq=seq, tq=tq_sb)
        y = merge_branches(xf, o_a, o_b, o_c, wm, w_branch[l].astype(BF16),
                           b_merge[l].reshape(N_BRANCH, 1, D_MODEL), tm=tm_in, tn=512)
        xf = out_ln(y, w_out[l].astype(BF16), xf, row(ln_mix_g[l]), row(ln_mix_b[l]), alpha=alpha, tm=tm_ln)

        w_kv_mem = jnp.concatenate([mem_w_k[l], mem_w_v[l]], axis=1).astype(BF16)
        kv_mem = proj_heads(memf, w_kv_mem, ones_kv, tm=min(512, batch * mem_len), tn=512)
        xf = mem_attn_ln(xf, mem_w_q[l].astype(BF16), kv_mem, mem_w_o[l].astype(BF16),
                         row(ln_mem_g[l]), row(ln_mem_b[l]), alpha=alpha, seq=seq, mem_len=mem_len, tm=tm_ln)

        wr = moe_w_router[l]
        wr_hi = wr.astype(BF16)
        wr_lo = (wr - wr_hi.astype(F32)).astype(BF16)
        mask, wsel = router(xf, jnp.stack([wr_hi, wr_lo]), row(moe_b_router[l]), tm=tm_in)
        before = jnp.cumsum(mask, axis=0) - mask
        counts = jnp.sum(mask, axis=0).astype(jnp.int32)
        padded = (counts + bm - 1) // bm * bm
        pad_end = jnp.cumsum(padded)
        pad_start = pad_end - padded
        slot = pad_start[None, :] + before.astype(jnp.int32)
        top_e = lax.top_k(mask, TOP_K)[1]
        pos4 = jnp.take_along_axis(slot, top_e, axis=1).astype(jnp.int32)
        w4 = jnp.take_along_axis(wsel, top_e, axis=1)
        blk_e = jnp.minimum(jnp.searchsorted(pad_end, jnp.arange(n_blocks) * bm, side='right'),
                            N_EXPERTS - 1).astype(jnp.int32)
        tok_ids = jnp.broadcast_to(jnp.arange(t, dtype=jnp.int32)[:, None], (t, TOP_K))
        row_tok = jnp.zeros((n_rows,), jnp.int32).at[pos4.reshape(-1)].set(
            tok_ids.reshape(-1), unique_indices=True)
        x_rows = sc_gather_rows(xf, row_tok, chunk=sc_chunk)
        y_rows = experts(x_rows, blk_e, moe_w_gate[l], moe_b_gate[l].reshape(N_EXPERTS, 1, D_EXPERT),
                         moe_w_up[l], moe_b_up[l].reshape(N_EXPERTS, 1, D_EXPERT),
                         moe_w_down[l], moe_b_down[l].reshape(N_EXPERTS, 1, D_MODEL), bm=bm)
        y4 = sc_gather_rows(y_rows, pos4.T.reshape(-1), chunk=sc_chunk).reshape(TOP_K, t, D_MODEL)
        xf = moe_ln(xf, y4, _pad_cols(w4, 128), row(ln_moe_g[l]), row(ln_moe_b[l]), alpha=alpha, tm=tm_ln)

    return xf.reshape(batch, seq, D_MODEL)


def kernel(x, mem, w_in, mla_q_norm, mla_w_q_up, mla_kv_norm, mla_w_kv_up, nsa_pe_k, nsa_pe_v, nsa_w1_k, nsa_w1_v, nsa_w2_k, nsa_w2_v, w_branch, b_merge, w_out, ln_mix_g, ln_mix_b, mem_w_q, mem_w_k, mem_w_v, mem_w_o, ln_mem_g, ln_mem_b, moe_w_router, moe_b_router, moe_w_gate, moe_b_gate, moe_w_up, moe_b_up, moe_w_down, moe_b_down, ln_moe_g, ln_moe_b):
    return _forward(x, mem, w_in, mla_q_norm, mla_w_q_up, mla_kv_norm, mla_w_kv_up,
                    nsa_pe_k, nsa_pe_v, nsa_w1_k, nsa_w1_v, nsa_w2_k, nsa_w2_v,
                    w_branch, b_merge, w_out, ln_mix_g, ln_mix_b,
                    mem_w_q, mem_w_k, mem_w_v, mem_w_o, ln_mem_g, ln_mem_b,
                    moe_w_router, moe_b_router, moe_w_gate, moe_b_gate, moe_w_up, moe_b_up,
                    moe_w_down, moe_b_down, ln_moe_g, ln_moe_b)
```

```python
import functools

import numpy as np
import jax
import jax.numpy as jnp
from jax import lax
from jax.experimental import pallas as pl
from jax.experimental.pallas import tpu as pltpu
from jax.experimental.pallas import tpu_sc as plsc

F32 = jnp.float32
BF16 = jnp.bfloat16

D_MODEL = 2048
HEAD_DIM = 128
MLA_HEADS = 8
MLA_Q_RANK = 512
MLA_KV_RANK = 256
MLA_NOPE = 128
MLA_ROPE = 64
ROPE_THETA = 10000.0
NSA_HEADS = 8
NSA_GROUPS = 2
NSA_HG = NSA_HEADS // NSA_GROUPS
NSA_CMP_LEN = 32
NSA_CMP_STRIDE = 16
NSA_SEL_LEN = 64
NSA_TOPK = 16
NSA_WINDOW = 512
SB_HEADS = 8
MEM_HEADS = 4
N_EXPERTS = 32
TOP_K = 4
D_EXPERT = 512
SWIGLU_LIMIT = 7.0
SWIGLU_ALPHA = 1.702
N_BRANCH = 3
BRANCH_WIDTH = 1024
LN_EPS = 1e-5
RMS_EPS = 1e-6
NEG = -1e30
BIG = 1e30
SB_UNDERFLOW_LOG = -100.0

OFF_CQ = 0
OFF_CKV = 512
OFF_KR = 768
OFF_NSA_Q = 832
OFF_NSA_KV = 1856
OFF_NSA_GATE = 3392
OFF_SB = 3416
OFF_MERGE = 6488

VMEM_LIMIT_V7X = 56 * 1024 * 1024
MOE_BLOCK_ROWS = 256
SC_CORES_V7X = 2
SC_SUBCORES_V7X = 16


def _cp(sem, vmem=VMEM_LIMIT_V7X):
    return pltpu.CompilerParams(dimension_semantics=sem, vmem_limit_bytes=vmem)


def _dot(a, b):
    return jnp.dot(a, b, preferred_element_type=F32)


def _dot_nt(a, b):
    return lax.dot_general(a, b, (((1,), (1,)), ((), ())), preferred_element_type=F32)


def _layer_norm(z, g, b):
    mu = jnp.mean(z, axis=-1, keepdims=True)
    zc = z - mu
    var = jnp.mean(zc * zc, axis=-1, keepdims=True)
    return zc * lax.rsqrt(var + LN_EPS) * g + b


def _rms_norm(z, g):
    return z * lax.rsqrt(jnp.mean(z * z, axis=-1, keepdims=True) + RMS_EPS) * g


def _proj_heads_kernel(a_ref, w_ref, s_ref, o_ref, abf_ref, *, n_heads_per_tile):
    @pl.when(pl.program_id(1) == 0)
    def _():
        abf_ref[...] = a_ref[...].astype(BF16)

    acc = _dot(abf_ref[...], w_ref[...]) * s_ref[...]
    for c in range(n_heads_per_tile):
        o_ref[c] = acc[:, c * HEAD_DIM:(c + 1) * HEAD_DIM].astype(o_ref.dtype)


def proj_heads(a, w, scale, *, tm, tn):
    m, k = a.shape
    n = w.shape[1]
    hpt = tn // HEAD_DIM
    return pl.pallas_call(
        functools.partial(_proj_heads_kernel, n_heads_per_tile=hpt),
        out_shape=jax.ShapeDtypeStruct((n // HEAD_DIM, m, HEAD_DIM), BF16),
        grid=(m // tm, n // tn),
        in_specs=[
            pl.BlockSpec((tm, k), lambda i, j: (i, 0)),
            pl.BlockSpec((k, tn), lambda i, j: (0, j)),
            pl.BlockSpec((1, tn), lambda i, j: (0, j)),
        ],
        out_specs=pl.BlockSpec((hpt, tm, HEAD_DIM), lambda i, j: (j, i, 0)),
        scratch_shapes=[pltpu.VMEM((tm, k), BF16)],
        compiler_params=_cp(("arbitrary", "arbitrary")),
    )(a, w, scale)


def _mla_in_kernel(x_ref, w_ref, qg_ref, kg_ref, wq_ref, wkv_ref, cos_ref, sin_ref,
                   q_ref, k_ref, v_ref, g_ref):
    xb = x_ref[...].astype(BF16)
    h = _dot(xb, w_ref[...])
    cq = h[:, 0:512]
    ckv = h[:, 512:768]
    kr1 = h[:, 768:896]
    kr2 = h[:, 896:1024]
    g_ref[...] = jax.nn.sigmoid(h[:, 1024:1280])
    cos = cos_ref[...]
    sin = sin_ref[...]
    scale = (MLA_NOPE + MLA_ROPE) ** -0.5
    nq = _rms_norm(cq, qg_ref[...]).astype(BF16)
    q3 = _dot(nq, wq_ref[...])
    for hh in range(MLA_HEADS):
        lo, hi = hh * 128, (hh + 1) * 128
        q_ref[hh, :, 0:128] = (q3[:, lo:hi] * scale).astype(BF16)
        rot = q3[:, 1024 + lo:1024 + hi] * cos + q3[:, 2048 + lo:2048 + hi] * sin
        q_ref[hh, :, 128:256] = (rot * scale).astype(BF16)
    nkv = _rms_norm(ckv, kg_ref[...]).astype(BF16)
    kv = _dot(nkv, wkv_ref[...])
    krot = (kr1 * cos + kr2 * sin).astype(BF16)
    for hh in range(MLA_HEADS):
        lo, hi = hh * 128, (hh + 1) * 128
        k_ref[hh, :, 0:128] = kv[:, lo:hi].astype(BF16)
        k_ref[hh, :, 128:256] = krot
        v_ref[hh] = kv[:, 1024 + lo:1024 + hi].astype(BF16)


def mla_in(x, w_mla, qg, kg, wq3, wkv, cos128, sin128, *, seq, tm):
    t = x.shape[0]
    npos = seq // tm
    full = lambda shape: pl.BlockSpec(shape, lambda i: (0,) * len(shape))
    return pl.pallas_call(
        _mla_in_kernel,
        out_shape=(
            jax.ShapeDtypeStruct((MLA_HEADS, t, 256), BF16),
            jax.ShapeDtypeStruct((MLA_HEADS, t, 256), BF16),
            jax.ShapeDtypeStruct((MLA_HEADS, t, 128), BF16),
            jax.ShapeDtypeStruct((t, 256), F32),
        ),
        grid=(t // tm,),
        in_specs=[
            pl.BlockSpec((tm, D_MODEL), lambda i: (i, 0)),
            full((D_MODEL, 1280)),
            full((1, MLA_Q_RANK)),
            full((1, MLA_KV_RANK)),
            full((MLA_Q_RANK, 3072)),
            full((MLA_KV_RANK, 2048)),
            pl.BlockSpec((tm, 128), lambda i: (i % npos, 0)),
            pl.BlockSpec((tm, 128), lambda i: (i % npos, 0)),
        ],
        out_specs=(
            pl.BlockSpec((MLA_HEADS, tm, 256), lambda i: (0, i, 0)),
            pl.BlockSpec((MLA_HEADS, tm, 256), lambda i: (0, i, 0)),
            pl.BlockSpec((MLA_HEADS, tm, 128), lambda i: (0, i, 0)),
            pl.BlockSpec((tm, 256), lambda i: (i, 0)),
        ),
        compiler_params=_cp(("arbitrary",)),
    )(x, w_mla, qg, kg, wq3, wkv, cos128, sin128)


def _mla_attn_kernel(q_ref, k_ref, v_ref, o_ref, *, tq):
    qi = pl.program_id(2)
    q = q_ref[...]

    def step(kt, carry, diag):
        m, l, acc = carry
        k0 = pl.multiple_of(kt * tq, tq)
        k = k_ref[pl.ds(k0, tq), :]
        v = v_ref[pl.ds(k0, tq), :]
        s = _dot_nt(q, k)
        if diag:
            row = lax.broadcasted_iota(jnp.int32, (tq, tq), 0)
            col = lax.broadcasted_iota(jnp.int32, (tq, tq), 1)
            s = jnp.where(col <= row, s, NEG)
        m_new = jnp.maximum(m, jnp.max(s, axis=1, keepdims=True))
        alpha = jnp.exp(m - m_new)
        p = jnp.exp(s - m_new)
        l = alpha * l + jnp.sum(p, axis=1, keepdims=True)
        acc = alpha * acc + _dot(p.astype(BF16), v)
        return m_new, l, acc

    init = (jnp.full((tq, 1), NEG, F32), jnp.zeros((tq, 1), F32), jnp.zeros((tq, 128), F32))
    carry = lax.fori_loop(0, qi, lambda kt, c: step(kt, c, False), init)
    m, l, acc = step(qi, carry, True)
    o_ref[...] = (acc / l).astype(o_ref.dtype)


def mla_attn(q, k, v, *, batch, seq, tq):
    nq = seq // tq
    q4 = q.reshape(MLA_HEADS, batch, seq, 256)
    k4 = k.reshape(MLA_HEADS, batch, seq, 256)
    v4 = v.reshape(MLA_HEADS, batch, seq, 128)
    return pl.pallas_call(
        functools.partial(_mla_attn_kernel, tq=tq),
        out_shape=jax.ShapeDtypeStruct((batch * seq, MLA_HEADS * 128), BF16),
        grid=(MLA_HEADS, batch, nq),
        in_specs=[
            pl.BlockSpec((None, None, tq, 256), lambda h, b, i: (h, b, i, 0)),
            pl.BlockSpec((None, None, seq, 256), lambda h, b, i: (h, b, 0, 0)),
            pl.BlockSpec((None, None, seq, 128), lambda h, b, i: (h, b, 0, 0)),
        ],
        out_specs=pl.BlockSpec((tq, 128), lambda h, b, i: (b * nq + i, h)),
        compiler_params=_cp(("arbitrary", "arbitrary", "arbitrary")),
    )(q4, k4, v4)


def _sb_attn_kernel(q_ref, k_ref, v_ref, u_ref, o_ref, *, tq):
    qi = pl.program_id(2)
    q = q_ref[...]
    u = u_ref[...]

    def step(kt, carry, diag):
        run, acc = carry
        k0 = pl.multiple_of(kt * tq, tq)
        k = k_ref[pl.ds(k0, tq), :]
        v = v_ref[pl.ds(k0, tq), :]
        z = _dot_nt(q, k)
        l1m = -(jnp.maximum(z, 0.0) + jnp.log(1.0 + jnp.exp(-jnp.abs(z))))
        if diag:
            row = lax.broadcasted_iota(jnp.int32, (tq, tq), 0)
            col = lax.broadcasted_iota(jnp.int32, (tq, tq), 1)
            strict = col < row
            l1m_m = jnp.where(strict, l1m, 0.0)
        else:
            l1m_m = l1m
        hi = l1m_m.astype(BF16)
        lo = (l1m_m - hi.astype(F32)).astype(BF16)
        between = _dot(hi, u) + _dot(lo, u)
        a = jnp.exp(z + l1m + between + run)
        if diag:
            a = jnp.where(strict, a, 0.0)
        acc = acc + _dot(a.astype(BF16), v)
        run = run + between[:, 0:1] + l1m_m[:, 0:1]
        return run, acc

    init = (jnp.zeros((tq, 1), F32), jnp.zeros((tq, 128), F32))
    run, acc = step(qi, init, True)

    def more(c):
        j, run, _ = c
        return (j < qi) & (jnp.max(run) > SB_UNDERFLOW_LOG)

    def body(c):
        j, run, acc = c
        run, acc = step(qi - 1 - j, (run, acc), False)
        return j + 1, run, acc

    _, run, acc = lax.while_loop(more, body, (jnp.int32(0), run, acc))
    o_ref[...] = acc.astype(o_ref.dtype)


def sb_attn(hm, u, *, head0, batch, seq, tq):
    nq = seq // tq
    hm4 = hm.reshape(hm.shape[0], batch, seq, HEAD_DIM)
    return pl.pallas_call(
        functools.partial(_sb_attn_kernel, tq=tq),
        out_shape=jax.ShapeDtypeStruct((batch * seq, SB_HEADS * HEAD_DIM), BF16),
        grid=(SB_HEADS, batch, nq),
        in_specs=[
            pl.BlockSpec((None, None, tq, HEAD_DIM), lambda h, b, i: (head0 + h, b, i, 0)),
            pl.BlockSpec((None, None, seq, HEAD_DIM), lambda h, b, i: (head0 + SB_HEADS + h, b, 0, 0)),
            pl.BlockSpec((None, None, seq, HEAD_DIM), lambda h, b, i: (head0 + 2 * SB_HEADS + h, b, 0, 0)),
            pl.BlockSpec((tq, tq), lambda h, b, i: (0, 0)),
        ],
        out_specs=pl.BlockSpec((tq, HEAD_DIM), lambda h, b, i: (b * nq + i, h)),
        compiler_params=_cp(("arbitrary", "arbitrary", "arbitrary")),
    )(hm4, hm4, hm4, u)


def _nsa_cmp_kernel(c_ref, w1_ref, pe_ref, w2_ref, o_ref, *, nc):
    c = c_ref[...]
    half = NSA_CMP_STRIDE * HEAD_DIM
    a1 = _dot(c, w1_ref[0:half, :])
    a2 = _dot(c, w1_ref[half:2 * half, :])
    pc = _dot(pe_ref[...], w1_ref[...])[0:1, :]
    pre = a1 + pltpu.roll(a2, nc - 1, 0) + pc
    act = 0.5 * pre * (1.0 + jnp.tanh(0.7978845608028654 * (pre + 0.044715 * (pre * pre * pre))))
    o_ref[...] = _dot(act.astype(BF16), w2_ref[...]).astype(BF16)


def nsa_compress(cmp_heads, w1, pe, w2, *, batch, seq):
    nc = seq // NSA_CMP_STRIDE
    head0 = 0
    hm4 = cmp_heads.reshape(4, batch, nc, NSA_CMP_STRIDE * HEAD_DIM)
    return pl.pallas_call(
        functools.partial(_nsa_cmp_kernel, nc=nc),
        out_shape=jax.ShapeDtypeStruct((4, batch, nc, HEAD_DIM), BF16),
        grid=(4, batch),
        in_specs=[
            pl.BlockSpec((None, None, nc, NSA_CMP_STRIDE * HEAD_DIM), lambda c, b: (head0 + c, b, 0, 0)),
            pl.BlockSpec((None, NSA_CMP_LEN * HEAD_DIM, HEAD_DIM), lambda c, b: (c // 2, 0, 0)),
            pl.BlockSpec((None, 8, NSA_CMP_LEN * HEAD_DIM), lambda c, b: (c // 2, 0, 0)),
            pl.BlockSpec((None, HEAD_DIM, HEAD_DIM), lambda c, b: (c // 2, 0, 0)),
        ],
        out_specs=pl.BlockSpec((None, None, nc, HEAD_DIM), lambda c, b: (c, b, 0, 0)),
        compiler_params=_cp(("arbitrary", "arbitrary")),
    )(hm4, w1, pe, w2)


def _nsa_attn_kernel(q_ref, qa_ref, kc_ref, vc_ref, ks_ref, vs_ref, kw_ref, vw_ref, g_ref, ov_ref, e_ref,
                     kpos_ref, cpos_ref, o_ref, *, tq, tk, seq, n_sel, n_top):
    qi = pl.program_id(2)
    t0 = qi * tq
    rows = NSA_HG * tq
    nc = seq // NSA_CMP_STRIDE
    q = jnp.concatenate([q_ref[...].reshape(rows, HEAD_DIM), qa_ref[...]], axis=1)

    rid = lax.broadcasted_iota(jnp.int32, (rows, 1), 0)
    trow = t0 + lax.bitwise_and(rid, tq - 1)

    def masked_softmax(s, valid):
        sm = jnp.where(valid, s, NEG)
        m = jnp.max(sm, axis=1, keepdims=True)
        e = jnp.where(valid, jnp.exp(sm - m), 0.0)
        d = jnp.sum(e, axis=1, keepdims=True)
        return e * (1.0 / jnp.where(d > 0.0, d, 1.0))

    n_i = lax.broadcasted_iota(jnp.int32, (1, nc), 1)
    end = n_i * NSA_CMP_STRIDE + (NSA_CMP_LEN - 1)
    s_c = _dot_nt(q, jnp.concatenate([kc_ref[...], cpos_ref[...]], axis=1))
    p_c = masked_softmax(s_c, end <= trow)
    o_c = _dot(p_c.astype(BF16), vc_ref[...])

    psum = p_c[0:tq] + p_c[tq:2 * tq] + p_c[2 * tq:3 * tq] + p_c[3 * tq:4 * tq]
    p_hi = psum.astype(BF16)
    p_lo = (psum - p_hi.astype(F32)).astype(BF16)
    ov = ov_ref[...]
    imp = _dot(p_hi, ov) + _dot(p_lo, ov)
    cur = lax.shift_right_logical(t0 + lax.broadcasted_iota(jnp.int32, (tq, 1), 0),
                                  int(np.log2(NSA_SEL_LEN)))
    blk = lax.broadcasted_iota(jnp.int32, (tq, n_sel), 1)
    forced = (blk == 0) | (blk == cur) | (blk == cur - 1)
    key = jnp.where(blk > cur, -BIG, jnp.where(forced, BIG, imp))
    rank = jnp.zeros((tq, n_sel), F32)
    lane1 = lax.broadcasted_iota(jnp.int32, (1, n_sel), 1)
    for i in range(n_sel):
        vi = key[:, i:i + 1]
        tie = jnp.where(lane1 > i, 1.0, 0.0)
        rank = rank + jnp.where(vi > key, 1.0, jnp.where(vi == key, tie, 0.0))
    selm = jnp.where((rank < float(n_top)) & (blk <= cur), 1.0, 0.0).astype(BF16)

    def sel_step(kt, carry, diag):
        m, l, acc = carry
        k0 = pl.multiple_of(kt * tk, tk)
        kk = jnp.concatenate([ks_ref[pl.ds(k0, tk), :], kpos_ref[pl.ds(k0, tk), :]], axis=1)
        vv = vs_ref[pl.ds(k0, tk), :]
        mex = _dot(selm, e_ref[kt])
        mex4 = jnp.concatenate([mex] * NSA_HG, axis=0)
        s = _dot_nt(q, kk)
        valid = mex4 > 0.5
        if diag:
            spos = k0 + lax.broadcasted_iota(jnp.int32, (1, tk), 1)
            valid = valid & (spos <= trow)
        sm = jnp.where(valid, s, NEG)
        m_new = jnp.maximum(m, jnp.max(sm, axis=1, keepdims=True))
        alpha = jnp.exp(m - m_new)
        p = jnp.exp(sm - m_new)
        l = alpha * l + jnp.sum(p, axis=1, keepdims=True)
        acc = alpha * acc + _dot(p.astype(BF16), vv)
        return m_new, l, acc

    kt_last = t0 // tk
    init = (jnp.full((rows, 1), NEG, F32), jnp.zeros((rows, 1), F32), jnp.zeros((rows, HEAD_DIM), F32))
    carry = lax.fori_loop(0, kt_last, lambda kt, c: sel_step(kt, c, False), init)
    _, l_s, acc_s = sel_step(kt_last, carry, True)
    o_s = acc_s * (1.0 / l_s)

    wk = NSA_WINDOW + tq
    ks0 = pl.multiple_of(jnp.maximum(t0 - NSA_WINDOW, 0), tq)
    kw = jnp.concatenate([kw_ref[pl.ds(ks0, wk), :], kpos_ref[pl.ds(ks0, wk), :]], axis=1)
    vw = vw_ref[pl.ds(ks0, wk), :]
    wpos = ks0 + lax.broadcasted_iota(jnp.int32, (1, wk), 1)
    dw = trow - wpos
    s_w = _dot_nt(q, kw)
    p_w = masked_softmax(s_w, (dw >= 0) & (dw < NSA_WINDOW))
    o_w = _dot(p_w.astype(BF16), vw)

    gt = g_ref[...]
    for hg in range(NSA_HG):
        sl = slice(hg * tq, (hg + 1) * tq)
        o = (gt[:, 3 * hg:3 * hg + 1] * o_c[sl] + gt[:, 3 * hg + 1:3 * hg + 2] * o_s[sl]
             + gt[:, 3 * hg + 2:3 * hg + 3] * o_w[sl])
        o_ref[:, hg * HEAD_DIM:(hg + 1) * HEAD_DIM] = o.astype(o_ref.dtype)


def nsa_attn(hm, cmp, gates, consts, *, q_head0, kv_head0, batch, seq, tq, tk):
    ov, e, qa, kpos, cpos = consts
    nq = seq // tq
    nc = seq // NSA_CMP_STRIDE
    n_sel = seq // NSA_SEL_LEN
    n_top = min(NSA_TOPK, n_sel)
    assert tk % tq == 0 and seq % tk == 0 and seq >= NSA_WINDOW + tq and NSA_WINDOW % tq == 0
    hm4 = hm.reshape(hm.shape[0], batch, seq, HEAD_DIM)
    kv_spec = lambda off: pl.BlockSpec((None, None, seq, HEAD_DIM),
                                       lambda b, g, i: (kv_head0 + off + g, b, 0, 0))
    return pl.pallas_call(
        functools.partial(_nsa_attn_kernel, tq=tq, tk=tk, seq=seq, n_sel=n_sel, n_top=n_top),
        out_shape=jax.ShapeDtypeStruct((batch * seq, NSA_HEADS * HEAD_DIM), BF16),
        grid=(batch, NSA_GROUPS, nq),
        in_specs=[
            pl.BlockSpec((NSA_HG, None, tq, HEAD_DIM), lambda b, g, i: (q_head0 // NSA_HG + g, b, i, 0)),
            pl.BlockSpec((None, NSA_HG * tq, 128), lambda b, g, i: (g, 0, 0)),
            pl.BlockSpec((None, None, nc, HEAD_DIM), lambda b, g, i: (g, b, 0, 0)),
            pl.BlockSpec((None, None, nc, HEAD_DIM), lambda b, g, i: (2 + g, b, 0, 0)),
            kv_spec(4), kv_spec(6), kv_spec(8), kv_spec(10),
            pl.BlockSpec((tq, 128), lambda b, g, i: (b * nq + i, g)),
            pl.BlockSpec((nc, n_sel), lambda b, g, i: (0, 0)),
            pl.BlockSpec((seq // tk, n_sel, tk), lambda b, g, i: (0, 0, 0)),
            pl.BlockSpec((seq, 128), lambda b, g, i: (0, 0)),
            pl.BlockSpec((nc, 128), lambda b, g, i: (0, 0)),
        ],
        out_specs=pl.BlockSpec((tq, NSA_HG * HEAD_DIM), lambda b, g, i: (b * nq + i, g)),
        compiler_params=_cp(("arbitrary", "arbitrary", "arbitrary")),
    )(hm4, qa, cmp, cmp, hm4, hm4, hm4, hm4, gates, ov, e, kpos, cpos)


def _merge_kernel(x_ref, oa_ref, ob_ref, oc_ref, wm_ref, wb_ref, bm_ref, y_ref, xb_ref):
    @pl.when(pl.program_id(1) == 0)
    def _():
        xb_ref[...] = x_ref[...].astype(BF16)

    xb = xb_ref[...]
    acc = None
    for br, o_ref in enumerate((oa_ref, ob_ref, oc_ref)):
        gate = jax.nn.sigmoid(_dot(xb, wm_ref[br]) + bm_ref[br])
        term = gate * _dot(o_ref[...], wb_ref[br])
        acc = term if acc is None else acc + term
    y_ref[...] = acc.astype(y_ref.dtype)


def merge_branches(x, o_a, o_b, o_c, wm, wb, bm, *, tm, tn):
    t = x.shape[0]
    o_spec = pl.BlockSpec((tm, BRANCH_WIDTH), lambda i, j: (i, 0))
    return pl.pallas_call(
        _merge_kernel,
        out_shape=jax.ShapeDtypeStruct((t, D_MODEL), BF16),
        grid=(t // tm, D_MODEL // tn),
        in_specs=[
            pl.BlockSpec((tm, D_MODEL), lambda i, j: (i, 0)),
            o_spec, o_spec, o_spec,
            pl.BlockSpec((N_BRANCH, D_MODEL, tn), lambda i, j: (0, 0, j)),
            pl.BlockSpec((N_BRANCH, BRANCH_WIDTH, tn), lambda i, j: (0, 0, j)),
            pl.BlockSpec((N_BRANCH, 1, tn), lambda i, j: (0, 0, j)),
        ],
        out_specs=pl.BlockSpec((tm, tn), lambda i, j: (i, j)),
        scratch_shapes=[pltpu.VMEM((tm, D_MODEL), BF16)],
        compiler_params=_cp(("arbitrary", "arbitrary")),
    )(x, o_a, o_b, o_c, wm, wb, bm)


def _out_ln_kernel(y_ref, w_ref, x_ref, g_ref, b_ref, o_ref, *, alpha):
    h = _dot(y_ref[...], w_ref[...])
    o_ref[...] = _layer_norm(alpha * x_ref[...] + h, g_ref[...], b_ref[...])


def out_ln(y, w, x, g, b, *, alpha, tm):
    t = x.shape[0]
    return pl.pallas_call(
        functools.partial(_out_ln_kernel, alpha=alpha),
        out_shape=jax.ShapeDtypeStruct((t, D_MODEL), F32),
        grid=(t // tm,),
        in_specs=[
            pl.BlockSpec((tm, D_MODEL), lambda i: (i, 0)),
            pl.BlockSpec((D_MODEL, D_MODEL), lambda i: (0, 0)),
            pl.BlockSpec((tm, D_MODEL), lambda i: (i, 0)),
            pl.BlockSpec((1, D_MODEL), lambda i: (0, 0)),
            pl.BlockSpec((1, D_MODEL), lambda i: (0, 0)),
        ],
        out_specs=pl.BlockSpec((tm, D_MODEL), lambda i: (i, 0)),
        compiler_params=_cp(("arbitrary",)),
    )(y, w, x, g, b)


def _mem_attn_kernel(x_ref, wq_ref, k_ref, v_ref, wo_ref, g_ref, b_ref, o_ref, *, alpha):
    x = x_ref[...]
    q = _dot(x.astype(BF16), wq_ref[...]) * (HEAD_DIM ** -0.5)
    outs = []
    for h in range(MEM_HEADS):
        qh = q[:, h * HEAD_DIM:(h + 1) * HEAD_DIM].astype(BF16)
        s = _dot_nt(qh, k_ref[h])
        m = jnp.max(s, axis=1, keepdims=True)
        e = jnp.exp(s - m)
        p = e * (1.0 / jnp.sum(e, axis=1, keepdims=True))
        outs.append(_dot(p.astype(BF16), v_ref[h]).astype(BF16))
    o = jnp.concatenate(outs, axis=1)
    h_out = _dot(o, wo_ref[...])
    o_ref[...] = _layer_norm(alpha * x + h_out, g_ref[...], b_ref[...])


def mem_attn_ln(x, wq, kv, wo, g, b, *, alpha, seq, mem_len, tm):
    t = x.shape[0]
    per_b = seq // tm
    kv4 = kv.reshape(2 * MEM_HEADS, t // seq, mem_len, HEAD_DIM)
    width = MEM_HEADS * HEAD_DIM
    return pl.pallas_call(
        functools.partial(_mem_attn_kernel, alpha=alpha),
        out_shape=jax.ShapeDtypeStruct((t, D_MODEL), F32),
        grid=(t // tm,),
        in_specs=[
            pl.BlockSpec((tm, D_MODEL), lambda i: (i, 0)),
            pl.BlockSpec((D_MODEL, width), lambda i: (0, 0)),
            pl.BlockSpec((MEM_HEADS, None, mem_len, HEAD_DIM), lambda i: (0, i // per_b, 0, 0)),
            pl.BlockSpec((MEM_HEADS, None, mem_len, HEAD_DIM), lambda i: (1, i // per_b, 0, 0)),
            pl.BlockSpec((width, D_MODEL), lambda i: (0, 0)),
            pl.BlockSpec((1, D_MODEL), lambda i: (0, 0)),
            pl.BlockSpec((1, D_MODEL), lambda i: (0, 0)),
        ],
        out_specs=pl.BlockSpec((tm, D_MODEL), lambda i: (i, 0)),
        compiler_params=_cp(("arbitrary",)),
    )(x, wq, kv4, kv4, wo, g, b)


def _router_kernel(x_ref, w_ref, b_ref, mask_ref, wsel_ref):
    x = x_ref[...]
    xh = x.astype(BF16)
    xl = (x - xh.astype(F32)).astype(BF16)
    logits = _dot(xh, w_ref[0]) + _dot(xh, w_ref[1]) + _dot(xl, w_ref[0]) + b_ref[...]
    tm = logits.shape[0]
    lane = lax.broadcasted_iota(jnp.int32, (tm, N_EXPERTS), 1)
    work = logits
    hots, vals = [], []
    for _ in range(TOP_K):
        m = jnp.max(work, axis=1, keepdims=True)
        idx = jnp.min(jnp.where(work == m, lane, N_EXPERTS), axis=1, keepdims=True)
        hot = lane == idx
        hots.append(hot)
        vals.append(m)
        work = jnp.where(hot, -jnp.inf, work)
    es = [jnp.exp(v - vals[0]) for v in vals]
    inv = 1.0 / (es[0] + es[1] + es[2] + es[3])
    mask = jnp.zeros((tm, N_EXPERTS), F32)
    wsel = jnp.zeros((tm, N_EXPERTS), F32)
    for hot, e in zip(hots, es):
        mask = mask + jnp.where(hot, 1.0, 0.0)
        wsel = wsel + jnp.where(hot, e * inv, 0.0)
    mask_ref[...] = mask
    wsel_ref[...] = wsel


def router(x, w_hl, b, *, tm):
    t = x.shape[0]
    return pl.pallas_call(
        _router_kernel,
        out_shape=(jax.ShapeDtypeStruct((t, N_EXPERTS), F32), jax.ShapeDtypeStruct((t, N_EXPERTS), F32)),
        grid=(t // tm,),
        in_specs=[
            pl.BlockSpec((tm, D_MODEL), lambda i: (i, 0)),
            pl.BlockSpec((2, D_MODEL, N_EXPERTS), lambda i: (0, 0, 0)),
            pl.BlockSpec((1, N_EXPERTS), lambda i: (0, 0)),
        ],
        out_specs=(pl.BlockSpec((tm, N_EXPERTS), lambda i: (i, 0)),
                   pl.BlockSpec((tm, N_EXPERTS), lambda i: (i, 0))),
        compiler_params=_cp(("arbitrary",)),
    )(x, w_hl, b)


def sc_gather_rows(table, idx, *, chunk):
    n = idx.shape[0]
    d = table.shape[1]
    workers = SC_CORES_V7X * SC_SUBCORES_V7X
    per_w = n // workers
    assert n % (workers * chunk) == 0 and chunk % 8 == 0 and chunk <= 128
    mesh = plsc.VectorSubcoreMesh(core_axis_name="c", subcore_axis_name="s")

    @functools.partial(
        pl.kernel, mesh=mesh,
        out_type=jax.ShapeDtypeStruct((n, d), table.dtype),
        scratch_types=[pltpu.VMEM((chunk,), jnp.int32), pltpu.VMEM((chunk, d), table.dtype),
                       pltpu.SemaphoreType.DMA],
    )
    def gather(table_hbm, idx_hbm, out_hbm, idx_v, rows_v, sem):
        wid = lax.axis_index("s") * SC_CORES_V7X + lax.axis_index("c")
        base = wid * per_w

        @pl.loop(0, per_w // chunk)
        def _(j):
            off = pl.multiple_of(base + j * chunk, 8)
            pltpu.sync_copy(idx_hbm.at[pl.ds(off, chunk)], idx_v)
            pltpu.async_copy(table_hbm.at[idx_v], rows_v, sem).wait()
            pltpu.sync_copy(rows_v, out_hbm.at[pl.ds(off, chunk)])

    return gather(table, idx)


def _experts_kernel(be_ref, x_ref, wg_ref, bg_ref, wu_ref, bu_ref, wd_ref, bd_ref, y_ref,
                    wgb_ref, wub_ref, wdb_ref):
    i = pl.program_id(0)
    prev = be_ref[jnp.maximum(i - 1, 0)]

    @pl.when((i == 0) | (be_ref[i] != prev))
    def _():
        wgb_ref[...] = wg_ref[...].astype(BF16)
        wub_ref[...] = wu_ref[...].astype(BF16)
        wdb_ref[...] = wd_ref[...].astype(BF16)

    xb = x_ref[...].astype(BF16)
    g = jnp.minimum(_dot(xb, wgb_ref[...]) + bg_ref[...], SWIGLU_LIMIT)
    u = jnp.clip(_dot(xb, wub_ref[...]) + bu_ref[...], -SWIGLU_LIMIT, SWIGLU_LIMIT)
    hdn = (u + 1.0) * (g * jax.nn.sigmoid(SWIGLU_ALPHA * g))
    y_ref[...] = _dot(hdn.astype(BF16), wdb_ref[...]) + bd_ref[...]


def experts(x_rows, blk_e, wg, bg, wu, bu, wd, bd, *, layer, bm):
    n_rows, d = x_rows.shape
    f = wg.shape[3]
    w_spec = lambda shape: pl.BlockSpec((None, None) + shape, lambda i, be: (layer, be[i], 0, 0))
    grid_spec = pltpu.PrefetchScalarGridSpec(
        num_scalar_prefetch=1,
        grid=(n_rows // bm,),
        in_specs=[
            pl.BlockSpec((bm, d), lambda i, be: (i, 0)),
            w_spec((d, f)), w_spec((1, f)), w_spec((d, f)), w_spec((1, f)), w_spec((f, d)), w_spec((1, d)),
        ],
        out_specs=pl.BlockSpec((bm, d), lambda i, be: (i, 0)),
        scratch_shapes=[pltpu.VMEM((d, f), BF16), pltpu.VMEM((d, f), BF16), pltpu.VMEM((f, d), BF16)],
    )
    return pl.pallas_call(
        _experts_kernel,
        out_shape=jax.ShapeDtypeStruct((n_rows, d), F32),
        grid_spec=grid_spec,
        compiler_params=_cp(("arbitrary",)),
    )(blk_e, x_rows, wg, bg, wu, bu, wd, bd)


def _moe_ln_kernel(x_ref, y_ref, w_ref, g_ref, b_ref, o_ref, *, alpha):
    w = w_ref[...]
    y = w[:, 0:1] * y_ref[0]
    for k in range(1, TOP_K):
        y = y + w[:, k:k + 1] * y_ref[k]
    o_ref[...] = _layer_norm(alpha * x_ref[...] + y, g_ref[...], b_ref[...])


def moe_ln(x, y4, w4p, g, b, *, alpha, tm):
    t = x.shape[0]
    return pl.pallas_call(
        functools.partial(_moe_ln_kernel, alpha=alpha),
        out_shape=jax.ShapeDtypeStruct((t, D_MODEL), F32),
        grid=(t // tm,),
        in_specs=[
            pl.BlockSpec((tm, D_MODEL), lambda i: (i, 0)),
            pl.BlockSpec((TOP_K, tm, D_MODEL), lambda i: (0, i, 0)),
            pl.BlockSpec((tm, 128), lambda i: (i, 0)),
            pl.BlockSpec((1, D_MODEL), lambda i: (0, 0)),
            pl.BlockSpec((1, D_MODEL), lambda i: (0, 0)),
        ],
        out_specs=pl.BlockSpec((tm, D_MODEL), lambda i: (i, 0)),
        compiler_params=_cp(("arbitrary",)),
    )(x, y4, w4p, g, b)


def _rope_tables(seq):
    inv = np.asarray(ROPE_THETA ** (-np.arange(0, MLA_ROPE, 2) / MLA_ROPE), np.float32)
    ang = jnp.arange(seq, dtype=F32)[:, None] * jnp.asarray(inv)[None, :]
    cos, sin = jnp.cos(ang), jnp.sin(ang)
    zeros = jnp.zeros((seq, 128 - MLA_ROPE), F32)
    return (jnp.concatenate([cos, cos, zeros], axis=1), jnp.concatenate([-sin, sin, zeros], axis=1))


def _nsa_constants(seq, tq, tk):
    nc = seq // NSA_CMP_STRIDE
    qa = np.zeros((NSA_GROUPS, NSA_HG * tq, 128), np.float32)
    for g in range(NSA_GROUPS):
        for hg in range(NSA_HG):
            slope = 2.0 ** (-8.0 * (g * NSA_HG + hg + 1) / NSA_HEADS)
            qa[g, hg * tq:(hg + 1) * tq, 0] = slope * NSA_SEL_LEN
            qa[g, hg * tq:(hg + 1) * tq, 1] = slope
            qa[g, hg * tq:(hg + 1) * tq, 2] = slope * NSA_CMP_STRIDE
            qa[g, hg * tq:(hg + 1) * tq, 3] = slope * (NSA_CMP_LEN - 1) / 2.0
    kpos = np.zeros((seq, 128), np.float32)
    kpos[:, 0] = np.arange(seq) // NSA_SEL_LEN
    kpos[:, 1] = np.arange(seq) % NSA_SEL_LEN
    cpos = np.zeros((nc, 128), np.float32)
    cpos[:, 2] = np.arange(nc)
    cpos[:, 3] = 1.0
    for arr in (qa, kpos, cpos):
        assert np.array_equal(arr.astype(BF16).astype(np.float32), arr)
    n_cmp = (seq - NSA_CMP_LEN) // NSA_CMP_STRIDE + 1
    n_sel = seq // NSA_SEL_LEN
    cs = np.arange(nc) * NSA_CMP_STRIDE
    ss = np.arange(n_sel) * NSA_SEL_LEN
    ov = np.clip(np.minimum(cs[:, None] + NSA_CMP_LEN, ss[None, :] + NSA_SEL_LEN)
                 - np.maximum(cs[:, None], ss[None, :]), 0, None) / NSA_CMP_LEN
    ov[n_cmp:] = 0.0
    e = (np.arange(seq)[None, :] // NSA_SEL_LEN == np.arange(n_sel)[:, None]).astype(np.float32)
    e = e.reshape(n_sel, seq // tk, tk).transpose(1, 0, 2)
    return tuple(jnp.asarray(a, BF16) for a in (ov, e, qa, kpos, cpos))


def _pad_cols(w, width):
    return jnp.pad(w, ((0, 0), (0, width - w.shape[1])))


def _swap_halves(w):
    half = w.shape[1] // 2
    return jnp.concatenate([w[:, half:], w[:, :half]], axis=1)


def _layer_weights(w_in, w_q_up, w_kv_up):
    kr = w_in[:, OFF_KR:OFF_NSA_Q]
    gate = w_in[:, OFF_NSA_GATE:OFF_SB]
    per_g = NSA_HG * 3
    w_mla = jnp.concatenate([
        w_in[:, OFF_CQ:OFF_KR],
        _pad_cols(kr, 128), _pad_cols(_swap_halves(kr), 128),
        _pad_cols(gate[:, :per_g], 128), _pad_cols(gate[:, per_g:], 128)], axis=1).astype(BF16)
    wq = w_q_up.reshape(MLA_Q_RANK, MLA_HEADS, MLA_NOPE + MLA_ROPE)
    rope = wq[:, :, MLA_NOPE:]
    rope_sw = jnp.concatenate([rope[:, :, MLA_ROPE // 2:], rope[:, :, :MLA_ROPE // 2]], axis=2)
    pad = ((0, 0), (0, 0), (0, 128 - MLA_ROPE))
    wq3 = jnp.concatenate([
        wq[:, :, :MLA_NOPE].reshape(MLA_Q_RANK, -1),
        jnp.pad(rope, pad).reshape(MLA_Q_RANK, -1),
        jnp.pad(rope_sw, pad).reshape(MLA_Q_RANK, -1)], axis=1).astype(BF16)
    wkv = w_kv_up.reshape(MLA_KV_RANK, MLA_HEADS, 2, 128)
    wkv = jnp.concatenate([wkv[:, :, 0].reshape(MLA_KV_RANK, -1),
                           wkv[:, :, 1].reshape(MLA_KV_RANK, -1)], axis=1).astype(BF16)
    w_heads = w_in[:, OFF_NSA_Q:OFF_NSA_GATE]
    w_heads = jnp.concatenate([w_heads, w_in[:, OFF_SB:OFF_MERGE]], axis=1).astype(BF16)
    wm = w_in[:, OFF_MERGE:].reshape(D_MODEL, N_BRANCH, D_MODEL).transpose(1, 0, 2).astype(BF16)
    return w_mla, wq3, wkv, w_heads, wm


def _forward(x, mem, w_in, mla_q_norm, mla_w_q_up, mla_kv_norm, mla_w_kv_up,
             nsa_pe_k, nsa_pe_v, nsa_w1_k, nsa_w1_v, nsa_w2_k, nsa_w2_v,
             w_branch, b_merge, w_out, ln_mix_g, ln_mix_b,
             mem_w_q, mem_w_k, mem_w_v, mem_w_o, ln_mem_g, ln_mem_b,
             moe_w_router, moe_b_router, moe_w_gate, moe_b_gate, moe_w_up, moe_b_up,
             moe_w_down, moe_b_down, ln_moe_g, ln_moe_b):
    batch, seq, _ = x.shape
    mem_len = mem.shape[1]
    depth = w_in.shape[0]
    t = batch * seq
    alpha = float((2 * depth) ** 0.25)
    bm = MOE_BLOCK_ROWS
    n_rows = t * TOP_K + N_EXPERTS * bm
    n_blocks = n_rows // bm

    tm_in = min(512, seq)
    tq_mla = min(512, seq)
    tq_sb = 256
    tq_nsa, tk_nsa = 128, 512
    tm_ln = 256
    sc_chunk = 32

    cos128, sin128 = _rope_tables(seq)
    nsa_consts = _nsa_constants(seq, tq_nsa, tk_nsa)
    u_sb = jnp.asarray(np.arange(tq_sb)[:, None] > np.arange(tq_sb)[None, :], BF16)
    n_qheads = NSA_HEADS
    n_kvheads = 3 * 2 * NSA_GROUPS
    head_scale = np.ones((1, (n_qheads + n_kvheads + 3 * SB_HEADS) * HEAD_DIM), np.float32)
    head_scale[:, :n_qheads * HEAD_DIM] = HEAD_DIM ** -0.5
    sb0 = n_qheads + n_kvheads
    head_scale[:, sb0 * HEAD_DIM:(sb0 + SB_HEADS) * HEAD_DIM] = HEAD_DIM ** -0.5
    head_scale = jnp.asarray(head_scale)
    ones_kv = jnp.ones((1, 2 * MEM_HEADS * HEAD_DIM), F32)

    b_gate4 = moe_b_gate.reshape(depth, N_EXPERTS, 1, D_EXPERT)
    b_up4 = moe_b_up.reshape(depth, N_EXPERTS, 1, D_EXPERT)
    b_down4 = moe_b_down.reshape(depth, N_EXPERTS, 1, D_MODEL)

    xf = x.reshape(t, D_MODEL)
    memf = mem.reshape(batch * mem_len, D_MODEL)
    row = lambda v: v.reshape(1, -1)

    for l in range(depth):
        w_mla, wq3, wkv, w_heads, wm = _layer_weights(w_in[l], mla_w_q_up[l], mla_w_kv_up[l])

        q_a, k_a, v_a, gates = mla_in(xf, w_mla, row(mla_q_norm[l]), row(mla_kv_norm[l]), wq3, wkv,
                                      cos128, sin128, seq=seq, tm=tm_ln)
        hm = proj_heads(xf, w_heads, head_scale, tm=tm_in, tn=512)
        o_a = mla_attn(q_a, k_a, v_a, batch=batch, seq=seq, tq=tq_mla)
        w1 = jnp.stack([nsa_w1_k[l], nsa_w1_v[l]]).astype(BF16)
        pe = jnp.stack([nsa_pe_k[l], nsa_pe_v[l]]).reshape(2, 1, -1)
        pe = jnp.broadcast_to(pe, (2, 8, pe.shape[-1])).astype(BF16)
        w2 = jnp.stack([nsa_w2_k[l], nsa_w2_v[l]]).astype(BF16)
        cmp = nsa_compress(hm[n_qheads:n_qheads + 4], w1, pe, w2, batch=batch, seq=seq)
        o_b = nsa_attn(hm, cmp, gates, nsa_consts, q_head0=0, kv_head0=n_qheads,
                       batch=batch, seq=seq, tq=tq_nsa, tk=tk_nsa)
        o_c = sb_attn(hm, u_sb, head0=sb0, batch=batch, seq=seq, tq=tq_sb)
        y = merge_branches(xf, o_a, o_b, o_c, wm, w_branch[l].astype(BF16),
                           b_merge[l].reshape(N_BRANCH, 1, D_MODEL), tm=tm_in, tn=512)
        xf = out_ln(y, w_out[l].astype(BF16), xf, row(ln_mix_g[l]), row(ln_mix_b[l]), alpha=alpha, tm=tm_ln)

        w_kv_mem = jnp.concatenate([mem_w_k[l], mem_w_v[l]], axis=1).astype(BF16)
        kv_mem = proj_heads(memf, w_kv_mem, ones_kv, tm=min(512, batch * mem_len), tn=512)
        xf = mem_attn_ln(xf, mem_w_q[l].astype(BF16), kv_mem, mem_w_o[l].astype(BF16),
                         row(ln_mem_g[l]), row(ln_mem_b[l]), alpha=alpha, seq=seq, mem_len=mem_len, tm=tm_ln)

        wr = moe_w_router[l]
        wr_hi = wr.astype(BF16)
        wr_lo = (wr - wr_hi.astype(F32)).astype(BF16)
        mask, wsel = router(xf, jnp.stack([wr_hi, wr_lo]), row(moe_b_router[l]), tm=tm_in)
        before = jnp.cumsum(mask, axis=0) - mask
        counts = jnp.sum(mask, axis=0).astype(jnp.int32)
        padded = (counts + bm - 1) // bm * bm
        pad_end = jnp.cumsum(padded)
        pad_start = pad_end - padded
        slot = pad_start[None, :] + before.astype(jnp.int32)
        top_e = lax.top_k(mask, TOP_K)[1]
        pos4 = jnp.take_along_axis(slot, top_e, axis=1).astype(jnp.int32)
        w4 = jnp.take_along_axis(wsel, top_e, axis=1)
        blk_e = jnp.minimum(jnp.searchsorted(pad_end, jnp.arange(n_blocks) * bm, side='right'),
                            N_EXPERTS - 1).astype(jnp.int32)
        tok_ids = jnp.broadcast_to(jnp.arange(t, dtype=jnp.int32)[:, None], (t, TOP_K))
        row_tok = jnp.zeros((n_rows,), jnp.int32).at[pos4.reshape(-1)].set(
            tok_ids.reshape(-1), unique_indices=True)
        x_rows = sc_gather_rows(xf, row_tok, chunk=sc_chunk)
        y_rows = experts(x_rows, blk_e, moe_w_gate, b_gate4, moe_w_up, b_up4, moe_w_down, b_down4,
                         layer=l, bm=bm)
        y4 = sc_gather_rows(y_rows, pos4.T.reshape(-1), chunk=sc_chunk).reshape(TOP_K, t, D_MODEL)
        xf = moe_ln(xf, y4, _pad_cols(w4, 128), row(ln_moe_g[l]), row(ln_moe_b[l]), alpha=alpha, tm=tm_ln)

    return xf.reshape(batch, seq, D_MODEL)


def kernel(x, mem, w_in, mla_q_norm, mla_w_q_up, mla_kv_norm, mla_w_kv_up, nsa_pe_k, nsa_pe_v, nsa_w1_k, nsa_w1_v, nsa_w2_k, nsa_w2_v, w_branch, b_merge, w_out, ln_mix_g, ln_mix_b, mem_w_q, mem_w_k, mem_w_v, mem_w_o, ln_mem_g, ln_mem_b, moe_w_router, moe_b_router, moe_w_gate, moe_b_gate, moe_w_up, moe_b_up, moe_w_down, moe_b_down, ln_moe_g, ln_moe_b):
    return _forward(x, mem, w_in, mla_q_norm, mla_w_q_up, mla_kv_norm, mla_w_kv_up,
                    nsa_pe_k, nsa_pe_v, nsa_w1_k, nsa_w1_v, nsa_w2_k, nsa_w2_v,
                    w_branch, b_merge, w_out, ln_mix_g, ln_mix_b,
                    mem_w_q, mem_w_k, mem_w_v, mem_w_o, ln_mem_g, ln_mem_b,
                    moe_w_router, moe_b_router, moe_w_gate, moe_b_gate, moe_w_up, moe_b_up,
                    moe_w_down, moe_b_down, ln_moe_g, ln_moe_b)
```

```python
import functools

import numpy as np
import jax
import jax.numpy as jnp
from jax import lax
from jax.experimental import pallas as pl
from jax.experimental.pallas import tpu as pltpu
from jax.experimental.pallas import tpu_sc as plsc

F32 = jnp.float32
BF16 = jnp.bfloat16

D_MODEL = 2048
HEAD_DIM = 128
MLA_HEADS = 8
MLA_Q_RANK = 512
MLA_KV_RANK = 256
MLA_NOPE = 128
MLA_ROPE = 64
ROPE_THETA = 10000.0
NSA_HEADS = 8
NSA_GROUPS = 2
NSA_HG = NSA_HEADS // NSA_GROUPS
NSA_CMP_LEN = 32
NSA_CMP_STRIDE = 16
NSA_SEL_LEN = 64
NSA_TOPK = 16
NSA_WINDOW = 512
SB_HEADS = 8
MEM_HEADS = 4
N_EXPERTS = 32
TOP_K = 4
D_EXPERT = 512
SWIGLU_LIMIT = 7.0
SWIGLU_ALPHA = 1.702
N_BRANCH = 3
BRANCH_WIDTH = 1024
LN_EPS = 1e-5
RMS_EPS = 1e-6
NEG = -1e30
BIG = 1e30
SB_UNDERFLOW_LOG = -100.0

OFF_CQ = 0
OFF_CKV = 512
OFF_KR = 768
OFF_NSA_Q = 832
OFF_NSA_KV = 1856
OFF_NSA_GATE = 3392
OFF_SB = 3416
OFF_MERGE = 6488

VMEM_LIMIT_V7X = 56 * 1024 * 1024
MOE_BLOCK_ROWS = 256
SC_CORES_V7X = 2
SC_SUBCORES_V7X = 16


def _cp(sem, vmem=VMEM_LIMIT_V7X):
    return pltpu.CompilerParams(dimension_semantics=sem, vmem_limit_bytes=vmem)


def _dot(a, b):
    return jnp.dot(a, b, preferred_element_type=F32)


def _dot_nt(a, b):
    return lax.dot_general(a, b, (((1,), (1,)), ((), ())), preferred_element_type=F32)


def _layer_norm(z, g, b):
    mu = jnp.mean(z, axis=-1, keepdims=True)
    zc = z - mu
    var = jnp.mean(zc * zc, axis=-1, keepdims=True)
    return zc * lax.rsqrt(var + LN_EPS) * g + b


def _rms_norm(z, g):
    return z * lax.rsqrt(jnp.mean(z * z, axis=-1, keepdims=True) + RMS_EPS) * g


def _pack_bf16_pairs(z):
    n = z.shape[1] // 2
    bits = pltpu.bitcast(z.astype(BF16).astype(F32), jnp.uint32)
    return lax.shift_right_logical(bits[:, :n], jnp.uint32(16)) | (bits[:, n:] & jnp.uint32(0xFFFF0000))


def _unpack_bf16_pairs(w):
    lo = pltpu.bitcast(lax.shift_left(w, jnp.uint32(16)), F32)
    hi = pltpu.bitcast(w & jnp.uint32(0xFFFF0000), F32)
    return lo, hi


def _proj_heads_kernel(a_ref, w_ref, s_ref, o_ref, abf_ref, *, n_heads_per_tile):
    @pl.when(pl.program_id(1) == 0)
    def _():
        abf_ref[...] = a_ref[...].astype(BF16)

    acc = _dot(abf_ref[...], w_ref[...]) * s_ref[...]
    for c in range(n_heads_per_tile):
        o_ref[c] = acc[:, c * HEAD_DIM:(c + 1) * HEAD_DIM].astype(o_ref.dtype)


def proj_heads(a, w, scale, *, tm, tn):
    m, k = a.shape
    n = w.shape[1]
    hpt = tn // HEAD_DIM
    return pl.pallas_call(
        functools.partial(_proj_heads_kernel, n_heads_per_tile=hpt),
        out_shape=jax.ShapeDtypeStruct((n // HEAD_DIM, m, HEAD_DIM), BF16),
        grid=(m // tm, n // tn),
        in_specs=[
            pl.BlockSpec((tm, k), lambda i, j: (i, 0)),
            pl.BlockSpec((k, tn), lambda i, j: (0, j)),
            pl.BlockSpec((1, tn), lambda i, j: (0, j)),
        ],
        out_specs=pl.BlockSpec((hpt, tm, HEAD_DIM), lambda i, j: (j, i, 0)),
        scratch_shapes=[pltpu.VMEM((tm, k), BF16)],
        compiler_params=_cp(("arbitrary", "arbitrary")),
    )(a, w, scale)


def _mla_in_kernel(x_ref, w_ref, qg_ref, kg_ref, wq_ref, wkv_ref, cos_ref, sin_ref,
                   q_ref, k_ref, v_ref, g_ref):
    xb = x_ref[...].astype(BF16)
    h = _dot(xb, w_ref[...])
    cq = h[:, 0:512]
    ckv = h[:, 512:768]
    kr1 = h[:, 768:896]
    kr2 = h[:, 896:1024]
    g_ref[...] = jax.nn.sigmoid(h[:, 1024:1280])
    cos = cos_ref[...]
    sin = sin_ref[...]
    scale = (MLA_NOPE + MLA_ROPE) ** -0.5
    nq = _rms_norm(cq, qg_ref[...]).astype(BF16)
    q3 = _dot(nq, wq_ref[...])
    for hh in range(MLA_HEADS):
        lo, hi = hh * 128, (hh + 1) * 128
        q_ref[hh, :, 0:128] = (q3[:, lo:hi] * scale).astype(BF16)
        rot = q3[:, 1024 + lo:1024 + hi] * cos + q3[:, 2048 + lo:2048 + hi] * sin
        q_ref[hh, :, 128:256] = (rot * scale).astype(BF16)
    nkv = _rms_norm(ckv, kg_ref[...]).astype(BF16)
    kv = _dot(nkv, wkv_ref[...])
    krot = (kr1 * cos + kr2 * sin).astype(BF16)
    for hh in range(MLA_HEADS):
        lo, hi = hh * 128, (hh + 1) * 128
        k_ref[hh, :, 0:128] = kv[:, lo:hi].astype(BF16)
        k_ref[hh, :, 128:256] = krot
        v_ref[hh] = kv[:, 1024 + lo:1024 + hi].astype(BF16)


def mla_in(x, w_mla, qg, kg, wq3, wkv, cos128, sin128, *, seq, tm):
    t = x.shape[0]
    npos = seq // tm
    full = lambda shape: pl.BlockSpec(shape, lambda i: (0,) * len(shape))
    return pl.pallas_call(
        _mla_in_kernel,
        out_shape=(
            jax.ShapeDtypeStruct((MLA_HEADS, t, 256), BF16),
            jax.ShapeDtypeStruct((MLA_HEADS, t, 256), BF16),
            jax.ShapeDtypeStruct((MLA_HEADS, t, 128), BF16),
            jax.ShapeDtypeStruct((t, 256), F32),
        ),
        grid=(t // tm,),
        in_specs=[
            pl.BlockSpec((tm, D_MODEL), lambda i: (i, 0)),
            full((D_MODEL, 1280)),
            full((1, MLA_Q_RANK)),
            full((1, MLA_KV_RANK)),
            full((MLA_Q_RANK, 3072)),
            full((MLA_KV_RANK, 2048)),
            pl.BlockSpec((tm, 128), lambda i: (i % npos, 0)),
            pl.BlockSpec((tm, 128), lambda i: (i % npos, 0)),
        ],
        out_specs=(
            pl.BlockSpec((MLA_HEADS, tm, 256), lambda i: (0, i, 0)),
            pl.BlockSpec((MLA_HEADS, tm, 256), lambda i: (0, i, 0)),
            pl.BlockSpec((MLA_HEADS, tm, 128), lambda i: (0, i, 0)),
            pl.BlockSpec((tm, 256), lambda i: (i, 0)),
        ),
        compiler_params=_cp(("arbitrary",)),
    )(x, w_mla, qg, kg, wq3, wkv, cos128, sin128)


def _mla_attn_kernel(q_ref, k_ref, v_ref, o_ref, *, tq, heads):
    qi = pl.program_id(2)

    def head_step(h, kt, carry, diag):
        m, l, acc = carry
        k0 = pl.multiple_of(kt * tq, tq)
        k = k_ref[h, pl.ds(k0, tq), :]
        v = v_ref[h, pl.ds(k0, tq), :]
        s = _dot_nt(q_ref[h], k)
        if diag:
            row = lax.broadcasted_iota(jnp.int32, (tq, tq), 0)
            col = lax.broadcasted_iota(jnp.int32, (tq, tq), 1)
            s = jnp.where(col <= row, s, NEG)
        m_new = jnp.maximum(m, jnp.max(s, axis=1, keepdims=True))
        alpha = jnp.exp(m - m_new)
        p = jnp.exp((s - m_new).astype(BF16))
        l = alpha * l + jnp.sum(p.astype(F32), axis=1, keepdims=True)
        acc = alpha * acc + _dot(p, v)
        return m_new, l, acc

    def step(kt, carries, diag):
        return tuple(head_step(h, kt, carries[h], diag) for h in range(heads))

    init = (jnp.full((tq, 1), NEG, F32), jnp.zeros((tq, 1), F32), jnp.zeros((tq, 128), F32))
    carries = lax.fori_loop(0, qi, lambda kt, c: step(kt, c, False), (init,) * heads)
    carries = step(qi, carries, True)
    for h, (_, l, acc) in enumerate(carries):
        o_ref[:, h * 128:(h + 1) * 128] = (acc / l).astype(o_ref.dtype)


def mla_attn(q, k, v, *, batch, seq, tq, heads):
    nq = seq // tq
    q4 = q.reshape(MLA_HEADS, batch, seq, 256)
    k4 = k.reshape(MLA_HEADS, batch, seq, 256)
    v4 = v.reshape(MLA_HEADS, batch, seq, 128)
    return pl.pallas_call(
        functools.partial(_mla_attn_kernel, tq=tq, heads=heads),
        out_shape=jax.ShapeDtypeStruct((batch * seq, MLA_HEADS * 128), BF16),
        grid=(MLA_HEADS // heads, batch, nq),
        in_specs=[
            pl.BlockSpec((heads, None, tq, 256), lambda h, b, i: (h, b, i, 0)),
            pl.BlockSpec((heads, None, seq, 256), lambda h, b, i: (h, b, 0, 0)),
            pl.BlockSpec((heads, None, seq, 128), lambda h, b, i: (h, b, 0, 0)),
        ],
        out_specs=pl.BlockSpec((tq, heads * 128), lambda h, b, i: (b * nq + i, h)),
        compiler_params=_cp(("arbitrary", "arbitrary", "arbitrary")),
    )(q4, k4, v4)


def _sb_attn_kernel(q_ref, k_ref, v_ref, u_ref, o_ref, *, tq, heads):
    qi = pl.program_id(2)
    u = u_ref[...]

    def head_step(h, kt, carry, diag):
        run, acc = carry
        k0 = pl.multiple_of(kt * tq, tq)
        k = k_ref[h, pl.ds(k0, tq), :]
        v = v_ref[h, pl.ds(k0, tq), :]
        z = _dot_nt(q_ref[h], k)
        l1m = -(jnp.maximum(z, 0.0) + jnp.log(1.0 + jnp.exp(-jnp.abs(z))))
        if diag:
            row = lax.broadcasted_iota(jnp.int32, (tq, tq), 0)
            col = lax.broadcasted_iota(jnp.int32, (tq, tq), 1)
            strict = col < row
            l1m_m = jnp.where(strict, l1m, 0.0)
        else:
            l1m_m = l1m
        hi = l1m_m.astype(BF16)
        lo = (l1m_m - hi.astype(F32)).astype(BF16)
        between = _dot(hi, u) + _dot(lo, u)
        a = jnp.exp(z + l1m + between + run)
        if diag:
            a = jnp.where(strict, a, 0.0)
        acc = acc + _dot(a.astype(BF16), v)
        run = run + between[:, 0:1] + l1m_m[:, 0:1]
        return run, acc

    def step(kt, carries, diag):
        return tuple(head_step(h, kt, carries[h], diag) for h in range(heads))

    init = (jnp.zeros((tq, 1), F32), jnp.zeros((tq, 128), F32))
    carries = step(qi, (init,) * heads, True)

    def more(c):
        j, carries = c
        top = carries[0][0]
        for run, _ in carries[1:]:
            top = jnp.maximum(top, run)
        return (j < qi) & (jnp.max(top) > SB_UNDERFLOW_LOG)

    def body(c):
        j, carries = c
        return j + 1, step(qi - 1 - j, carries, False)

    _, carries = lax.while_loop(more, body, (jnp.int32(0), carries))
    for h, (_, acc) in enumerate(carries):
        o_ref[:, h * HEAD_DIM:(h + 1) * HEAD_DIM] = acc.astype(o_ref.dtype)


def sb_attn(hm, u, *, head0, batch, seq, tq, heads):
    nq = seq // tq
    assert head0 % heads == 0 and SB_HEADS % heads == 0
    hm4 = hm.reshape(hm.shape[0], batch, seq, HEAD_DIM)
    blk0 = head0 // heads
    per_part = SB_HEADS // heads
    return pl.pallas_call(
        functools.partial(_sb_attn_kernel, tq=tq, heads=heads),
        out_shape=jax.ShapeDtypeStruct((batch * seq, SB_HEADS * HEAD_DIM), BF16),
        grid=(per_part, batch, nq),
        in_specs=[
            pl.BlockSpec((heads, None, tq, HEAD_DIM), lambda h, b, i: (blk0 + h, b, i, 0)),
            pl.BlockSpec((heads, None, seq, HEAD_DIM), lambda h, b, i: (blk0 + per_part + h, b, 0, 0)),
            pl.BlockSpec((heads, None, seq, HEAD_DIM), lambda h, b, i: (blk0 + 2 * per_part + h, b, 0, 0)),
            pl.BlockSpec((tq, tq), lambda h, b, i: (0, 0)),
        ],
        out_specs=pl.BlockSpec((tq, heads * HEAD_DIM), lambda h, b, i: (b * nq + i, h)),
        compiler_params=_cp(("arbitrary", "arbitrary", "arbitrary")),
    )(hm4, hm4, hm4, u)


def _nsa_cmp_kernel(c_ref, w1_ref, pe_ref, w2_ref, o_ref, *, nc):
    c = c_ref[...]
    half = NSA_CMP_STRIDE * HEAD_DIM
    a1 = _dot(c, w1_ref[0:half, :])
    a2 = _dot(c, w1_ref[half:2 * half, :])
    pc = _dot(pe_ref[...], w1_ref[...])[0:1, :]
    pre = a1 + pltpu.roll(a2, nc - 1, 0) + pc
    act = 0.5 * pre * (1.0 + jnp.tanh(0.7978845608028654 * (pre + 0.044715 * (pre * pre * pre))))
    o_ref[...] = _dot(act.astype(BF16), w2_ref[...]).astype(BF16)


def nsa_compress(cmp_heads, w1, pe, w2, *, batch, seq):
    nc = seq // NSA_CMP_STRIDE
    head0 = 0
    hm4 = cmp_heads.reshape(4, batch, nc, NSA_CMP_STRIDE * HEAD_DIM)
    return pl.pallas_call(
        functools.partial(_nsa_cmp_kernel, nc=nc),
        out_shape=jax.ShapeDtypeStruct((4, batch, nc, HEAD_DIM), BF16),
        grid=(4, batch),
        in_specs=[
            pl.BlockSpec((None, None, nc, NSA_CMP_STRIDE * HEAD_DIM), lambda c, b: (head0 + c, b, 0, 0)),
            pl.BlockSpec((None, NSA_CMP_LEN * HEAD_DIM, HEAD_DIM), lambda c, b: (c // 2, 0, 0)),
            pl.BlockSpec((None, 8, NSA_CMP_LEN * HEAD_DIM), lambda c, b: (c // 2, 0, 0)),
            pl.BlockSpec((None, HEAD_DIM, HEAD_DIM), lambda c, b: (c // 2, 0, 0)),
        ],
        out_specs=pl.BlockSpec((None, None, nc, HEAD_DIM), lambda c, b: (c, b, 0, 0)),
        compiler_params=_cp(("arbitrary", "arbitrary")),
    )(hm4, w1, pe, w2)


def _nsa_attn_kernel(q_ref, qa_ref, kc_ref, vc_ref, ks_ref, vs_ref, kw_ref, vw_ref, g_ref, ov_ref, e_ref,
                     kpos_ref, cpos_ref, o_ref, *, tq, tk, seq, n_sel, n_top):
    qi = pl.program_id(2)
    t0 = qi * tq
    rows = NSA_HG * tq
    nc = seq // NSA_CMP_STRIDE
    q = jnp.concatenate([q_ref[...].reshape(rows, HEAD_DIM), qa_ref[...]], axis=1)

    rid = lax.broadcasted_iota(jnp.int32, (rows, 1), 0)
    trow = t0 + lax.bitwise_and(rid, tq - 1)

    def masked_softmax(s, valid):
        sm = jnp.where(valid, s, NEG)
        m = jnp.max(sm, axis=1, keepdims=True)
        e = jnp.where(valid, jnp.exp(sm - m), 0.0)
        d = jnp.sum(e, axis=1, keepdims=True)
        return e * (1.0 / jnp.where(d > 0.0, d, 1.0))

    n_i = lax.broadcasted_iota(jnp.int32, (1, nc), 1)
    end = n_i * NSA_CMP_STRIDE + (NSA_CMP_LEN - 1)
    s_c = _dot_nt(q, jnp.concatenate([kc_ref[...], cpos_ref[...]], axis=1))
    p_c = masked_softmax(s_c, end <= trow)
    o_c = _dot(p_c.astype(BF16), vc_ref[...])

    psum = p_c[0:tq] + p_c[tq:2 * tq] + p_c[2 * tq:3 * tq] + p_c[3 * tq:4 * tq]
    p_hi = psum.astype(BF16)
    p_lo = (psum - p_hi.astype(F32)).astype(BF16)
    ov = ov_ref[...]
    imp = _dot(p_hi, ov) + _dot(p_lo, ov)
    cur = lax.shift_right_logical(t0 + lax.broadcasted_iota(jnp.int32, (tq, 1), 0),
                                  int(np.log2(NSA_SEL_LEN)))
    blk = lax.broadcasted_iota(jnp.int32, (tq, n_sel), 1)
    forced = (blk == 0) | (blk == cur) | (blk == cur - 1)
    key = jnp.where(blk > cur, -BIG, jnp.where(forced, BIG, imp))
    rank = jnp.zeros((tq, n_sel), F32)
    lane1 = lax.broadcasted_iota(jnp.int32, (1, n_sel), 1)
    for i in range(n_sel):
        vi = key[:, i:i + 1]
        tie = jnp.where(lane1 > i, 1.0, 0.0)
        rank = rank + jnp.where(vi > key, 1.0, jnp.where(vi == key, tie, 0.0))
    selm = jnp.where((rank < float(n_top)) & (blk <= cur), 1.0, 0.0).astype(BF16)

    def sel_step(kt, carry, diag):
        m, l, acc = carry
        k0 = pl.multiple_of(kt * tk, tk)
        kk = jnp.concatenate([ks_ref[pl.ds(k0, tk), :], kpos_ref[pl.ds(k0, tk), :]], axis=1)
        vv = vs_ref[pl.ds(k0, tk), :]
        mex = _dot(selm, e_ref[kt])
        mex4 = jnp.concatenate([mex] * NSA_HG, axis=0)
        s = _dot_nt(q, kk)
        valid = mex4 > 0.5
        if diag:
            spos = k0 + lax.broadcasted_iota(jnp.int32, (1, tk), 1)
            valid = valid & (spos <= trow)
        sm = jnp.where(valid, s, NEG)
        m_new = jnp.maximum(m, jnp.max(sm, axis=1, keepdims=True))
        alpha = jnp.exp(m - m_new)
        p = jnp.exp((sm - m_new).astype(BF16))
        l = alpha * l + jnp.sum(p.astype(F32), axis=1, keepdims=True)
        acc = alpha * acc + _dot(p, vv)
        return m_new, l, acc

    kt_last = t0 // tk
    init = (jnp.full((rows, 1), NEG, F32), jnp.zeros((rows, 1), F32), jnp.zeros((rows, HEAD_DIM), F32))
    carry = lax.fori_loop(0, kt_last, lambda kt, c: sel_step(kt, c, False), init)
    _, l_s, acc_s = sel_step(kt_last, carry, True)
    o_s = acc_s * (1.0 / l_s)

    wk = NSA_WINDOW + tq
    ks0 = pl.multiple_of(jnp.maximum(t0 - NSA_WINDOW, 0), tq)
    kw = jnp.concatenate([kw_ref[pl.ds(ks0, wk), :], kpos_ref[pl.ds(ks0, wk), :]], axis=1)
    vw = vw_ref[pl.ds(ks0, wk), :]
    wpos = ks0 + lax.broadcasted_iota(jnp.int32, (1, wk), 1)
    dw = trow - wpos
    s_w = _dot_nt(q, kw)
    p_w = masked_softmax(s_w, (dw >= 0) & (dw < NSA_WINDOW))
    o_w = _dot(p_w.astype(BF16), vw)

    gt = g_ref[...]
    for hg in range(NSA_HG):
        sl = slice(hg * tq, (hg + 1) * tq)
        o = (gt[:, 3 * hg:3 * hg + 1] * o_c[sl] + gt[:, 3 * hg + 1:3 * hg + 2] * o_s[sl]
             + gt[:, 3 * hg + 2:3 * hg + 3] * o_w[sl])
        o_ref[:, hg * HEAD_DIM:(hg + 1) * HEAD_DIM] = o.astype(o_ref.dtype)


def nsa_attn(hm, cmp, gates, consts, *, q_head0, kv_head0, batch, seq, tq, tk):
    ov, e, qa, kpos, cpos = consts
    nq = seq // tq
    nc = seq // NSA_CMP_STRIDE
    n_sel = seq // NSA_SEL_LEN
    n_top = min(NSA_TOPK, n_sel)
    assert tk % tq == 0 and seq % tk == 0 and seq >= NSA_WINDOW + tq and NSA_WINDOW % tq == 0
    hm4 = hm.reshape(hm.shape[0], batch, seq, HEAD_DIM)
    kv_spec = lambda off: pl.BlockSpec((None, None, seq, HEAD_DIM),
                                       lambda b, g, i: (kv_head0 + off + g, b, 0, 0))
    return pl.pallas_call(
        functools.partial(_nsa_attn_kernel, tq=tq, tk=tk, seq=seq, n_sel=n_sel, n_top=n_top),
        out_shape=jax.ShapeDtypeStruct((batch * seq, NSA_HEADS * HEAD_DIM), BF16),
        grid=(batch, NSA_GROUPS, nq),
        in_specs=[
            pl.BlockSpec((NSA_HG, None, tq, HEAD_DIM), lambda b, g, i: (q_head0 // NSA_HG + g, b, i, 0)),
            pl.BlockSpec((None, NSA_HG * tq, 128), lambda b, g, i: (g, 0, 0)),
            pl.BlockSpec((None, None, nc, HEAD_DIM), lambda b, g, i: (g, b, 0, 0)),
            pl.BlockSpec((None, None, nc, HEAD_DIM), lambda b, g, i: (2 + g, b, 0, 0)),
            kv_spec(4), kv_spec(6), kv_spec(8), kv_spec(10),
            pl.BlockSpec((tq, 128), lambda b, g, i: (b * nq + i, g)),
            pl.BlockSpec((nc, n_sel), lambda b, g, i: (0, 0)),
            pl.BlockSpec((seq // tk, n_sel, tk), lambda b, g, i: (0, 0, 0)),
            pl.BlockSpec((seq, 128), lambda b, g, i: (0, 0)),
            pl.BlockSpec((nc, 128), lambda b, g, i: (0, 0)),
        ],
        out_specs=pl.BlockSpec((tq, NSA_HG * HEAD_DIM), lambda b, g, i: (b * nq + i, g)),
        compiler_params=_cp(("arbitrary", "arbitrary", "arbitrary")),
    )(hm4, qa, cmp, cmp, hm4, hm4, hm4, hm4, gates, ov, e, kpos, cpos)


def _merge_kernel(x_ref, oa_ref, ob_ref, oc_ref, wm_ref, wb_ref, bm_ref, y_ref, xb_ref):
    @pl.when(pl.program_id(1) == 0)
    def _():
        xb_ref[...] = x_ref[...].astype(BF16)

    xb = xb_ref[...]
    acc = None
    for br, o_ref in enumerate((oa_ref, ob_ref, oc_ref)):
        gate = jax.nn.sigmoid(_dot(xb, wm_ref[br]) + bm_ref[br])
        term = gate * _dot(o_ref[...], wb_ref[br])
        acc = term if acc is None else acc + term
    y_ref[...] = acc.astype(y_ref.dtype)


def merge_branches(x, o_a, o_b, o_c, wm, wb, bm, *, tm, tn):
    t = x.shape[0]
    o_spec = pl.BlockSpec((tm, BRANCH_WIDTH), lambda i, j: (i, 0))
    return pl.pallas_call(
        _merge_kernel,
        out_shape=jax.ShapeDtypeStruct((t, D_MODEL), BF16),
        grid=(t // tm, D_MODEL // tn),
        in_specs=[
            pl.BlockSpec((tm, D_MODEL), lambda i, j: (i, 0)),
            o_spec, o_spec, o_spec,
            pl.BlockSpec((N_BRANCH, D_MODEL, tn), lambda i, j: (0, 0, j)),
            pl.BlockSpec((N_BRANCH, BRANCH_WIDTH, tn), lambda i, j: (0, 0, j)),
            pl.BlockSpec((N_BRANCH, 1, tn), lambda i, j: (0, 0, j)),
        ],
        out_specs=pl.BlockSpec((tm, tn), lambda i, j: (i, j)),
        scratch_shapes=[pltpu.VMEM((tm, D_MODEL), BF16)],
        compiler_params=_cp(("arbitrary", "arbitrary")),
    )(x, o_a, o_b, o_c, wm, wb, bm)


def _out_ln_kernel(y_ref, w_ref, x_ref, g_ref, b_ref, o_ref, *, alpha):
    h = _dot(y_ref[...], w_ref[...])
    o_ref[...] = _layer_norm(alpha * x_ref[...] + h, g_ref[...], b_ref[...])


def out_ln(y, w, x, g, b, *, alpha, tm):
    t = x.shape[0]
    return pl.pallas_call(
        functools.partial(_out_ln_kernel, alpha=alpha),
        out_shape=jax.ShapeDtypeStruct((t, D_MODEL), F32),
        grid=(t // tm,),
        in_specs=[
            pl.BlockSpec((tm, D_MODEL), lambda i: (i, 0)),
            pl.BlockSpec((D_MODEL, D_MODEL), lambda i: (0, 0)),
            pl.BlockSpec((tm, D_MODEL), lambda i: (i, 0)),
            pl.BlockSpec((1, D_MODEL), lambda i: (0, 0)),
            pl.BlockSpec((1, D_MODEL), lambda i: (0, 0)),
        ],
        out_specs=pl.BlockSpec((tm, D_MODEL), lambda i: (i, 0)),
        compiler_params=_cp(("arbitrary",)),
    )(y, w, x, g, b)


def _mem_attn_kernel(x_ref, wq_ref, k_ref, v_ref, wo_ref, g_ref, b_ref, o_ref, *, alpha):
    x = x_ref[...]
    q = _dot(x.astype(BF16), wq_ref[...]) * (HEAD_DIM ** -0.5)
    outs = []
    for h in range(MEM_HEADS):
        qh = q[:, h * HEAD_DIM:(h + 1) * HEAD_DIM].astype(BF16)
        s = _dot_nt(qh, k_ref[h])
        m = jnp.max(s, axis=1, keepdims=True)
        e = jnp.exp(s - m)
        p = e * (1.0 / jnp.sum(e, axis=1, keepdims=True))
        outs.append(_dot(p.astype(BF16), v_ref[h]).astype(BF16))
    o = jnp.concatenate(outs, axis=1)
    h_out = _dot(o, wo_ref[...])
    o_ref[...] = _layer_norm(alpha * x + h_out, g_ref[...], b_ref[...])


def mem_attn_ln(x, wq, kv, wo, g, b, *, alpha, seq, mem_len, tm):
    t = x.shape[0]
    per_b = seq // tm
    kv4 = kv.reshape(2 * MEM_HEADS, t // seq, mem_len, HEAD_DIM)
    width = MEM_HEADS * HEAD_DIM
    return pl.pallas_call(
        functools.partial(_mem_attn_kernel, alpha=alpha),
        out_shape=jax.ShapeDtypeStruct((t, D_MODEL), F32),
        grid=(t // tm,),
        in_specs=[
            pl.BlockSpec((tm, D_MODEL), lambda i: (i, 0)),
            pl.BlockSpec((D_MODEL, width), lambda i: (0, 0)),
            pl.BlockSpec((MEM_HEADS, None, mem_len, HEAD_DIM), lambda i: (0, i // per_b, 0, 0)),
            pl.BlockSpec((MEM_HEADS, None, mem_len, HEAD_DIM), lambda i: (1, i // per_b, 0, 0)),
            pl.BlockSpec((width, D_MODEL), lambda i: (0, 0)),
            pl.BlockSpec((1, D_MODEL), lambda i: (0, 0)),
            pl.BlockSpec((1, D_MODEL), lambda i: (0, 0)),
        ],
        out_specs=pl.BlockSpec((tm, D_MODEL), lambda i: (i, 0)),
        compiler_params=_cp(("arbitrary",)),
    )(x, wq, kv4, kv4, wo, g, b)


def _router_kernel(x_ref, w_ref, b_ref, mask_ref, wsel_ref, xp_ref):
    x = x_ref[...]
    xp_ref[...] = _pack_bf16_pairs(x)
    xh = x.astype(BF16)
    xl = (x - xh.astype(F32)).astype(BF16)
    logits = _dot(xh, w_ref[0]) + _dot(xh, w_ref[1]) + _dot(xl, w_ref[0]) + b_ref[...]
    tm = logits.shape[0]
    lane = lax.broadcasted_iota(jnp.int32, (tm, N_EXPERTS), 1)
    work = logits
    hots, vals = [], []
    for _ in range(TOP_K):
        m = jnp.max(work, axis=1, keepdims=True)
        idx = jnp.min(jnp.where(work == m, lane, N_EXPERTS), axis=1, keepdims=True)
        hot = lane == idx
        hots.append(hot)
        vals.append(m)
        work = jnp.where(hot, -jnp.inf, work)
    es = [jnp.exp(v - vals[0]) for v in vals]
    inv = 1.0 / (es[0] + es[1] + es[2] + es[3])
    mask = jnp.zeros((tm, N_EXPERTS), F32)
    wsel = jnp.zeros((tm, N_EXPERTS), F32)
    for hot, e in zip(hots, es):
        mask = mask + jnp.where(hot, 1.0, 0.0)
        wsel = wsel + jnp.where(hot, e * inv, 0.0)
    mask_ref[...] = mask
    wsel_ref[...] = wsel


def router(x, w_hl, b, *, tm):
    t = x.shape[0]
    return pl.pallas_call(
        _router_kernel,
        out_shape=(jax.ShapeDtypeStruct((t, N_EXPERTS), F32), jax.ShapeDtypeStruct((t, N_EXPERTS), F32),
                   jax.ShapeDtypeStruct((t, D_MODEL // 2), jnp.uint32)),
        grid=(t // tm,),
        in_specs=[
            pl.BlockSpec((tm, D_MODEL), lambda i: (i, 0)),
            pl.BlockSpec((2, D_MODEL, N_EXPERTS), lambda i: (0, 0, 0)),
            pl.BlockSpec((1, N_EXPERTS), lambda i: (0, 0)),
        ],
        out_specs=(pl.BlockSpec((tm, N_EXPERTS), lambda i: (i, 0)),
                   pl.BlockSpec((tm, N_EXPERTS), lambda i: (i, 0)),
                   pl.BlockSpec((tm, D_MODEL // 2), lambda i: (i, 0))),
        compiler_params=_cp(("arbitrary",)),
    )(x, w_hl, b)


def sc_gather_rows(table, idx, *, chunk):
    n = idx.shape[0]
    d = table.shape[1]
    workers = SC_CORES_V7X * SC_SUBCORES_V7X
    per_w = n // workers
    assert n % (workers * chunk) == 0 and chunk % 8 == 0 and chunk <= 128
    mesh = plsc.VectorSubcoreMesh(core_axis_name="c", subcore_axis_name="s")

    @functools.partial(
        pl.kernel, mesh=mesh,
        out_type=jax.ShapeDtypeStruct((n, d), table.dtype),
        scratch_types=[pltpu.VMEM((chunk,), jnp.int32), pltpu.VMEM((chunk, d), table.dtype),
                       pltpu.SemaphoreType.DMA],
    )
    def gather(table_hbm, idx_hbm, out_hbm, idx_v, rows_v, sem):
        wid = lax.axis_index("s") * SC_CORES_V7X + lax.axis_index("c")
        base = wid * per_w

        @pl.loop(0, per_w // chunk)
        def _(j):
            off = pl.multiple_of(base + j * chunk, 8)
            pltpu.sync_copy(idx_hbm.at[pl.ds(off, chunk)], idx_v)
            pltpu.async_copy(table_hbm.at[idx_v], rows_v, sem).wait()
            pltpu.sync_copy(rows_v, out_hbm.at[pl.ds(off, chunk)])

    return gather(table, idx)


def _experts_kernel(be_ref, x_ref, wg_ref, bg_ref, wu_ref, bu_ref, wd_ref, bd_ref, y_ref,
                    wgb_ref, wub_ref, wdb_ref):
    i = pl.program_id(0)
    prev = be_ref[jnp.maximum(i - 1, 0)]

    @pl.when((i == 0) | (be_ref[i] != prev))
    def _():
        wgb_ref[...] = wg_ref[...].astype(BF16)
        wub_ref[...] = wu_ref[...].astype(BF16)
        wdb_ref[...] = wd_ref[...].astype(BF16)

    x_lo, x_hi = _unpack_bf16_pairs(x_ref[...])
    xb = jnp.concatenate([x_lo.astype(BF16), x_hi.astype(BF16)], axis=1)
    g = jnp.minimum(_dot(xb, wgb_ref[...]) + bg_ref[...], SWIGLU_LIMIT)
    u = jnp.clip(_dot(xb, wub_ref[...]) + bu_ref[...], -SWIGLU_LIMIT, SWIGLU_LIMIT)
    hdn = (u + 1.0) * (g * jax.nn.sigmoid(SWIGLU_ALPHA * g))
    y_ref[...] = _pack_bf16_pairs(_dot(hdn.astype(BF16), wdb_ref[...]) + bd_ref[...])


def experts(x_rows, blk_e, wg, bg, wu, bu, wd, bd, *, layer, bm):
    n_rows, dp = x_rows.shape
    d, f = wg.shape[2], wg.shape[3]
    w_spec = lambda shape: pl.BlockSpec((None, None) + shape, lambda i, be: (layer, be[i], 0, 0))
    grid_spec = pltpu.PrefetchScalarGridSpec(
        num_scalar_prefetch=1,
        grid=(n_rows // bm,),
        in_specs=[
            pl.BlockSpec((bm, dp), lambda i, be: (i, 0)),
            w_spec((d, f)), w_spec((1, f)), w_spec((d, f)), w_spec((1, f)), w_spec((f, d)), w_spec((1, d)),
        ],
        out_specs=pl.BlockSpec((bm, dp), lambda i, be: (i, 0)),
        scratch_shapes=[pltpu.VMEM((d, f), BF16), pltpu.VMEM((d, f), BF16), pltpu.VMEM((f, d), BF16)],
    )
    return pl.pallas_call(
        _experts_kernel,
        out_shape=jax.ShapeDtypeStruct((n_rows, dp), jnp.uint32),
        grid_spec=grid_spec,
        compiler_params=_cp(("arbitrary",)),
    )(blk_e, x_rows, wg, bg, wu, bu, wd, bd)


def _moe_ln_kernel(x_ref, y_ref, w_ref, g_ref, b_ref, o_ref, *, alpha):
    w = w_ref[...]
    y_lo = y_hi = None
    for k in range(TOP_K):
        lo, hi = _unpack_bf16_pairs(y_ref[k])
        wk = w[:, k:k + 1]
        y_lo = wk * lo if y_lo is None else y_lo + wk * lo
        y_hi = wk * hi if y_hi is None else y_hi + wk * hi
    y = jnp.concatenate([y_lo, y_hi], axis=1)
    o_ref[...] = _layer_norm(alpha * x_ref[...] + y, g_ref[...], b_ref[...])


def moe_ln(x, y4, w4p, g, b, *, alpha, tm):
    t = x.shape[0]
    return pl.pallas_call(
        functools.partial(_moe_ln_kernel, alpha=alpha),
        out_shape=jax.ShapeDtypeStruct((t, D_MODEL), F32),
        grid=(t // tm,),
        in_specs=[
            pl.BlockSpec((tm, D_MODEL), lambda i: (i, 0)),
            pl.BlockSpec((TOP_K, tm, D_MODEL // 2), lambda i: (0, i, 0)),
            pl.BlockSpec((tm, 128), lambda i: (i, 0)),
            pl.BlockSpec((1, D_MODEL), lambda i: (0, 0)),
            pl.BlockSpec((1, D_MODEL), lambda i: (0, 0)),
        ],
        out_specs=pl.BlockSpec((tm, D_MODEL), lambda i: (i, 0)),
        compiler_params=_cp(("arbitrary",)),
    )(x, y4, w4p, g, b)


def _rope_tables(seq):
    inv = np.asarray(ROPE_THETA ** (-np.arange(0, MLA_ROPE, 2) / MLA_ROPE), np.float32)
    ang = jnp.arange(seq, dtype=F32)[:, None] * jnp.asarray(inv)[None, :]
    cos, sin = jnp.cos(ang), jnp.sin(ang)
    zeros = jnp.zeros((seq, 128 - MLA_ROPE), F32)
    return (jnp.concatenate([cos, cos, zeros], axis=1), jnp.concatenate([-sin, sin, zeros], axis=1))


def _nsa_constants(seq, tq, tk):
    nc = seq // NSA_CMP_STRIDE
    qa = np.zeros((NSA_GROUPS, NSA_HG * tq, 128), np.float32)
    for g in range(NSA_GROUPS):
        for hg in range(NSA_HG):
            slope = 2.0 ** (-8.0 * (g * NSA_HG + hg + 1) / NSA_HEADS)
            qa[g, hg * tq:(hg + 1) * tq, 0] = slope * NSA_SEL_LEN
            qa[g, hg * tq:(hg + 1) * tq, 1] = slope
            qa[g, hg * tq:(hg + 1) * tq, 2] = slope * NSA_CMP_STRIDE
            qa[g, hg * tq:(hg + 1) * tq, 3] = slope * (NSA_CMP_LEN - 1) / 2.0
    kpos = np.zeros((seq, 128), np.float32)
    kpos[:, 0] = np.arange(seq) // NSA_SEL_LEN
    kpos[:, 1] = np.arange(seq) % NSA_SEL_LEN
    cpos = np.zeros((nc, 128), np.float32)
    cpos[:, 2] = np.arange(nc)
    cpos[:, 3] = 1.0
    for arr in (qa, kpos, cpos):
        assert np.array_equal(arr.astype(BF16).astype(np.float32), arr)
    n_cmp = (seq - NSA_CMP_LEN) // NSA_CMP_STRIDE + 1
    n_sel = seq // NSA_SEL_LEN
    cs = np.arange(nc) * NSA_CMP_STRIDE
    ss = np.arange(n_sel) * NSA_SEL_LEN
    ov = np.clip(np.minimum(cs[:, None] + NSA_CMP_LEN, ss[None, :] + NSA_SEL_LEN)
                 - np.maximum(cs[:, None], ss[None, :]), 0, None) / NSA_CMP_LEN
    ov[n_cmp:] = 0.0
    e = (np.arange(seq)[None, :] // NSA_SEL_LEN == np.arange(n_sel)[:, None]).astype(np.float32)
    e = e.reshape(n_sel, seq // tk, tk).transpose(1, 0, 2)
    return tuple(jnp.asarray(a, BF16) for a in (ov, e, qa, kpos, cpos))


def _pad_cols(w, width):
    return jnp.pad(w, ((0, 0), (0, width - w.shape[1])))


def _swap_halves(w):
    half = w.shape[1] // 2
    return jnp.concatenate([w[:, half:], w[:, :half]], axis=1)


def _layer_weights(w_in, w_q_up, w_kv_up):
    kr = w_in[:, OFF_KR:OFF_NSA_Q]
    gate = w_in[:, OFF_NSA_GATE:OFF_SB]
    per_g = NSA_HG * 3
    w_mla = jnp.concatenate([
        w_in[:, OFF_CQ:OFF_KR],
        _pad_cols(kr, 128), _pad_cols(_swap_halves(kr), 128),
        _pad_cols(gate[:, :per_g], 128), _pad_cols(gate[:, per_g:], 128)], axis=1).astype(BF16)
    wq = w_q_up.reshape(MLA_Q_RANK, MLA_HEADS, MLA_NOPE + MLA_ROPE)
    rope = wq[:, :, MLA_NOPE:]
    rope_sw = jnp.concatenate([rope[:, :, MLA_ROPE // 2:], rope[:, :, :MLA_ROPE // 2]], axis=2)
    pad = ((0, 0), (0, 0), (0, 128 - MLA_ROPE))
    wq3 = jnp.concatenate([
        wq[:, :, :MLA_NOPE].reshape(MLA_Q_RANK, -1),
        jnp.pad(rope, pad).reshape(MLA_Q_RANK, -1),
        jnp.pad(rope_sw, pad).reshape(MLA_Q_RANK, -1)], axis=1).astype(BF16)
    wkv = w_kv_up.reshape(MLA_KV_RANK, MLA_HEADS, 2, 128)
    wkv = jnp.concatenate([wkv[:, :, 0].reshape(MLA_KV_RANK, -1),
                           wkv[:, :, 1].reshape(MLA_KV_RANK, -1)], axis=1).astype(BF16)
    w_heads = w_in[:, OFF_NSA_Q:OFF_NSA_GATE]
    w_heads = jnp.concatenate([w_heads, w_in[:, OFF_SB:OFF_MERGE]], axis=1).astype(BF16)
    wm = w_in[:, OFF_MERGE:].reshape(D_MODEL, N_BRANCH, D_MODEL).transpose(1, 0, 2).astype(BF16)
    return w_mla, wq3, wkv, w_heads, wm


def _forward(x, mem, w_in, mla_q_norm, mla_w_q_up, mla_kv_norm, mla_w_kv_up,
             nsa_pe_k, nsa_pe_v, nsa_w1_k, nsa_w1_v, nsa_w2_k, nsa_w2_v,
             w_branch, b_merge, w_out, ln_mix_g, ln_mix_b,
             mem_w_q, mem_w_k, mem_w_v, mem_w_o, ln_mem_g, ln_mem_b,
             moe_w_router, moe_b_router, moe_w_gate, moe_b_gate, moe_w_up, moe_b_up,
             moe_w_down, moe_b_down, ln_moe_g, ln_moe_b):
    batch, seq, _ = x.shape
    mem_len = mem.shape[1]
    depth = w_in.shape[0]
    t = batch * seq
    alpha = float((2 * depth) ** 0.25)
    bm = MOE_BLOCK_ROWS
    n_rows = t * TOP_K + N_EXPERTS * bm
    n_blocks = n_rows // bm

    tm_in = min(512, seq)
    tq_mla = min(512, seq)
    tq_sb = 256
    tq_nsa, tk_nsa = 128, 512
    tm_ln = 256
    sc_chunk = 64

    cos128, sin128 = _rope_tables(seq)
    nsa_consts = _nsa_constants(seq, tq_nsa, tk_nsa)
    u_sb = jnp.asarray(np.arange(tq_sb)[:, None] > np.arange(tq_sb)[None, :], BF16)
    n_qheads = NSA_HEADS
    n_kvheads = 3 * 2 * NSA_GROUPS
    head_scale = np.ones((1, (n_qheads + n_kvheads + 3 * SB_HEADS) * HEAD_DIM), np.float32)
    head_scale[:, :n_qheads * HEAD_DIM] = HEAD_DIM ** -0.5
    sb0 = n_qheads + n_kvheads
    head_scale[:, sb0 * HEAD_DIM:(sb0 + SB_HEADS) * HEAD_DIM] = HEAD_DIM ** -0.5
    head_scale = jnp.asarray(head_scale)
    ones_kv = jnp.ones((1, 2 * MEM_HEADS * HEAD_DIM), F32)

    b_gate4 = moe_b_gate.reshape(depth, N_EXPERTS, 1, D_EXPERT)
    b_up4 = moe_b_up.reshape(depth, N_EXPERTS, 1, D_EXPERT)
    b_down4 = moe_b_down.reshape(depth, N_EXPERTS, 1, D_MODEL)

    xf = x.reshape(t, D_MODEL)
    memf = mem.reshape(batch * mem_len, D_MODEL)
    row = lambda v: v.reshape(1, -1)

    for l in range(depth):
        w_mla, wq3, wkv, w_heads, wm = _layer_weights(w_in[l], mla_w_q_up[l], mla_w_kv_up[l])

        q_a, k_a, v_a, gates = mla_in(xf, w_mla, row(mla_q_norm[l]), row(mla_kv_norm[l]), wq3, wkv,
                                      cos128, sin128, seq=seq, tm=tm_ln)
        hm = proj_heads(xf, w_heads, head_scale, tm=tm_in, tn=512)
        o_a = mla_attn(q_a, k_a, v_a, batch=batch, seq=seq, tq=tq_mla, heads=2)
        w1 = jnp.stack([nsa_w1_k[l], nsa_w1_v[l]]).astype(BF16)
        pe = jnp.stack([nsa_pe_k[l], nsa_pe_v[l]]).reshape(2, 1, -1)
        pe = jnp.broadcast_to(pe, (2, 8, pe.shape[-1])).astype(BF16)
        w2 = jnp.stack([nsa_w2_k[l], nsa_w2_v[l]]).astype(BF16)
        cmp = nsa_compress(hm[n_qheads:n_qheads + 4], w1, pe, w2, batch=batch, seq=seq)
        o_b = nsa_attn(hm, cmp, gates, nsa_consts, q_head0=0, kv_head0=n_qheads,
                       batch=batch, seq=seq, tq=tq_nsa, tk=tk_nsa)
        o_c = sb_attn(hm, u_sb, head0=sb0, batch=batch, seq=seq, tq=tq_sb, heads=4)
        y = merge_branches(xf, o_a, o_b, o_c, wm, w_branch[l].astype(BF16),
                           b_merge[l].reshape(N_BRANCH, 1, D_MODEL), tm=tm_in, tn=512)
        xf = out_ln(y, w_out[l].astype(BF16), xf, row(ln_mix_g[l]), row(ln_mix_b[l]), alpha=alpha, tm=tm_ln)

        w_kv_mem = jnp.concatenate([mem_w_k[l], mem_w_v[l]], axis=1).astype(BF16)
        kv_mem = proj_heads(memf, w_kv_mem, ones_kv, tm=min(512, batch * mem_len), tn=512)
        xf = mem_attn_ln(xf, mem_w_q[l].astype(BF16), kv_mem, mem_w_o[l].astype(BF16),
                         row(ln_mem_g[l]), row(ln_mem_b[l]), alpha=alpha, seq=seq, mem_len=mem_len, tm=tm_ln)

        wr = moe_w_router[l]
        wr_hi = wr.astype(BF16)
        wr_lo = (wr - wr_hi.astype(F32)).astype(BF16)
        mask, wsel, x_packed = router(xf, jnp.stack([wr_hi, wr_lo]), row(moe_b_router[l]), tm=tm_in)
        before = jnp.cumsum(mask, axis=0) - mask
        counts = jnp.sum(mask, axis=0).astype(jnp.int32)
        padded = (counts + bm - 1) // bm * bm
        pad_end = jnp.cumsum(padded)
        pad_start = pad_end - padded
        slot = pad_start[None, :] + before.astype(jnp.int32)
        top_e = lax.top_k(mask, TOP_K)[1]
        pos4 = jnp.take_along_axis(slot, top_e, axis=1).astype(jnp.int32)
        w4 = jnp.take_along_axis(wsel, top_e, axis=1)
        blk_e = jnp.minimum(jnp.searchsorted(pad_end, jnp.arange(n_blocks) * bm, side='right'),
                            N_EXPERTS - 1).astype(jnp.int32)
        tok_ids = jnp.broadcast_to(jnp.arange(t, dtype=jnp.int32)[:, None], (t, TOP_K))
        row_tok = jnp.zeros((n_rows,), jnp.int32).at[pos4.reshape(-1)].set(
            tok_ids.reshape(-1), unique_indices=True)
        x_rows = sc_gather_rows(x_packed, row_tok, chunk=sc_chunk)
        y_rows = experts(x_rows, blk_e, moe_w_gate, b_gate4, moe_w_up, b_up4, moe_w_down, b_down4,
                         layer=l, bm=bm)
        y4 = sc_gather_rows(y_rows, pos4.T.reshape(-1), chunk=sc_chunk).reshape(TOP_K, t, D_MODEL // 2)
        xf = moe_ln(xf, y4, _pad_cols(w4, 128), row(ln_moe_g[l]), row(ln_moe_b[l]), alpha=alpha, tm=tm_ln)

    return xf.reshape(batch, seq, D_MODEL)


def kernel(x, mem, w_in, mla_q_norm, mla_w_q_up, mla_kv_norm, mla_w_kv_up, nsa_pe_k, nsa_pe_v, nsa_w1_k, nsa_w1_v, nsa_w2_k, nsa_w2_v, w_branch, b_merge, w_out, ln_mix_g, ln_mix_b, mem_w_q, mem_w_k, mem_w_v, mem_w_o, ln_mem_g, ln_mem_b, moe_w_router, moe_b_router, moe_w_gate, moe_b_gate, moe_w_up, moe_b_up, moe_w_down, moe_b_down, ln_moe_g, ln_moe_b):
    return _forward(x, mem, w_in, mla_q_norm, mla_w_q_up, mla_kv_norm, mla_w_kv_up,
                    nsa_pe_k, nsa_pe_v, nsa_w1_k, nsa_w1_v, nsa_w2_k, nsa_w2_v,
                    w_branch, b_merge, w_out, ln_mix_g, ln_mix_b,
                    mem_w_q, mem_w_k, mem_w_v, mem_w_o, ln_mem_g, ln_mem_b,
                    moe_w_router, moe_b_router, moe_w_gate, moe_b_gate, moe_w_up, moe_b_up,
                    moe_w_down, moe_b_down, ln_moe_g, ln_moe_b)
```

```python
import functools

import numpy as np
import jax
import jax.numpy as jnp
from jax import lax
from jax.experimental import pallas as pl
from jax.experimental.pallas import tpu as pltpu
from jax.experimental.pallas import tpu_sc as plsc

F32 = jnp.float32
BF16 = jnp.bfloat16

D_MODEL = 2048
HEAD_DIM = 128
MLA_HEADS = 8
MLA_Q_RANK = 512
MLA_KV_RANK = 256
MLA_NOPE = 128
MLA_ROPE = 64
ROPE_THETA = 10000.0
NSA_HEADS = 8
NSA_GROUPS = 2
NSA_HG = NSA_HEADS // NSA_GROUPS
NSA_CMP_LEN = 32
NSA_CMP_STRIDE = 16
NSA_SEL_LEN = 64
NSA_TOPK = 16
NSA_WINDOW = 512
SB_HEADS = 8
MEM_HEADS = 4
N_EXPERTS = 32
TOP_K = 4
D_EXPERT = 512
SWIGLU_LIMIT = 7.0
SWIGLU_ALPHA = 1.702
N_BRANCH = 3
BRANCH_WIDTH = 1024
LN_EPS = 1e-5
RMS_EPS = 1e-6
NEG = -1e30
BIG = 1e30
SB_UNDERFLOW_LOG = -100.0

OFF_CQ = 0
OFF_CKV = 512
OFF_KR = 768
OFF_NSA_Q = 832
OFF_NSA_KV = 1856
OFF_NSA_GATE = 3392
OFF_SB = 3416
OFF_MERGE = 6488

VMEM_LIMIT_V7X = 56 * 1024 * 1024
MOE_BLOCK_ROWS = 256
SC_CORES_V7X = 2
SC_SUBCORES_V7X = 16


def _cp(sem, vmem=VMEM_LIMIT_V7X):
    return pltpu.CompilerParams(dimension_semantics=sem, vmem_limit_bytes=vmem)


def _dot(a, b):
    return jnp.dot(a, b, preferred_element_type=F32)


def _dot_nt(a, b):
    return lax.dot_general(a, b, (((1,), (1,)), ((), ())), preferred_element_type=F32)


def _layer_norm(z, g, b):
    mu = jnp.mean(z, axis=-1, keepdims=True)
    zc = z - mu
    var = jnp.mean(zc * zc, axis=-1, keepdims=True)
    return zc * lax.rsqrt(var + LN_EPS) * g + b


def _rms_norm(z, g):
    return z * lax.rsqrt(jnp.mean(z * z, axis=-1, keepdims=True) + RMS_EPS) * g


def _pack_bf16_pairs(z):
    n = z.shape[1] // 2
    bits = pltpu.bitcast(z.astype(BF16).astype(F32), jnp.uint32)
    return lax.shift_right_logical(bits[:, :n], jnp.uint32(16)) | (bits[:, n:] & jnp.uint32(0xFFFF0000))


def _unpack_bf16_pairs(w):
    lo = pltpu.bitcast(lax.shift_left(w, jnp.uint32(16)), F32)
    hi = pltpu.bitcast(w & jnp.uint32(0xFFFF0000), F32)
    return lo, hi


def _proj_heads_kernel(a_ref, w_ref, s_ref, o_ref, abf_ref, *, n_heads_per_tile):
    @pl.when(pl.program_id(1) == 0)
    def _():
        abf_ref[...] = a_ref[...].astype(BF16)

    acc = _dot(abf_ref[...], w_ref[...]) * s_ref[...]
    for c in range(n_heads_per_tile):
        o_ref[c] = acc[:, c * HEAD_DIM:(c + 1) * HEAD_DIM].astype(o_ref.dtype)


def proj_heads(a, w, scale, *, tm, tn):
    m, k = a.shape
    n = w.shape[1]
    hpt = tn // HEAD_DIM
    return pl.pallas_call(
        functools.partial(_proj_heads_kernel, n_heads_per_tile=hpt),
        out_shape=jax.ShapeDtypeStruct((n // HEAD_DIM, m, HEAD_DIM), BF16),
        grid=(m // tm, n // tn),
        in_specs=[
            pl.BlockSpec((tm, k), lambda i, j: (i, 0)),
            pl.BlockSpec((k, tn), lambda i, j: (0, j)),
            pl.BlockSpec((1, tn), lambda i, j: (0, j)),
        ],
        out_specs=pl.BlockSpec((hpt, tm, HEAD_DIM), lambda i, j: (j, i, 0)),
        scratch_shapes=[pltpu.VMEM((tm, k), BF16)],
        compiler_params=_cp(("arbitrary", "arbitrary")),
    )(a, w, scale)


def _mla_in_kernel(x_ref, w_ref, qg_ref, kg_ref, wq_ref, wkv_ref, cos_ref, sin_ref,
                   q_ref, k_ref, v_ref, g_ref):
    xb = x_ref[...].astype(BF16)
    h = _dot(xb, w_ref[...])
    cq = h[:, 0:512]
    ckv = h[:, 512:768]
    kr1 = h[:, 768:896]
    kr2 = h[:, 896:1024]
    g_ref[...] = jax.nn.sigmoid(h[:, 1024:1280])
    cos = cos_ref[...]
    sin = sin_ref[...]
    scale = (MLA_NOPE + MLA_ROPE) ** -0.5
    nq = _rms_norm(cq, qg_ref[...]).astype(BF16)
    q3 = _dot(nq, wq_ref[...])
    for hh in range(MLA_HEADS):
        lo, hi = hh * 128, (hh + 1) * 128
        q_ref[hh, :, 0:128] = (q3[:, lo:hi] * scale).astype(BF16)
        rot = q3[:, 1024 + lo:1024 + hi] * cos + q3[:, 2048 + lo:2048 + hi] * sin
        q_ref[hh, :, 128:256] = (rot * scale).astype(BF16)
    nkv = _rms_norm(ckv, kg_ref[...]).astype(BF16)
    kv = _dot(nkv, wkv_ref[...])
    krot = (kr1 * cos + kr2 * sin).astype(BF16)
    for hh in range(MLA_HEADS):
        lo, hi = hh * 128, (hh + 1) * 128
        k_ref[hh, :, 0:128] = kv[:, lo:hi].astype(BF16)
        k_ref[hh, :, 128:256] = krot
        v_ref[hh] = kv[:, 1024 + lo:1024 + hi].astype(BF16)


def mla_in(x, w_mla, qg, kg, wq3, wkv, cos128, sin128, *, seq, tm):
    t = x.shape[0]
    npos = seq // tm
    full = lambda shape: pl.BlockSpec(shape, lambda i: (0,) * len(shape))
    return pl.pallas_call(
        _mla_in_kernel,
        out_shape=(
            jax.ShapeDtypeStruct((MLA_HEADS, t, 256), BF16),
            jax.ShapeDtypeStruct((MLA_HEADS, t, 256), BF16),
            jax.ShapeDtypeStruct((MLA_HEADS, t, 128), BF16),
            jax.ShapeDtypeStruct((t, 256), F32),
        ),
        grid=(t // tm,),
        in_specs=[
            pl.BlockSpec((tm, D_MODEL), lambda i: (i, 0)),
            full((D_MODEL, 1280)),
            full((1, MLA_Q_RANK)),
            full((1, MLA_KV_RANK)),
            full((MLA_Q_RANK, 3072)),
            full((MLA_KV_RANK, 2048)),
            pl.BlockSpec((tm, 128), lambda i: (i % npos, 0)),
            pl.BlockSpec((tm, 128), lambda i: (i % npos, 0)),
        ],
        out_specs=(
            pl.BlockSpec((MLA_HEADS, tm, 256), lambda i: (0, i, 0)),
            pl.BlockSpec((MLA_HEADS, tm, 256), lambda i: (0, i, 0)),
            pl.BlockSpec((MLA_HEADS, tm, 128), lambda i: (0, i, 0)),
            pl.BlockSpec((tm, 256), lambda i: (i, 0)),
        ),
        compiler_params=_cp(("arbitrary",)),
    )(x, w_mla, qg, kg, wq3, wkv, cos128, sin128)


def _mla_attn_kernel(q_ref, k_ref, v_ref, o_ref, *, tq, heads):
    qi = pl.program_id(2)

    def head_step(h, kt, carry, diag):
        m, l, acc = carry
        k0 = pl.multiple_of(kt * tq, tq)
        k = k_ref[h, pl.ds(k0, tq), :]
        v = v_ref[h, pl.ds(k0, tq), :]
        s = _dot_nt(q_ref[h], k)
        if diag:
            row = lax.broadcasted_iota(jnp.int32, (tq, tq), 0)
            col = lax.broadcasted_iota(jnp.int32, (tq, tq), 1)
            s = jnp.where(col <= row, s, NEG)
        m_new = jnp.maximum(m, jnp.max(s, axis=1, keepdims=True))
        alpha = jnp.exp(m - m_new)
        p = jnp.exp((s - m_new).astype(BF16))
        l = alpha * l + jnp.sum(p.astype(F32), axis=1, keepdims=True)
        acc = alpha * acc + _dot(p, v)
        return m_new, l, acc

    def step(kt, carries, diag):
        return tuple(head_step(h, kt, carries[h], diag) for h in range(heads))

    init = (jnp.full((tq, 1), NEG, F32), jnp.zeros((tq, 1), F32), jnp.zeros((tq, 128), F32))
    carries = lax.fori_loop(0, qi, lambda kt, c: step(kt, c, False), (init,) * heads)
    carries = step(qi, carries, True)
    for h, (_, l, acc) in enumerate(carries):
        o_ref[:, h * 128:(h + 1) * 128] = (acc / l).astype(o_ref.dtype)


def mla_attn(q, k, v, *, batch, seq, tq, heads):
    nq = seq // tq
    q4 = q.reshape(MLA_HEADS, batch, seq, 256)
    k4 = k.reshape(MLA_HEADS, batch, seq, 256)
    v4 = v.reshape(MLA_HEADS, batch, seq, 128)
    return pl.pallas_call(
        functools.partial(_mla_attn_kernel, tq=tq, heads=heads),
        out_shape=jax.ShapeDtypeStruct((batch * seq, MLA_HEADS * 128), BF16),
        grid=(MLA_HEADS // heads, batch, nq),
        in_specs=[
            pl.BlockSpec((heads, None, tq, 256), lambda h, b, i: (h, b, i, 0)),
            pl.BlockSpec((heads, None, seq, 256), lambda h, b, i: (h, b, 0, 0)),
            pl.BlockSpec((heads, None, seq, 128), lambda h, b, i: (h, b, 0, 0)),
        ],
        out_specs=pl.BlockSpec((tq, heads * 128), lambda h, b, i: (b * nq + i, h)),
        compiler_params=_cp(("arbitrary", "arbitrary", "arbitrary")),
    )(q4, k4, v4)


def _sb_attn_kernel(q_ref, k_ref, v_ref, u_ref, o_ref, *, tq, heads):
    qi = pl.program_id(2)
    u = u_ref[...]

    def head_step(h, kt, carry, diag):
        run, acc = carry
        k0 = pl.multiple_of(kt * tq, tq)
        k = k_ref[h, pl.ds(k0, tq), :]
        v = v_ref[h, pl.ds(k0, tq), :]
        z = _dot_nt(q_ref[h], k)
        l1m = -(jnp.maximum(z, 0.0) + jnp.log(1.0 + jnp.exp(-jnp.abs(z))))
        if diag:
            row = lax.broadcasted_iota(jnp.int32, (tq, tq), 0)
            col = lax.broadcasted_iota(jnp.int32, (tq, tq), 1)
            strict = col < row
            l1m_m = jnp.where(strict, l1m, 0.0)
        else:
            l1m_m = l1m
        hi = l1m_m.astype(BF16)
        lo = (l1m_m - hi.astype(F32)).astype(BF16)
        between = _dot(hi, u) + _dot(lo, u)
        a = jnp.exp(z + l1m + between + run)
        if diag:
            a = jnp.where(strict, a, 0.0)
        acc = acc + _dot(a.astype(BF16), v)
        run = run + between[:, 0:1] + l1m_m[:, 0:1]
        return run, acc

    def step(kt, carries, diag):
        return tuple(head_step(h, kt, carries[h], diag) for h in range(heads))

    init = (jnp.zeros((tq, 1), F32), jnp.zeros((tq, 128), F32))
    carries = step(qi, (init,) * heads, True)

    def more(c):
        j, carries = c
        top = carries[0][0]
        for run, _ in carries[1:]:
            top = jnp.maximum(top, run)
        return (j < qi) & (jnp.max(top) > SB_UNDERFLOW_LOG)

    def body(c):
        j, carries = c
        return j + 1, step(qi - 1 - j, carries, False)

    _, carries = lax.while_loop(more, body, (jnp.int32(0), carries))
    for h, (_, acc) in enumerate(carries):
        o_ref[:, h * HEAD_DIM:(h + 1) * HEAD_DIM] = acc.astype(o_ref.dtype)


def sb_attn(hm, u, *, head0, batch, seq, tq, heads):
    nq = seq // tq
    assert head0 % heads == 0 and SB_HEADS % heads == 0
    hm4 = hm.reshape(hm.shape[0], batch, seq, HEAD_DIM)
    blk0 = head0 // heads
    per_part = SB_HEADS // heads
    return pl.pallas_call(
        functools.partial(_sb_attn_kernel, tq=tq, heads=heads),
        out_shape=jax.ShapeDtypeStruct((batch * seq, SB_HEADS * HEAD_DIM), BF16),
        grid=(per_part, batch, nq),
        in_specs=[
            pl.BlockSpec((heads, None, tq, HEAD_DIM), lambda h, b, i: (blk0 + h, b, i, 0)),
            pl.BlockSpec((heads, None, seq, HEAD_DIM), lambda h, b, i: (blk0 + per_part + h, b, 0, 0)),
            pl.BlockSpec((heads, None, seq, HEAD_DIM), lambda h, b, i: (blk0 + 2 * per_part + h, b, 0, 0)),
            pl.BlockSpec((tq, tq), lambda h, b, i: (0, 0)),
        ],
        out_specs=pl.BlockSpec((tq, heads * HEAD_DIM), lambda h, b, i: (b * nq + i, h)),
        compiler_params=_cp(("arbitrary", "arbitrary", "arbitrary")),
    )(hm4, hm4, hm4, u)


def _nsa_cmp_kernel(c_ref, w1_ref, pe_ref, w2_ref, o_ref, *, nc):
    c = c_ref[...]
    half = NSA_CMP_STRIDE * HEAD_DIM
    a1 = _dot(c, w1_ref[0:half, :])
    a2 = _dot(c, w1_ref[half:2 * half, :])
    pc = _dot(pe_ref[...], w1_ref[...])[0:1, :]
    pre = a1 + pltpu.roll(a2, nc - 1, 0) + pc
    act = 0.5 * pre * (1.0 + jnp.tanh(0.7978845608028654 * (pre + 0.044715 * (pre * pre * pre))))
    o_ref[...] = _dot(act.astype(BF16), w2_ref[...]).astype(BF16)


def nsa_compress(cmp_heads, w1, pe, w2, *, batch, seq):
    nc = seq // NSA_CMP_STRIDE
    head0 = 0
    hm4 = cmp_heads.reshape(4, batch, nc, NSA_CMP_STRIDE * HEAD_DIM)
    return pl.pallas_call(
        functools.partial(_nsa_cmp_kernel, nc=nc),
        out_shape=jax.ShapeDtypeStruct((4, batch, nc, HEAD_DIM), BF16),
        grid=(4, batch),
        in_specs=[
            pl.BlockSpec((None, None, nc, NSA_CMP_STRIDE * HEAD_DIM), lambda c, b: (head0 + c, b, 0, 0)),
            pl.BlockSpec((None, NSA_CMP_LEN * HEAD_DIM, HEAD_DIM), lambda c, b: (c // 2, 0, 0)),
            pl.BlockSpec((None, 8, NSA_CMP_LEN * HEAD_DIM), lambda c, b: (c // 2, 0, 0)),
            pl.BlockSpec((None, HEAD_DIM, HEAD_DIM), lambda c, b: (c // 2, 0, 0)),
        ],
        out_specs=pl.BlockSpec((None, None, nc, HEAD_DIM), lambda c, b: (c, b, 0, 0)),
        compiler_params=_cp(("arbitrary", "arbitrary")),
    )(hm4, w1, pe, w2)


def _nsa_attn_kernel(q_ref, qa_ref, kc_ref, vc_ref, ks_ref, vs_ref, kw_ref, vw_ref, g_ref, ovt_ref, e_ref,
                     kpos_ref, cpos_ref, o_ref, *, tq, tk, seq, n_sel, n_top):
    qi = pl.program_id(2)
    t0 = qi * tq
    rows = NSA_HG * tq
    nc = seq // NSA_CMP_STRIDE
    q = jnp.concatenate([q_ref[...].reshape(rows, HEAD_DIM), qa_ref[...]], axis=1)

    rid = lax.broadcasted_iota(jnp.int32, (rows, 1), 0)
    trow = t0 + lax.bitwise_and(rid, tq - 1)

    def masked_softmax(s, valid):
        sm = jnp.where(valid, s, NEG)
        m = jnp.max(sm, axis=1, keepdims=True)
        e = jnp.where(valid, jnp.exp(sm - m), 0.0)
        d = jnp.sum(e, axis=1, keepdims=True)
        return e * (1.0 / jnp.where(d > 0.0, d, 1.0))

    n_i = lax.broadcasted_iota(jnp.int32, (1, nc), 1)
    end = n_i * NSA_CMP_STRIDE + (NSA_CMP_LEN - 1)
    s_c = _dot_nt(q, jnp.concatenate([kc_ref[...], cpos_ref[...]], axis=1))
    p_c = masked_softmax(s_c, end <= trow)
    o_c = _dot(p_c.astype(BF16), vc_ref[...])

    psum = p_c[0:tq] + p_c[tq:2 * tq] + p_c[2 * tq:3 * tq] + p_c[3 * tq:4 * tq]
    p_hi = psum.astype(BF16)
    p_lo = (psum - p_hi.astype(F32)).astype(BF16)
    ovt = ovt_ref[...]
    imp = _dot_nt(ovt, p_hi) + _dot_nt(ovt, p_lo)
    cur = lax.shift_right_logical(t0 + lax.broadcasted_iota(jnp.int32, (1, tq), 1),
                                  int(np.log2(NSA_SEL_LEN)))
    blk = lax.broadcasted_iota(jnp.int32, (n_sel, tq), 0)
    forced = (blk == 0) | (blk == cur) | (blk == cur - 1)
    key = jnp.where(blk > cur, -BIG, jnp.where(forced, BIG, imp))
    sub = lax.broadcasted_iota(jnp.int32, (8, tq), 0)
    chunks = [key[8 * r:8 * r + 8, :] for r in range(n_sel // 8)]
    ranks = [jnp.zeros((8, tq), F32) for _ in chunks]
    for i in range(n_sel):
        vi = key[i:i + 1, :]
        for r, kc in enumerate(chunks):
            gt = jnp.where(vi > kc, 1.0, 0.0)
            if r < i // 8:
                win = gt
            else:
                ge = jnp.where(vi >= kc, 1.0, 0.0)
                win = ge if r > i // 8 else jnp.where(sub > i % 8, ge, gt)
            ranks[r] = ranks[r] + win
    rank = jnp.concatenate(ranks, axis=0)
    selm_t = jnp.where((rank < float(n_top)) & (blk <= cur), 1.0, 0.0).astype(BF16)
    eye = (lax.broadcasted_iota(jnp.int32, (n_sel, n_sel), 0)
           == lax.broadcasted_iota(jnp.int32, (n_sel, n_sel), 1)).astype(F32).astype(BF16)
    selm = lax.dot_general(selm_t, eye, (((0,), (0,)), ((), ())), preferred_element_type=F32).astype(BF16)

    def sel_step(kt, carry, diag):
        m, l, acc = carry
        k0 = pl.multiple_of(kt * tk, tk)
        kk = jnp.concatenate([ks_ref[pl.ds(k0, tk), :], kpos_ref[pl.ds(k0, tk), :]], axis=1)
        vv = vs_ref[pl.ds(k0, tk), :]
        mex = _dot(selm, e_ref[kt])
        mex4 = jnp.concatenate([mex] * NSA_HG, axis=0)
        s = _dot_nt(q, kk)
        valid = mex4 > 0.5
        if diag:
            spos = k0 + lax.broadcasted_iota(jnp.int32, (1, tk), 1)
            valid = valid & (spos <= trow)
        sm = jnp.where(valid, s, NEG)
        m_new = jnp.maximum(m, jnp.max(sm, axis=1, keepdims=True))
        alpha = jnp.exp(m - m_new)
        p = jnp.exp((sm - m_new).astype(BF16))
        l = alpha * l + jnp.sum(p.astype(F32), axis=1, keepdims=True)
        acc = alpha * acc + _dot(p, vv)
        return m_new, l, acc

    kt_last = t0 // tk
    init = (jnp.full((rows, 1), NEG, F32), jnp.zeros((rows, 1), F32), jnp.zeros((rows, HEAD_DIM), F32))
    carry = lax.fori_loop(0, kt_last, lambda kt, c: sel_step(kt, c, False), init)
    _, l_s, acc_s = sel_step(kt_last, carry, True)
    o_s = acc_s * (1.0 / l_s)

    wk = NSA_WINDOW + tq
    ks0 = pl.multiple_of(jnp.maximum(t0 - NSA_WINDOW, 0), tq)
    kw = jnp.concatenate([kw_ref[pl.ds(ks0, wk), :], kpos_ref[pl.ds(ks0, wk), :]], axis=1)
    vw = vw_ref[pl.ds(ks0, wk), :]
    wpos = ks0 + lax.broadcasted_iota(jnp.int32, (1, wk), 1)
    dw = trow - wpos
    sm_w = jnp.where((dw >= 0) & (dw < NSA_WINDOW), _dot_nt(q, kw), NEG)
    p_w = jnp.exp((sm_w - jnp.max(sm_w, axis=1, keepdims=True)).astype(BF16))
    o_w = _dot(p_w, vw) * (1.0 / jnp.sum(p_w.astype(F32), axis=1, keepdims=True))

    gt = g_ref[...]
    for hg in range(NSA_HG):
        sl = slice(hg * tq, (hg + 1) * tq)
        o = (gt[:, 3 * hg:3 * hg + 1] * o_c[sl] + gt[:, 3 * hg + 1:3 * hg + 2] * o_s[sl]
             + gt[:, 3 * hg + 2:3 * hg + 3] * o_w[sl])
        o_ref[:, hg * HEAD_DIM:(hg + 1) * HEAD_DIM] = o.astype(o_ref.dtype)


def nsa_attn(hm, cmp, gates, consts, *, q_head0, kv_head0, batch, seq, tq, tk):
    ov, e, qa, kpos, cpos = consts
    nq = seq // tq
    nc = seq // NSA_CMP_STRIDE
    n_sel = seq // NSA_SEL_LEN
    n_top = min(NSA_TOPK, n_sel)
    assert tk % tq == 0 and seq % tk == 0 and seq >= NSA_WINDOW + tq and NSA_WINDOW % tq == 0
    hm4 = hm.reshape(hm.shape[0], batch, seq, HEAD_DIM)
    kv_spec = lambda off: pl.BlockSpec((None, None, seq, HEAD_DIM),
                                       lambda b, g, i: (kv_head0 + off + g, b, 0, 0))
    return pl.pallas_call(
        functools.partial(_nsa_attn_kernel, tq=tq, tk=tk, seq=seq, n_sel=n_sel, n_top=n_top),
        out_shape=jax.ShapeDtypeStruct((batch * seq, NSA_HEADS * HEAD_DIM), BF16),
        grid=(batch, NSA_GROUPS, nq),
        in_specs=[
            pl.BlockSpec((NSA_HG, None, tq, HEAD_DIM), lambda b, g, i: (q_head0 // NSA_HG + g, b, i, 0)),
            pl.BlockSpec((None, NSA_HG * tq, 128), lambda b, g, i: (g, 0, 0)),
            pl.BlockSpec((None, None, nc, HEAD_DIM), lambda b, g, i: (g, b, 0, 0)),
            pl.BlockSpec((None, None, nc, HEAD_DIM), lambda b, g, i: (2 + g, b, 0, 0)),
            kv_spec(4), kv_spec(6), kv_spec(8), kv_spec(10),
            pl.BlockSpec((tq, 128), lambda b, g, i: (b * nq + i, g)),
            pl.BlockSpec((n_sel, nc), lambda b, g, i: (0, 0)),
            pl.BlockSpec((seq // tk, n_sel, tk), lambda b, g, i: (0, 0, 0)),
            pl.BlockSpec((seq, 128), lambda b, g, i: (0, 0)),
            pl.BlockSpec((nc, 128), lambda b, g, i: (0, 0)),
        ],
        out_specs=pl.BlockSpec((tq, NSA_HG * HEAD_DIM), lambda b, g, i: (b * nq + i, g)),
        compiler_params=_cp(("arbitrary", "arbitrary", "arbitrary")),
    )(hm4, qa, cmp, cmp, hm4, hm4, hm4, hm4, gates, ov, e, kpos, cpos)


def _merge_kernel(x_ref, oa_ref, ob_ref, oc_ref, wm_ref, wb_ref, bm_ref, y_ref, xb_ref):
    @pl.when(pl.program_id(1) == 0)
    def _():
        xb_ref[...] = x_ref[...].astype(BF16)

    xb = xb_ref[...]
    acc = None
    for br, o_ref in enumerate((oa_ref, ob_ref, oc_ref)):
        gate = jax.nn.sigmoid(_dot(xb, wm_ref[br]) + bm_ref[br])
        term = gate * _dot(o_ref[...], wb_ref[br])
        acc = term if acc is None else acc + term
    y_ref[...] = acc.astype(y_ref.dtype)


def merge_branches(x, o_a, o_b, o_c, wm, wb, bm, *, tm, tn):
    t = x.shape[0]
    o_spec = pl.BlockSpec((tm, BRANCH_WIDTH), lambda i, j: (i, 0))
    return pl.pallas_call(
        _merge_kernel,
        out_shape=jax.ShapeDtypeStruct((t, D_MODEL), BF16),
        grid=(t // tm, D_MODEL // tn),
        in_specs=[
            pl.BlockSpec((tm, D_MODEL), lambda i, j: (i, 0)),
            o_spec, o_spec, o_spec,
            pl.BlockSpec((N_BRANCH, D_MODEL, tn), lambda i, j: (0, 0, j)),
            pl.BlockSpec((N_BRANCH, BRANCH_WIDTH, tn), lambda i, j: (0, 0, j)),
            pl.BlockSpec((N_BRANCH, 1, tn), lambda i, j: (0, 0, j)),
        ],
        out_specs=pl.BlockSpec((tm, tn), lambda i, j: (i, j)),
        scratch_shapes=[pltpu.VMEM((tm, D_MODEL), BF16)],
        compiler_params=_cp(("arbitrary", "arbitrary")),
    )(x, o_a, o_b, o_c, wm, wb, bm)


def _out_ln_kernel(y_ref, w_ref, x_ref, g_ref, b_ref, o_ref, *, alpha):
    h = _dot(y_ref[...], w_ref[...])
    o_ref[...] = _layer_norm(alpha * x_ref[...] + h, g_ref[...], b_ref[...])


def out_ln(y, w, x, g, b, *, alpha, tm):
    t = x.shape[0]
    return pl.pallas_call(
        functools.partial(_out_ln_kernel, alpha=alpha),
        out_shape=jax.ShapeDtypeStruct((t, D_MODEL), F32),
        grid=(t // tm,),
        in_specs=[
            pl.BlockSpec((tm, D_MODEL), lambda i: (i, 0)),
            pl.BlockSpec((D_MODEL, D_MODEL), lambda i: (0, 0)),
            pl.BlockSpec((tm, D_MODEL), lambda i: (i, 0)),
            pl.BlockSpec((1, D_MODEL), lambda i: (0, 0)),
            pl.BlockSpec((1, D_MODEL), lambda i: (0, 0)),
        ],
        out_specs=pl.BlockSpec((tm, D_MODEL), lambda i: (i, 0)),
        compiler_params=_cp(("arbitrary",)),
    )(y, w, x, g, b)


def _mem_attn_kernel(x_ref, wq_ref, k_ref, v_ref, wo_ref, g_ref, b_ref, o_ref, *, alpha):
    x = x_ref[...]
    q = _dot(x.astype(BF16), wq_ref[...]) * (HEAD_DIM ** -0.5)
    outs = []
    for h in range(MEM_HEADS):
        qh = q[:, h * HEAD_DIM:(h + 1) * HEAD_DIM].astype(BF16)
        s = _dot_nt(qh, k_ref[h])
        m = jnp.max(s, axis=1, keepdims=True)
        e = jnp.exp(s - m)
        p = e * (1.0 / jnp.sum(e, axis=1, keepdims=True))
        outs.append(_dot(p.astype(BF16), v_ref[h]).astype(BF16))
    o = jnp.concatenate(outs, axis=1)
    h_out = _dot(o, wo_ref[...])
    o_ref[...] = _layer_norm(alpha * x + h_out, g_ref[...], b_ref[...])


def mem_attn_ln(x, wq, kv, wo, g, b, *, alpha, seq, mem_len, tm):
    t = x.shape[0]
    per_b = seq // tm
    kv4 = kv.reshape(2 * MEM_HEADS, t // seq, mem_len, HEAD_DIM)
    width = MEM_HEADS * HEAD_DIM
    return pl.pallas_call(
        functools.partial(_mem_attn_kernel, alpha=alpha),
        out_shape=jax.ShapeDtypeStruct((t, D_MODEL), F32),
        grid=(t // tm,),
        in_specs=[
            pl.BlockSpec((tm, D_MODEL), lambda i: (i, 0)),
            pl.BlockSpec((D_MODEL, width), lambda i: (0, 0)),
            pl.BlockSpec((MEM_HEADS, None, mem_len, HEAD_DIM), lambda i: (0, i // per_b, 0, 0)),
            pl.BlockSpec((MEM_HEADS, None, mem_len, HEAD_DIM), lambda i: (1, i // per_b, 0, 0)),
            pl.BlockSpec((width, D_MODEL), lambda i: (0, 0)),
            pl.BlockSpec((1, D_MODEL), lambda i: (0, 0)),
            pl.BlockSpec((1, D_MODEL), lambda i: (0, 0)),
        ],
        out_specs=pl.BlockSpec((tm, D_MODEL), lambda i: (i, 0)),
        compiler_params=_cp(("arbitrary",)),
    )(x, wq, kv4, kv4, wo, g, b)


def _router_kernel(x_ref, w_ref, b_ref, tri_ref, mask_ref, wsel_ref, xp_ref, before_ref, count_ref):
    @pl.when(pl.program_id(0) == 0)
    def _():
        count_ref[...] = jnp.zeros_like(count_ref)

    x = x_ref[...]
    xp_ref[...] = _pack_bf16_pairs(x)
    xh = x.astype(BF16)
    xl = (x - xh.astype(F32)).astype(BF16)
    logits = _dot(xh, w_ref[0]) + _dot(xh, w_ref[1]) + _dot(xl, w_ref[0]) + b_ref[...]
    tm = logits.shape[0]
    lane = lax.broadcasted_iota(jnp.int32, (tm, N_EXPERTS), 1)
    work = logits
    hots, vals = [], []
    for _ in range(TOP_K):
        m = jnp.max(work, axis=1, keepdims=True)
        idx = jnp.min(jnp.where(work == m, lane, N_EXPERTS), axis=1, keepdims=True)
        hot = lane == idx
        hots.append(hot)
        vals.append(m)
        work = jnp.where(hot, -jnp.inf, work)
    es = [jnp.exp(v - vals[0]) for v in vals]
    inv = 1.0 / (es[0] + es[1] + es[2] + es[3])
    mask = jnp.zeros((tm, N_EXPERTS), F32)
    wsel = jnp.zeros((tm, N_EXPERTS), F32)
    for hot, e in zip(hots, es):
        mask = mask + jnp.where(hot, 1.0, 0.0)
        wsel = wsel + jnp.where(hot, e * inv, 0.0)
    mask_ref[...] = mask
    wsel_ref[...] = wsel
    before_ref[...] = count_ref[...] + _dot(tri_ref[...], mask.astype(BF16))
    count_ref[...] = count_ref[...] + jnp.sum(mask, axis=0, keepdims=True)


def router(x, w_hl, b, tri, *, tm):
    t = x.shape[0]
    tok32 = lambda: pl.BlockSpec((tm, N_EXPERTS), lambda i: (i, 0))
    return pl.pallas_call(
        _router_kernel,
        out_shape=(jax.ShapeDtypeStruct((t, N_EXPERTS), F32), jax.ShapeDtypeStruct((t, N_EXPERTS), F32),
                   jax.ShapeDtypeStruct((t, D_MODEL // 2), jnp.uint32),
                   jax.ShapeDtypeStruct((t, N_EXPERTS), F32)),
        grid=(t // tm,),
        in_specs=[
            pl.BlockSpec((tm, D_MODEL), lambda i: (i, 0)),
            pl.BlockSpec((2, D_MODEL, N_EXPERTS), lambda i: (0, 0, 0)),
            pl.BlockSpec((1, N_EXPERTS), lambda i: (0, 0)),
            pl.BlockSpec((tm, tm), lambda i: (0, 0)),
        ],
        out_specs=(tok32(), tok32(), pl.BlockSpec((tm, D_MODEL // 2), lambda i: (i, 0)), tok32()),
        scratch_shapes=[pltpu.VMEM((1, N_EXPERTS), F32)],
        compiler_params=_cp(("arbitrary",)),
    )(x, w_hl, b, tri)


def sc_gather_rows(table, idx, *, chunk):
    n = idx.shape[0]
    d = table.shape[1]
    workers = SC_CORES_V7X * SC_SUBCORES_V7X
    per_w = n // workers
    assert n % (workers * chunk) == 0 and chunk % 8 == 0 and chunk <= 128
    mesh = plsc.VectorSubcoreMesh(core_axis_name="c", subcore_axis_name="s")

    @functools.partial(
        pl.kernel, mesh=mesh,
        out_type=jax.ShapeDtypeStruct((n, d), table.dtype),
        scratch_types=[pltpu.VMEM((chunk,), jnp.int32), pltpu.VMEM((chunk, d), table.dtype),
                       pltpu.SemaphoreType.DMA],
    )
    def gather(table_hbm, idx_hbm, out_hbm, idx_v, rows_v, sem):
        wid = lax.axis_index("s") * SC_CORES_V7X + lax.axis_index("c")
        base = wid * per_w

        @pl.loop(0, per_w // chunk)
        def _(j):
            off = pl.multiple_of(base + j * chunk, 8)
            pltpu.sync_copy(idx_hbm.at[pl.ds(off, chunk)], idx_v)
            pltpu.async_copy(table_hbm.at[idx_v], rows_v, sem).wait()
            pltpu.sync_copy(rows_v, out_hbm.at[pl.ds(off, chunk)])

    return gather(table, idx)


def _experts_kernel(be_ref, x_ref, wg_ref, bg_ref, wu_ref, bu_ref, wd_ref, bd_ref, y_ref,
                    wgb_ref, wub_ref, wdb_ref):
    i = pl.program_id(0)
    prev = be_ref[jnp.maximum(i - 1, 0)]

    @pl.when((i == 0) | (be_ref[i] != prev))
    def _():
        wgb_ref[...] = wg_ref[...].astype(BF16)
        wub_ref[...] = wu_ref[...].astype(BF16)
        wdb_ref[...] = wd_ref[...].astype(BF16)

    x_lo, x_hi = _unpack_bf16_pairs(x_ref[...])
    xb = jnp.concatenate([x_lo.astype(BF16), x_hi.astype(BF16)], axis=1)
    g = jnp.minimum(_dot(xb, wgb_ref[...]) + bg_ref[...], SWIGLU_LIMIT)
    u = jnp.clip(_dot(xb, wub_ref[...]) + bu_ref[...], -SWIGLU_LIMIT, SWIGLU_LIMIT)
    hdn = (u + 1.0) * (g * jax.nn.sigmoid(SWIGLU_ALPHA * g))
    y_ref[...] = _pack_bf16_pairs(_dot(hdn.astype(BF16), wdb_ref[...]) + bd_ref[...])


def experts(x_rows, blk_e, wg, bg, wu, bu, wd, bd, *, layer, bm):
    n_rows, dp = x_rows.shape
    d, f = wg.shape[2], wg.shape[3]
    w_spec = lambda shape: pl.BlockSpec((None, None) + shape, lambda i, be: (layer, be[i], 0, 0))
    grid_spec = pltpu.PrefetchScalarGridSpec(
        num_scalar_prefetch=1,
        grid=(n_rows // bm,),
        in_specs=[
            pl.BlockSpec((bm, dp), lambda i, be: (i, 0)),
            w_spec((d, f)), w_spec((1, f)), w_spec((d, f)), w_spec((1, f)), w_spec((f, d)), w_spec((1, d)),
        ],
        out_specs=pl.BlockSpec((bm, dp), lambda i, be: (i, 0)),
        scratch_shapes=[pltpu.VMEM((d, f), BF16), pltpu.VMEM((d, f), BF16), pltpu.VMEM((f, d), BF16)],
    )
    return pl.pallas_call(
        _experts_kernel,
        out_shape=jax.ShapeDtypeStruct((n_rows, dp), jnp.uint32),
        grid_spec=grid_spec,
        compiler_params=_cp(("arbitrary",)),
    )(blk_e, x_rows, wg, bg, wu, bu, wd, bd)


def _moe_ln_kernel(x_ref, y_ref, w_ref, g_ref, b_ref, o_ref, *, alpha):
    w = w_ref[...]
    y_lo = y_hi = None
    for k in range(TOP_K):
        lo, hi = _unpack_bf16_pairs(y_ref[k])
        wk = w[:, k:k + 1]
        y_lo = wk * lo if y_lo is None else y_lo + wk * lo
        y_hi = wk * hi if y_hi is None else y_hi + wk * hi
    y = jnp.concatenate([y_lo, y_hi], axis=1)
    o_ref[...] = _layer_norm(alpha * x_ref[...] + y, g_ref[...], b_ref[...])


def moe_ln(x, y4, w4p, g, b, *, alpha, tm):
    t = x.shape[0]
    return pl.pallas_call(
        functools.partial(_moe_ln_kernel, alpha=alpha),
        out_shape=jax.ShapeDtypeStruct((t, D_MODEL), F32),
        grid=(t // tm,),
        in_specs=[
            pl.BlockSpec((tm, D_MODEL), lambda i: (i, 0)),
            pl.BlockSpec((TOP_K, tm, D_MODEL // 2), lambda i: (0, i, 0)),
            pl.BlockSpec((tm, 128), lambda i: (i, 0)),
            pl.BlockSpec((1, D_MODEL), lambda i: (0, 0)),
            pl.BlockSpec((1, D_MODEL), lambda i: (0, 0)),
        ],
        out_specs=pl.BlockSpec((tm, D_MODEL), lambda i: (i, 0)),
        compiler_params=_cp(("arbitrary",)),
    )(x, y4, w4p, g, b)


def _rope_tables(seq):
    inv = np.asarray(ROPE_THETA ** (-np.arange(0, MLA_ROPE, 2) / MLA_ROPE), np.float32)
    ang = jnp.arange(seq, dtype=F32)[:, None] * jnp.asarray(inv)[None, :]
    cos, sin = jnp.cos(ang), jnp.sin(ang)
    zeros = jnp.zeros((seq, 128 - MLA_ROPE), F32)
    return (jnp.concatenate([cos, cos, zeros], axis=1), jnp.concatenate([-sin, sin, zeros], axis=1))


def _nsa_constants(seq, tq, tk):
    nc = seq // NSA_CMP_STRIDE
    qa = np.zeros((NSA_GROUPS, NSA_HG * tq, 128), np.float32)
    for g in range(NSA_GROUPS):
        for hg in range(NSA_HG):
            slope = 2.0 ** (-8.0 * (g * NSA_HG + hg + 1) / NSA_HEADS)
            qa[g, hg * tq:(hg + 1) * tq, 0] = slope * NSA_SEL_LEN
            qa[g, hg * tq:(hg + 1) * tq, 1] = slope
            qa[g, hg * tq:(hg + 1) * tq, 2] = slope * NSA_CMP_STRIDE
            qa[g, hg * tq:(hg + 1) * tq, 3] = slope * (NSA_CMP_LEN - 1) / 2.0
    kpos = np.zeros((seq, 128), np.float32)
    kpos[:, 0] = np.arange(seq) // NSA_SEL_LEN
    kpos[:, 1] = np.arange(seq) % NSA_SEL_LEN
    cpos = np.zeros((nc, 128), np.float32)
    cpos[:, 2] = np.arange(nc)
    cpos[:, 3] = 1.0
    for arr in (qa, kpos, cpos):
        assert np.array_equal(arr.astype(BF16).astype(np.float32), arr)
    n_cmp = (seq - NSA_CMP_LEN) // NSA_CMP_STRIDE + 1
    n_sel = seq // NSA_SEL_LEN
    cs = np.arange(nc) * NSA_CMP_STRIDE
    ss = np.arange(n_sel) * NSA_SEL_LEN
    ov = np.clip(np.minimum(cs[:, None] + NSA_CMP_LEN, ss[None, :] + NSA_SEL_LEN)
                 - np.maximum(cs[:, None], ss[None, :]), 0, None) / NSA_CMP_LEN
    ov[n_cmp:] = 0.0
    e = (np.arange(seq)[None, :] // NSA_SEL_LEN == np.arange(n_sel)[:, None]).astype(np.float32)
    e = e.reshape(n_sel, seq // tk, tk).transpose(1, 0, 2)
    return tuple(jnp.asarray(a, BF16) for a in (ov.T, e, qa, kpos, cpos))


def _pad_cols(w, width):
    return jnp.pad(w, ((0, 0), (0, width - w.shape[1])))


def _swap_halves(w):
    half = w.shape[1] // 2
    return jnp.concatenate([w[:, half:], w[:, :half]], axis=1)


def _layer_weights(w_in, w_q_up, w_kv_up):
    kr = w_in[:, OFF_KR:OFF_NSA_Q]
    gate = w_in[:, OFF_NSA_GATE:OFF_SB]
    per_g = NSA_HG * 3
    w_mla = jnp.concatenate([
        w_in[:, OFF_CQ:OFF_KR],
        _pad_cols(kr, 128), _pad_cols(_swap_halves(kr), 128),
        _pad_cols(gate[:, :per_g], 128), _pad_cols(gate[:, per_g:], 128)], axis=1).astype(BF16)
    wq = w_q_up.reshape(MLA_Q_RANK, MLA_HEADS, MLA_NOPE + MLA_ROPE)
    rope = wq[:, :, MLA_NOPE:]
    rope_sw = jnp.concatenate([rope[:, :, MLA_ROPE // 2:], rope[:, :, :MLA_ROPE // 2]], axis=2)
    pad = ((0, 0), (0, 0), (0, 128 - MLA_ROPE))
    wq3 = jnp.concatenate([
        wq[:, :, :MLA_NOPE].reshape(MLA_Q_RANK, -1),
        jnp.pad(rope, pad).reshape(MLA_Q_RANK, -1),
        jnp.pad(rope_sw, pad).reshape(MLA_Q_RANK, -1)], axis=1).astype(BF16)
    wkv = w_kv_up.reshape(MLA_KV_RANK, MLA_HEADS, 2, 128)
    wkv = jnp.concatenate([wkv[:, :, 0].reshape(MLA_KV_RANK, -1),
                           wkv[:, :, 1].reshape(MLA_KV_RANK, -1)], axis=1).astype(BF16)
    w_heads = w_in[:, OFF_NSA_Q:OFF_NSA_GATE]
    w_heads = jnp.concatenate([w_heads, w_in[:, OFF_SB:OFF_MERGE]], axis=1).astype(BF16)
    wm = w_in[:, OFF_MERGE:].reshape(D_MODEL, N_BRANCH, D_MODEL).transpose(1, 0, 2).astype(BF16)
    return w_mla, wq3, wkv, w_heads, wm


def _forward(x, mem, w_in, mla_q_norm, mla_w_q_up, mla_kv_norm, mla_w_kv_up,
             nsa_pe_k, nsa_pe_v, nsa_w1_k, nsa_w1_v, nsa_w2_k, nsa_w2_v,
             w_branch, b_merge, w_out, ln_mix_g, ln_mix_b,
             mem_w_q, mem_w_k, mem_w_v, mem_w_o, ln_mem_g, ln_mem_b,
             moe_w_router, moe_b_router, moe_w_gate, moe_b_gate, moe_w_up, moe_b_up,
             moe_w_down, moe_b_down, ln_moe_g, ln_moe_b):
    batch, seq, _ = x.shape
    mem_len = mem.shape[1]
    depth = w_in.shape[0]
    t = batch * seq
    alpha = float((2 * depth) ** 0.25)
    bm = MOE_BLOCK_ROWS
    n_rows = t * TOP_K + N_EXPERTS * bm
    n_blocks = n_rows // bm

    tm_in = min(512, seq)
    tq_mla = min(512, seq)
    tq_sb = 256
    tq_nsa, tk_nsa = 128, 512
    tm_ln = 256
    sc_chunk = 64

    cos128, sin128 = _rope_tables(seq)
    nsa_consts = _nsa_constants(seq, tq_nsa, tk_nsa)
    u_sb = jnp.asarray(np.arange(tq_sb)[:, None] > np.arange(tq_sb)[None, :], BF16)
    tri_router = jnp.asarray(np.arange(tm_in)[:, None] > np.arange(tm_in)[None, :], BF16)
    n_qheads = NSA_HEADS
    n_kvheads = 3 * 2 * NSA_GROUPS
    head_scale = np.ones((1, (n_qheads + n_kvheads + 3 * SB_HEADS) * HEAD_DIM), np.float32)
    head_scale[:, :n_qheads * HEAD_DIM] = HEAD_DIM ** -0.5
    sb0 = n_qheads + n_kvheads
    head_scale[:, sb0 * HEAD_DIM:(sb0 + SB_HEADS) * HEAD_DIM] = HEAD_DIM ** -0.5
    head_scale = jnp.asarray(head_scale)
    ones_kv = jnp.ones((1, 2 * MEM_HEADS * HEAD_DIM), F32)

    b_gate4 = moe_b_gate.reshape(depth, N_EXPERTS, 1, D_EXPERT)
    b_up4 = moe_b_up.reshape(depth, N_EXPERTS, 1, D_EXPERT)
    b_down4 = moe_b_down.reshape(depth, N_EXPERTS, 1, D_MODEL)

    xf = x.reshape(t, D_MODEL)
    memf = mem.reshape(batch * mem_len, D_MODEL)
    row = lambda v: v.reshape(1, -1)

    for l in range(depth):
        w_mla, wq3, wkv, w_heads, wm = _layer_weights(w_in[l], mla_w_q_up[l], mla_w_kv_up[l])

        q_a, k_a, v_a, gates = mla_in(xf, w_mla, row(mla_q_norm[l]), row(mla_kv_norm[l]), wq3, wkv,
                                      cos128, sin128, seq=seq, tm=tm_ln)
        hm = proj_heads(xf, w_heads, head_scale, tm=tm_in, tn=11 * HEAD_DIM)
        o_a = mla_attn(q_a, k_a, v_a, batch=batch, seq=seq, tq=tq_mla, heads=2)
        w1 = jnp.stack([nsa_w1_k[l], nsa_w1_v[l]]).astype(BF16)
        pe = jnp.stack([nsa_pe_k[l], nsa_pe_v[l]]).reshape(2, 1, -1)
        pe = jnp.broadcast_to(pe, (2, 8, pe.shape[-1])).astype(BF16)
        w2 = jnp.stack([nsa_w2_k[l], nsa_w2_v[l]]).astype(BF16)
        cmp = nsa_compress(hm[n_qheads:n_qheads + 4], w1, pe, w2, batch=batch, seq=seq)
        o_b = nsa_attn(hm, cmp, gates, nsa_consts, q_head0=0, kv_head0=n_qheads,
                       batch=batch, seq=seq, tq=tq_nsa, tk=tk_nsa)
        o_c = sb_attn(hm, u_sb, head0=sb0, batch=batch, seq=seq, tq=tq_sb, heads=4)
        y = merge_branches(xf, o_a, o_b, o_c, wm, w_branch[l].astype(BF16),
                           b_merge[l].reshape(N_BRANCH, 1, D_MODEL), tm=tm_in, tn=512)
        xf = out_ln(y, w_out[l].astype(BF16), xf, row(ln_mix_g[l]), row(ln_mix_b[l]), alpha=alpha, tm=tm_ln)

        w_kv_mem = jnp.concatenate([mem_w_k[l], mem_w_v[l]], axis=1).astype(BF16)
        kv_mem = proj_heads(memf, w_kv_mem, ones_kv, tm=min(512, batch * mem_len), tn=512)
        xf = mem_attn_ln(xf, mem_w_q[l].astype(BF16), kv_mem, mem_w_o[l].astype(BF16),
                         row(ln_mem_g[l]), row(ln_mem_b[l]), alpha=alpha, seq=seq, mem_len=mem_len, tm=tm_ln)

        wr = moe_w_router[l]
        wr_hi = wr.astype(BF16)
        wr_lo = (wr - wr_hi.astype(F32)).astype(BF16)
        mask, wsel, x_packed, before = router(xf, jnp.stack([wr_hi, wr_lo]), row(moe_b_router[l]),
                                              tri_router, tm=tm_in)
        counts = (before[-1] + mask[-1]).astype(jnp.int32)
        padded = (counts + bm - 1) // bm * bm
        pad_end = jnp.cumsum(padded)
        pad_start = pad_end - padded
        slot = pad_start[None, :] + before.astype(jnp.int32)
        top_e = lax.top_k(mask, TOP_K)[1]
        pos4 = jnp.take_along_axis(slot, top_e, axis=1).astype(jnp.int32)
        w4 = jnp.take_along_axis(wsel, top_e, axis=1)
        blk_e = jnp.minimum(jnp.searchsorted(pad_end, jnp.arange(n_blocks) * bm, side='right'),
                            N_EXPERTS - 1).astype(jnp.int32)
        tok_ids = jnp.broadcast_to(jnp.arange(t, dtype=jnp.int32)[:, None], (t, TOP_K))
        row_tok = (jnp.arange(n_rows, dtype=jnp.int32) % t).at[pos4.reshape(-1)].set(
            tok_ids.reshape(-1), unique_indices=True)
        x_rows = sc_gather_rows(x_packed, row_tok, chunk=sc_chunk)
        y_rows = experts(x_rows, blk_e, moe_w_gate, b_gate4, moe_w_up, b_up4, moe_w_down, b_down4,
                         layer=l, bm=bm)
        y4 = sc_gather_rows(y_rows, pos4.T.reshape(-1), chunk=sc_chunk).reshape(TOP_K, t, D_MODEL // 2)
        xf = moe_ln(xf, y4, _pad_cols(w4, 128), row(ln_moe_g[l]), row(ln_moe_b[l]), alpha=alpha, tm=tm_ln)

    return xf.reshape(batch, seq, D_MODEL)


def kernel(x, mem, w_in, mla_q_norm, mla_w_q_up, mla_kv_norm, mla_w_kv_up, nsa_pe_k, nsa_pe_v, nsa_w1_k, nsa_w1_v, nsa_w2_k, nsa_w2_v, w_branch, b_merge, w_out, ln_mix_g, ln_mix_b, mem_w_q, mem_w_k, mem_w_v, mem_w_o, ln_mem_g, ln_mem_b, moe_w_router, moe_b_router, moe_w_gate, moe_b_gate, moe_w_up, moe_b_up, moe_w_down, moe_b_down, ln_moe_g, ln_moe_b):
    return _forward(x, mem, w_in, mla_q_norm, mla_w_q_up, mla_kv_norm, mla_w_kv_up,
                    nsa_pe_k, nsa_pe_v, nsa_w1_k, nsa_w1_v, nsa_w2_k, nsa_w2_v,
                    w_branch, b_merge, w_out, ln_mix_g, ln_mix_b,
                    mem_w_q, mem_w_k, mem_w_v, mem_w_o, ln_mem_g, ln_mem_b,
                    moe_w_router, moe_b_router, moe_w_gate, moe_b_gate, moe_w_up, moe_b_up,
                    moe_w_down, moe_b_down, ln_moe_g, ln_moe_b)
```

```python
import functools

import numpy as np
import jax
import jax.numpy as jnp
from jax import lax
from jax.experimental import pallas as pl
from jax.experimental.pallas import tpu as pltpu
from jax.experimental.pallas import tpu_sc as plsc

F32 = jnp.float32
BF16 = jnp.bfloat16

D_MODEL = 2048
HEAD_DIM = 128
MLA_HEADS = 8
MLA_Q_RANK = 512
MLA_KV_RANK = 256
MLA_NOPE = 128
MLA_ROPE = 64
ROPE_THETA = 10000.0
NSA_HEADS = 8
NSA_GROUPS = 2
NSA_HG = NSA_HEADS // NSA_GROUPS
NSA_CMP_LEN = 32
NSA_CMP_STRIDE = 16
NSA_SEL_LEN = 64
NSA_TOPK = 16
NSA_WINDOW = 512
SB_HEADS = 8
MEM_HEADS = 4
N_EXPERTS = 32
TOP_K = 4
D_EXPERT = 512
SWIGLU_LIMIT = 7.0
SWIGLU_ALPHA = 1.702
N_BRANCH = 3
BRANCH_WIDTH = 1024
LN_EPS = 1e-5
RMS_EPS = 1e-6
NEG = -1e30
BIG = 1e30
SB_UNDERFLOW_LOG = -100.0

OFF_CQ = 0
OFF_CKV = 512
OFF_KR = 768
OFF_NSA_Q = 832
OFF_NSA_KV = 1856
OFF_NSA_GATE = 3392
OFF_SB = 3416
OFF_MERGE = 6488

VMEM_LIMIT_V7X = 56 * 1024 * 1024
MOE_BLOCK_ROWS = 256
SC_CORES_V7X = 2
SC_SUBCORES_V7X = 16


def _cp(sem, vmem=VMEM_LIMIT_V7X):
    return pltpu.CompilerParams(dimension_semantics=sem, vmem_limit_bytes=vmem)


def _dot(a, b):
    return jnp.dot(a, b, preferred_element_type=F32)


def _dot_nt(a, b):
    return lax.dot_general(a, b, (((1,), (1,)), ((), ())), preferred_element_type=F32)


def _layer_norm(z, g, b):
    mu = jnp.mean(z, axis=-1, keepdims=True)
    zc = z - mu
    var = jnp.mean(zc * zc, axis=-1, keepdims=True)
    return zc * lax.rsqrt(var + LN_EPS) * g + b


def _rms_norm(z, g):
    return z * lax.rsqrt(jnp.mean(z * z, axis=-1, keepdims=True) + RMS_EPS) * g


def _pack_bf16_pairs(z):
    n = z.shape[1] // 2
    bits = pltpu.bitcast(z.astype(BF16).astype(F32), jnp.uint32)
    return lax.shift_right_logical(bits[:, :n], jnp.uint32(16)) | (bits[:, n:] & jnp.uint32(0xFFFF0000))


def _unpack_bf16_pairs(w):
    lo = pltpu.bitcast(lax.shift_left(w, jnp.uint32(16)), F32)
    hi = pltpu.bitcast(w & jnp.uint32(0xFFFF0000), F32)
    return lo, hi


def _proj_heads_kernel(a_ref, w_ref, s_ref, o_ref, abf_ref, *, n_heads_per_tile):
    @pl.when(pl.program_id(1) == 0)
    def _():
        abf_ref[...] = a_ref[...].astype(BF16)

    acc = _dot(abf_ref[...], w_ref[...]) * s_ref[...]
    for c in range(n_heads_per_tile):
        o_ref[c] = acc[:, c * HEAD_DIM:(c + 1) * HEAD_DIM].astype(o_ref.dtype)


def proj_heads(a, w, scale, *, tm, tn):
    m, k = a.shape
    n = w.shape[1]
    hpt = tn // HEAD_DIM
    return pl.pallas_call(
        functools.partial(_proj_heads_kernel, n_heads_per_tile=hpt),
        out_shape=jax.ShapeDtypeStruct((n // HEAD_DIM, m, HEAD_DIM), BF16),
        grid=(m // tm, n // tn),
        in_specs=[
            pl.BlockSpec((tm, k), lambda i, j: (i, 0)),
            pl.BlockSpec((k, tn), lambda i, j: (0, j)),
            pl.BlockSpec((1, tn), lambda i, j: (0, j)),
        ],
        out_specs=pl.BlockSpec((hpt, tm, HEAD_DIM), lambda i, j: (j, i, 0)),
        scratch_shapes=[pltpu.VMEM((tm, k), BF16)],
        compiler_params=_cp(("arbitrary", "arbitrary")),
    )(a, w, scale)


def _mla_in_kernel(x_ref, w_ref, qg_ref, kg_ref, wq_ref, wkv_ref, cos_ref, sin_ref,
                   q_ref, k_ref, v_ref, g_ref):
    xb = x_ref[...].astype(BF16)
    h = _dot(xb, w_ref[...])
    cq = h[:, 0:512]
    ckv = h[:, 512:768]
    kr1 = h[:, 768:896]
    kr2 = h[:, 896:1024]
    g_ref[...] = jax.nn.sigmoid(h[:, 1024:1280])
    cos = cos_ref[...]
    sin = sin_ref[...]
    scale = (MLA_NOPE + MLA_ROPE) ** -0.5
    nq = _rms_norm(cq, qg_ref[...]).astype(BF16)
    q3 = _dot(nq, wq_ref[...])
    for hh in range(MLA_HEADS):
        lo, hi = hh * 128, (hh + 1) * 128
        q_ref[hh, :, 0:128] = (q3[:, lo:hi] * scale).astype(BF16)
        rot = q3[:, 1024 + lo:1024 + hi] * cos + q3[:, 2048 + lo:2048 + hi] * sin
        q_ref[hh, :, 128:256] = (rot * scale).astype(BF16)
    nkv = _rms_norm(ckv, kg_ref[...]).astype(BF16)
    kv = _dot(nkv, wkv_ref[...])
    krot = (kr1 * cos + kr2 * sin).astype(BF16)
    for hh in range(MLA_HEADS):
        lo, hi = hh * 128, (hh + 1) * 128
        k_ref[hh, :, 0:128] = kv[:, lo:hi].astype(BF16)
        k_ref[hh, :, 128:256] = krot
        v_ref[hh] = kv[:, 1024 + lo:1024 + hi].astype(BF16)


def mla_in(x, w_mla, qg, kg, wq3, wkv, cos128, sin128, *, seq, tm):
    t = x.shape[0]
    npos = seq // tm
    full = lambda shape: pl.BlockSpec(shape, lambda i: (0,) * len(shape))
    return pl.pallas_call(
        _mla_in_kernel,
        out_shape=(
            jax.ShapeDtypeStruct((MLA_HEADS, t, 256), BF16),
            jax.ShapeDtypeStruct((MLA_HEADS, t, 256), BF16),
            jax.ShapeDtypeStruct((MLA_HEADS, t, 128), BF16),
            jax.ShapeDtypeStruct((t, 256), F32),
        ),
        grid=(t // tm,),
        in_specs=[
            pl.BlockSpec((tm, D_MODEL), lambda i: (i, 0)),
            full((D_MODEL, 1280)),
            full((1, MLA_Q_RANK)),
            full((1, MLA_KV_RANK)),
            full((MLA_Q_RANK, 3072)),
            full((MLA_KV_RANK, 2048)),
            pl.BlockSpec((tm, 128), lambda i: (i % npos, 0)),
            pl.BlockSpec((tm, 128), lambda i: (i % npos, 0)),
        ],
        out_specs=(
            pl.BlockSpec((MLA_HEADS, tm, 256), lambda i: (0, i, 0)),
            pl.BlockSpec((MLA_HEADS, tm, 256), lambda i: (0, i, 0)),
            pl.BlockSpec((MLA_HEADS, tm, 128), lambda i: (0, i, 0)),
            pl.BlockSpec((tm, 256), lambda i: (i, 0)),
        ),
        compiler_params=_cp(("arbitrary",)),
    )(x, w_mla, qg, kg, wq3, wkv, cos128, sin128)


def _mla_attn_kernel(q_ref, k_ref, v_ref, o_ref, *, tq, heads):
    qi = pl.program_id(2)

    def head_step(h, kt, carry, diag):
        m, l, acc = carry
        k0 = pl.multiple_of(kt * tq, tq)
        k = k_ref[h, pl.ds(k0, tq), :]
        v = v_ref[h, pl.ds(k0, tq), :]
        s = _dot_nt(q_ref[h], k)
        if diag:
            row = lax.broadcasted_iota(jnp.int32, (tq, tq), 0)
            col = lax.broadcasted_iota(jnp.int32, (tq, tq), 1)
            s = jnp.where(col <= row, s, NEG)
        m_new = jnp.maximum(m, jnp.max(s, axis=1, keepdims=True))
        alpha = jnp.exp(m - m_new)
        p = jnp.exp((s - m_new).astype(BF16))
        l = alpha * l + jnp.sum(p.astype(F32), axis=1, keepdims=True)
        acc = alpha * acc + _dot(p, v)
        return m_new, l, acc

    def step(kt, carries, diag):
        return tuple(head_step(h, kt, carries[h], diag) for h in range(heads))

    init = (jnp.full((tq, 1), NEG, F32), jnp.zeros((tq, 1), F32), jnp.zeros((tq, 128), F32))
    carries = lax.fori_loop(0, qi, lambda kt, c: step(kt, c, False), (init,) * heads)
    carries = step(qi, carries, True)
    for h, (_, l, acc) in enumerate(carries):
        o_ref[:, h * 128:(h + 1) * 128] = (acc / l).astype(o_ref.dtype)


def mla_attn(q, k, v, *, batch, seq, tq, heads):
    nq = seq // tq
    q4 = q.reshape(MLA_HEADS, batch, seq, 256)
    k4 = k.reshape(MLA_HEADS, batch, seq, 256)
    v4 = v.reshape(MLA_HEADS, batch, seq, 128)
    return pl.pallas_call(
        functools.partial(_mla_attn_kernel, tq=tq, heads=heads),
        out_shape=jax.ShapeDtypeStruct((batch * seq, MLA_HEADS * 128), BF16),
        grid=(MLA_HEADS // heads, batch, nq),
        in_specs=[
            pl.BlockSpec((heads, None, tq, 256), lambda h, b, i: (h, b, i, 0)),
            pl.BlockSpec((heads, None, seq, 256), lambda h, b, i: (h, b, 0, 0)),
            pl.BlockSpec((heads, None, seq, 128), lambda h, b, i: (h, b, 0, 0)),
        ],
        out_specs=pl.BlockSpec((tq, heads * 128), lambda h, b, i: (b * nq + i, h)),
        compiler_params=_cp(("arbitrary", "arbitrary", "arbitrary")),
    )(q4, k4, v4)


def _sb_attn_kernel(q_ref, k_ref, v_ref, u_ref, o_ref, *, tq, heads):
    qi = pl.program_id(2)
    u = u_ref[...]

    def head_step(h, kt, carry, diag):
        run, acc = carry
        k0 = pl.multiple_of(kt * tq, tq)
        k = k_ref[h, pl.ds(k0, tq), :]
        v = v_ref[h, pl.ds(k0, tq), :]
        z = _dot_nt(q_ref[h], k)
        l1m = -(jnp.maximum(z, 0.0) + jnp.log(1.0 + jnp.exp(-jnp.abs(z))))
        if diag:
            row = lax.broadcasted_iota(jnp.int32, (tq, tq), 0)
            col = lax.broadcasted_iota(jnp.int32, (tq, tq), 1)
            strict = col < row
            l1m_m = jnp.where(strict, l1m, 0.0)
        else:
            l1m_m = l1m
        hi = l1m_m.astype(BF16)
        lo = (l1m_m - hi.astype(F32)).astype(BF16)
        between = _dot(hi, u) + _dot(lo, u)
        a = jnp.exp(z + l1m + between + run)
        if diag:
            a = jnp.where(strict, a, 0.0)
        acc = acc + _dot(a.astype(BF16), v)
        run = run + between[:, 0:1] + l1m_m[:, 0:1]
        return run, acc

    def step(kt, carries, diag):
        return tuple(head_step(h, kt, carries[h], diag) for h in range(heads))

    init = (jnp.zeros((tq, 1), F32), jnp.zeros((tq, 128), F32))
    carries = step(qi, (init,) * heads, True)

    def more(c):
        j, carries = c
        top = carries[0][0]
        for run, _ in carries[1:]:
            top = jnp.maximum(top, run)
        return (j < qi) & (jnp.max(top) > SB_UNDERFLOW_LOG)

    def body(c):
        j, carries = c
        return j + 1, step(qi - 1 - j, carries, False)

    _, carries = lax.while_loop(more, body, (jnp.int32(0), carries))
    for h, (_, acc) in enumerate(carries):
        o_ref[:, h * HEAD_DIM:(h + 1) * HEAD_DIM] = acc.astype(o_ref.dtype)


def sb_attn(hm, u, *, head0, batch, seq, tq, heads):
    nq = seq // tq
    assert head0 % heads == 0 and SB_HEADS % heads == 0
    hm4 = hm.reshape(hm.shape[0], batch, seq, HEAD_DIM)
    blk0 = head0 // heads
    per_part = SB_HEADS // heads
    return pl.pallas_call(
        functools.partial(_sb_attn_kernel, tq=tq, heads=heads),
        out_shape=jax.ShapeDtypeStruct((batch * seq, SB_HEADS * HEAD_DIM), BF16),
        grid=(per_part, batch, nq),
        in_specs=[
            pl.BlockSpec((heads, None, tq, HEAD_DIM), lambda h, b, i: (blk0 + h, b, i, 0)),
            pl.BlockSpec((heads, None, seq, HEAD_DIM), lambda h, b, i: (blk0 + per_part + h, b, 0, 0)),
            pl.BlockSpec((heads, None, seq, HEAD_DIM), lambda h, b, i: (blk0 + 2 * per_part + h, b, 0, 0)),
            pl.BlockSpec((tq, tq), lambda h, b, i: (0, 0)),
        ],
        out_specs=pl.BlockSpec((tq, heads * HEAD_DIM), lambda h, b, i: (b * nq + i, h)),
        compiler_params=_cp(("arbitrary", "arbitrary", "arbitrary")),
    )(hm4, hm4, hm4, u)


def _nsa_cmp_kernel(c_ref, w1_ref, pe_ref, w2_ref, o_ref, *, nc):
    c = c_ref[...]
    half = NSA_CMP_STRIDE * HEAD_DIM
    a1 = _dot(c, w1_ref[0:half, :])
    a2 = _dot(c, w1_ref[half:2 * half, :])
    pc = _dot(pe_ref[...], w1_ref[...])[0:1, :]
    pre = a1 + pltpu.roll(a2, nc - 1, 0) + pc
    act = 0.5 * pre * (1.0 + jnp.tanh(0.7978845608028654 * (pre + 0.044715 * (pre * pre * pre))))
    o_ref[...] = _dot(act.astype(BF16), w2_ref[...]).astype(BF16)


def nsa_compress(cmp_heads, w1, pe, w2, *, batch, seq):
    nc = seq // NSA_CMP_STRIDE
    head0 = 0
    hm4 = cmp_heads.reshape(4, batch, nc, NSA_CMP_STRIDE * HEAD_DIM)
    return pl.pallas_call(
        functools.partial(_nsa_cmp_kernel, nc=nc),
        out_shape=jax.ShapeDtypeStruct((4, batch, nc, HEAD_DIM), BF16),
        grid=(4, batch),
        in_specs=[
            pl.BlockSpec((None, None, nc, NSA_CMP_STRIDE * HEAD_DIM), lambda c, b: (head0 + c, b, 0, 0)),
            pl.BlockSpec((None, NSA_CMP_LEN * HEAD_DIM, HEAD_DIM), lambda c, b: (c // 2, 0, 0)),
            pl.BlockSpec((None, 8, NSA_CMP_LEN * HEAD_DIM), lambda c, b: (c // 2, 0, 0)),
            pl.BlockSpec((None, HEAD_DIM, HEAD_DIM), lambda c, b: (c // 2, 0, 0)),
        ],
        out_specs=pl.BlockSpec((None, None, nc, HEAD_DIM), lambda c, b: (c, b, 0, 0)),
        compiler_params=_cp(("arbitrary", "arbitrary")),
    )(hm4, w1, pe, w2)


def _nsa_attn_kernel(q_ref, qa_ref, kc_ref, vc_ref, ks_ref, vs_ref, kw_ref, vw_ref, g_ref, ovt_ref, e_ref,
                     kpos_ref, cpos_ref, o_ref, *, tq, tk, seq, n_sel, n_top):
    qi = pl.program_id(2)
    t0 = qi * tq
    rows = NSA_HG * tq
    nc = seq // NSA_CMP_STRIDE
    q = jnp.concatenate([q_ref[...].reshape(rows, HEAD_DIM), qa_ref[...]], axis=1)

    rid = lax.broadcasted_iota(jnp.int32, (rows, 1), 0)
    trow = t0 + lax.bitwise_and(rid, tq - 1)

    def masked_softmax(s, valid):
        sm = jnp.where(valid, s, NEG)
        m = jnp.max(sm, axis=1, keepdims=True)
        e = jnp.where(valid, jnp.exp(sm - m), 0.0)
        d = jnp.sum(e, axis=1, keepdims=True)
        return e * (1.0 / jnp.where(d > 0.0, d, 1.0))

    n_i = lax.broadcasted_iota(jnp.int32, (1, nc), 1)
    end = n_i * NSA_CMP_STRIDE + (NSA_CMP_LEN - 1)
    s_c = _dot_nt(q, jnp.concatenate([kc_ref[...], cpos_ref[...]], axis=1))
    p_c = masked_softmax(s_c, end <= trow)
    o_c = _dot(p_c.astype(BF16), vc_ref[...])

    psum = p_c[0:tq] + p_c[tq:2 * tq] + p_c[2 * tq:3 * tq] + p_c[3 * tq:4 * tq]
    p_hi = psum.astype(BF16)
    p_lo = (psum - p_hi.astype(F32)).astype(BF16)
    ovt = ovt_ref[...]
    imp = _dot_nt(ovt, p_hi) + _dot_nt(ovt, p_lo)
    cur = lax.shift_right_logical(t0 + lax.broadcasted_iota(jnp.int32, (1, tq), 1),
                                  int(np.log2(NSA_SEL_LEN)))
    blk = lax.broadcasted_iota(jnp.int32, (n_sel, tq), 0)
    forced = (blk == 0) | (blk == cur) | (blk == cur - 1)
    key = jnp.where(blk > cur, -BIG, jnp.where(forced, BIG, imp))
    sub = lax.broadcasted_iota(jnp.int32, (8, tq), 0)
    chunks = [key[8 * r:8 * r + 8, :] for r in range(n_sel // 8)]
    ranks = [jnp.zeros((8, tq), F32) for _ in chunks]
    for i in range(n_sel):
        vi = key[i:i + 1, :]
        for r, kc in enumerate(chunks):
            gt = jnp.where(vi > kc, 1.0, 0.0)
            if r < i // 8:
                win = gt
            else:
                ge = jnp.where(vi >= kc, 1.0, 0.0)
                win = ge if r > i // 8 else jnp.where(sub > i % 8, ge, gt)
            ranks[r] = ranks[r] + win
    rank = jnp.concatenate(ranks, axis=0)
    selm_t = jnp.where((rank < float(n_top)) & (blk <= cur), 1.0, 0.0).astype(BF16)
    eye = (lax.broadcasted_iota(jnp.int32, (n_sel, n_sel), 0)
           == lax.broadcasted_iota(jnp.int32, (n_sel, n_sel), 1)).astype(F32).astype(BF16)
    selm = lax.dot_general(selm_t, eye, (((0,), (0,)), ((), ())), preferred_element_type=F32).astype(BF16)

    def sel_step(kt, carry, diag):
        m, l, acc = carry
        k0 = pl.multiple_of(kt * tk, tk)
        kk = jnp.concatenate([ks_ref[pl.ds(k0, tk), :], kpos_ref[pl.ds(k0, tk), :]], axis=1)
        vv = vs_ref[pl.ds(k0, tk), :]
        mex = _dot(selm, e_ref[kt])
        mex4 = jnp.concatenate([mex] * NSA_HG, axis=0)
        s = _dot_nt(q, kk)
        valid = mex4 > 0.5
        if diag:
            spos = k0 + lax.broadcasted_iota(jnp.int32, (1, tk), 1)
            valid = valid & (spos <= trow)
        sm = jnp.where(valid, s, NEG)
        m_new = jnp.maximum(m, jnp.max(sm, axis=1, keepdims=True))
        alpha = jnp.exp(m - m_new)
        p = jnp.exp((sm - m_new).astype(BF16))
        l = alpha * l + jnp.sum(p.astype(F32), axis=1, keepdims=True)
        acc = alpha * acc + _dot(p, vv)
        return m_new, l, acc

    kt_last = t0 // tk
    init = (jnp.full((rows, 1), NEG, F32), jnp.zeros((rows, 1), F32), jnp.zeros((rows, HEAD_DIM), F32))
    carry = lax.fori_loop(0, kt_last, lambda kt, c: sel_step(kt, c, False), init)
    _, l_s, acc_s = sel_step(kt_last, carry, True)
    o_s = acc_s * (1.0 / l_s)

    wk = NSA_WINDOW + tq
    ks0 = pl.multiple_of(jnp.maximum(t0 - NSA_WINDOW, 0), tq)
    kw = jnp.concatenate([kw_ref[pl.ds(ks0, wk), :], kpos_ref[pl.ds(ks0, wk), :]], axis=1)
    vw = vw_ref[pl.ds(ks0, wk), :]
    wpos = ks0 + lax.broadcasted_iota(jnp.int32, (1, wk), 1)
    dw = trow - wpos
    sm_w = jnp.where((dw >= 0) & (dw < NSA_WINDOW), _dot_nt(q, kw), NEG)
    p_w = jnp.exp((sm_w - jnp.max(sm_w, axis=1, keepdims=True)).astype(BF16))
    o_w = _dot(p_w, vw) * (1.0 / jnp.sum(p_w.astype(F32), axis=1, keepdims=True))

    gt = g_ref[...]
    for hg in range(NSA_HG):
        sl = slice(hg * tq, (hg + 1) * tq)
        o = (gt[:, 3 * hg:3 * hg + 1] * o_c[sl] + gt[:, 3 * hg + 1:3 * hg + 2] * o_s[sl]
             + gt[:, 3 * hg + 2:3 * hg + 3] * o_w[sl])
        o_ref[:, hg * HEAD_DIM:(hg + 1) * HEAD_DIM] = o.astype(o_ref.dtype)


def nsa_attn(hm, cmp, gates, consts, *, q_head0, kv_head0, batch, seq, tq, tk):
    ov, e, qa, kpos, cpos = consts
    nq = seq // tq
    nc = seq // NSA_CMP_STRIDE
    n_sel = seq // NSA_SEL_LEN
    n_top = min(NSA_TOPK, n_sel)
    assert tk % tq == 0 and seq % tk == 0 and seq >= NSA_WINDOW + tq and NSA_WINDOW % tq == 0
    hm4 = hm.reshape(hm.shape[0], batch, seq, HEAD_DIM)
    kv_spec = lambda off: pl.BlockSpec((None, None, seq, HEAD_DIM),
                                       lambda b, g, i: (kv_head0 + off + g, b, 0, 0))
    return pl.pallas_call(
        functools.partial(_nsa_attn_kernel, tq=tq, tk=tk, seq=seq, n_sel=n_sel, n_top=n_top),
        out_shape=jax.ShapeDtypeStruct((batch * seq, NSA_HEADS * HEAD_DIM), BF16),
        grid=(batch, NSA_GROUPS, nq),
        in_specs=[
            pl.BlockSpec((NSA_HG, None, tq, HEAD_DIM), lambda b, g, i: (q_head0 // NSA_HG + g, b, i, 0)),
            pl.BlockSpec((None, NSA_HG * tq, 128), lambda b, g, i: (g, 0, 0)),
            pl.BlockSpec((None, None, nc, HEAD_DIM), lambda b, g, i: (g, b, 0, 0)),
            pl.BlockSpec((None, None, nc, HEAD_DIM), lambda b, g, i: (2 + g, b, 0, 0)),
            kv_spec(4), kv_spec(6), kv_spec(8), kv_spec(10),
            pl.BlockSpec((tq, 128), lambda b, g, i: (b * nq + i, g)),
            pl.BlockSpec((n_sel, nc), lambda b, g, i: (0, 0)),
            pl.BlockSpec((seq // tk, n_sel, tk), lambda b, g, i: (0, 0, 0)),
            pl.BlockSpec((seq, 128), lambda b, g, i: (0, 0)),
            pl.BlockSpec((nc, 128), lambda b, g, i: (0, 0)),
        ],
        out_specs=pl.BlockSpec((tq, NSA_HG * HEAD_DIM), lambda b, g, i: (b * nq + i, g)),
        compiler_params=_cp(("arbitrary", "arbitrary", "arbitrary")),
    )(hm4, qa, cmp, cmp, hm4, hm4, hm4, hm4, gates, ov, e, kpos, cpos)


def _merge_kernel(x_ref, oa_ref, ob_ref, oc_ref, wm_ref, wb_ref, bm_ref, y_ref, xb_ref):
    @pl.when(pl.program_id(1) == 0)
    def _():
        xb_ref[...] = x_ref[...].astype(BF16)

    xb = xb_ref[...]
    acc = None
    for br, o_ref in enumerate((oa_ref, ob_ref, oc_ref)):
        gate = jax.nn.sigmoid(_dot(xb, wm_ref[br]) + bm_ref[br])
        term = gate * _dot(o_ref[...], wb_ref[br])
        acc = term if acc is None else acc + term
    y_ref[...] = acc.astype(y_ref.dtype)


def merge_branches(x, o_a, o_b, o_c, wm, wb, bm, *, tm, tn):
    t = x.shape[0]
    o_spec = pl.BlockSpec((tm, BRANCH_WIDTH), lambda i, j: (i, 0))
    return pl.pallas_call(
        _merge_kernel,
        out_shape=jax.ShapeDtypeStruct((t, D_MODEL), BF16),
        grid=(t // tm, D_MODEL // tn),
        in_specs=[
            pl.BlockSpec((tm, D_MODEL), lambda i, j: (i, 0)),
            o_spec, o_spec, o_spec,
            pl.BlockSpec((N_BRANCH, D_MODEL, tn), lambda i, j: (0, 0, j)),
            pl.BlockSpec((N_BRANCH, BRANCH_WIDTH, tn), lambda i, j: (0, 0, j)),
            pl.BlockSpec((N_BRANCH, 1, tn), lambda i, j: (0, 0, j)),
        ],
        out_specs=pl.BlockSpec((tm, tn), lambda i, j: (i, j)),
        scratch_shapes=[pltpu.VMEM((tm, D_MODEL), BF16)],
        compiler_params=_cp(("arbitrary", "arbitrary")),
    )(x, o_a, o_b, o_c, wm, wb, bm)


def _out_ln_kernel(y_ref, w_ref, x_ref, g_ref, b_ref, o_ref, *, alpha):
    h = _dot(y_ref[...], w_ref[...])
    o_ref[...] = _layer_norm(alpha * x_ref[...] + h, g_ref[...], b_ref[...])


def out_ln(y, w, x, g, b, *, alpha, tm):
    t = x.shape[0]
    return pl.pallas_call(
        functools.partial(_out_ln_kernel, alpha=alpha),
        out_shape=jax.ShapeDtypeStruct((t, D_MODEL), F32),
        grid=(t // tm,),
        in_specs=[
            pl.BlockSpec((tm, D_MODEL), lambda i: (i, 0)),
            pl.BlockSpec((D_MODEL, D_MODEL), lambda i: (0, 0)),
            pl.BlockSpec((tm, D_MODEL), lambda i: (i, 0)),
            pl.BlockSpec((1, D_MODEL), lambda i: (0, 0)),
            pl.BlockSpec((1, D_MODEL), lambda i: (0, 0)),
        ],
        out_specs=pl.BlockSpec((tm, D_MODEL), lambda i: (i, 0)),
        compiler_params=_cp(("arbitrary",)),
    )(y, w, x, g, b)


def _mem_attn_kernel(x_ref, wq_ref, k_ref, v_ref, wo_ref, g_ref, b_ref, o_ref, *, alpha):
    x = x_ref[...]
    q = _dot(x.astype(BF16), wq_ref[...]) * (HEAD_DIM ** -0.5)
    outs = []
    for h in range(MEM_HEADS):
        qh = q[:, h * HEAD_DIM:(h + 1) * HEAD_DIM].astype(BF16)
        s = _dot_nt(qh, k_ref[h])
        m = jnp.max(s, axis=1, keepdims=True)
        e = jnp.exp(s - m)
        p = e * (1.0 / jnp.sum(e, axis=1, keepdims=True))
        outs.append(_dot(p.astype(BF16), v_ref[h]).astype(BF16))
    o = jnp.concatenate(outs, axis=1)
    h_out = _dot(o, wo_ref[...])
    o_ref[...] = _layer_norm(alpha * x + h_out, g_ref[...], b_ref[...])


def mem_attn_ln(x, wq, kv, wo, g, b, *, alpha, seq, mem_len, tm):
    t = x.shape[0]
    per_b = seq // tm
    kv4 = kv.reshape(2 * MEM_HEADS, t // seq, mem_len, HEAD_DIM)
    width = MEM_HEADS * HEAD_DIM
    return pl.pallas_call(
        functools.partial(_mem_attn_kernel, alpha=alpha),
        out_shape=jax.ShapeDtypeStruct((t, D_MODEL), F32),
        grid=(t // tm,),
        in_specs=[
            pl.BlockSpec((tm, D_MODEL), lambda i: (i, 0)),
            pl.BlockSpec((D_MODEL, width), lambda i: (0, 0)),
            pl.BlockSpec((MEM_HEADS, None, mem_len, HEAD_DIM), lambda i: (0, i // per_b, 0, 0)),
            pl.BlockSpec((MEM_HEADS, None, mem_len, HEAD_DIM), lambda i: (1, i // per_b, 0, 0)),
            pl.BlockSpec((width, D_MODEL), lambda i: (0, 0)),
            pl.BlockSpec((1, D_MODEL), lambda i: (0, 0)),
            pl.BlockSpec((1, D_MODEL), lambda i: (0, 0)),
        ],
        out_specs=pl.BlockSpec((tm, D_MODEL), lambda i: (i, 0)),
        compiler_params=_cp(("arbitrary",)),
    )(x, wq, kv4, kv4, wo, g, b)


def _router_kernel(x_ref, w_ref, b_ref, tri_ref, mask_ref, wsel_ref, xp_ref, before_ref, count_ref):
    @pl.when(pl.program_id(0) == 0)
    def _():
        count_ref[...] = jnp.zeros_like(count_ref)

    x = x_ref[...]
    xp_ref[...] = _pack_bf16_pairs(x)
    xh = x.astype(BF16)
    xl = (x - xh.astype(F32)).astype(BF16)
    logits = _dot(xh, w_ref[0]) + _dot(xh, w_ref[1]) + _dot(xl, w_ref[0]) + b_ref[...]
    tm = logits.shape[0]
    lane = lax.broadcasted_iota(jnp.int32, (tm, N_EXPERTS), 1)
    work = logits
    hots, vals = [], []
    for _ in range(TOP_K):
        m = jnp.max(work, axis=1, keepdims=True)
        idx = jnp.min(jnp.where(work == m, lane, N_EXPERTS), axis=1, keepdims=True)
        hot = lane == idx
        hots.append(hot)
        vals.append(m)
        work = jnp.where(hot, -jnp.inf, work)
    es = [jnp.exp(v - vals[0]) for v in vals]
    inv = 1.0 / (es[0] + es[1] + es[2] + es[3])
    mask = jnp.zeros((tm, N_EXPERTS), F32)
    wsel = jnp.zeros((tm, N_EXPERTS), F32)
    for hot, e in zip(hots, es):
        mask = mask + jnp.where(hot, 1.0, 0.0)
        wsel = wsel + jnp.where(hot, e * inv, 0.0)
    mask_ref[...] = mask
    wsel_ref[...] = wsel
    before_ref[...] = count_ref[...] + _dot(tri_ref[...], mask.astype(BF16))
    count_ref[...] = count_ref[...] + jnp.sum(mask, axis=0, keepdims=True)


def router(x, w_hl, b, tri, *, tm):
    t = x.shape[0]
    tok32 = lambda: pl.BlockSpec((tm, N_EXPERTS), lambda i: (i, 0))
    return pl.pallas_call(
        _router_kernel,
        out_shape=(jax.ShapeDtypeStruct((t, N_EXPERTS), F32), jax.ShapeDtypeStruct((t, N_EXPERTS), F32),
                   jax.ShapeDtypeStruct((t, D_MODEL // 2), jnp.uint32),
                   jax.ShapeDtypeStruct((t, N_EXPERTS), F32)),
        grid=(t // tm,),
        in_specs=[
            pl.BlockSpec((tm, D_MODEL), lambda i: (i, 0)),
            pl.BlockSpec((2, D_MODEL, N_EXPERTS), lambda i: (0, 0, 0)),
            pl.BlockSpec((1, N_EXPERTS), lambda i: (0, 0)),
            pl.BlockSpec((tm, tm), lambda i: (0, 0)),
        ],
        out_specs=(tok32(), tok32(), pl.BlockSpec((tm, D_MODEL // 2), lambda i: (i, 0)), tok32()),
        scratch_shapes=[pltpu.VMEM((1, N_EXPERTS), F32)],
        compiler_params=_cp(("arbitrary",)),
    )(x, w_hl, b, tri)


def sc_gather_rows(table, idx, *, chunk):
    n = idx.shape[0]
    d = table.shape[1]
    workers = SC_CORES_V7X * SC_SUBCORES_V7X
    per_w = n // workers
    assert n % (workers * chunk) == 0 and chunk % 8 == 0 and chunk <= 128
    mesh = plsc.VectorSubcoreMesh(core_axis_name="c", subcore_axis_name="s")

    @functools.partial(
        pl.kernel, mesh=mesh,
        out_type=jax.ShapeDtypeStruct((n, d), table.dtype),
        scratch_types=[pltpu.VMEM((chunk,), jnp.int32), pltpu.VMEM((chunk, d), table.dtype),
                       pltpu.SemaphoreType.DMA],
    )
    def gather(table_hbm, idx_hbm, out_hbm, idx_v, rows_v, sem):
        wid = lax.axis_index("s") * SC_CORES_V7X + lax.axis_index("c")
        base = wid * per_w

        @pl.loop(0, per_w // chunk)
        def _(j):
            off = pl.multiple_of(base + j * chunk, 8)
            pltpu.sync_copy(idx_hbm.at[pl.ds(off, chunk)], idx_v)
            pltpu.async_copy(table_hbm.at[idx_v], rows_v, sem).wait()
            pltpu.sync_copy(rows_v, out_hbm.at[pl.ds(off, chunk)])

    return gather(table, idx)


def sc_scatter_rows(rows, idx, n_out, *, copies, chunk):
    t, d = rows.shape
    workers = SC_CORES_V7X * SC_SUBCORES_V7X
    per_w = t // workers
    assert idx.shape == (copies * t,) and t % (workers * chunk) == 0 and chunk % 8 == 0 and chunk <= 128
    mesh = plsc.VectorSubcoreMesh(core_axis_name="c", subcore_axis_name="s")

    @functools.partial(
        pl.kernel, mesh=mesh,
        out_type=jax.ShapeDtypeStruct((n_out, d), rows.dtype),
        scratch_types=[pltpu.VMEM((chunk,), jnp.int32), pltpu.VMEM((chunk, d), rows.dtype)],
    )
    def scatter(rows_hbm, idx_hbm, out_hbm, idx_v, rows_v):
        wid = lax.axis_index("s") * SC_CORES_V7X + lax.axis_index("c")
        base = wid * per_w

        @pl.loop(0, per_w // chunk)
        def _(j):
            off = pl.multiple_of(base + j * chunk, 8)
            pltpu.sync_copy(rows_hbm.at[pl.ds(off, chunk)], rows_v)
            for k in range(copies):
                pltpu.sync_copy(idx_hbm.at[pl.ds(pl.multiple_of(k * t + off, 8), chunk)], idx_v)
                pltpu.sync_copy(rows_v, out_hbm.at[idx_v])

    return scatter(rows, idx)


def _experts_kernel(be_ref, x_ref, wg_ref, bg_ref, wu_ref, bu_ref, wd_ref, bd_ref, y_ref,
                    wgb_ref, wub_ref, wdb_ref):
    i = pl.program_id(0)
    prev = be_ref[jnp.maximum(i - 1, 0)]

    @pl.when((i == 0) | (be_ref[i] != prev))
    def _():
        wgb_ref[...] = wg_ref[...].astype(BF16)
        wub_ref[...] = wu_ref[...].astype(BF16)
        wdb_ref[...] = wd_ref[...].astype(BF16)

    x_lo, x_hi = _unpack_bf16_pairs(x_ref[...])
    xb = jnp.concatenate([x_lo.astype(BF16), x_hi.astype(BF16)], axis=1)
    g = jnp.minimum(_dot(xb, wgb_ref[...]) + bg_ref[...], SWIGLU_LIMIT)
    u = jnp.clip(_dot(xb, wub_ref[...]) + bu_ref[...], -SWIGLU_LIMIT, SWIGLU_LIMIT)
    hdn = (u + 1.0) * (g * jax.nn.sigmoid(SWIGLU_ALPHA * g))
    y_ref[...] = _pack_bf16_pairs(_dot(hdn.astype(BF16), wdb_ref[...]) + bd_ref[...])


def experts(x_rows, blk_e, wg, bg, wu, bu, wd, bd, *, layer, bm):
    n_rows, dp = x_rows.shape
    d, f = wg.shape[2], wg.shape[3]
    w_spec = lambda shape: pl.BlockSpec((None, None) + shape, lambda i, be: (layer, be[i], 0, 0))
    grid_spec = pltpu.PrefetchScalarGridSpec(
        num_scalar_prefetch=1,
        grid=(n_rows // bm,),
        in_specs=[
            pl.BlockSpec((bm, dp), lambda i, be: (i, 0)),
            w_spec((d, f)), w_spec((1, f)), w_spec((d, f)), w_spec((1, f)), w_spec((f, d)), w_spec((1, d)),
        ],
        out_specs=pl.BlockSpec((bm, dp), lambda i, be: (i, 0)),
        scratch_shapes=[pltpu.VMEM((d, f), BF16), pltpu.VMEM((d, f), BF16), pltpu.VMEM((f, d), BF16)],
    )
    return pl.pallas_call(
        _experts_kernel,
        out_shape=jax.ShapeDtypeStruct((n_rows, dp), jnp.uint32),
        grid_spec=grid_spec,
        compiler_params=_cp(("arbitrary",)),
    )(blk_e, x_rows, wg, bg, wu, bu, wd, bd)


def _moe_ln_kernel(x_ref, y_ref, w_ref, g_ref, b_ref, o_ref, *, alpha):
    w = w_ref[...]
    y_lo = y_hi = None
    for k in range(TOP_K):
        lo, hi = _unpack_bf16_pairs(y_ref[k])
        wk = w[:, k:k + 1]
        y_lo = wk * lo if y_lo is None else y_lo + wk * lo
        y_hi = wk * hi if y_hi is None else y_hi + wk * hi
    y = jnp.concatenate([y_lo, y_hi], axis=1)
    o_ref[...] = _layer_norm(alpha * x_ref[...] + y, g_ref[...], b_ref[...])


def moe_ln(x, y4, w4p, g, b, *, alpha, tm):
    t = x.shape[0]
    return pl.pallas_call(
        functools.partial(_moe_ln_kernel, alpha=alpha),
        out_shape=jax.ShapeDtypeStruct((t, D_MODEL), F32),
        grid=(t // tm,),
        in_specs=[
            pl.BlockSpec((tm, D_MODEL), lambda i: (i, 0)),
            pl.BlockSpec((TOP_K, tm, D_MODEL // 2), lambda i: (0, i, 0)),
            pl.BlockSpec((tm, 128), lambda i: (i, 0)),
            pl.BlockSpec((1, D_MODEL), lambda i: (0, 0)),
            pl.BlockSpec((1, D_MODEL), lambda i: (0, 0)),
        ],
        out_specs=pl.BlockSpec((tm, D_MODEL), lambda i: (i, 0)),
        compiler_params=_cp(("arbitrary",)),
    )(x, y4, w4p, g, b)


def _rope_tables(seq):
    inv = np.asarray(ROPE_THETA ** (-np.arange(0, MLA_ROPE, 2) / MLA_ROPE), np.float32)
    ang = jnp.arange(seq, dtype=F32)[:, None] * jnp.asarray(inv)[None, :]
    cos, sin = jnp.cos(ang), jnp.sin(ang)
    zeros = jnp.zeros((seq, 128 - MLA_ROPE), F32)
    return (jnp.concatenate([cos, cos, zeros], axis=1), jnp.concatenate([-sin, sin, zeros], axis=1))


def _nsa_constants(seq, tq, tk):
    nc = seq // NSA_CMP_STRIDE
    qa = np.zeros((NSA_GROUPS, NSA_HG * tq, 128), np.float32)
    for g in range(NSA_GROUPS):
        for hg in range(NSA_HG):
            slope = 2.0 ** (-8.0 * (g * NSA_HG + hg + 1) / NSA_HEADS)
            qa[g, hg * tq:(hg + 1) * tq, 0] = slope * NSA_SEL_LEN
            qa[g, hg * tq:(hg + 1) * tq, 1] = slope
            qa[g, hg * tq:(hg + 1) * tq, 2] = slope * NSA_CMP_STRIDE
            qa[g, hg * tq:(hg + 1) * tq, 3] = slope * (NSA_CMP_LEN - 1) / 2.0
    kpos = np.zeros((seq, 128), np.float32)
    kpos[:, 0] = np.arange(seq) // NSA_SEL_LEN
    kpos[:, 1] = np.arange(seq) % NSA_SEL_LEN
    cpos = np.zeros((nc, 128), np.float32)
    cpos[:, 2] = np.arange(nc)
    cpos[:, 3] = 1.0
    for arr in (qa, kpos, cpos):
        assert np.array_equal(arr.astype(BF16).astype(np.float32), arr)
    n_cmp = (seq - NSA_CMP_LEN) // NSA_CMP_STRIDE + 1
    n_sel = seq // NSA_SEL_LEN
    cs = np.arange(nc) * NSA_CMP_STRIDE
    ss = np.arange(n_sel) * NSA_SEL_LEN
    ov = np.clip(np.minimum(cs[:, None] + NSA_CMP_LEN, ss[None, :] + NSA_SEL_LEN)
                 - np.maximum(cs[:, None], ss[None, :]), 0, None) / NSA_CMP_LEN
    ov[n_cmp:] = 0.0
    e = (np.arange(seq)[None, :] // NSA_SEL_LEN == np.arange(n_sel)[:, None]).astype(np.float32)
    e = e.reshape(n_sel, seq // tk, tk).transpose(1, 0, 2)
    return tuple(jnp.asarray(a, BF16) for a in (ov.T, e, qa, kpos, cpos))


def _pad_cols(w, width):
    return jnp.pad(w, ((0, 0), (0, width - w.shape[1])))


def _swap_halves(w):
    half = w.shape[1] // 2
    return jnp.concatenate([w[:, half:], w[:, :half]], axis=1)


def _layer_weights(w_in, w_q_up, w_kv_up):
    kr = w_in[:, OFF_KR:OFF_NSA_Q]
    gate = w_in[:, OFF_NSA_GATE:OFF_SB]
    per_g = NSA_HG * 3
    w_mla = jnp.concatenate([
        w_in[:, OFF_CQ:OFF_KR],
        _pad_cols(kr, 128), _pad_cols(_swap_halves(kr), 128),
        _pad_cols(gate[:, :per_g], 128), _pad_cols(gate[:, per_g:], 128)], axis=1).astype(BF16)
    wq = w_q_up.reshape(MLA_Q_RANK, MLA_HEADS, MLA_NOPE + MLA_ROPE)
    rope = wq[:, :, MLA_NOPE:]
    rope_sw = jnp.concatenate([rope[:, :, MLA_ROPE // 2:], rope[:, :, :MLA_ROPE // 2]], axis=2)
    pad = ((0, 0), (0, 0), (0, 128 - MLA_ROPE))
    wq3 = jnp.concatenate([
        wq[:, :, :MLA_NOPE].reshape(MLA_Q_RANK, -1),
        jnp.pad(rope, pad).reshape(MLA_Q_RANK, -1),
        jnp.pad(rope_sw, pad).reshape(MLA_Q_RANK, -1)], axis=1).astype(BF16)
    wkv = w_kv_up.reshape(MLA_KV_RANK, MLA_HEADS, 2, 128)
    wkv = jnp.concatenate([wkv[:, :, 0].reshape(MLA_KV_RANK, -1),
                           wkv[:, :, 1].reshape(MLA_KV_RANK, -1)], axis=1).astype(BF16)
    w_heads = w_in[:, OFF_NSA_Q:OFF_NSA_GATE]
    w_heads = jnp.concatenate([w_heads, w_in[:, OFF_SB:OFF_MERGE]], axis=1).astype(BF16)
    wm = w_in[:, OFF_MERGE:].reshape(D_MODEL, N_BRANCH, D_MODEL).transpose(1, 0, 2).astype(BF16)
    return w_mla, wq3, wkv, w_heads, wm


def _forward(x, mem, w_in, mla_q_norm, mla_w_q_up, mla_kv_norm, mla_w_kv_up,
             nsa_pe_k, nsa_pe_v, nsa_w1_k, nsa_w1_v, nsa_w2_k, nsa_w2_v,
             w_branch, b_merge, w_out, ln_mix_g, ln_mix_b,
             mem_w_q, mem_w_k, mem_w_v, mem_w_o, ln_mem_g, ln_mem_b,
             moe_w_router, moe_b_router, moe_w_gate, moe_b_gate, moe_w_up, moe_b_up,
             moe_w_down, moe_b_down, ln_moe_g, ln_moe_b):
    batch, seq, _ = x.shape
    mem_len = mem.shape[1]
    depth = w_in.shape[0]
    t = batch * seq
    alpha = float((2 * depth) ** 0.25)
    bm = MOE_BLOCK_ROWS
    n_rows = t * TOP_K + N_EXPERTS * bm
    n_blocks = n_rows // bm

    tm_in = min(512, seq)
    tq_mla = min(512, seq)
    tq_sb = 256
    tq_nsa, tk_nsa = 128, 512
    tm_ln = 256
    sc_chunk = 64

    cos128, sin128 = _rope_tables(seq)
    nsa_consts = _nsa_constants(seq, tq_nsa, tk_nsa)
    u_sb = jnp.asarray(np.arange(tq_sb)[:, None] > np.arange(tq_sb)[None, :], BF16)
    tri_router = jnp.asarray(np.arange(tm_in)[:, None] > np.arange(tm_in)[None, :], BF16)
    n_qheads = NSA_HEADS
    n_kvheads = 3 * 2 * NSA_GROUPS
    head_scale = np.ones((1, (n_qheads + n_kvheads + 3 * SB_HEADS) * HEAD_DIM), np.float32)
    head_scale[:, :n_qheads * HEAD_DIM] = HEAD_DIM ** -0.5
    sb0 = n_qheads + n_kvheads
    head_scale[:, sb0 * HEAD_DIM:(sb0 + SB_HEADS) * HEAD_DIM] = HEAD_DIM ** -0.5
    head_scale = jnp.asarray(head_scale)
    ones_kv = jnp.ones((1, 2 * MEM_HEADS * HEAD_DIM), F32)

    b_gate4 = moe_b_gate.reshape(depth, N_EXPERTS, 1, D_EXPERT)
    b_up4 = moe_b_up.reshape(depth, N_EXPERTS, 1, D_EXPERT)
    b_down4 = moe_b_down.reshape(depth, N_EXPERTS, 1, D_MODEL)

    xf = x.reshape(t, D_MODEL)
    memf = mem.reshape(batch * mem_len, D_MODEL)
    row = lambda v: v.reshape(1, -1)

    for l in range(depth):
        w_mla, wq3, wkv, w_heads, wm = _layer_weights(w_in[l], mla_w_q_up[l], mla_w_kv_up[l])

        q_a, k_a, v_a, gates = mla_in(xf, w_mla, row(mla_q_norm[l]), row(mla_kv_norm[l]), wq3, wkv,
                                      cos128, sin128, seq=seq, tm=tm_ln)
        hm = proj_heads(xf, w_heads, head_scale, tm=tm_in, tn=11 * HEAD_DIM)
        o_a = mla_attn(q_a, k_a, v_a, batch=batch, seq=seq, tq=tq_mla, heads=2)
        w1 = jnp.stack([nsa_w1_k[l], nsa_w1_v[l]]).astype(BF16)
        pe = jnp.stack([nsa_pe_k[l], nsa_pe_v[l]]).reshape(2, 1, -1)
        pe = jnp.broadcast_to(pe, (2, 8, pe.shape[-1])).astype(BF16)
        w2 = jnp.stack([nsa_w2_k[l], nsa_w2_v[l]]).astype(BF16)
        cmp = nsa_compress(hm[n_qheads:n_qheads + 4], w1, pe, w2, batch=batch, seq=seq)
        o_b = nsa_attn(hm, cmp, gates, nsa_consts, q_head0=0, kv_head0=n_qheads,
                       batch=batch, seq=seq, tq=tq_nsa, tk=tk_nsa)
        o_c = sb_attn(hm, u_sb, head0=sb0, batch=batch, seq=seq, tq=tq_sb, heads=4)
        y = merge_branches(xf, o_a, o_b, o_c, wm, w_branch[l].astype(BF16),
                           b_merge[l].reshape(N_BRANCH, 1, D_MODEL), tm=tm_in, tn=512)
        xf = out_ln(y, w_out[l].astype(BF16), xf, row(ln_mix_g[l]), row(ln_mix_b[l]), alpha=alpha, tm=tm_ln)

        w_kv_mem = jnp.concatenate([mem_w_k[l], mem_w_v[l]], axis=1).astype(BF16)
        kv_mem = proj_heads(memf, w_kv_mem, ones_kv, tm=min(512, batch * mem_len), tn=512)
        xf = mem_attn_ln(xf, mem_w_q[l].astype(BF16), kv_mem, mem_w_o[l].astype(BF16),
                         row(ln_mem_g[l]), row(ln_mem_b[l]), alpha=alpha, seq=seq, mem_len=mem_len, tm=tm_ln)

        wr = moe_w_router[l]
        wr_hi = wr.astype(BF16)
        wr_lo = (wr - wr_hi.astype(F32)).astype(BF16)
        mask, wsel, x_packed, before = router(xf, jnp.stack([wr_hi, wr_lo]), row(moe_b_router[l]),
                                              tri_router, tm=tm_in)
        counts = (before[-1] + mask[-1]).astype(jnp.int32)
        padded = (counts + bm - 1) // bm * bm
        pad_end = jnp.cumsum(padded)
        pad_start = pad_end - padded
        slot = pad_start[None, :] + before.astype(jnp.int32)
        top_e = lax.top_k(mask, TOP_K)[1]
        pos4 = jnp.take_along_axis(slot, top_e, axis=1).astype(jnp.int32)
        w4 = jnp.take_along_axis(wsel, top_e, axis=1)
        blk_row0 = jnp.arange(n_blocks, dtype=jnp.int32)[:, None] * bm
        blk_e = jnp.minimum(jnp.sum((pad_end[None, :] <= blk_row0).astype(jnp.int32), axis=1), N_EXPERTS - 1)
        pos_kmajor = pos4.T.reshape(-1)
        x_rows = sc_scatter_rows(x_packed, pos_kmajor, n_rows, copies=TOP_K, chunk=sc_chunk)
        y_rows = experts(x_rows, blk_e, moe_w_gate, b_gate4, moe_w_up, b_up4, moe_w_down, b_down4,
                         layer=l, bm=bm)
        y4 = sc_gather_rows(y_rows, pos_kmajor, chunk=sc_chunk).reshape(TOP_K, t, D_MODEL // 2)
        xf = moe_ln(xf, y4, _pad_cols(w4, 128), row(ln_moe_g[l]), row(ln_moe_b[l]), alpha=alpha, tm=tm_ln)

    return xf.reshape(batch, seq, D_MODEL)


def kernel(x, mem, w_in, mla_q_norm, mla_w_q_up, mla_kv_norm, mla_w_kv_up, nsa_pe_k, nsa_pe_v, nsa_w1_k, nsa_w1_v, nsa_w2_k, nsa_w2_v, w_branch, b_merge, w_out, ln_mix_g, ln_mix_b, mem_w_q, mem_w_k, mem_w_v, mem_w_o, ln_mem_g, ln_mem_b, moe_w_router, moe_b_router, moe_w_gate, moe_b_gate, moe_w_up, moe_b_up, moe_w_down, moe_b_down, ln_moe_g, ln_moe_b):
    return _forward(x, mem, w_in, mla_q_norm, mla_w_q_up, mla_kv_norm, mla_w_kv_up,
                    nsa_pe_k, nsa_pe_v, nsa_w1_k, nsa_w1_v, nsa_w2_k, nsa_w2_v,
                    w_branch, b_merge, w_out, ln_mix_g, ln_mix_b,
                    mem_w_q, mem_w_k, mem_w_v, mem_w_o, ln_mem_g, ln_mem_b,
                    moe_w_router, moe_b_router, moe_w_gate, moe_b_gate, moe_w_up, moe_b_up,
                    moe_w_down, moe_b_down, ln_moe_g, ln_moe_b)
```

```python
import functools

import numpy as np
import jax
import jax.numpy as jnp
from jax import lax
from jax.experimental import pallas as pl
from jax.experimental.pallas import tpu as pltpu
from jax.experimental.pallas import tpu_sc as plsc

F32 = jnp.float32
BF16 = jnp.bfloat16

D_MODEL = 2048
HEAD_DIM = 128
MLA_HEADS = 8
MLA_Q_RANK = 512
MLA_KV_RANK = 256
MLA_NOPE = 128
MLA_ROPE = 64
ROPE_THETA = 10000.0
NSA_HEADS = 8
NSA_GROUPS = 2
NSA_HG = NSA_HEADS // NSA_GROUPS
NSA_CMP_LEN = 32
NSA_CMP_STRIDE = 16
NSA_SEL_LEN = 64
NSA_TOPK = 16
NSA_WINDOW = 512
SB_HEADS = 8
MEM_HEADS = 4
N_EXPERTS = 32
TOP_K = 4
D_EXPERT = 512
SWIGLU_LIMIT = 7.0
SWIGLU_ALPHA = 1.702
N_BRANCH = 3
BRANCH_WIDTH = 1024
LN_EPS = 1e-5
RMS_EPS = 1e-6
NEG = -1e30
BIG = 1e30
SB_UNDERFLOW_LOG = -100.0

OFF_CQ = 0
OFF_CKV = 512
OFF_KR = 768
OFF_NSA_Q = 832
OFF_NSA_KV = 1856
OFF_NSA_GATE = 3392
OFF_SB = 3416
OFF_MERGE = 6488

VMEM_LIMIT_V7X = 56 * 1024 * 1024
MOE_BLOCK_ROWS = 512
SC_CORES_V7X = 2
SC_SUBCORES_V7X = 16


def _cp(sem, vmem=VMEM_LIMIT_V7X):
    return pltpu.CompilerParams(dimension_semantics=sem, vmem_limit_bytes=vmem)


def _dot(a, b):
    return jnp.dot(a, b, preferred_element_type=F32)


def _dot_nt(a, b):
    return lax.dot_general(a, b, (((1,), (1,)), ((), ())), preferred_element_type=F32)


def _layer_norm(z, g, b):
    mu = jnp.mean(z, axis=-1, keepdims=True)
    zc = z - mu
    var = jnp.mean(zc * zc, axis=-1, keepdims=True)
    return zc * lax.rsqrt(var + LN_EPS) * g + b


def _rms_norm(z, g):
    return z * lax.rsqrt(jnp.mean(z * z, axis=-1, keepdims=True) + RMS_EPS) * g


def _pack_bf16_pairs(z):
    n = z.shape[1] // 2
    bits = pltpu.bitcast(z.astype(BF16).astype(F32), jnp.uint32)
    return lax.shift_right_logical(bits[:, :n], jnp.uint32(16)) | (bits[:, n:] & jnp.uint32(0xFFFF0000))


def _unpack_bf16_pairs(w):
    lo = pltpu.bitcast(lax.shift_left(w, jnp.uint32(16)), F32)
    hi = pltpu.bitcast(w & jnp.uint32(0xFFFF0000), F32)
    return lo, hi


def _proj_heads_kernel(a_ref, w_ref, s_ref, o_ref, abf_ref, *, n_heads_per_tile):
    @pl.when(pl.program_id(1) == 0)
    def _():
        abf_ref[...] = a_ref[...].astype(BF16)

    acc = _dot(abf_ref[...], w_ref[...]) * s_ref[...]
    for c in range(n_heads_per_tile):
        o_ref[c] = acc[:, c * HEAD_DIM:(c + 1) * HEAD_DIM].astype(o_ref.dtype)


def proj_heads(a, w, scale, *, tm, tn):
    m, k = a.shape
    n = w.shape[1]
    hpt = tn // HEAD_DIM
    return pl.pallas_call(
        functools.partial(_proj_heads_kernel, n_heads_per_tile=hpt),
        out_shape=jax.ShapeDtypeStruct((n // HEAD_DIM, m, HEAD_DIM), BF16),
        grid=(m // tm, n // tn),
        in_specs=[
            pl.BlockSpec((tm, k), lambda i, j: (i, 0)),
            pl.BlockSpec((k, tn), lambda i, j: (0, j)),
            pl.BlockSpec((1, tn), lambda i, j: (0, j)),
        ],
        out_specs=pl.BlockSpec((hpt, tm, HEAD_DIM), lambda i, j: (j, i, 0)),
        scratch_shapes=[pltpu.VMEM((tm, k), BF16)],
        compiler_params=_cp(("arbitrary", "arbitrary")),
    )(a, w, scale)


def _mla_in_kernel(x_ref, w_ref, qg_ref, kg_ref, wq_ref, wkv_ref, cos_ref, sin_ref,
                   q_ref, k_ref, v_ref, g_ref):
    xb = x_ref[...].astype(BF16)
    h = _dot(xb, w_ref[...])
    cq = h[:, 0:512]
    ckv = h[:, 512:768]
    kr1 = h[:, 768:896]
    kr2 = h[:, 896:1024]
    g_ref[...] = jax.nn.sigmoid(h[:, 1024:1280])
    cos = cos_ref[...]
    sin = sin_ref[...]
    scale = (MLA_NOPE + MLA_ROPE) ** -0.5
    nq = _rms_norm(cq, qg_ref[...]).astype(BF16)
    q3 = _dot(nq, wq_ref[...])
    for hh in range(MLA_HEADS):
        lo, hi = hh * 128, (hh + 1) * 128
        q_ref[hh, :, 0:128] = (q3[:, lo:hi] * scale).astype(BF16)
        rot = q3[:, 1024 + lo:1024 + hi] * cos + q3[:, 2048 + lo:2048 + hi] * sin
        q_ref[hh, :, 128:256] = (rot * scale).astype(BF16)
    nkv = _rms_norm(ckv, kg_ref[...]).astype(BF16)
    kv = _dot(nkv, wkv_ref[...])
    krot = (kr1 * cos + kr2 * sin).astype(BF16)
    for hh in range(MLA_HEADS):
        lo, hi = hh * 128, (hh + 1) * 128
        k_ref[hh, :, 0:128] = kv[:, lo:hi].astype(BF16)
        k_ref[hh, :, 128:256] = krot
        v_ref[hh] = kv[:, 1024 + lo:1024 + hi].astype(BF16)


def mla_in(x, w_mla, qg, kg, wq3, wkv, cos128, sin128, *, seq, tm):
    t = x.shape[0]
    npos = seq // tm
    full = lambda shape: pl.BlockSpec(shape, lambda i: (0,) * len(shape))
    return pl.pallas_call(
        _mla_in_kernel,
        out_shape=(
            jax.ShapeDtypeStruct((MLA_HEADS, t, 256), BF16),
            jax.ShapeDtypeStruct((MLA_HEADS, t, 256), BF16),
            jax.ShapeDtypeStruct((MLA_HEADS, t, 128), BF16),
            jax.ShapeDtypeStruct((t, 256), F32),
        ),
        grid=(t // tm,),
        in_specs=[
            pl.BlockSpec((tm, D_MODEL), lambda i: (i, 0)),
            full((D_MODEL, 1280)),
            full((1, MLA_Q_RANK)),
            full((1, MLA_KV_RANK)),
            full((MLA_Q_RANK, 3072)),
            full((MLA_KV_RANK, 2048)),
            pl.BlockSpec((tm, 128), lambda i: (i % npos, 0)),
            pl.BlockSpec((tm, 128), lambda i: (i % npos, 0)),
        ],
        out_specs=(
            pl.BlockSpec((MLA_HEADS, tm, 256), lambda i: (0, i, 0)),
            pl.BlockSpec((MLA_HEADS, tm, 256), lambda i: (0, i, 0)),
            pl.BlockSpec((MLA_HEADS, tm, 128), lambda i: (0, i, 0)),
            pl.BlockSpec((tm, 256), lambda i: (i, 0)),
        ),
        compiler_params=_cp(("arbitrary",)),
    )(x, w_mla, qg, kg, wq3, wkv, cos128, sin128)


def _mla_attn_kernel(q_ref, k_ref, v_ref, o_ref, s_ref, *, tq, heads):
    qi = pl.program_id(2)

    def scores(h, kt):
        k0 = pl.multiple_of(kt * tq, tq)
        return _dot_nt(q_ref[h], k_ref[h, pl.ds(k0, tq), :])

    def consume(h, kt, s, carry, diag):
        m, l, acc = carry
        k0 = pl.multiple_of(kt * tq, tq)
        v = v_ref[h, pl.ds(k0, tq), :]
        if diag:
            row = lax.broadcasted_iota(jnp.int32, (tq, tq), 0)
            col = lax.broadcasted_iota(jnp.int32, (tq, tq), 1)
            s = jnp.where(col <= row, s, NEG)
        m_new = jnp.maximum(m, jnp.max(s, axis=1, keepdims=True))
        alpha = jnp.exp(m - m_new)
        p = jnp.exp((s - m_new).astype(BF16))
        l = alpha * l + jnp.sum(p.astype(F32), axis=1, keepdims=True)
        acc = alpha * acc + _dot(p, v)
        return m_new, l, acc

    def fill(slot, kt):
        for h in range(heads):
            s_ref[slot, h] = scores(h, kt)

    def drain(slot, kt, carries, diag):
        return tuple(consume(h, kt, s_ref[slot, h], carries[h], diag) for h in range(heads))

    def body(j, carries):
        kt = 2 * j
        fill(1, kt + 1)
        carries = drain(0, kt, carries, False)
        fill(0, kt + 2)
        return drain(1, kt + 1, carries, False)

    init = (jnp.full((tq, 1), NEG, F32), jnp.zeros((tq, 1), F32), jnp.zeros((tq, 128), F32))
    fill(0, 0)
    carries = lax.fori_loop(0, qi // 2, body, (init,) * heads)

    def even_tail(carries):
        return drain(0, qi, carries, True)

    def odd_tail(carries):
        fill(1, qi)
        return drain(1, qi, drain(0, qi - 1, carries, False), True)

    carries = lax.cond(lax.rem(qi, 2) == 1, odd_tail, even_tail, carries)
    for h, (_, l, acc) in enumerate(carries):
        o_ref[:, h * 128:(h + 1) * 128] = (acc / l).astype(o_ref.dtype)


def mla_attn(q, k, v, *, batch, seq, tq, heads):
    nq = seq // tq
    q4 = q.reshape(MLA_HEADS, batch, seq, 256)
    k4 = k.reshape(MLA_HEADS, batch, seq, 256)
    v4 = v.reshape(MLA_HEADS, batch, seq, 128)
    return pl.pallas_call(
        functools.partial(_mla_attn_kernel, tq=tq, heads=heads),
        out_shape=jax.ShapeDtypeStruct((batch * seq, MLA_HEADS * 128), BF16),
        grid=(MLA_HEADS // heads, batch, nq),
        in_specs=[
            pl.BlockSpec((heads, None, tq, 256), lambda h, b, i: (h, b, i, 0)),
            pl.BlockSpec((heads, None, seq, 256), lambda h, b, i: (h, b, 0, 0)),
            pl.BlockSpec((heads, None, seq, 128), lambda h, b, i: (h, b, 0, 0)),
        ],
        out_specs=pl.BlockSpec((tq, heads * 128), lambda h, b, i: (b * nq + i, h)),
        scratch_shapes=[pltpu.VMEM((2, heads, tq, tq), F32)],
        compiler_params=_cp(("arbitrary", "arbitrary", "arbitrary")),
    )(q4, k4, v4)


def _sb_attn_kernel(q_ref, k_ref, v_ref, u_ref, o_ref, *, tq, heads):
    qi = pl.program_id(2)
    u = u_ref[...]

    def head_step(h, kt, carry, diag):
        run, acc = carry
        k0 = pl.multiple_of(kt * tq, tq)
        k = k_ref[h, pl.ds(k0, tq), :]
        v = v_ref[h, pl.ds(k0, tq), :]
        z = _dot_nt(q_ref[h], k)
        l1m = -(jnp.maximum(z, 0.0) + jnp.log(1.0 + jnp.exp(-jnp.abs(z))))
        if diag:
            row = lax.broadcasted_iota(jnp.int32, (tq, tq), 0)
            col = lax.broadcasted_iota(jnp.int32, (tq, tq), 1)
            strict = col < row
            l1m_m = jnp.where(strict, l1m, 0.0)
        else:
            l1m_m = l1m
        hi = l1m_m.astype(BF16)
        lo = (l1m_m - hi.astype(F32)).astype(BF16)
        between = _dot(hi, u) + _dot(lo, u)
        a = jnp.exp(z + l1m + between + run)
        if diag:
            a = jnp.where(strict, a, 0.0)
        acc = acc + _dot(a.astype(BF16), v)
        run = run + between[:, 0:1] + l1m_m[:, 0:1]
        return run, acc

    def step(kt, carries, diag):
        return tuple(head_step(h, kt, carries[h], diag) for h in range(heads))

    init = (jnp.zeros((tq, 1), F32), jnp.zeros((tq, 128), F32))
    carries = step(qi, (init,) * heads, True)

    def more(c):
        j, carries = c
        top = carries[0][0]
        for run, _ in carries[1:]:
            top = jnp.maximum(top, run)
        return (j < qi) & (jnp.max(top) > SB_UNDERFLOW_LOG)

    def body(c):
        j, carries = c
        return j + 1, step(qi - 1 - j, carries, False)

    _, carries = lax.while_loop(more, body, (jnp.int32(0), carries))
    for h, (_, acc) in enumerate(carries):
        o_ref[:, h * HEAD_DIM:(h + 1) * HEAD_DIM] = acc.astype(o_ref.dtype)


def sb_attn(hm, u, *, head0, batch, seq, tq, heads):
    nq = seq // tq
    assert head0 % heads == 0 and SB_HEADS % heads == 0
    hm4 = hm.reshape(hm.shape[0], batch, seq, HEAD_DIM)
    blk0 = head0 // heads
    per_part = SB_HEADS // heads
    return pl.pallas_call(
        functools.partial(_sb_attn_kernel, tq=tq, heads=heads),
        out_shape=jax.ShapeDtypeStruct((batch * seq, SB_HEADS * HEAD_DIM), BF16),
        grid=(per_part, batch, nq),
        in_specs=[
            pl.BlockSpec((heads, None, tq, HEAD_DIM), lambda h, b, i: (blk0 + h, b, i, 0)),
            pl.BlockSpec((heads, None, seq, HEAD_DIM), lambda h, b, i: (blk0 + per_part + h, b, 0, 0)),
            pl.BlockSpec((heads, None, seq, HEAD_DIM), lambda h, b, i: (blk0 + 2 * per_part + h, b, 0, 0)),
            pl.BlockSpec((tq, tq), lambda h, b, i: (0, 0)),
        ],
        out_specs=pl.BlockSpec((tq, heads * HEAD_DIM), lambda h, b, i: (b * nq + i, h)),
        compiler_params=_cp(("arbitrary", "arbitrary", "arbitrary")),
    )(hm4, hm4, hm4, u)


def _nsa_cmp_kernel(c_ref, w1_ref, pe_ref, w2_ref, o_ref, *, nc):
    c = c_ref[...]
    half = NSA_CMP_STRIDE * HEAD_DIM
    a1 = _dot(c, w1_ref[0:half, :])
    a2 = _dot(c, w1_ref[half:2 * half, :])
    pc = _dot(pe_ref[...], w1_ref[...])[0:1, :]
    pre = a1 + pltpu.roll(a2, nc - 1, 0) + pc
    act = 0.5 * pre * (1.0 + jnp.tanh(0.7978845608028654 * (pre + 0.044715 * (pre * pre * pre))))
    o_ref[...] = _dot(act.astype(BF16), w2_ref[...]).astype(BF16)


def nsa_compress(cmp_heads, w1, pe, w2, *, batch, seq):
    nc = seq // NSA_CMP_STRIDE
    head0 = 0
    hm4 = cmp_heads.reshape(4, batch, nc, NSA_CMP_STRIDE * HEAD_DIM)
    return pl.pallas_call(
        functools.partial(_nsa_cmp_kernel, nc=nc),
        out_shape=jax.ShapeDtypeStruct((4, batch, nc, HEAD_DIM), BF16),
        grid=(4, batch),
        in_specs=[
            pl.BlockSpec((None, None, nc, NSA_CMP_STRIDE * HEAD_DIM), lambda c, b: (head0 + c, b, 0, 0)),
            pl.BlockSpec((None, NSA_CMP_LEN * HEAD_DIM, HEAD_DIM), lambda c, b: (c // 2, 0, 0)),
            pl.BlockSpec((None, 8, NSA_CMP_LEN * HEAD_DIM), lambda c, b: (c // 2, 0, 0)),
            pl.BlockSpec((None, HEAD_DIM, HEAD_DIM), lambda c, b: (c // 2, 0, 0)),
        ],
        out_specs=pl.BlockSpec((None, None, nc, HEAD_DIM), lambda c, b: (c, b, 0, 0)),
        compiler_params=_cp(("arbitrary", "arbitrary")),
    )(hm4, w1, pe, w2)


def _nsa_attn_kernel(q_ref, qa_ref, kc_ref, vc_ref, ks_ref, vs_ref, kw_ref, vw_ref, g_ref, ovt_ref, e_ref,
                     kpos_ref, cpos_ref, o_ref, *, tq, tk, seq, n_sel, n_top):
    qi = pl.program_id(2)
    t0 = qi * tq
    rows = NSA_HG * tq
    nc = seq // NSA_CMP_STRIDE
    q = jnp.concatenate([q_ref[...].reshape(rows, HEAD_DIM), qa_ref[...]], axis=1)

    rid = lax.broadcasted_iota(jnp.int32, (rows, 1), 0)
    trow = t0 + lax.bitwise_and(rid, tq - 1)

    def masked_softmax(s, valid):
        sm = jnp.where(valid, s, NEG)
        m = jnp.max(sm, axis=1, keepdims=True)
        e = jnp.where(valid, jnp.exp(sm - m), 0.0)
        d = jnp.sum(e, axis=1, keepdims=True)
        return e * (1.0 / jnp.where(d > 0.0, d, 1.0))

    n_i = lax.broadcasted_iota(jnp.int32, (1, nc), 1)
    end = n_i * NSA_CMP_STRIDE + (NSA_CMP_LEN - 1)
    s_c = _dot_nt(q, jnp.concatenate([kc_ref[...], cpos_ref[...]], axis=1))
    p_c = masked_softmax(s_c, end <= trow)
    o_c = _dot(p_c.astype(BF16), vc_ref[...])

    psum = p_c[0:tq] + p_c[tq:2 * tq] + p_c[2 * tq:3 * tq] + p_c[3 * tq:4 * tq]
    p_hi = psum.astype(BF16)
    p_lo = (psum - p_hi.astype(F32)).astype(BF16)
    ovt = ovt_ref[...]
    imp = _dot_nt(ovt, p_hi) + _dot_nt(ovt, p_lo)
    cur = lax.shift_right_logical(t0 + lax.broadcasted_iota(jnp.int32, (1, tq), 1),
                                  int(np.log2(NSA_SEL_LEN)))
    blk = lax.broadcasted_iota(jnp.int32, (n_sel, tq), 0)
    forced = (blk == 0) | (blk == cur) | (blk == cur - 1)
    key = jnp.where(blk > cur, -BIG, jnp.where(forced, BIG, imp))
    sub = lax.broadcasted_iota(jnp.int32, (8, tq), 0)
    chunks = [key[8 * r:8 * r + 8, :] for r in range(n_sel // 8)]
    ranks = [jnp.zeros((8, tq), F32) for _ in chunks]
    for i in range(n_sel):
        vi = key[i:i + 1, :]
        for r, kc in enumerate(chunks):
            gt = jnp.where(vi > kc, 1.0, 0.0)
            if r < i // 8:
                win = gt
            else:
                ge = jnp.where(vi >= kc, 1.0, 0.0)
                win = ge if r > i // 8 else jnp.where(sub > i % 8, ge, gt)
            ranks[r] = ranks[r] + win
    rank = jnp.concatenate(ranks, axis=0)
    selm_t = jnp.where((rank < float(n_top)) & (blk <= cur), 1.0, 0.0).astype(BF16)
    eye = (lax.broadcasted_iota(jnp.int32, (n_sel, n_sel), 0)
           == lax.broadcasted_iota(jnp.int32, (n_sel, n_sel), 1)).astype(F32).astype(BF16)
    selm = lax.dot_general(selm_t, eye, (((0,), (0,)), ((), ())), preferred_element_type=F32).astype(BF16)

    def sel_step(kt, carry, diag):
        m, l, acc = carry
        k0 = pl.multiple_of(kt * tk, tk)
        kk = jnp.concatenate([ks_ref[pl.ds(k0, tk), :], kpos_ref[pl.ds(k0, tk), :]], axis=1)
        vv = vs_ref[pl.ds(k0, tk), :]
        mex = _dot(selm, e_ref[kt])
        bias = (mex - 1.0) * BIG
        sm = _dot_nt(q, kk) + jnp.concatenate([bias] * NSA_HG, axis=0)
        if diag:
            spos = k0 + lax.broadcasted_iota(jnp.int32, (1, tk), 1)
            sm = jnp.where(spos <= trow, sm, NEG)
        m_new = jnp.maximum(m, jnp.max(sm, axis=1, keepdims=True))
        alpha = jnp.exp(m - m_new)
        p = jnp.exp((sm - m_new).astype(BF16))
        l = alpha * l + jnp.sum(p.astype(F32), axis=1, keepdims=True)
        acc = alpha * acc + _dot(p, vv)
        return m_new, l, acc

    kt_last = t0 // tk
    init = (jnp.full((rows, 1), NEG, F32), jnp.zeros((rows, 1), F32), jnp.zeros((rows, HEAD_DIM), F32))
    carry = lax.fori_loop(0, kt_last, lambda kt, c: sel_step(kt, c, False), init)
    _, l_s, acc_s = sel_step(kt_last, carry, True)
    o_s = acc_s * (1.0 / l_s)

    wk = NSA_WINDOW + tq
    ks0 = pl.multiple_of(jnp.maximum(t0 - NSA_WINDOW, 0), tq)
    kw = jnp.concatenate([kw_ref[pl.ds(ks0, wk), :], kpos_ref[pl.ds(ks0, wk), :]], axis=1)
    vw = vw_ref[pl.ds(ks0, wk), :]
    wpos = ks0 + lax.broadcasted_iota(jnp.int32, (1, wk), 1)
    dw = trow - wpos
    in_window = pltpu.bitcast(dw, jnp.uint32) < jnp.uint32(NSA_WINDOW)
    sm_w = jnp.where(in_window, _dot_nt(q, kw), NEG)
    p_w = jnp.exp((sm_w - jnp.max(sm_w, axis=1, keepdims=True)).astype(BF16))
    o_w = _dot(p_w, vw) * (1.0 / jnp.sum(p_w.astype(F32), axis=1, keepdims=True))

    gt = g_ref[...]
    for hg in range(NSA_HG):
        sl = slice(hg * tq, (hg + 1) * tq)
        o = (gt[:, 3 * hg:3 * hg + 1] * o_c[sl] + gt[:, 3 * hg + 1:3 * hg + 2] * o_s[sl]
             + gt[:, 3 * hg + 2:3 * hg + 3] * o_w[sl])
        o_ref[:, hg * HEAD_DIM:(hg + 1) * HEAD_DIM] = o.astype(o_ref.dtype)


def nsa_attn(hm, cmp, gates, consts, *, q_head0, kv_head0, batch, seq, tq, tk):
    ov, e, qa, kpos, cpos = consts
    nq = seq // tq
    nc = seq // NSA_CMP_STRIDE
    n_sel = seq // NSA_SEL_LEN
    n_top = min(NSA_TOPK, n_sel)
    assert tk % tq == 0 and seq % tk == 0 and seq >= NSA_WINDOW + tq and NSA_WINDOW % tq == 0
    hm4 = hm.reshape(hm.shape[0], batch, seq, HEAD_DIM)
    kv_spec = lambda off: pl.BlockSpec((None, None, seq, HEAD_DIM),
                                       lambda b, g, i: (kv_head0 + off + g, b, 0, 0))
    return pl.pallas_call(
        functools.partial(_nsa_attn_kernel, tq=tq, tk=tk, seq=seq, n_sel=n_sel, n_top=n_top),
        out_shape=jax.ShapeDtypeStruct((batch * seq, NSA_HEADS * HEAD_DIM), BF16),
        grid=(batch, NSA_GROUPS, nq),
        in_specs=[
            pl.BlockSpec((NSA_HG, None, tq, HEAD_DIM), lambda b, g, i: (q_head0 // NSA_HG + g, b, i, 0)),
            pl.BlockSpec((None, NSA_HG * tq, 128), lambda b, g, i: (g, 0, 0)),
            pl.BlockSpec((None, None, nc, HEAD_DIM), lambda b, g, i: (g, b, 0, 0)),
            pl.BlockSpec((None, None, nc, HEAD_DIM), lambda b, g, i: (2 + g, b, 0, 0)),
            kv_spec(4), kv_spec(6), kv_spec(8), kv_spec(10),
            pl.BlockSpec((tq, 128), lambda b, g, i: (b * nq + i, g)),
            pl.BlockSpec((n_sel, nc), lambda b, g, i: (0, 0)),
            pl.BlockSpec((seq // tk, n_sel, tk), lambda b, g, i: (0, 0, 0)),
            pl.BlockSpec((seq, 128), lambda b, g, i: (0, 0)),
            pl.BlockSpec((nc, 128), lambda b, g, i: (0, 0)),
        ],
        out_specs=pl.BlockSpec((tq, NSA_HG * HEAD_DIM), lambda b, g, i: (b * nq + i, g)),
        compiler_params=_cp(("arbitrary", "arbitrary", "arbitrary")),
    )(hm4, qa, cmp, cmp, hm4, hm4, hm4, hm4, gates, ov, e, kpos, cpos)


def _merge_kernel(x_ref, oa_ref, ob_ref, oc_ref, wm_ref, wb_ref, bm_ref, y_ref, xb_ref):
    @pl.when(pl.program_id(1) == 0)
    def _():
        xb_ref[...] = x_ref[...].astype(BF16)

    xb = xb_ref[...]
    acc = None
    for br, o_ref in enumerate((oa_ref, ob_ref, oc_ref)):
        gate = jax.nn.sigmoid(_dot(xb, wm_ref[br]) + bm_ref[br])
        term = gate * _dot(o_ref[...], wb_ref[br])
        acc = term if acc is None else acc + term
    y_ref[...] = acc.astype(y_ref.dtype)


def merge_branches(x, o_a, o_b, o_c, wm, wb, bm, *, tm, tn):
    t = x.shape[0]
    o_spec = pl.BlockSpec((tm, BRANCH_WIDTH), lambda i, j: (i, 0))
    return pl.pallas_call(
        _merge_kernel,
        out_shape=jax.ShapeDtypeStruct((t, D_MODEL), BF16),
        grid=(t // tm, D_MODEL // tn),
        in_specs=[
            pl.BlockSpec((tm, D_MODEL), lambda i, j: (i, 0)),
            o_spec, o_spec, o_spec,
            pl.BlockSpec((N_BRANCH, D_MODEL, tn), lambda i, j: (0, 0, j)),
            pl.BlockSpec((N_BRANCH, BRANCH_WIDTH, tn), lambda i, j: (0, 0, j)),
            pl.BlockSpec((N_BRANCH, 1, tn), lambda i, j: (0, 0, j)),
        ],
        out_specs=pl.BlockSpec((tm, tn), lambda i, j: (i, j)),
        scratch_shapes=[pltpu.VMEM((tm, D_MODEL), BF16)],
        compiler_params=_cp(("arbitrary", "arbitrary")),
    )(x, o_a, o_b, o_c, wm, wb, bm)


def _out_ln_kernel(y_ref, w_ref, x_ref, g_ref, b_ref, o_ref, *, alpha):
    h = _dot(y_ref[...], w_ref[...])
    o_ref[...] = _layer_norm(alpha * x_ref[...] + h, g_ref[...], b_ref[...])


def out_ln(y, w, x, g, b, *, alpha, tm):
    t = x.shape[0]
    return pl.pallas_call(
        functools.partial(_out_ln_kernel, alpha=alpha),
        out_shape=jax.ShapeDtypeStruct((t, D_MODEL), F32),
        grid=(t // tm,),
        in_specs=[
            pl.BlockSpec((tm, D_MODEL), lambda i: (i, 0)),
            pl.BlockSpec((D_MODEL, D_MODEL), lambda i: (0, 0)),
            pl.BlockSpec((tm, D_MODEL), lambda i: (i, 0)),
            pl.BlockSpec((1, D_MODEL), lambda i: (0, 0)),
            pl.BlockSpec((1, D_MODEL), lambda i: (0, 0)),
        ],
        out_specs=pl.BlockSpec((tm, D_MODEL), lambda i: (i, 0)),
        compiler_params=_cp(("arbitrary",)),
    )(y, w, x, g, b)


def _mem_attn_kernel(x_ref, wq_ref, k_ref, v_ref, wo_ref, g_ref, b_ref, o_ref, *, alpha):
    x = x_ref[...]
    q = _dot(x.astype(BF16), wq_ref[...]) * (HEAD_DIM ** -0.5)
    outs = []
    for h in range(MEM_HEADS):
        qh = q[:, h * HEAD_DIM:(h + 1) * HEAD_DIM].astype(BF16)
        s = _dot_nt(qh, k_ref[h])
        m = jnp.max(s, axis=1, keepdims=True)
        e = jnp.exp(s - m)
        p = e * (1.0 / jnp.sum(e, axis=1, keepdims=True))
        outs.append(_dot(p.astype(BF16), v_ref[h]).astype(BF16))
    o = jnp.concatenate(outs, axis=1)
    h_out = _dot(o, wo_ref[...])
    o_ref[...] = _layer_norm(alpha * x + h_out, g_ref[...], b_ref[...])


def mem_attn_ln(x, wq, kv, wo, g, b, *, alpha, seq, mem_len, tm):
    t = x.shape[0]
    per_b = seq // tm
    kv4 = kv.reshape(2 * MEM_HEADS, t // seq, mem_len, HEAD_DIM)
    width = MEM_HEADS * HEAD_DIM
    return pl.pallas_call(
        functools.partial(_mem_attn_kernel, alpha=alpha),
        out_shape=jax.ShapeDtypeStruct((t, D_MODEL), F32),
        grid=(t // tm,),
        in_specs=[
            pl.BlockSpec((tm, D_MODEL), lambda i: (i, 0)),
            pl.BlockSpec((D_MODEL, width), lambda i: (0, 0)),
            pl.BlockSpec((MEM_HEADS, None, mem_len, HEAD_DIM), lambda i: (0, i // per_b, 0, 0)),
            pl.BlockSpec((MEM_HEADS, None, mem_len, HEAD_DIM), lambda i: (1, i // per_b, 0, 0)),
            pl.BlockSpec((width, D_MODEL), lambda i: (0, 0)),
            pl.BlockSpec((1, D_MODEL), lambda i: (0, 0)),
            pl.BlockSpec((1, D_MODEL), lambda i: (0, 0)),
        ],
        out_specs=pl.BlockSpec((tm, D_MODEL), lambda i: (i, 0)),
        compiler_params=_cp(("arbitrary",)),
    )(x, wq, kv4, kv4, wo, g, b)


def _router_kernel(x_ref, w_ref, b_ref, tri_ref, mask_ref, wsel_ref, xp_ref, before_ref, count_ref):
    @pl.when(pl.program_id(0) == 0)
    def _():
        count_ref[...] = jnp.zeros_like(count_ref)

    x = x_ref[...]
    xp_ref[...] = _pack_bf16_pairs(x)
    xh = x.astype(BF16)
    xl = (x - xh.astype(F32)).astype(BF16)
    logits = _dot(xh, w_ref[0]) + _dot(xh, w_ref[1]) + _dot(xl, w_ref[0]) + b_ref[...]
    tm = logits.shape[0]
    lane = lax.broadcasted_iota(jnp.int32, (tm, N_EXPERTS), 1)
    work = logits
    hots, vals = [], []
    for _ in range(TOP_K):
        m = jnp.max(work, axis=1, keepdims=True)
        idx = jnp.min(jnp.where(work == m, lane, N_EXPERTS), axis=1, keepdims=True)
        hot = lane == idx
        hots.append(hot)
        vals.append(m)
        work = jnp.where(hot, -jnp.inf, work)
    es = [jnp.exp(v - vals[0]) for v in vals]
    inv = 1.0 / (es[0] + es[1] + es[2] + es[3])
    mask = jnp.zeros((tm, N_EXPERTS), F32)
    wsel = jnp.zeros((tm, N_EXPERTS), F32)
    for hot, e in zip(hots, es):
        mask = mask + jnp.where(hot, 1.0, 0.0)
        wsel = wsel + jnp.where(hot, e * inv, 0.0)
    mask_ref[...] = mask
    wsel_ref[...] = wsel
    before_ref[...] = count_ref[...] + _dot(tri_ref[...], mask.astype(BF16))
    count_ref[...] = count_ref[...] + jnp.sum(mask, axis=0, keepdims=True)


def router(x, w_hl, b, tri, *, tm):
    t = x.shape[0]
    tok32 = lambda: pl.BlockSpec((tm, N_EXPERTS), lambda i: (i, 0))
    return pl.pallas_call(
        _router_kernel,
        out_shape=(jax.ShapeDtypeStruct((t, N_EXPERTS), F32), jax.ShapeDtypeStruct((t, N_EXPERTS), F32),
                   jax.ShapeDtypeStruct((t, D_MODEL // 2), jnp.uint32),
                   jax.ShapeDtypeStruct((t, N_EXPERTS), F32)),
        grid=(t // tm,),
        in_specs=[
            pl.BlockSpec((tm, D_MODEL), lambda i: (i, 0)),
            pl.BlockSpec((2, D_MODEL, N_EXPERTS), lambda i: (0, 0, 0)),
            pl.BlockSpec((1, N_EXPERTS), lambda i: (0, 0)),
            pl.BlockSpec((tm, tm), lambda i: (0, 0)),
        ],
        out_specs=(tok32(), tok32(), pl.BlockSpec((tm, D_MODEL // 2), lambda i: (i, 0)), tok32()),
        scratch_shapes=[pltpu.VMEM((1, N_EXPERTS), F32)],
        compiler_params=_cp(("arbitrary",)),
    )(x, w_hl, b, tri)


def sc_gather_rows(table, idx, *, chunk):
    n = idx.shape[0]
    d = table.shape[1]
    workers = SC_CORES_V7X * SC_SUBCORES_V7X
    per_w = n // workers
    assert n % (workers * chunk) == 0 and chunk % 8 == 0 and chunk <= 128
    mesh = plsc.VectorSubcoreMesh(core_axis_name="c", subcore_axis_name="s")

    @functools.partial(
        pl.kernel, mesh=mesh,
        out_type=jax.ShapeDtypeStruct((n, d), table.dtype),
        scratch_types=[pltpu.VMEM((chunk,), jnp.int32), pltpu.VMEM((chunk, d), table.dtype),
                       pltpu.SemaphoreType.DMA],
    )
    def gather(table_hbm, idx_hbm, out_hbm, idx_v, rows_v, sem):
        wid = lax.axis_index("s") * SC_CORES_V7X + lax.axis_index("c")
        base = wid * per_w

        @pl.loop(0, per_w // chunk)
        def _(j):
            off = pl.multiple_of(base + j * chunk, 8)
            pltpu.sync_copy(idx_hbm.at[pl.ds(off, chunk)], idx_v)
            pltpu.async_copy(table_hbm.at[idx_v], rows_v, sem).wait()
            pltpu.sync_copy(rows_v, out_hbm.at[pl.ds(off, chunk)])

    return gather(table, idx)


def sc_scatter_rows(rows, idx, n_out, *, copies, chunk):
    t, d = rows.shape
    workers = SC_CORES_V7X * SC_SUBCORES_V7X
    per_w = t // workers
    assert idx.shape == (copies * t,) and t % (workers * chunk) == 0 and chunk % 8 == 0 and chunk <= 128
    mesh = plsc.VectorSubcoreMesh(core_axis_name="c", subcore_axis_name="s")

    @functools.partial(
        pl.kernel, mesh=mesh,
        out_type=jax.ShapeDtypeStruct((n_out, d), rows.dtype),
        scratch_types=[pltpu.VMEM((chunk,), jnp.int32), pltpu.VMEM((chunk, d), rows.dtype)],
    )
    def scatter(rows_hbm, idx_hbm, out_hbm, idx_v, rows_v):
        wid = lax.axis_index("s") * SC_CORES_V7X + lax.axis_index("c")
        base = wid * per_w

        @pl.loop(0, per_w // chunk)
        def _(j):
            off = pl.multiple_of(base + j * chunk, 8)
            pltpu.sync_copy(rows_hbm.at[pl.ds(off, chunk)], rows_v)
            for k in range(copies):
                pltpu.sync_copy(idx_hbm.at[pl.ds(pl.multiple_of(k * t + off, 8), chunk)], idx_v)
                pltpu.sync_copy(rows_v, out_hbm.at[idx_v])

    return scatter(rows, idx)


def _experts_kernel(be_ref, x_ref, wg_ref, bg_ref, wu_ref, bu_ref, wd_ref, bd_ref, y_ref,
                    wgb_ref, wub_ref, wdb_ref):
    i = pl.program_id(0)
    prev = be_ref[jnp.maximum(i - 1, 0)]

    @pl.when((i == 0) | (be_ref[i] != prev))
    def _():
        wgb_ref[...] = wg_ref[...].astype(BF16)
        wub_ref[...] = wu_ref[...].astype(BF16)
        wdb_ref[...] = wd_ref[...].astype(BF16)

    x_lo, x_hi = _unpack_bf16_pairs(x_ref[...])
    xb = jnp.concatenate([x_lo.astype(BF16), x_hi.astype(BF16)], axis=1)
    g = jnp.minimum(_dot(xb, wgb_ref[...]) + bg_ref[...], SWIGLU_LIMIT)
    u = jnp.clip(_dot(xb, wub_ref[...]) + bu_ref[...], -SWIGLU_LIMIT, SWIGLU_LIMIT)
    hdn = (u + 1.0) * (g * jax.nn.sigmoid(SWIGLU_ALPHA * g))
    y_ref[...] = _pack_bf16_pairs(_dot(hdn.astype(BF16), wdb_ref[...]) + bd_ref[...])


def experts(x_rows, blk_e, wg, bg, wu, bu, wd, bd, *, layer, bm):
    n_rows, dp = x_rows.shape
    d, f = wg.shape[2], wg.shape[3]
    w_spec = lambda shape: pl.BlockSpec((None, None) + shape, lambda i, be: (layer, be[i], 0, 0))
    grid_spec = pltpu.PrefetchScalarGridSpec(
        num_scalar_prefetch=1,
        grid=(n_rows // bm,),
        in_specs=[
            pl.BlockSpec((bm, dp), lambda i, be: (i, 0)),
            w_spec((d, f)), w_spec((1, f)), w_spec((d, f)), w_spec((1, f)), w_spec((f, d)), w_spec((1, d)),
        ],
        out_specs=pl.BlockSpec((bm, dp), lambda i, be: (i, 0)),
        scratch_shapes=[pltpu.VMEM((d, f), BF16), pltpu.VMEM((d, f), BF16), pltpu.VMEM((f, d), BF16)],
    )
    return pl.pallas_call(
        _experts_kernel,
        out_shape=jax.ShapeDtypeStruct((n_rows, dp), jnp.uint32),
        grid_spec=grid_spec,
        compiler_params=_cp(("arbitrary",)),
    )(blk_e, x_rows, wg, bg, wu, bu, wd, bd)


def _moe_ln_kernel(x_ref, y_ref, w_ref, g_ref, b_ref, o_ref, *, alpha):
    w = w_ref[...]
    y_lo = y_hi = None
    for k in range(TOP_K):
        lo, hi = _unpack_bf16_pairs(y_ref[k])
        wk = w[:, k:k + 1]
        y_lo = wk * lo if y_lo is None else y_lo + wk * lo
        y_hi = wk * hi if y_hi is None else y_hi + wk * hi
    y = jnp.concatenate([y_lo, y_hi], axis=1)
    o_ref[...] = _layer_norm(alpha * x_ref[...] + y, g_ref[...], b_ref[...])


def moe_ln(x, y4, w4p, g, b, *, alpha, tm):
    t = x.shape[0]
    return pl.pallas_call(
        functools.partial(_moe_ln_kernel, alpha=alpha),
        out_shape=jax.ShapeDtypeStruct((t, D_MODEL), F32),
        grid=(t // tm,),
        in_specs=[
            pl.BlockSpec((tm, D_MODEL), lambda i: (i, 0)),
            pl.BlockSpec((TOP_K, tm, D_MODEL // 2), lambda i: (0, i, 0)),
            pl.BlockSpec((tm, 128), lambda i: (i, 0)),
            pl.BlockSpec((1, D_MODEL), lambda i: (0, 0)),
            pl.BlockSpec((1, D_MODEL), lambda i: (0, 0)),
        ],
        out_specs=pl.BlockSpec((tm, D_MODEL), lambda i: (i, 0)),
        compiler_params=_cp(("arbitrary",)),
    )(x, y4, w4p, g, b)


def _rope_tables(seq):
    inv = np.asarray(ROPE_THETA ** (-np.arange(0, MLA_ROPE, 2) / MLA_ROPE), np.float32)
    ang = jnp.arange(seq, dtype=F32)[:, None] * jnp.asarray(inv)[None, :]
    cos, sin = jnp.cos(ang), jnp.sin(ang)
    zeros = jnp.zeros((seq, 128 - MLA_ROPE), F32)
    return (jnp.concatenate([cos, cos, zeros], axis=1), jnp.concatenate([-sin, sin, zeros], axis=1))


def _nsa_constants(seq, tq, tk):
    nc = seq // NSA_CMP_STRIDE
    qa = np.zeros((NSA_GROUPS, NSA_HG * tq, 128), np.float32)
    for g in range(NSA_GROUPS):
        for hg in range(NSA_HG):
            slope = 2.0 ** (-8.0 * (g * NSA_HG + hg + 1) / NSA_HEADS)
            qa[g, hg * tq:(hg + 1) * tq, 0] = slope * NSA_SEL_LEN
            qa[g, hg * tq:(hg + 1) * tq, 1] = slope
            qa[g, hg * tq:(hg + 1) * tq, 2] = slope * NSA_CMP_STRIDE
            qa[g, hg * tq:(hg + 1) * tq, 3] = slope * (NSA_CMP_LEN - 1) / 2.0
    kpos = np.zeros((seq, 128), np.float32)
    kpos[:, 0] = np.arange(seq) // NSA_SEL_LEN
    kpos[:, 1] = np.arange(seq) % NSA_SEL_LEN
    cpos = np.zeros((nc, 128), np.float32)
    cpos[:, 2] = np.arange(nc)
    cpos[:, 3] = 1.0
    for arr in (qa, kpos, cpos):
        assert np.array_equal(arr.astype(BF16).astype(np.float32), arr)
    n_cmp = (seq - NSA_CMP_LEN) // NSA_CMP_STRIDE + 1
    n_sel = seq // NSA_SEL_LEN
    cs = np.arange(nc) * NSA_CMP_STRIDE
    ss = np.arange(n_sel) * NSA_SEL_LEN
    ov = np.clip(np.minimum(cs[:, None] + NSA_CMP_LEN, ss[None, :] + NSA_SEL_LEN)
                 - np.maximum(cs[:, None], ss[None, :]), 0, None) / NSA_CMP_LEN
    ov[n_cmp:] = 0.0
    e = (np.arange(seq)[None, :] // NSA_SEL_LEN == np.arange(n_sel)[:, None]).astype(np.float32)
    e = e.reshape(n_sel, seq // tk, tk).transpose(1, 0, 2)
    return tuple(jnp.asarray(a, BF16) for a in (ov.T, e, qa, kpos, cpos))


def _pad_cols(w, width):
    return jnp.pad(w, ((0, 0), (0, width - w.shape[1])))


def _swap_halves(w):
    half = w.shape[1] // 2
    return jnp.concatenate([w[:, half:], w[:, :half]], axis=1)


def _layer_weights(w_in, w_q_up, w_kv_up):
    kr = w_in[:, OFF_KR:OFF_NSA_Q]
    gate = w_in[:, OFF_NSA_GATE:OFF_SB]
    per_g = NSA_HG * 3
    w_mla = jnp.concatenate([
        w_in[:, OFF_CQ:OFF_KR],
        _pad_cols(kr, 128), _pad_cols(_swap_halves(kr), 128),
        _pad_cols(gate[:, :per_g], 128), _pad_cols(gate[:, per_g:], 128)], axis=1).astype(BF16)
    wq = w_q_up.reshape(MLA_Q_RANK, MLA_HEADS, MLA_NOPE + MLA_ROPE)
    rope = wq[:, :, MLA_NOPE:]
    rope_sw = jnp.concatenate([rope[:, :, MLA_ROPE // 2:], rope[:, :, :MLA_ROPE // 2]], axis=2)
    pad = ((0, 0), (0, 0), (0, 128 - MLA_ROPE))
    wq3 = jnp.concatenate([
        wq[:, :, :MLA_NOPE].reshape(MLA_Q_RANK, -1),
        jnp.pad(rope, pad).reshape(MLA_Q_RANK, -1),
        jnp.pad(rope_sw, pad).reshape(MLA_Q_RANK, -1)], axis=1).astype(BF16)
    wkv = w_kv_up.reshape(MLA_KV_RANK, MLA_HEADS, 2, 128)
    wkv = jnp.concatenate([wkv[:, :, 0].reshape(MLA_KV_RANK, -1),
                           wkv[:, :, 1].reshape(MLA_KV_RANK, -1)], axis=1).astype(BF16)
    w_heads = w_in[:, OFF_NSA_Q:OFF_NSA_GATE]
    w_heads = jnp.concatenate([w_heads, w_in[:, OFF_SB:OFF_MERGE]], axis=1).astype(BF16)
    wm = w_in[:, OFF_MERGE:].reshape(D_MODEL, N_BRANCH, D_MODEL).transpose(1, 0, 2).astype(BF16)
    return w_mla, wq3, wkv, w_heads, wm


def _forward(x, mem, w_in, mla_q_norm, mla_w_q_up, mla_kv_norm, mla_w_kv_up,
             nsa_pe_k, nsa_pe_v, nsa_w1_k, nsa_w1_v, nsa_w2_k, nsa_w2_v,
             w_branch, b_merge, w_out, ln_mix_g, ln_mix_b,
             mem_w_q, mem_w_k, mem_w_v, mem_w_o, ln_mem_g, ln_mem_b,
             moe_w_router, moe_b_router, moe_w_gate, moe_b_gate, moe_w_up, moe_b_up,
             moe_w_down, moe_b_down, ln_moe_g, ln_moe_b):
    batch, seq, _ = x.shape
    mem_len = mem.shape[1]
    depth = w_in.shape[0]
    t = batch * seq
    alpha = float((2 * depth) ** 0.25)
    bm = MOE_BLOCK_ROWS
    n_rows = t * TOP_K + N_EXPERTS * bm
    n_blocks = n_rows // bm

    tm_in = min(512, seq)
    tq_mla = min(512, seq)
    tq_sb = 256
    tq_nsa, tk_nsa = 128, 512
    tm_ln = 256
    sc_chunk = 64

    cos128, sin128 = _rope_tables(seq)
    nsa_consts = _nsa_constants(seq, tq_nsa, tk_nsa)
    u_sb = jnp.asarray(np.arange(tq_sb)[:, None] > np.arange(tq_sb)[None, :], BF16)
    tri_router = jnp.asarray(np.arange(tm_in)[:, None] > np.arange(tm_in)[None, :], BF16)
    n_qheads = NSA_HEADS
    n_kvheads = 3 * 2 * NSA_GROUPS
    head_scale = np.ones((1, (n_qheads + n_kvheads + 3 * SB_HEADS) * HEAD_DIM), np.float32)
    head_scale[:, :n_qheads * HEAD_DIM] = HEAD_DIM ** -0.5
    sb0 = n_qheads + n_kvheads
    head_scale[:, sb0 * HEAD_DIM:(sb0 + SB_HEADS) * HEAD_DIM] = HEAD_DIM ** -0.5
    head_scale = jnp.asarray(head_scale)
    ones_kv = jnp.ones((1, 2 * MEM_HEADS * HEAD_DIM), F32)

    b_gate4 = moe_b_gate.reshape(depth, N_EXPERTS, 1, D_EXPERT)
    b_up4 = moe_b_up.reshape(depth, N_EXPERTS, 1, D_EXPERT)
    b_down4 = moe_b_down.reshape(depth, N_EXPERTS, 1, D_MODEL)

    xf = x.reshape(t, D_MODEL)
    memf = mem.reshape(batch * mem_len, D_MODEL)
    row = lambda v: v.reshape(1, -1)

    for l in range(depth):
        w_mla, wq3, wkv, w_heads, wm = _layer_weights(w_in[l], mla_w_q_up[l], mla_w_kv_up[l])

        q_a, k_a, v_a, gates = mla_in(xf, w_mla, row(mla_q_norm[l]), row(mla_kv_norm[l]), wq3, wkv,
                                      cos128, sin128, seq=seq, tm=tm_ln)
        hm = proj_heads(xf, w_heads, head_scale, tm=tm_in, tn=11 * HEAD_DIM)
        o_a = mla_attn(q_a, k_a, v_a, batch=batch, seq=seq, tq=tq_mla, heads=2)
        w1 = jnp.stack([nsa_w1_k[l], nsa_w1_v[l]]).astype(BF16)
        pe = jnp.stack([nsa_pe_k[l], nsa_pe_v[l]]).reshape(2, 1, -1)
        pe = jnp.broadcast_to(pe, (2, 8, pe.shape[-1])).astype(BF16)
        w2 = jnp.stack([nsa_w2_k[l], nsa_w2_v[l]]).astype(BF16)
        cmp = nsa_compress(hm[n_qheads:n_qheads + 4], w1, pe, w2, batch=batch, seq=seq)
        o_b = nsa_attn(hm, cmp, gates, nsa_consts, q_head0=0, kv_head0=n_qheads,
                       batch=batch, seq=seq, tq=tq_nsa, tk=tk_nsa)
        o_c = sb_attn(hm, u_sb, head0=sb0, batch=batch, seq=seq, tq=tq_sb, heads=4)
        y = merge_branches(xf, o_a, o_b, o_c, wm, w_branch[l].astype(BF16),
                           b_merge[l].reshape(N_BRANCH, 1, D_MODEL), tm=tm_in, tn=512)
        xf = out_ln(y, w_out[l].astype(BF16), xf, row(ln_mix_g[l]), row(ln_mix_b[l]), alpha=alpha, tm=tm_ln)

        w_kv_mem = jnp.concatenate([mem_w_k[l], mem_w_v[l]], axis=1).astype(BF16)
        kv_mem = proj_heads(memf, w_kv_mem, ones_kv, tm=min(512, batch * mem_len), tn=512)
        xf = mem_attn_ln(xf, mem_w_q[l].astype(BF16), kv_mem, mem_w_o[l].astype(BF16),
                         row(ln_mem_g[l]), row(ln_mem_b[l]), alpha=alpha, seq=seq, mem_len=mem_len, tm=tm_ln)

        wr = moe_w_router[l]
        wr_hi = wr.astype(BF16)
        wr_lo = (wr - wr_hi.astype(F32)).astype(BF16)
        mask, wsel, x_packed, before = router(xf, jnp.stack([wr_hi, wr_lo]), row(moe_b_router[l]),
                                              tri_router, tm=tm_in)
        counts = (before[-1] + mask[-1]).astype(jnp.int32)
        padded = (counts + bm - 1) // bm * bm
        pad_end = jnp.cumsum(padded)
        pad_start = pad_end - padded
        slot = pad_start[None, :] + before.astype(jnp.int32)
        top_e = lax.top_k(mask, TOP_K)[1]
        pos4 = jnp.take_along_axis(slot, top_e, axis=1).astype(jnp.int32)
        w4 = jnp.take_along_axis(wsel, top_e, axis=1)
        blk_row0 = jnp.arange(n_blocks, dtype=jnp.int32)[:, None] * bm
        blk_e = jnp.minimum(jnp.sum((pad_end[None, :] <= blk_row0).astype(jnp.int32), axis=1), N_EXPERTS - 1)
        pos_kmajor = pos4.T.reshape(-1)
        x_rows = sc_scatter_rows(x_packed, pos_kmajor, n_rows, copies=TOP_K, chunk=sc_chunk)
        y_rows = experts(x_rows, blk_e, moe_w_gate, b_gate4, moe_w_up, b_up4, moe_w_down, b_down4,
                         layer=l, bm=bm)
        y4 = sc_gather_rows(y_rows, pos_kmajor, chunk=sc_chunk).reshape(TOP_K, t, D_MODEL // 2)
        xf = moe_ln(xf, y4, _pad_cols(w4, 128), row(ln_moe_g[l]), row(ln_moe_b[l]), alpha=alpha, tm=tm_ln)

    return xf.reshape(batch, seq, D_MODEL)


def kernel(x, mem, w_in, mla_q_norm, mla_w_q_up, mla_kv_norm, mla_w_kv_up, nsa_pe_k, nsa_pe_v, nsa_w1_k, nsa_w1_v, nsa_w2_k, nsa_w2_v, w_branch, b_merge, w_out, ln_mix_g, ln_mix_b, mem_w_q, mem_w_k, mem_w_v, mem_w_o, ln_mem_g, ln_mem_b, moe_w_router, moe_b_router, moe_w_gate, moe_b_gate, moe_w_up, moe_b_up, moe_w_down, moe_b_down, ln_moe_g, ln_moe_b):
    return _forward(x, mem, w_in, mla_q_norm, mla_w_q_up, mla_kv_norm, mla_w_kv_up,
                    nsa_pe_k, nsa_pe_v, nsa_w1_k, nsa_w1_v, nsa_w2_k, nsa_w2_v,
                    w_branch, b_merge, w_out, ln_mix_g, ln_mix_b,
                    mem_w_q, mem_w_k, mem_w_v, mem_w_o, ln_mem_g, ln_mem_b,
                    moe_w_router, moe_b_router, moe_w_gate, moe_b_gate, moe_w_up, moe_b_up,
                    moe_w_down, moe_b_down, ln_moe_g, ln_moe_b)
```

```python
import functools

import numpy as np
import jax
import jax.numpy as jnp
from jax import lax
from jax.experimental import pallas as pl
from jax.experimental.pallas import tpu as pltpu
from jax.experimental.pallas import tpu_sc as plsc

F32 = jnp.float32
BF16 = jnp.bfloat16

D_MODEL = 2048
HEAD_DIM = 128
MLA_HEADS = 8
MLA_Q_RANK = 512
MLA_KV_RANK = 256
MLA_NOPE = 128
MLA_ROPE = 64
ROPE_THETA = 10000.0
NSA_HEADS = 8
NSA_GROUPS = 2
NSA_HG = NSA_HEADS // NSA_GROUPS
NSA_CMP_LEN = 32
NSA_CMP_STRIDE = 16
NSA_SEL_LEN = 64
NSA_TOPK = 16
NSA_WINDOW = 512
SB_HEADS = 8
MEM_HEADS = 4
N_EXPERTS = 32
TOP_K = 4
D_EXPERT = 512
SWIGLU_LIMIT = 7.0
SWIGLU_ALPHA = 1.702
N_BRANCH = 3
BRANCH_WIDTH = 1024
LN_EPS = 1e-5
RMS_EPS = 1e-6
NEG = -1e30
BIG = 1e30
SB_UNDERFLOW_LOG = -100.0

OFF_CQ = 0
OFF_CKV = 512
OFF_KR = 768
OFF_NSA_Q = 832
OFF_NSA_KV = 1856
OFF_NSA_GATE = 3392
OFF_SB = 3416
OFF_MERGE = 6488

VMEM_LIMIT_V7X = 56 * 1024 * 1024
MOE_BLOCK_ROWS = 512
SC_CORES_V7X = 2
SC_SUBCORES_V7X = 16


def _cp(sem, vmem=VMEM_LIMIT_V7X):
    return pltpu.CompilerParams(dimension_semantics=sem, vmem_limit_bytes=vmem)


def _dot(a, b):
    return jnp.dot(a, b, preferred_element_type=F32)


def _dot_nt(a, b):
    return lax.dot_general(a, b, (((1,), (1,)), ((), ())), preferred_element_type=F32)


def _layer_norm(z, g, b):
    mu = jnp.mean(z, axis=-1, keepdims=True)
    zc = z - mu
    var = jnp.mean(zc * zc, axis=-1, keepdims=True)
    return zc * lax.rsqrt(var + LN_EPS) * g + b


def _rms_norm(z, g):
    return z * lax.rsqrt(jnp.mean(z * z, axis=-1, keepdims=True) + RMS_EPS) * g


def _pack_bf16_pairs(z):
    n = z.shape[1] // 2
    bits = pltpu.bitcast(z.astype(BF16).astype(F32), jnp.uint32)
    return lax.shift_right_logical(bits[:, :n], jnp.uint32(16)) | (bits[:, n:] & jnp.uint32(0xFFFF0000))


def _unpack_bf16_pairs(w):
    lo = pltpu.bitcast(lax.shift_left(w, jnp.uint32(16)), F32)
    hi = pltpu.bitcast(w & jnp.uint32(0xFFFF0000), F32)
    return lo, hi


def _proj_heads_kernel(a_ref, w_ref, s_ref, o_ref, abf_ref, *, n_heads_per_tile):
    @pl.when(pl.program_id(1) == 0)
    def _():
        abf_ref[...] = a_ref[...].astype(BF16)

    acc = _dot(abf_ref[...], w_ref[...]) * s_ref[...]
    for c in range(n_heads_per_tile):
        o_ref[c] = acc[:, c * HEAD_DIM:(c + 1) * HEAD_DIM].astype(o_ref.dtype)


def proj_heads(a, w, scale, *, tm, tn):
    m, k = a.shape
    n = w.shape[1]
    hpt = tn // HEAD_DIM
    return pl.pallas_call(
        functools.partial(_proj_heads_kernel, n_heads_per_tile=hpt),
        out_shape=jax.ShapeDtypeStruct((n // HEAD_DIM, m, HEAD_DIM), BF16),
        grid=(m // tm, n // tn),
        in_specs=[
            pl.BlockSpec((tm, k), lambda i, j: (i, 0)),
            pl.BlockSpec((k, tn), lambda i, j: (0, j)),
            pl.BlockSpec((1, tn), lambda i, j: (0, j)),
        ],
        out_specs=pl.BlockSpec((hpt, tm, HEAD_DIM), lambda i, j: (j, i, 0)),
        scratch_shapes=[pltpu.VMEM((tm, k), BF16)],
        compiler_params=_cp(("arbitrary", "arbitrary")),
    )(a, w, scale)


def _mla_in_kernel(x_ref, w_ref, qg_ref, kg_ref, wq_ref, wkv_ref, cos_ref, sin_ref,
                   q_ref, k_ref, v_ref, g_ref):
    xb = x_ref[...].astype(BF16)
    h = _dot(xb, w_ref[...])
    cq = h[:, 0:512]
    ckv = h[:, 512:768]
    kr1 = h[:, 768:896]
    kr2 = h[:, 896:1024]
    g_ref[...] = jax.nn.sigmoid(h[:, 1024:1280])
    cos = cos_ref[...]
    sin = sin_ref[...]
    scale = (MLA_NOPE + MLA_ROPE) ** -0.5
    nq = _rms_norm(cq, qg_ref[...]).astype(BF16)
    q3 = _dot(nq, wq_ref[...])
    for hh in range(MLA_HEADS):
        lo, hi = hh * 128, (hh + 1) * 128
        q_ref[hh, :, 0:128] = (q3[:, lo:hi] * scale).astype(BF16)
        rot = q3[:, 1024 + lo:1024 + hi] * cos + q3[:, 2048 + lo:2048 + hi] * sin
        q_ref[hh, :, 128:256] = (rot * scale).astype(BF16)
    nkv = _rms_norm(ckv, kg_ref[...]).astype(BF16)
    kv = _dot(nkv, wkv_ref[...])
    krot = (kr1 * cos + kr2 * sin).astype(BF16)
    for hh in range(MLA_HEADS):
        lo, hi = hh * 128, (hh + 1) * 128
        k_ref[hh, :, 0:128] = kv[:, lo:hi].astype(BF16)
        k_ref[hh, :, 128:256] = krot
        v_ref[hh] = kv[:, 1024 + lo:1024 + hi].astype(BF16)


def mla_in(x, w_mla, qg, kg, wq3, wkv, cos128, sin128, *, seq, tm):
    t = x.shape[0]
    npos = seq // tm
    full = lambda shape: pl.BlockSpec(shape, lambda i: (0,) * len(shape))
    return pl.pallas_call(
        _mla_in_kernel,
        out_shape=(
            jax.ShapeDtypeStruct((MLA_HEADS, t, 256), BF16),
            jax.ShapeDtypeStruct((MLA_HEADS, t, 256), BF16),
            jax.ShapeDtypeStruct((MLA_HEADS, t, 128), BF16),
            jax.ShapeDtypeStruct((t, 256), F32),
        ),
        grid=(t // tm,),
        in_specs=[
            pl.BlockSpec((tm, D_MODEL), lambda i: (i, 0)),
            full((D_MODEL, 1280)),
            full((1, MLA_Q_RANK)),
            full((1, MLA_KV_RANK)),
            full((MLA_Q_RANK, 3072)),
            full((MLA_KV_RANK, 2048)),
            pl.BlockSpec((tm, 128), lambda i: (i % npos, 0)),
            pl.BlockSpec((tm, 128), lambda i: (i % npos, 0)),
        ],
        out_specs=(
            pl.BlockSpec((MLA_HEADS, tm, 256), lambda i: (0, i, 0)),
            pl.BlockSpec((MLA_HEADS, tm, 256), lambda i: (0, i, 0)),
            pl.BlockSpec((MLA_HEADS, tm, 128), lambda i: (0, i, 0)),
            pl.BlockSpec((tm, 256), lambda i: (i, 0)),
        ),
        compiler_params=_cp(("arbitrary",)),
    )(x, w_mla, qg, kg, wq3, wkv, cos128, sin128)


def _mla_attn_kernel(q_ref, k_ref, v_ref, o_ref, s_ref, *, tq, heads):
    qi = pl.program_id(2)

    def scores(h, kt):
        k0 = pl.multiple_of(kt * tq, tq)
        return _dot_nt(q_ref[h], k_ref[h, pl.ds(k0, tq), :])

    def consume(h, kt, s, carry, diag):
        m, l, acc = carry
        k0 = pl.multiple_of(kt * tq, tq)
        v = v_ref[h, pl.ds(k0, tq), :]
        if diag:
            row = lax.broadcasted_iota(jnp.int32, (tq, tq), 0)
            col = lax.broadcasted_iota(jnp.int32, (tq, tq), 1)
            s = jnp.where(col <= row, s, NEG)
        m_new = jnp.maximum(m, jnp.max(s, axis=1, keepdims=True))
        alpha = jnp.exp(m - m_new)
        p = jnp.exp((s - m_new).astype(BF16))
        l = alpha * l + jnp.sum(p.astype(F32), axis=1, keepdims=True)
        acc = alpha * acc + _dot(p, v)
        return m_new, l, acc

    def fill(slot, kt):
        for h in range(heads):
            s_ref[slot, h] = scores(h, kt)

    def drain(slot, kt, carries, diag):
        return tuple(consume(h, kt, s_ref[slot, h], carries[h], diag) for h in range(heads))

    def body(j, carries):
        kt = 2 * j
        fill(1, kt + 1)
        carries = drain(0, kt, carries, False)
        fill(0, kt + 2)
        return drain(1, kt + 1, carries, False)

    init = (jnp.full((tq, 1), NEG, F32), jnp.zeros((tq, 1), F32), jnp.zeros((tq, 128), F32))
    fill(0, 0)
    carries = lax.fori_loop(0, qi // 2, body, (init,) * heads)

    def even_tail(carries):
        return drain(0, qi, carries, True)

    def odd_tail(carries):
        fill(1, qi)
        return drain(1, qi, drain(0, qi - 1, carries, False), True)

    carries = lax.cond(lax.rem(qi, 2) == 1, odd_tail, even_tail, carries)
    for h, (_, l, acc) in enumerate(carries):
        o_ref[:, h * 128:(h + 1) * 128] = (acc / l).astype(o_ref.dtype)


def mla_attn(q, k, v, *, batch, seq, tq, heads):
    nq = seq // tq
    q4 = q.reshape(MLA_HEADS, batch, seq, 256)
    k4 = k.reshape(MLA_HEADS, batch, seq, 256)
    v4 = v.reshape(MLA_HEADS, batch, seq, 128)
    return pl.pallas_call(
        functools.partial(_mla_attn_kernel, tq=tq, heads=heads),
        out_shape=jax.ShapeDtypeStruct((batch * seq, MLA_HEADS * 128), BF16),
        grid=(MLA_HEADS // heads, batch, nq),
        in_specs=[
            pl.BlockSpec((heads, None, tq, 256), lambda h, b, i: (h, b, i, 0)),
            pl.BlockSpec((heads, None, seq, 256), lambda h, b, i: (h, b, 0, 0)),
            pl.BlockSpec((heads, None, seq, 128), lambda h, b, i: (h, b, 0, 0)),
        ],
        out_specs=pl.BlockSpec((tq, heads * 128), lambda h, b, i: (b * nq + i, h)),
        scratch_shapes=[pltpu.VMEM((2, heads, tq, tq), F32)],
        compiler_params=_cp(("arbitrary", "arbitrary", "arbitrary")),
    )(q4, k4, v4)


def _sb_attn_kernel(q_ref, k_ref, v_ref, u_ref, o_ref, *, tq, heads):
    qi = pl.program_id(2)
    u = u_ref[...]

    def head_step(h, kt, carry, diag):
        run, acc = carry
        k0 = pl.multiple_of(kt * tq, tq)
        k = k_ref[h, pl.ds(k0, tq), :]
        v = v_ref[h, pl.ds(k0, tq), :]
        z = _dot_nt(q_ref[h], k)
        l1m = -(jnp.maximum(z, 0.0) + jnp.log(1.0 + jnp.exp(-jnp.abs(z))))
        if diag:
            row = lax.broadcasted_iota(jnp.int32, (tq, tq), 0)
            col = lax.broadcasted_iota(jnp.int32, (tq, tq), 1)
            strict = col < row
            l1m_m = jnp.where(strict, l1m, 0.0)
        else:
            l1m_m = l1m
        hi = l1m_m.astype(BF16)
        lo = (l1m_m - hi.astype(F32)).astype(BF16)
        between = _dot(hi, u) + _dot(lo, u)
        a = jnp.exp(z + l1m + between + run)
        if diag:
            a = jnp.where(strict, a, 0.0)
        acc = acc + _dot(a.astype(BF16), v)
        run = run + between[:, 0:1] + l1m_m[:, 0:1]
        return run, acc

    def step(kt, carries, diag):
        return tuple(head_step(h, kt, carries[h], diag) for h in range(heads))

    init = (jnp.zeros((tq, 1), F32), jnp.zeros((tq, 128), F32))
    carries = step(qi, (init,) * heads, True)

    def more(c):
        j, carries = c
        top = carries[0][0]
        for run, _ in carries[1:]:
            top = jnp.maximum(top, run)
        return (j < qi) & (jnp.max(top) > SB_UNDERFLOW_LOG)

    def body(c):
        j, carries = c
        return j + 1, step(qi - 1 - j, carries, False)

    _, carries = lax.while_loop(more, body, (jnp.int32(0), carries))
    for h, (_, acc) in enumerate(carries):
        o_ref[:, h * HEAD_DIM:(h + 1) * HEAD_DIM] = acc.astype(o_ref.dtype)


def sb_attn(hm, u, *, head0, batch, seq, tq, heads):
    nq = seq // tq
    assert head0 % heads == 0 and SB_HEADS % heads == 0
    hm4 = hm.reshape(hm.shape[0], batch, seq, HEAD_DIM)
    blk0 = head0 // heads
    per_part = SB_HEADS // heads
    return pl.pallas_call(
        functools.partial(_sb_attn_kernel, tq=tq, heads=heads),
        out_shape=jax.ShapeDtypeStruct((batch * seq, SB_HEADS * HEAD_DIM), BF16),
        grid=(per_part, batch, nq),
        in_specs=[
            pl.BlockSpec((heads, None, tq, HEAD_DIM), lambda h, b, i: (blk0 + h, b, i, 0)),
            pl.BlockSpec((heads, None, seq, HEAD_DIM), lambda h, b, i: (blk0 + per_part + h, b, 0, 0)),
            pl.BlockSpec((heads, None, seq, HEAD_DIM), lambda h, b, i: (blk0 + 2 * per_part + h, b, 0, 0)),
            pl.BlockSpec((tq, tq), lambda h, b, i: (0, 0)),
        ],
        out_specs=pl.BlockSpec((tq, heads * HEAD_DIM), lambda h, b, i: (b * nq + i, h)),
        compiler_params=_cp(("arbitrary", "arbitrary", "arbitrary")),
    )(hm4, hm4, hm4, u)


def _nsa_cmp_kernel(c_ref, w1_ref, pe_ref, w2_ref, o_ref, *, nc):
    c = c_ref[...]
    half = NSA_CMP_STRIDE * HEAD_DIM
    a1 = _dot(c, w1_ref[0:half, :])
    a2 = _dot(c, w1_ref[half:2 * half, :])
    pc = _dot(pe_ref[...], w1_ref[...])[0:1, :]
    pre = a1 + pltpu.roll(a2, nc - 1, 0) + pc
    act = 0.5 * pre * (1.0 + jnp.tanh(0.7978845608028654 * (pre + 0.044715 * (pre * pre * pre))))
    o_ref[...] = _dot(act.astype(BF16), w2_ref[...]).astype(BF16)


def nsa_compress(cmp_heads, w1, pe, w2, *, batch, seq):
    nc = seq // NSA_CMP_STRIDE
    head0 = 0
    hm4 = cmp_heads.reshape(4, batch, nc, NSA_CMP_STRIDE * HEAD_DIM)
    return pl.pallas_call(
        functools.partial(_nsa_cmp_kernel, nc=nc),
        out_shape=jax.ShapeDtypeStruct((4, batch, nc, HEAD_DIM), BF16),
        grid=(4, batch),
        in_specs=[
            pl.BlockSpec((None, None, nc, NSA_CMP_STRIDE * HEAD_DIM), lambda c, b: (head0 + c, b, 0, 0)),
            pl.BlockSpec((None, NSA_CMP_LEN * HEAD_DIM, HEAD_DIM), lambda c, b: (c // 2, 0, 0)),
            pl.BlockSpec((None, 8, NSA_CMP_LEN * HEAD_DIM), lambda c, b: (c // 2, 0, 0)),
            pl.BlockSpec((None, HEAD_DIM, HEAD_DIM), lambda c, b: (c // 2, 0, 0)),
        ],
        out_specs=pl.BlockSpec((None, None, nc, HEAD_DIM), lambda c, b: (c, b, 0, 0)),
        compiler_params=_cp(("arbitrary", "arbitrary")),
    )(hm4, w1, pe, w2)


def _nsa_attn_kernel(q_ref, qa_ref, kc_ref, vc_ref, ks_ref, vs_ref, kw_ref, vw_ref, g_ref, ovt_ref, e_ref,
                     kpos_ref, cpos_ref, o_ref, ss_ref, *, tq, tk, seq, n_sel, n_top):
    qi = pl.program_id(2)
    t0 = qi * tq
    rows = NSA_HG * tq
    nc = seq // NSA_CMP_STRIDE
    q = jnp.concatenate([q_ref[...].reshape(rows, HEAD_DIM), qa_ref[...]], axis=1)

    rid = lax.broadcasted_iota(jnp.int32, (rows, 1), 0)
    trow = t0 + lax.bitwise_and(rid, tq - 1)

    def masked_softmax(s, valid):
        sm = jnp.where(valid, s, NEG)
        m = jnp.max(sm, axis=1, keepdims=True)
        e = jnp.where(valid, jnp.exp(sm - m), 0.0)
        d = jnp.sum(e, axis=1, keepdims=True)
        return e * (1.0 / jnp.where(d > 0.0, d, 1.0))

    n_i = lax.broadcasted_iota(jnp.int32, (1, nc), 1)
    end = n_i * NSA_CMP_STRIDE + (NSA_CMP_LEN - 1)
    s_c = _dot_nt(q, jnp.concatenate([kc_ref[...], cpos_ref[...]], axis=1))
    p_c = masked_softmax(s_c, end <= trow)
    o_c = _dot(p_c.astype(BF16), vc_ref[...])

    psum = p_c[0:tq] + p_c[tq:2 * tq] + p_c[2 * tq:3 * tq] + p_c[3 * tq:4 * tq]
    p_hi = psum.astype(BF16)
    p_lo = (psum - p_hi.astype(F32)).astype(BF16)
    ovt = ovt_ref[...]
    imp = _dot_nt(ovt, p_hi) + _dot_nt(ovt, p_lo)
    cur = lax.shift_right_logical(t0 + lax.broadcasted_iota(jnp.int32, (1, tq), 1),
                                  int(np.log2(NSA_SEL_LEN)))
    blk = lax.broadcasted_iota(jnp.int32, (n_sel, tq), 0)
    forced = (blk == 0) | (blk == cur) | (blk == cur - 1)
    key = jnp.where(blk > cur, -BIG, jnp.where(forced, BIG, imp))
    sub = lax.broadcasted_iota(jnp.int32, (8, tq), 0)
    chunks = [key[8 * r:8 * r + 8, :] for r in range(n_sel // 8)]
    ranks = [jnp.zeros((8, tq), F32) for _ in chunks]
    for i in range(n_sel):
        vi = key[i:i + 1, :]
        for r, kc in enumerate(chunks):
            gt = jnp.where(vi > kc, 1.0, 0.0)
            if r < i // 8:
                win = gt
            else:
                ge = jnp.where(vi >= kc, 1.0, 0.0)
                win = ge if r > i // 8 else jnp.where(sub > i % 8, ge, gt)
            ranks[r] = ranks[r] + win
    rank = jnp.concatenate(ranks, axis=0)
    selm_t = jnp.where((rank < float(n_top)) & (blk <= cur), 1.0, 0.0).astype(BF16)
    eye = (lax.broadcasted_iota(jnp.int32, (n_sel, n_sel), 0)
           == lax.broadcasted_iota(jnp.int32, (n_sel, n_sel), 1)).astype(F32).astype(BF16)
    selm = lax.dot_general(selm_t, eye, (((0,), (0,)), ((), ())), preferred_element_type=F32).astype(BF16)

    def sel_fill(slot, kt):
        k0 = pl.multiple_of(kt * tk, tk)
        kk = jnp.concatenate([ks_ref[pl.ds(k0, tk), :], kpos_ref[pl.ds(k0, tk), :]], axis=1)
        mex = _dot(selm, e_ref[kt])
        bias = (mex - 1.0) * BIG
        ss_ref[slot] = _dot_nt(q, kk) + jnp.concatenate([bias] * NSA_HG, axis=0)

    def sel_drain(slot, kt, carry, diag):
        m, l, acc = carry
        k0 = pl.multiple_of(kt * tk, tk)
        vv = vs_ref[pl.ds(k0, tk), :]
        sm = ss_ref[slot]
        if diag:
            spos = k0 + lax.broadcasted_iota(jnp.int32, (1, tk), 1)
            sm = jnp.where(spos <= trow, sm, NEG)
        m_new = jnp.maximum(m, jnp.max(sm, axis=1, keepdims=True))
        alpha = jnp.exp(m - m_new)
        p = jnp.exp((sm - m_new).astype(BF16))
        l = alpha * l + jnp.sum(p.astype(F32), axis=1, keepdims=True)
        acc = alpha * acc + _dot(p, vv)
        return m_new, l, acc

    def sel_body(j, carry):
        kt = 2 * j
        sel_fill(1, kt + 1)
        carry = sel_drain(0, kt, carry, False)
        sel_fill(0, kt + 2)
        return sel_drain(1, kt + 1, carry, False)

    kt_last = t0 // tk
    init = (jnp.full((rows, 1), NEG, F32), jnp.zeros((rows, 1), F32), jnp.zeros((rows, HEAD_DIM), F32))
    sel_fill(0, 0)
    carry = lax.fori_loop(0, kt_last // 2, sel_body, init)

    def even_tail(carry):
        return sel_drain(0, kt_last, carry, True)

    def odd_tail(carry):
        sel_fill(1, kt_last)
        return sel_drain(1, kt_last, sel_drain(0, kt_last - 1, carry, False), True)

    _, l_s, acc_s = lax.cond(lax.rem(kt_last, 2) == 1, odd_tail, even_tail, carry)
    o_s = acc_s * (1.0 / l_s)

    wk = NSA_WINDOW + tq
    ks0 = pl.multiple_of(jnp.maximum(t0 - NSA_WINDOW, 0), tq)
    kw = jnp.concatenate([kw_ref[pl.ds(ks0, wk), :], kpos_ref[pl.ds(ks0, wk), :]], axis=1)
    vw = vw_ref[pl.ds(ks0, wk), :]
    wpos = ks0 + lax.broadcasted_iota(jnp.int32, (1, wk), 1)
    dw = trow - wpos
    in_window = pltpu.bitcast(dw, jnp.uint32) < jnp.uint32(NSA_WINDOW)
    sm_w = jnp.where(in_window, _dot_nt(q, kw), NEG)
    p_w = jnp.exp((sm_w - jnp.max(sm_w, axis=1, keepdims=True)).astype(BF16))
    o_w = _dot(p_w, vw) * (1.0 / jnp.sum(p_w.astype(F32), axis=1, keepdims=True))

    gt = g_ref[...]
    for hg in range(NSA_HG):
        sl = slice(hg * tq, (hg + 1) * tq)
        o = (gt[:, 3 * hg:3 * hg + 1] * o_c[sl] + gt[:, 3 * hg + 1:3 * hg + 2] * o_s[sl]
             + gt[:, 3 * hg + 2:3 * hg + 3] * o_w[sl])
        o_ref[:, hg * HEAD_DIM:(hg + 1) * HEAD_DIM] = o.astype(o_ref.dtype)


def nsa_attn(hm, cmp, gates, consts, *, q_head0, kv_head0, batch, seq, tq, tk):
    ov, e, qa, kpos, cpos = consts
    nq = seq // tq
    nc = seq // NSA_CMP_STRIDE
    n_sel = seq // NSA_SEL_LEN
    n_top = min(NSA_TOPK, n_sel)
    assert tk % tq == 0 and seq % tk == 0 and seq >= NSA_WINDOW + tq and NSA_WINDOW % tq == 0
    hm4 = hm.reshape(hm.shape[0], batch, seq, HEAD_DIM)
    kv_spec = lambda off: pl.BlockSpec((None, None, seq, HEAD_DIM),
                                       lambda b, g, i: (kv_head0 + off + g, b, 0, 0))
    return pl.pallas_call(
        functools.partial(_nsa_attn_kernel, tq=tq, tk=tk, seq=seq, n_sel=n_sel, n_top=n_top),
        out_shape=jax.ShapeDtypeStruct((batch * seq, NSA_HEADS * HEAD_DIM), BF16),
        grid=(batch, NSA_GROUPS, nq),
        in_specs=[
            pl.BlockSpec((NSA_HG, None, tq, HEAD_DIM), lambda b, g, i: (q_head0 // NSA_HG + g, b, i, 0)),
            pl.BlockSpec((None, NSA_HG * tq, 128), lambda b, g, i: (g, 0, 0)),
            pl.BlockSpec((None, None, nc, HEAD_DIM), lambda b, g, i: (g, b, 0, 0)),
            pl.BlockSpec((None, None, nc, HEAD_DIM), lambda b, g, i: (2 + g, b, 0, 0)),
            kv_spec(4), kv_spec(6), kv_spec(8), kv_spec(10),
            pl.BlockSpec((tq, 128), lambda b, g, i: (b * nq + i, g)),
            pl.BlockSpec((n_sel, nc), lambda b, g, i: (0, 0)),
            pl.BlockSpec((seq // tk, n_sel, tk), lambda b, g, i: (0, 0, 0)),
            pl.BlockSpec((seq, 128), lambda b, g, i: (0, 0)),
            pl.BlockSpec((nc, 128), lambda b, g, i: (0, 0)),
        ],
        out_specs=pl.BlockSpec((tq, NSA_HG * HEAD_DIM), lambda b, g, i: (b * nq + i, g)),
        scratch_shapes=[pltpu.VMEM((2, NSA_HG * tq, tk), F32)],
        compiler_params=_cp(("arbitrary", "arbitrary", "arbitrary")),
    )(hm4, qa, cmp, cmp, hm4, hm4, hm4, hm4, gates, ov, e, kpos, cpos)


def _merge_kernel(x_ref, oa_ref, ob_ref, oc_ref, wm_ref, wb_ref, bm_ref, y_ref, xb_ref):
    @pl.when(pl.program_id(1) == 0)
    def _():
        xb_ref[...] = x_ref[...].astype(BF16)

    xb = xb_ref[...]
    acc = None
    for br, o_ref in enumerate((oa_ref, ob_ref, oc_ref)):
        gate = jax.nn.sigmoid(_dot(xb, wm_ref[br]) + bm_ref[br])
        term = gate * _dot(o_ref[...], wb_ref[br])
        acc = term if acc is None else acc + term
    y_ref[...] = acc.astype(y_ref.dtype)


def merge_branches(x, o_a, o_b, o_c, wm, wb, bm, *, tm, tn):
    t = x.shape[0]
    o_spec = pl.BlockSpec((tm, BRANCH_WIDTH), lambda i, j: (i, 0))
    return pl.pallas_call(
        _merge_kernel,
        out_shape=jax.ShapeDtypeStruct((t, D_MODEL), BF16),
        grid=(t // tm, D_MODEL // tn),
        in_specs=[
            pl.BlockSpec((tm, D_MODEL), lambda i, j: (i, 0)),
            o_spec, o_spec, o_spec,
            pl.BlockSpec((N_BRANCH, D_MODEL, tn), lambda i, j: (0, 0, j)),
            pl.BlockSpec((N_BRANCH, BRANCH_WIDTH, tn), lambda i, j: (0, 0, j)),
            pl.BlockSpec((N_BRANCH, 1, tn), lambda i, j: (0, 0, j)),
        ],
        out_specs=pl.BlockSpec((tm, tn), lambda i, j: (i, j)),
        scratch_shapes=[pltpu.VMEM((tm, D_MODEL), BF16)],
        compiler_params=_cp(("arbitrary", "arbitrary")),
    )(x, o_a, o_b, o_c, wm, wb, bm)


def _out_ln_kernel(y_ref, w_ref, x_ref, g_ref, b_ref, o_ref, *, alpha):
    h = _dot(y_ref[...], w_ref[...])
    o_ref[...] = _layer_norm(alpha * x_ref[...] + h, g_ref[...], b_ref[...])


def out_ln(y, w, x, g, b, *, alpha, tm):
    t = x.shape[0]
    return pl.pallas_call(
        functools.partial(_out_ln_kernel, alpha=alpha),
        out_shape=jax.ShapeDtypeStruct((t, D_MODEL), F32),
        grid=(t // tm,),
        in_specs=[
            pl.BlockSpec((tm, D_MODEL), lambda i: (i, 0)),
            pl.BlockSpec((D_MODEL, D_MODEL), lambda i: (0, 0)),
            pl.BlockSpec((tm, D_MODEL), lambda i: (i, 0)),
            pl.BlockSpec((1, D_MODEL), lambda i: (0, 0)),
            pl.BlockSpec((1, D_MODEL), lambda i: (0, 0)),
        ],
        out_specs=pl.BlockSpec((tm, D_MODEL), lambda i: (i, 0)),
        compiler_params=_cp(("arbitrary",)),
    )(y, w, x, g, b)


def _mem_attn_kernel(x_ref, wq_ref, k_ref, v_ref, wo_ref, g_ref, b_ref, o_ref, *, alpha):
    x = x_ref[...]
    q = _dot(x.astype(BF16), wq_ref[...]) * (HEAD_DIM ** -0.5)
    outs = []
    for h in range(MEM_HEADS):
        qh = q[:, h * HEAD_DIM:(h + 1) * HEAD_DIM].astype(BF16)
        s = _dot_nt(qh, k_ref[h])
        m = jnp.max(s, axis=1, keepdims=True)
        e = jnp.exp(s - m)
        p = e * (1.0 / jnp.sum(e, axis=1, keepdims=True))
        outs.append(_dot(p.astype(BF16), v_ref[h]).astype(BF16))
    o = jnp.concatenate(outs, axis=1)
    h_out = _dot(o, wo_ref[...])
    o_ref[...] = _layer_norm(alpha * x + h_out, g_ref[...], b_ref[...])


def mem_attn_ln(x, wq, kv, wo, g, b, *, alpha, seq, mem_len, tm):
    t = x.shape[0]
    per_b = seq // tm
    kv4 = kv.reshape(2 * MEM_HEADS, t // seq, mem_len, HEAD_DIM)
    width = MEM_HEADS * HEAD_DIM
    return pl.pallas_call(
        functools.partial(_mem_attn_kernel, alpha=alpha),
        out_shape=jax.ShapeDtypeStruct((t, D_MODEL), F32),
        grid=(t // tm,),
        in_specs=[
            pl.BlockSpec((tm, D_MODEL), lambda i: (i, 0)),
            pl.BlockSpec((D_MODEL, width), lambda i: (0, 0)),
            pl.BlockSpec((MEM_HEADS, None, mem_len, HEAD_DIM), lambda i: (0, i // per_b, 0, 0)),
            pl.BlockSpec((MEM_HEADS, None, mem_len, HEAD_DIM), lambda i: (1, i // per_b, 0, 0)),
            pl.BlockSpec((width, D_MODEL), lambda i: (0, 0)),
            pl.BlockSpec((1, D_MODEL), lambda i: (0, 0)),
            pl.BlockSpec((1, D_MODEL), lambda i: (0, 0)),
        ],
        out_specs=pl.BlockSpec((tm, D_MODEL), lambda i: (i, 0)),
        compiler_params=_cp(("arbitrary",)),
    )(x, wq, kv4, kv4, wo, g, b)


def _router_kernel(x_ref, w_ref, b_ref, tri_ref, mask_ref, wsel_ref, xp_ref, before_ref, count_ref):
    @pl.when(pl.program_id(0) == 0)
    def _():
        count_ref[...] = jnp.zeros_like(count_ref)

    x = x_ref[...]
    xp_ref[...] = _pack_bf16_pairs(x)
    xh = x.astype(BF16)
    xl = (x - xh.astype(F32)).astype(BF16)
    logits = _dot(xh, w_ref[0]) + _dot(xh, w_ref[1]) + _dot(xl, w_ref[0]) + b_ref[...]
    tm = logits.shape[0]
    lane = lax.broadcasted_iota(jnp.int32, (tm, N_EXPERTS), 1)
    work = logits
    hots, vals = [], []
    for _ in range(TOP_K):
        m = jnp.max(work, axis=1, keepdims=True)
        idx = jnp.min(jnp.where(work == m, lane, N_EXPERTS), axis=1, keepdims=True)
        hot = lane == idx
        hots.append(hot)
        vals.append(m)
        work = jnp.where(hot, -jnp.inf, work)
    es = [jnp.exp(v - vals[0]) for v in vals]
    inv = 1.0 / (es[0] + es[1] + es[2] + es[3])
    mask = jnp.zeros((tm, N_EXPERTS), F32)
    wsel = jnp.zeros((tm, N_EXPERTS), F32)
    for hot, e in zip(hots, es):
        mask = mask + jnp.where(hot, 1.0, 0.0)
        wsel = wsel + jnp.where(hot, e * inv, 0.0)
    mask_ref[...] = mask
    wsel_ref[...] = wsel
    before_ref[...] = count_ref[...] + _dot(tri_ref[...], mask.astype(BF16))
    count_ref[...] = count_ref[...] + jnp.sum(mask, axis=0, keepdims=True)


def router(x, w_hl, b, tri, *, tm):
    t = x.shape[0]
    tok32 = lambda: pl.BlockSpec((tm, N_EXPERTS), lambda i: (i, 0))
    return pl.pallas_call(
        _router_kernel,
        out_shape=(jax.ShapeDtypeStruct((t, N_EXPERTS), F32), jax.ShapeDtypeStruct((t, N_EXPERTS), F32),
                   jax.ShapeDtypeStruct((t, D_MODEL // 2), jnp.uint32),
                   jax.ShapeDtypeStruct((t, N_EXPERTS), F32)),
        grid=(t // tm,),
        in_specs=[
            pl.BlockSpec((tm, D_MODEL), lambda i: (i, 0)),
            pl.BlockSpec((2, D_MODEL, N_EXPERTS), lambda i: (0, 0, 0)),
            pl.BlockSpec((1, N_EXPERTS), lambda i: (0, 0)),
            pl.BlockSpec((tm, tm), lambda i: (0, 0)),
        ],
        out_specs=(tok32(), tok32(), pl.BlockSpec((tm, D_MODEL // 2), lambda i: (i, 0)), tok32()),
        scratch_shapes=[pltpu.VMEM((1, N_EXPERTS), F32)],
        compiler_params=_cp(("arbitrary",)),
    )(x, w_hl, b, tri)


def sc_gather_rows(table, idx, *, chunk):
    n = idx.shape[0]
    d = table.shape[1]
    workers = SC_CORES_V7X * SC_SUBCORES_V7X
    per_w = n // workers
    assert n % (workers * chunk) == 0 and chunk % 8 == 0 and chunk <= 128
    mesh = plsc.VectorSubcoreMesh(core_axis_name="c", subcore_axis_name="s")

    @functools.partial(
        pl.kernel, mesh=mesh,
        out_type=jax.ShapeDtypeStruct((n, d), table.dtype),
        scratch_types=[pltpu.VMEM((chunk,), jnp.int32), pltpu.VMEM((chunk, d), table.dtype),
                       pltpu.SemaphoreType.DMA],
    )
    def gather(table_hbm, idx_hbm, out_hbm, idx_v, rows_v, sem):
        wid = lax.axis_index("s") * SC_CORES_V7X + lax.axis_index("c")
        base = wid * per_w

        @pl.loop(0, per_w // chunk)
        def _(j):
            off = pl.multiple_of(base + j * chunk, 8)
            pltpu.sync_copy(idx_hbm.at[pl.ds(off, chunk)], idx_v)
            pltpu.async_copy(table_hbm.at[idx_v], rows_v, sem).wait()
            pltpu.sync_copy(rows_v, out_hbm.at[pl.ds(off, chunk)])

    return gather(table, idx)


def sc_scatter_rows(rows, idx, n_out, *, copies, chunk):
    t, d = rows.shape
    workers = SC_CORES_V7X * SC_SUBCORES_V7X
    per_w = t // workers
    assert idx.shape == (copies * t,) and t % (workers * chunk) == 0 and chunk % 8 == 0 and chunk <= 128
    mesh = plsc.VectorSubcoreMesh(core_axis_name="c", subcore_axis_name="s")

    @functools.partial(
        pl.kernel, mesh=mesh,
        out_type=jax.ShapeDtypeStruct((n_out, d), rows.dtype),
        scratch_types=[pltpu.VMEM((chunk,), jnp.int32), pltpu.VMEM((chunk, d), rows.dtype)],
    )
    def scatter(rows_hbm, idx_hbm, out_hbm, idx_v, rows_v):
        wid = lax.axis_index("s") * SC_CORES_V7X + lax.axis_index("c")
        base = wid * per_w

        @pl.loop(0, per_w // chunk)
        def _(j):
            off = pl.multiple_of(base + j * chunk, 8)
            pltpu.sync_copy(rows_hbm.at[pl.ds(off, chunk)], rows_v)
            for k in range(copies):
                pltpu.sync_copy(idx_hbm.at[pl.ds(pl.multiple_of(k * t + off, 8), chunk)], idx_v)
                pltpu.sync_copy(rows_v, out_hbm.at[idx_v])

    return scatter(rows, idx)


def _experts_kernel(be_ref, x_ref, wg_ref, bg_ref, wu_ref, bu_ref, wd_ref, bd_ref, y_ref,
                    wgb_ref, wub_ref, wdb_ref):
    i = pl.program_id(0)
    prev = be_ref[jnp.maximum(i - 1, 0)]
    n_used = be_ref[pl.num_programs(0)]

    @pl.when((i < n_used) & ((i == 0) | (be_ref[i] != prev)))
    def _():
        wgb_ref[...] = wg_ref[...].astype(BF16)
        wub_ref[...] = wu_ref[...].astype(BF16)
        wdb_ref[...] = wd_ref[...].astype(BF16)

    @pl.when(i < n_used)
    def _():
        x_lo, x_hi = _unpack_bf16_pairs(x_ref[...])
        xb = jnp.concatenate([x_lo.astype(BF16), x_hi.astype(BF16)], axis=1)
        g = jnp.minimum(_dot(xb, wgb_ref[...]) + bg_ref[...], SWIGLU_LIMIT)
        u = jnp.clip(_dot(xb, wub_ref[...]) + bu_ref[...], -SWIGLU_LIMIT, SWIGLU_LIMIT)
        hdn = (u + 1.0) * (g * jax.nn.sigmoid(SWIGLU_ALPHA * g))
        y_ref[...] = _pack_bf16_pairs(_dot(hdn.astype(BF16), wdb_ref[...]) + bd_ref[...])


def experts(x_rows, blk_e, wg, bg, wu, bu, wd, bd, *, layer, bm):
    n_rows, dp = x_rows.shape
    d, f = wg.shape[2], wg.shape[3]
    w_spec = lambda shape: pl.BlockSpec((None, None) + shape, lambda i, be: (layer, be[i], 0, 0))
    grid_spec = pltpu.PrefetchScalarGridSpec(
        num_scalar_prefetch=1,
        grid=(n_rows // bm,),
        in_specs=[
            pl.BlockSpec((bm, dp), lambda i, be: (i, 0)),
            w_spec((d, f)), w_spec((1, f)), w_spec((d, f)), w_spec((1, f)), w_spec((f, d)), w_spec((1, d)),
        ],
        out_specs=pl.BlockSpec((bm, dp), lambda i, be: (i, 0)),
        scratch_shapes=[pltpu.VMEM((d, f), BF16), pltpu.VMEM((d, f), BF16), pltpu.VMEM((f, d), BF16)],
    )
    return pl.pallas_call(
        _experts_kernel,
        out_shape=jax.ShapeDtypeStruct((n_rows, dp), jnp.uint32),
        grid_spec=grid_spec,
        compiler_params=_cp(("arbitrary",)),
    )(blk_e, x_rows, wg, bg, wu, bu, wd, bd)


def _moe_ln_kernel(x_ref, y_ref, w_ref, g_ref, b_ref, o_ref, *, alpha):
    w = w_ref[...]
    y_lo = y_hi = None
    for k in range(TOP_K):
        lo, hi = _unpack_bf16_pairs(y_ref[k])
        wk = w[:, k:k + 1]
        y_lo = wk * lo if y_lo is None else y_lo + wk * lo
        y_hi = wk * hi if y_hi is None else y_hi + wk * hi
    y = jnp.concatenate([y_lo, y_hi], axis=1)
    o_ref[...] = _layer_norm(alpha * x_ref[...] + y, g_ref[...], b_ref[...])


def moe_ln(x, y4, w4p, g, b, *, alpha, tm):
    t = x.shape[0]
    return pl.pallas_call(
        functools.partial(_moe_ln_kernel, alpha=alpha),
        out_shape=jax.ShapeDtypeStruct((t, D_MODEL), F32),
        grid=(t // tm,),
        in_specs=[
            pl.BlockSpec((tm, D_MODEL), lambda i: (i, 0)),
            pl.BlockSpec((TOP_K, tm, D_MODEL // 2), lambda i: (0, i, 0)),
            pl.BlockSpec((tm, 128), lambda i: (i, 0)),
            pl.BlockSpec((1, D_MODEL), lambda i: (0, 0)),
            pl.BlockSpec((1, D_MODEL), lambda i: (0, 0)),
        ],
        out_specs=pl.BlockSpec((tm, D_MODEL), lambda i: (i, 0)),
        compiler_params=_cp(("arbitrary",)),
    )(x, y4, w4p, g, b)


def _rope_tables(seq):
    inv = np.asarray(ROPE_THETA ** (-np.arange(0, MLA_ROPE, 2) / MLA_ROPE), np.float32)
    ang = jnp.arange(seq, dtype=F32)[:, None] * jnp.asarray(inv)[None, :]
    cos, sin = jnp.cos(ang), jnp.sin(ang)
    zeros = jnp.zeros((seq, 128 - MLA_ROPE), F32)
    return (jnp.concatenate([cos, cos, zeros], axis=1), jnp.concatenate([-sin, sin, zeros], axis=1))


def _nsa_constants(seq, tq, tk):
    nc = seq // NSA_CMP_STRIDE
    qa = np.zeros((NSA_GROUPS, NSA_HG * tq, 128), np.float32)
    for g in range(NSA_GROUPS):
        for hg in range(NSA_HG):
            slope = 2.0 ** (-8.0 * (g * NSA_HG + hg + 1) / NSA_HEADS)
            qa[g, hg * tq:(hg + 1) * tq, 0] = slope * NSA_SEL_LEN
            qa[g, hg * tq:(hg + 1) * tq, 1] = slope
            qa[g, hg * tq:(hg + 1) * tq, 2] = slope * NSA_CMP_STRIDE
            qa[g, hg * tq:(hg + 1) * tq, 3] = slope * (NSA_CMP_LEN - 1) / 2.0
    kpos = np.zeros((seq, 128), np.float32)
    kpos[:, 0] = np.arange(seq) // NSA_SEL_LEN
    kpos[:, 1] = np.arange(seq) % NSA_SEL_LEN
    cpos = np.zeros((nc, 128), np.float32)
    cpos[:, 2] = np.arange(nc)
    cpos[:, 3] = 1.0
    for arr in (qa, kpos, cpos):
        assert np.array_equal(arr.astype(BF16).astype(np.float32), arr)
    n_cmp = (seq - NSA_CMP_LEN) // NSA_CMP_STRIDE + 1
    n_sel = seq // NSA_SEL_LEN
    cs = np.arange(nc) * NSA_CMP_STRIDE
    ss = np.arange(n_sel) * NSA_SEL_LEN
    ov = np.clip(np.minimum(cs[:, None] + NSA_CMP_LEN, ss[None, :] + NSA_SEL_LEN)
                 - np.maximum(cs[:, None], ss[None, :]), 0, None) / NSA_CMP_LEN
    ov[n_cmp:] = 0.0
    e = (np.arange(seq)[None, :] // NSA_SEL_LEN == np.arange(n_sel)[:, None]).astype(np.float32)
    e = e.reshape(n_sel, seq // tk, tk).transpose(1, 0, 2)
    return tuple(jnp.asarray(a, BF16) for a in (ov.T, e, qa, kpos, cpos))


def _pad_cols(w, width):
    return jnp.pad(w, ((0, 0), (0, width - w.shape[1])))


def _swap_halves(w):
    half = w.shape[1] // 2
    return jnp.concatenate([w[:, half:], w[:, :half]], axis=1)


def _layer_weights(w_in, w_q_up, w_kv_up):
    kr = w_in[:, OFF_KR:OFF_NSA_Q]
    gate = w_in[:, OFF_NSA_GATE:OFF_SB]
    per_g = NSA_HG * 3
    w_mla = jnp.concatenate([
        w_in[:, OFF_CQ:OFF_KR],
        _pad_cols(kr, 128), _pad_cols(_swap_halves(kr), 128),
        _pad_cols(gate[:, :per_g], 128), _pad_cols(gate[:, per_g:], 128)], axis=1).astype(BF16)
    wq = w_q_up.reshape(MLA_Q_RANK, MLA_HEADS, MLA_NOPE + MLA_ROPE)
    rope = wq[:, :, MLA_NOPE:]
    rope_sw = jnp.concatenate([rope[:, :, MLA_ROPE // 2:], rope[:, :, :MLA_ROPE // 2]], axis=2)
    pad = ((0, 0), (0, 0), (0, 128 - MLA_ROPE))
    wq3 = jnp.concatenate([
        wq[:, :, :MLA_NOPE].reshape(MLA_Q_RANK, -1),
        jnp.pad(rope, pad).reshape(MLA_Q_RANK, -1),
        jnp.pad(rope_sw, pad).reshape(MLA_Q_RANK, -1)], axis=1).astype(BF16)
    wkv = w_kv_up.reshape(MLA_KV_RANK, MLA_HEADS, 2, 128)
    wkv = jnp.concatenate([wkv[:, :, 0].reshape(MLA_KV_RANK, -1),
                           wkv[:, :, 1].reshape(MLA_KV_RANK, -1)], axis=1).astype(BF16)
    w_heads = jnp.concatenate([w_in[:, OFF_NSA_Q:OFF_NSA_KV], w_in[:, OFF_SB:OFF_MERGE],
                               w_in[:, OFF_NSA_KV:OFF_NSA_GATE]], axis=1).astype(BF16)
    wm = w_in[:, OFF_MERGE:].reshape(D_MODEL, N_BRANCH, D_MODEL).transpose(1, 0, 2).astype(BF16)
    return w_mla, wq3, wkv, w_heads, wm


def _forward(x, mem, w_in, mla_q_norm, mla_w_q_up, mla_kv_norm, mla_w_kv_up,
             nsa_pe_k, nsa_pe_v, nsa_w1_k, nsa_w1_v, nsa_w2_k, nsa_w2_v,
             w_branch, b_merge, w_out, ln_mix_g, ln_mix_b,
             mem_w_q, mem_w_k, mem_w_v, mem_w_o, ln_mem_g, ln_mem_b,
             moe_w_router, moe_b_router, moe_w_gate, moe_b_gate, moe_w_up, moe_b_up,
             moe_w_down, moe_b_down, ln_moe_g, ln_moe_b):
    batch, seq, _ = x.shape
    mem_len = mem.shape[1]
    depth = w_in.shape[0]
    t = batch * seq
    alpha = float((2 * depth) ** 0.25)
    bm = MOE_BLOCK_ROWS
    n_rows = t * TOP_K + N_EXPERTS * bm
    n_blocks = n_rows // bm

    tm_in = min(512, seq)
    tq_mla = min(512, seq)
    tq_sb = 256
    tq_nsa, tk_nsa = 128, 512
    tm_ln = 256
    sc_chunk = 64

    cos128, sin128 = _rope_tables(seq)
    nsa_consts = _nsa_constants(seq, tq_nsa, tk_nsa)
    u_sb = jnp.asarray(np.arange(tq_sb)[:, None] > np.arange(tq_sb)[None, :], BF16)
    tri_router = jnp.asarray(np.arange(tm_in)[:, None] > np.arange(tm_in)[None, :], BF16)
    n_qheads = NSA_HEADS
    n_kvheads = 3 * 2 * NSA_GROUPS
    head_scale = np.ones((1, (n_qheads + n_kvheads + 3 * SB_HEADS) * HEAD_DIM), np.float32)
    head_scale[:, :n_qheads * HEAD_DIM] = HEAD_DIM ** -0.5
    sb0 = n_qheads
    kv0 = sb0 + 3 * SB_HEADS
    head_scale[:, sb0 * HEAD_DIM:(sb0 + SB_HEADS) * HEAD_DIM] = HEAD_DIM ** -0.5
    head_scale = jnp.asarray(head_scale)
    ones_kv = jnp.ones((1, 2 * MEM_HEADS * HEAD_DIM), F32)

    b_gate4 = moe_b_gate.reshape(depth, N_EXPERTS, 1, D_EXPERT)
    b_up4 = moe_b_up.reshape(depth, N_EXPERTS, 1, D_EXPERT)
    b_down4 = moe_b_down.reshape(depth, N_EXPERTS, 1, D_MODEL)

    xf = x.reshape(t, D_MODEL)
    memf = mem.reshape(batch * mem_len, D_MODEL)
    row = lambda v: v.reshape(1, -1)

    for l in range(depth):
        w_mla, wq3, wkv, w_heads, wm = _layer_weights(w_in[l], mla_w_q_up[l], mla_w_kv_up[l])

        q_a, k_a, v_a, gates = mla_in(xf, w_mla, row(mla_q_norm[l]), row(mla_kv_norm[l]), wq3, wkv,
                                      cos128, sin128, seq=seq, tm=tm_ln)
        hm = proj_heads(xf, w_heads, head_scale, tm=tm_in, tn=11 * HEAD_DIM)
        o_a = mla_attn(q_a, k_a, v_a, batch=batch, seq=seq, tq=tq_mla, heads=2)
        w1 = jnp.stack([nsa_w1_k[l], nsa_w1_v[l]]).astype(BF16)
        pe = jnp.stack([nsa_pe_k[l], nsa_pe_v[l]]).reshape(2, 1, -1)
        pe = jnp.broadcast_to(pe, (2, 8, pe.shape[-1])).astype(BF16)
        w2 = jnp.stack([nsa_w2_k[l], nsa_w2_v[l]]).astype(BF16)
        cmp = nsa_compress(hm[kv0:kv0 + 4], w1, pe, w2, batch=batch, seq=seq)
        o_b = nsa_attn(hm, cmp, gates, nsa_consts, q_head0=0, kv_head0=kv0,
                       batch=batch, seq=seq, tq=tq_nsa, tk=tk_nsa)
        o_c = sb_attn(hm, u_sb, head0=sb0, batch=batch, seq=seq, tq=tq_sb, heads=4)
        y = merge_branches(xf, o_a, o_b, o_c, wm, w_branch[l].astype(BF16),
                           b_merge[l].reshape(N_BRANCH, 1, D_MODEL), tm=tm_in, tn=512)
        xf = out_ln(y, w_out[l].astype(BF16), xf, row(ln_mix_g[l]), row(ln_mix_b[l]), alpha=alpha, tm=tm_ln)

        w_kv_mem = jnp.concatenate([mem_w_k[l], mem_w_v[l]], axis=1).astype(BF16)
        kv_mem = proj_heads(memf, w_kv_mem, ones_kv, tm=min(512, batch * mem_len), tn=512)
        xf = mem_attn_ln(xf, mem_w_q[l].astype(BF16), kv_mem, mem_w_o[l].astype(BF16),
                         row(ln_mem_g[l]), row(ln_mem_b[l]), alpha=alpha, seq=seq, mem_len=mem_len, tm=tm_ln)

        wr = moe_w_router[l]
        wr_hi = wr.astype(BF16)
        wr_lo = (wr - wr_hi.astype(F32)).astype(BF16)
        mask, wsel, x_packed, before = router(xf, jnp.stack([wr_hi, wr_lo]), row(moe_b_router[l]),
                                              tri_router, tm=tm_in)
        counts = (before[-1] + mask[-1]).astype(jnp.int32)
        padded = (counts + bm - 1) // bm * bm
        pad_end = jnp.cumsum(padded)
        pad_start = pad_end - padded
        slot = pad_start[None, :] + before.astype(jnp.int32)
        top_e = lax.top_k(mask, TOP_K)[1]
        pos4 = jnp.take_along_axis(slot, top_e, axis=1).astype(jnp.int32)
        w4 = jnp.take_along_axis(wsel, top_e, axis=1)
        blk_row0 = jnp.arange(n_blocks, dtype=jnp.int32)[:, None] * bm
        blk_e = jnp.minimum(jnp.sum((pad_end[None, :] <= blk_row0).astype(jnp.int32), axis=1), N_EXPERTS - 1)
        blk_e = jnp.concatenate([blk_e, pad_end[-1:] // bm]).astype(jnp.int32)
        pos_kmajor = pos4.T.reshape(-1)
        x_rows = sc_scatter_rows(x_packed, pos_kmajor, n_rows, copies=TOP_K, chunk=sc_chunk)
        y_rows = experts(x_rows, blk_e, moe_w_gate, b_gate4, moe_w_up, b_up4, moe_w_down, b_down4,
                         layer=l, bm=bm)
        y4 = sc_gather_rows(y_rows, pos_kmajor, chunk=sc_chunk).reshape(TOP_K, t, D_MODEL // 2)
        xf = moe_ln(xf, y4, _pad_cols(w4, 128), row(ln_moe_g[l]), row(ln_moe_b[l]), alpha=alpha, tm=tm_ln)

    return xf.reshape(batch, seq, D_MODEL)


def kernel(x, mem, w_in, mla_q_norm, mla_w_q_up, mla_kv_norm, mla_w_kv_up, nsa_pe_k, nsa_pe_v, nsa_w1_k, nsa_w1_v, nsa_w2_k, nsa_w2_v, w_branch, b_merge, w_out, ln_mix_g, ln_mix_b, mem_w_q, mem_w_k, mem_w_v, mem_w_o, ln_mem_g, ln_mem_b, moe_w_router, moe_b_router, moe_w_gate, moe_b_gate, moe_w_up, moe_b_up, moe_w_down, moe_b_down, ln_moe_g, ln_moe_b):
    return _forward(x, mem, w_in, mla_q_norm, mla_w_q_up, mla_kv_norm, mla_w_kv_up,
                    nsa_pe_k, nsa_pe_v, nsa_w1_k, nsa_w1_v, nsa_w2_k, nsa_w2_v,
                    w_branch, b_merge, w_out, ln_mix_g, ln_mix_b,
                    mem_w_q, mem_w_k, mem_w_v, mem_w_o, ln_mem_g, ln_mem_b,
                    moe_w_router, moe_b_router, moe_w_gate, moe_b_gate, moe_w_up, moe_b_up,
                    moe_w_down, moe_b_down, ln_moe_g, ln_moe_b)
```

```python
import functools

import numpy as np
import jax
import jax.numpy as jnp
from jax import lax
from jax.experimental import pallas as pl
from jax.experimental.pallas import tpu as pltpu
from jax.experimental.pallas import tpu_sc as plsc

F32 = jnp.float32
BF16 = jnp.bfloat16

D_MODEL = 2048
HEAD_DIM = 128
MLA_HEADS = 8
MLA_Q_RANK = 512
MLA_KV_RANK = 256
MLA_NOPE = 128
MLA_ROPE = 64
ROPE_THETA = 10000.0
NSA_HEADS = 8
NSA_GROUPS = 2
NSA_HG = NSA_HEADS // NSA_GROUPS
NSA_CMP_LEN = 32
NSA_CMP_STRIDE = 16
NSA_SEL_LEN = 64
NSA_TOPK = 16
NSA_WINDOW = 512
SB_HEADS = 8
MEM_HEADS = 4
N_EXPERTS = 32
TOP_K = 4
D_EXPERT = 512
SWIGLU_LIMIT = 7.0
SWIGLU_ALPHA = 1.702
N_BRANCH = 3
BRANCH_WIDTH = 1024
LN_EPS = 1e-5
RMS_EPS = 1e-6
NEG = -1e30
BIG = 1e30
SB_UNDERFLOW_LOG = -100.0

OFF_CQ = 0
OFF_CKV = 512
OFF_KR = 768
OFF_NSA_Q = 832
OFF_NSA_KV = 1856
OFF_NSA_GATE = 3392
OFF_SB = 3416
OFF_MERGE = 6488

VMEM_LIMIT_V7X = 56 * 1024 * 1024
MOE_BLOCK_ROWS = 512
SC_CORES_V7X = 2
SC_SUBCORES_V7X = 16


def _cp(sem, vmem=VMEM_LIMIT_V7X):
    return pltpu.CompilerParams(dimension_semantics=sem, vmem_limit_bytes=vmem)


def _dot(a, b):
    return jnp.dot(a, b, preferred_element_type=F32)


def _dot_nt(a, b):
    return lax.dot_general(a, b, (((1,), (1,)), ((), ())), preferred_element_type=F32)


def _layer_norm(z, g, b):
    mu = jnp.mean(z, axis=-1, keepdims=True)
    zc = z - mu
    var = jnp.mean(zc * zc, axis=-1, keepdims=True)
    return zc * lax.rsqrt(var + LN_EPS) * g + b


def _rms_norm(z, g):
    return z * lax.rsqrt(jnp.mean(z * z, axis=-1, keepdims=True) + RMS_EPS) * g


def _pack_bf16_pairs(z):
    n = z.shape[1] // 2
    bits = pltpu.bitcast(z.astype(BF16).astype(F32), jnp.uint32)
    return lax.shift_right_logical(bits[:, :n], jnp.uint32(16)) | (bits[:, n:] & jnp.uint32(0xFFFF0000))


def _unpack_bf16_pairs(w):
    lo = pltpu.bitcast(lax.shift_left(w, jnp.uint32(16)), F32)
    hi = pltpu.bitcast(w & jnp.uint32(0xFFFF0000), F32)
    return lo, hi


def _proj_heads_kernel(a_ref, w_ref, s_ref, o_ref, abf_ref, *, n_heads_per_tile):
    @pl.when(pl.program_id(1) == 0)
    def _():
        abf_ref[...] = a_ref[...].astype(BF16)

    acc = _dot(abf_ref[...], w_ref[...]) * s_ref[...]
    for c in range(n_heads_per_tile):
        o_ref[c] = acc[:, c * HEAD_DIM:(c + 1) * HEAD_DIM].astype(o_ref.dtype)


def proj_heads(a, w, scale, *, tm, tn):
    m, k = a.shape
    n = w.shape[1]
    hpt = tn // HEAD_DIM
    return pl.pallas_call(
        functools.partial(_proj_heads_kernel, n_heads_per_tile=hpt),
        out_shape=jax.ShapeDtypeStruct((n // HEAD_DIM, m, HEAD_DIM), BF16),
        grid=(m // tm, n // tn),
        in_specs=[
            pl.BlockSpec((tm, k), lambda i, j: (i, 0)),
            pl.BlockSpec((k, tn), lambda i, j: (0, j)),
            pl.BlockSpec((1, tn), lambda i, j: (0, j)),
        ],
        out_specs=pl.BlockSpec((hpt, tm, HEAD_DIM), lambda i, j: (j, i, 0)),
        scratch_shapes=[pltpu.VMEM((tm, k), BF16)],
        compiler_params=_cp(("arbitrary", "arbitrary")),
    )(a, w, scale)


def _mla_in_kernel(x_ref, w_ref, qg_ref, kg_ref, wq_ref, wkv_ref, cos_ref, sin_ref,
                   q_ref, k_ref, v_ref, g_ref):
    xb = x_ref[...].astype(BF16)
    h = _dot(xb, w_ref[...])
    cq = h[:, 0:512]
    ckv = h[:, 512:768]
    kr1 = h[:, 768:896]
    kr2 = h[:, 896:1024]
    g_ref[...] = jax.nn.sigmoid(h[:, 1024:1280])
    cos = cos_ref[...]
    sin = sin_ref[...]
    scale = (MLA_NOPE + MLA_ROPE) ** -0.5
    nq = _rms_norm(cq, qg_ref[...]).astype(BF16)
    q3 = _dot(nq, wq_ref[...])
    for hh in range(MLA_HEADS):
        lo, hi = hh * 128, (hh + 1) * 128
        q_ref[hh, :, 0:128] = (q3[:, lo:hi] * scale).astype(BF16)
        rot = q3[:, 1024 + lo:1024 + hi] * cos + q3[:, 2048 + lo:2048 + hi] * sin
        q_ref[hh, :, 128:256] = (rot * scale).astype(BF16)
    nkv = _rms_norm(ckv, kg_ref[...]).astype(BF16)
    kv = _dot(nkv, wkv_ref[...])
    krot = (kr1 * cos + kr2 * sin).astype(BF16)
    for hh in range(MLA_HEADS):
        lo, hi = hh * 128, (hh + 1) * 128
        k_ref[hh, :, 0:128] = kv[:, lo:hi].astype(BF16)
        k_ref[hh, :, 128:256] = krot
        v_ref[hh] = kv[:, 1024 + lo:1024 + hi].astype(BF16)


def mla_in(x, w_mla, qg, kg, wq3, wkv, cos128, sin128, *, seq, tm):
    t = x.shape[0]
    npos = seq // tm
    full = lambda shape: pl.BlockSpec(shape, lambda i: (0,) * len(shape))
    return pl.pallas_call(
        _mla_in_kernel,
        out_shape=(
            jax.ShapeDtypeStruct((MLA_HEADS, t, 256), BF16),
            jax.ShapeDtypeStruct((MLA_HEADS, t, 256), BF16),
            jax.ShapeDtypeStruct((MLA_HEADS, t, 128), BF16),
            jax.ShapeDtypeStruct((t, 256), F32),
        ),
        grid=(t // tm,),
        in_specs=[
            pl.BlockSpec((tm, D_MODEL), lambda i: (i, 0)),
            full((D_MODEL, 1280)),
            full((1, MLA_Q_RANK)),
            full((1, MLA_KV_RANK)),
            full((MLA_Q_RANK, 3072)),
            full((MLA_KV_RANK, 2048)),
            pl.BlockSpec((tm, 128), lambda i: (i % npos, 0)),
            pl.BlockSpec((tm, 128), lambda i: (i % npos, 0)),
        ],
        out_specs=(
            pl.BlockSpec((MLA_HEADS, tm, 256), lambda i: (0, i, 0)),
            pl.BlockSpec((MLA_HEADS, tm, 256), lambda i: (0, i, 0)),
            pl.BlockSpec((MLA_HEADS, tm, 128), lambda i: (0, i, 0)),
            pl.BlockSpec((tm, 256), lambda i: (i, 0)),
        ),
        compiler_params=_cp(("arbitrary",)),
    )(x, w_mla, qg, kg, wq3, wkv, cos128, sin128)


def _mla_attn_kernel(q_ref, k_ref, v_ref, o_ref, s_ref, *, tq, heads):
    qi = pl.program_id(2)

    def scores(h, kt):
        k0 = pl.multiple_of(kt * tq, tq)
        return _dot_nt(q_ref[h], k_ref[h, pl.ds(k0, tq), :])

    def consume(h, kt, s, carry, diag):
        m, l, acc = carry
        k0 = pl.multiple_of(kt * tq, tq)
        v = v_ref[h, pl.ds(k0, tq), :]
        if diag:
            row = lax.broadcasted_iota(jnp.int32, (tq, tq), 0)
            col = lax.broadcasted_iota(jnp.int32, (tq, tq), 1)
            s = jnp.where(col <= row, s, NEG)
        m_new = jnp.maximum(m, jnp.max(s, axis=1, keepdims=True))
        alpha = jnp.exp(m - m_new)
        p = jnp.exp((s - m_new).astype(BF16))
        l = alpha * l + jnp.sum(p.astype(F32), axis=1, keepdims=True)
        acc = alpha * acc + _dot(p, v)
        return m_new, l, acc

    def fill(slot, kt):
        for h in range(heads):
            s_ref[slot, h] = scores(h, kt)

    def drain(slot, kt, carries, diag):
        return tuple(consume(h, kt, s_ref[slot, h], carries[h], diag) for h in range(heads))

    def body(j, carries):
        kt = 2 * j
        fill(1, kt + 1)
        carries = drain(0, kt, carries, False)
        fill(0, kt + 2)
        return drain(1, kt + 1, carries, False)

    init = (jnp.full((tq, 1), NEG, F32), jnp.zeros((tq, 1), F32), jnp.zeros((tq, 128), F32))
    fill(0, 0)
    carries = lax.fori_loop(0, qi // 2, body, (init,) * heads)

    def even_tail(carries):
        return drain(0, qi, carries, True)

    def odd_tail(carries):
        fill(1, qi)
        return drain(1, qi, drain(0, qi - 1, carries, False), True)

    carries = lax.cond(lax.rem(qi, 2) == 1, odd_tail, even_tail, carries)
    for h, (_, l, acc) in enumerate(carries):
        o_ref[:, h * 128:(h + 1) * 128] = (acc / l).astype(o_ref.dtype)


def mla_attn(q, k, v, *, batch, seq, tq, heads):
    nq = seq // tq
    q4 = q.reshape(MLA_HEADS, batch, seq, 256)
    k4 = k.reshape(MLA_HEADS, batch, seq, 256)
    v4 = v.reshape(MLA_HEADS, batch, seq, 128)
    return pl.pallas_call(
        functools.partial(_mla_attn_kernel, tq=tq, heads=heads),
        out_shape=jax.ShapeDtypeStruct((batch * seq, MLA_HEADS * 128), BF16),
        grid=(MLA_HEADS // heads, batch, nq),
        in_specs=[
            pl.BlockSpec((heads, None, tq, 256), lambda h, b, i: (h, b, i, 0)),
            pl.BlockSpec((heads, None, seq, 256), lambda h, b, i: (h, b, 0, 0)),
            pl.BlockSpec((heads, None, seq, 128), lambda h, b, i: (h, b, 0, 0)),
        ],
        out_specs=pl.BlockSpec((tq, heads * 128), lambda h, b, i: (b * nq + i, h)),
        scratch_shapes=[pltpu.VMEM((2, heads, tq, tq), F32)],
        compiler_params=_cp(("arbitrary", "arbitrary", "arbitrary")),
    )(q4, k4, v4)


def _sb_attn_kernel(q_ref, k_ref, v_ref, u_ref, o_ref, *, tq, heads):
    qi = pl.program_id(2)
    u = u_ref[...]

    def head_step(h, kt, carry, diag):
        run, acc = carry
        k0 = pl.multiple_of(kt * tq, tq)
        k = k_ref[h, pl.ds(k0, tq), :]
        v = v_ref[h, pl.ds(k0, tq), :]
        z = _dot_nt(q_ref[h], k)
        l1m = -(jnp.maximum(z, 0.0) + jnp.log(1.0 + jnp.exp(-jnp.abs(z))))
        if diag:
            row = lax.broadcasted_iota(jnp.int32, (tq, tq), 0)
            col = lax.broadcasted_iota(jnp.int32, (tq, tq), 1)
            strict = col < row
            l1m_m = jnp.where(strict, l1m, 0.0)
        else:
            l1m_m = l1m
        hi = l1m_m.astype(BF16)
        lo = (l1m_m - hi.astype(F32)).astype(BF16)
        between = _dot(hi, u) + _dot(lo, u)
        a = jnp.exp(z + l1m + between + run)
        if diag:
            a = jnp.where(strict, a, 0.0)
        acc = acc + _dot(a.astype(BF16), v)
        run = run + between[:, 0:1] + l1m_m[:, 0:1]
        return run, acc

    def step(kt, carries, diag):
        return tuple(head_step(h, kt, carries[h], diag) for h in range(heads))

    init = (jnp.zeros((tq, 1), F32), jnp.zeros((tq, 128), F32))
    carries = step(qi, (init,) * heads, True)

    def more(c):
        j, carries = c
        top = carries[0][0]
        for run, _ in carries[1:]:
            top = jnp.maximum(top, run)
        return (j < qi) & (jnp.max(top) > SB_UNDERFLOW_LOG)

    def body(c):
        j, carries = c
        return j + 1, step(qi - 1 - j, carries, False)

    _, carries = lax.while_loop(more, body, (jnp.int32(0), carries))
    for h, (_, acc) in enumerate(carries):
        o_ref[:, h * HEAD_DIM:(h + 1) * HEAD_DIM] = acc.astype(o_ref.dtype)


def sb_attn(hm, u, *, head0, batch, seq, tq, heads):
    nq = seq // tq
    assert head0 % heads == 0 and SB_HEADS % heads == 0
    hm4 = hm.reshape(hm.shape[0], batch, seq, HEAD_DIM)
    blk0 = head0 // heads
    per_part = SB_HEADS // heads
    return pl.pallas_call(
        functools.partial(_sb_attn_kernel, tq=tq, heads=heads),
        out_shape=jax.ShapeDtypeStruct((batch * seq, SB_HEADS * HEAD_DIM), BF16),
        grid=(per_part, batch, nq),
        in_specs=[
            pl.BlockSpec((heads, None, tq, HEAD_DIM), lambda h, b, i: (blk0 + h, b, i, 0)),
            pl.BlockSpec((heads, None, seq, HEAD_DIM), lambda h, b, i: (blk0 + per_part + h, b, 0, 0)),
            pl.BlockSpec((heads, None, seq, HEAD_DIM), lambda h, b, i: (blk0 + 2 * per_part + h, b, 0, 0)),
            pl.BlockSpec((tq, tq), lambda h, b, i: (0, 0)),
        ],
        out_specs=pl.BlockSpec((tq, heads * HEAD_DIM), lambda h, b, i: (b * nq + i, h)),
        compiler_params=_cp(("arbitrary", "arbitrary", "arbitrary")),
    )(hm4, hm4, hm4, u)


def _nsa_cmp_kernel(c_ref, w1_ref, pe_ref, w2_ref, o_ref, *, nc):
    c = c_ref[...]
    half = NSA_CMP_STRIDE * HEAD_DIM
    a1 = _dot(c, w1_ref[0:half, :])
    a2 = _dot(c, w1_ref[half:2 * half, :])
    pc = _dot(pe_ref[...], w1_ref[...])[0:1, :]
    pre = a1 + pltpu.roll(a2, nc - 1, 0) + pc
    act = 0.5 * pre * (1.0 + jnp.tanh(0.7978845608028654 * (pre + 0.044715 * (pre * pre * pre))))
    o_ref[...] = _dot(act.astype(BF16), w2_ref[...]).astype(BF16)


def nsa_compress(cmp_heads, w1, pe, w2, *, batch, seq):
    nc = seq // NSA_CMP_STRIDE
    head0 = 0
    hm4 = cmp_heads.reshape(4, batch, nc, NSA_CMP_STRIDE * HEAD_DIM)
    return pl.pallas_call(
        functools.partial(_nsa_cmp_kernel, nc=nc),
        out_shape=jax.ShapeDtypeStruct((4, batch, nc, HEAD_DIM), BF16),
        grid=(4, batch),
        in_specs=[
            pl.BlockSpec((None, None, nc, NSA_CMP_STRIDE * HEAD_DIM), lambda c, b: (head0 + c, b, 0, 0)),
            pl.BlockSpec((None, NSA_CMP_LEN * HEAD_DIM, HEAD_DIM), lambda c, b: (c // 2, 0, 0)),
            pl.BlockSpec((None, 8, NSA_CMP_LEN * HEAD_DIM), lambda c, b: (c // 2, 0, 0)),
            pl.BlockSpec((None, HEAD_DIM, HEAD_DIM), lambda c, b: (c // 2, 0, 0)),
        ],
        out_specs=pl.BlockSpec((None, None, nc, HEAD_DIM), lambda c, b: (c, b, 0, 0)),
        compiler_params=_cp(("arbitrary", "arbitrary")),
    )(hm4, w1, pe, w2)


def _nsa_attn_kernel(q_ref, qa_ref, kc_ref, vc_ref, ks_ref, vs_ref, kw_ref, vw_ref, g_ref, ovt_ref, e_ref,
                     kpos_ref, cpos_ref, o_ref, ss_ref, *, tq, tk, seq, n_sel, n_top):
    qi = pl.program_id(1)
    t0 = qi * tq
    rows = NSA_HG * tq
    nc = seq // NSA_CMP_STRIDE
    groups = range(NSA_GROUPS)

    rid = lax.broadcasted_iota(jnp.int32, (rows, 1), 0)
    trow = t0 + lax.bitwise_and(rid, tq - 1)

    def masked_softmax(s, valid):
        sm = jnp.where(valid, s, NEG)
        m = jnp.max(sm, axis=1, keepdims=True)
        e = jnp.where(valid, jnp.exp(sm - m), 0.0)
        d = jnp.sum(e, axis=1, keepdims=True)
        return e * (1.0 / jnp.where(d > 0.0, d, 1.0))

    n_i = lax.broadcasted_iota(jnp.int32, (1, nc), 1)
    end = n_i * NSA_CMP_STRIDE + (NSA_CMP_LEN - 1)
    cur = lax.shift_right_logical(t0 + lax.broadcasted_iota(jnp.int32, (1, tq), 1),
                                  int(np.log2(NSA_SEL_LEN)))
    blk = lax.broadcasted_iota(jnp.int32, (n_sel, tq), 0)
    forced = (blk == 0) | (blk == cur) | (blk == cur - 1)
    sub = lax.broadcasted_iota(jnp.int32, (8, tq), 0)
    eye = (lax.broadcasted_iota(jnp.int32, (n_sel, n_sel), 0)
           == lax.broadcasted_iota(jnp.int32, (n_sel, n_sel), 1)).astype(F32).astype(BF16)
    ovt = ovt_ref[...]

    def front(g):
        q = jnp.concatenate([q_ref[NSA_HG * g:NSA_HG * (g + 1)].reshape(rows, HEAD_DIM), qa_ref[g]], axis=1)

        s_c = _dot_nt(q, jnp.concatenate([kc_ref[g], cpos_ref[...]], axis=1))
        p_c = masked_softmax(s_c, end <= trow)
        o_c = _dot(p_c.astype(BF16), vc_ref[g])

        psum = p_c[0:tq] + p_c[tq:2 * tq] + p_c[2 * tq:3 * tq] + p_c[3 * tq:4 * tq]
        p_hi = psum.astype(BF16)
        p_lo = (psum - p_hi.astype(F32)).astype(BF16)
        imp = _dot_nt(ovt, p_hi) + _dot_nt(ovt, p_lo)
        key = jnp.where(blk > cur, -BIG, jnp.where(forced, BIG, imp))
        chunks = [key[8 * r:8 * r + 8, :] for r in range(n_sel // 8)]
        ranks = [jnp.zeros((8, tq), F32) for _ in chunks]
        for i in range(n_sel):
            vi = key[i:i + 1, :]
            for r, kc in enumerate(chunks):
                gt = jnp.where(vi > kc, 1.0, 0.0)
                if r < i // 8:
                    win = gt
                else:
                    ge = jnp.where(vi >= kc, 1.0, 0.0)
                    win = ge if r > i // 8 else jnp.where(sub > i % 8, ge, gt)
                ranks[r] = ranks[r] + win
        rank = jnp.concatenate(ranks, axis=0)
        selm_t = jnp.where((rank < float(n_top)) & (blk <= cur), 1.0, 0.0).astype(BF16)
        selm = lax.dot_general(selm_t, eye, (((0,), (0,)), ((), ())), preferred_element_type=F32).astype(BF16)
        return q, selm, o_c

    fronts = [front(g) for g in groups]

    def sel_fill(slot, kt):
        k0 = pl.multiple_of(kt * tk, tk)
        kp = kpos_ref[pl.ds(k0, tk), :]
        for g in groups:
            q, selm, _ = fronts[g]
            kk = jnp.concatenate([ks_ref[g, pl.ds(k0, tk), :], kp], axis=1)
            mex = _dot(selm, e_ref[kt])
            bias = (mex - 1.0) * BIG
            ss_ref[slot, g] = _dot_nt(q, kk) + jnp.concatenate([bias] * NSA_HG, axis=0)

    def sel_drain_group(g, slot, kt, carry, diag):
        m, l, acc = carry
        k0 = pl.multiple_of(kt * tk, tk)
        vv = vs_ref[g, pl.ds(k0, tk), :]
        sm = ss_ref[slot, g]
        if diag:
            spos = k0 + lax.broadcasted_iota(jnp.int32, (1, tk), 1)
            sm = jnp.where(spos <= trow, sm, NEG)
        m_new = jnp.maximum(m, jnp.max(sm, axis=1, keepdims=True))
        alpha = jnp.exp(m - m_new)
        p = jnp.exp((sm - m_new).astype(BF16))
        l = alpha * l + jnp.sum(p.astype(F32), axis=1, keepdims=True)
        acc = alpha * acc + _dot(p, vv)
        return m_new, l, acc

    def sel_drain(slot, kt, carries, diag):
        return tuple(sel_drain_group(g, slot, kt, carries[g], diag) for g in groups)

    def sel_body(j, carries):
        kt = 2 * j
        sel_fill(1, kt + 1)
        carries = sel_drain(0, kt, carries, False)
        sel_fill(0, kt + 2)
        return sel_drain(1, kt + 1, carries, False)

    kt_last = t0 // tk
    init = (jnp.full((rows, 1), NEG, F32), jnp.zeros((rows, 1), F32), jnp.zeros((rows, HEAD_DIM), F32))
    sel_fill(0, 0)
    carries = lax.fori_loop(0, kt_last // 2, sel_body, (init,) * NSA_GROUPS)

    def even_tail(carries):
        return sel_drain(0, kt_last, carries, True)

    def odd_tail(carries):
        sel_fill(1, kt_last)
        return sel_drain(1, kt_last, sel_drain(0, kt_last - 1, carries, False), True)

    carries = lax.cond(lax.rem(kt_last, 2) == 1, odd_tail, even_tail, carries)

    wk = NSA_WINDOW + tq
    ks0 = pl.multiple_of(jnp.maximum(t0 - NSA_WINDOW, 0), tq)
    kpw = kpos_ref[pl.ds(ks0, wk), :]
    wpos = ks0 + lax.broadcasted_iota(jnp.int32, (1, wk), 1)
    dw = trow - wpos
    in_window = pltpu.bitcast(dw, jnp.uint32) < jnp.uint32(NSA_WINDOW)
    gt = g_ref[...]
    for g in groups:
        q, _, o_c = fronts[g]
        _, l_s, acc_s = carries[g]
        o_s = acc_s * (1.0 / l_s)
        kw = jnp.concatenate([kw_ref[g, pl.ds(ks0, wk), :], kpw], axis=1)
        vw = vw_ref[g, pl.ds(ks0, wk), :]
        sm_w = jnp.where(in_window, _dot_nt(q, kw), NEG)
        p_w = jnp.exp((sm_w - jnp.max(sm_w, axis=1, keepdims=True)).astype(BF16))
        o_w = _dot(p_w, vw) * (1.0 / jnp.sum(p_w.astype(F32), axis=1, keepdims=True))
        for hg in range(NSA_HG):
            sl = slice(hg * tq, (hg + 1) * tq)
            c0 = 128 * g + 3 * hg
            o = (gt[:, c0:c0 + 1] * o_c[sl] + gt[:, c0 + 1:c0 + 2] * o_s[sl] + gt[:, c0 + 2:c0 + 3] * o_w[sl])
            h = NSA_HG * g + hg
            o_ref[:, h * HEAD_DIM:(h + 1) * HEAD_DIM] = o.astype(o_ref.dtype)


def nsa_attn(hm, cmp, gates, consts, *, q_head0, kv_head0, batch, seq, tq, tk):
    ov, e, qa, kpos, cpos = consts
    nq = seq // tq
    nc = seq // NSA_CMP_STRIDE
    n_sel = seq // NSA_SEL_LEN
    n_top = min(NSA_TOPK, n_sel)
    assert tk % tq == 0 and seq % tk == 0 and seq >= NSA_WINDOW + tq and NSA_WINDOW % tq == 0
    assert q_head0 % NSA_HEADS == 0 and kv_head0 % NSA_GROUPS == 0
    hm4 = hm.reshape(hm.shape[0], batch, seq, HEAD_DIM)
    kv_spec = lambda off: pl.BlockSpec((NSA_GROUPS, None, seq, HEAD_DIM),
                                       lambda b, i: ((kv_head0 + off) // NSA_GROUPS, b, 0, 0))
    return pl.pallas_call(
        functools.partial(_nsa_attn_kernel, tq=tq, tk=tk, seq=seq, n_sel=n_sel, n_top=n_top),
        out_shape=jax.ShapeDtypeStruct((batch * seq, NSA_HEADS * HEAD_DIM), BF16),
        grid=(batch, nq),
        in_specs=[
            pl.BlockSpec((NSA_HEADS, None, tq, HEAD_DIM), lambda b, i: (q_head0 // NSA_HEADS, b, i, 0)),
            pl.BlockSpec((NSA_GROUPS, NSA_HG * tq, 128), lambda b, i: (0, 0, 0)),
            pl.BlockSpec((NSA_GROUPS, None, nc, HEAD_DIM), lambda b, i: (0, b, 0, 0)),
            pl.BlockSpec((NSA_GROUPS, None, nc, HEAD_DIM), lambda b, i: (1, b, 0, 0)),
            kv_spec(4), kv_spec(6), kv_spec(8), kv_spec(10),
            pl.BlockSpec((tq, NSA_GROUPS * 128), lambda b, i: (b * nq + i, 0)),
            pl.BlockSpec((n_sel, nc), lambda b, i: (0, 0)),
            pl.BlockSpec((seq // tk, n_sel, tk), lambda b, i: (0, 0, 0)),
            pl.BlockSpec((seq, 128), lambda b, i: (0, 0)),
            pl.BlockSpec((nc, 128), lambda b, i: (0, 0)),
        ],
        out_specs=pl.BlockSpec((tq, NSA_HEADS * HEAD_DIM), lambda b, i: (b * nq + i, 0)),
        scratch_shapes=[pltpu.VMEM((2, NSA_GROUPS, NSA_HG * tq, tk), F32)],
        compiler_params=_cp(("arbitrary", "arbitrary")),
    )(hm4, qa, cmp, cmp, hm4, hm4, hm4, hm4, gates, ov, e, kpos, cpos)


def _merge_kernel(x_ref, oa_ref, ob_ref, oc_ref, wm_ref, wb_ref, bm_ref, y_ref, xb_ref):
    @pl.when(pl.program_id(1) == 0)
    def _():
        xb_ref[...] = x_ref[...].astype(BF16)

    xb = xb_ref[...]
    acc = None
    for br, o_ref in enumerate((oa_ref, ob_ref, oc_ref)):
        gate = jax.nn.sigmoid(_dot(xb, wm_ref[br]) + bm_ref[br])
        term = gate * _dot(o_ref[...], wb_ref[br])
        acc = term if acc is None else acc + term
    y_ref[...] = acc.astype(y_ref.dtype)


def merge_branches(x, o_a, o_b, o_c, wm, wb, bm, *, tm, tn):
    t = x.shape[0]
    o_spec = pl.BlockSpec((tm, BRANCH_WIDTH), lambda i, j: (i, 0))
    return pl.pallas_call(
        _merge_kernel,
        out_shape=jax.ShapeDtypeStruct((t, D_MODEL), BF16),
        grid=(t // tm, D_MODEL // tn),
        in_specs=[
            pl.BlockSpec((tm, D_MODEL), lambda i, j: (i, 0)),
            o_spec, o_spec, o_spec,
            pl.BlockSpec((N_BRANCH, D_MODEL, tn), lambda i, j: (0, 0, j)),
            pl.BlockSpec((N_BRANCH, BRANCH_WIDTH, tn), lambda i, j: (0, 0, j)),
            pl.BlockSpec((N_BRANCH, 1, tn), lambda i, j: (0, 0, j)),
        ],
        out_specs=pl.BlockSpec((tm, tn), lambda i, j: (i, j)),
        scratch_shapes=[pltpu.VMEM((tm, D_MODEL), BF16)],
        compiler_params=_cp(("arbitrary", "arbitrary")),
    )(x, o_a, o_b, o_c, wm, wb, bm)


def _out_ln_kernel(y_ref, w_ref, x_ref, g_ref, b_ref, o_ref, *, alpha):
    h = _dot(y_ref[...], w_ref[...])
    o_ref[...] = _layer_norm(alpha * x_ref[...] + h, g_ref[...], b_ref[...])


def out_ln(y, w, x, g, b, *, alpha, tm):
    t = x.shape[0]
    return pl.pallas_call(
        functools.partial(_out_ln_kernel, alpha=alpha),
        out_shape=jax.ShapeDtypeStruct((t, D_MODEL), F32),
        grid=(t // tm,),
        in_specs=[
            pl.BlockSpec((tm, D_MODEL), lambda i: (i, 0)),
            pl.BlockSpec((D_MODEL, D_MODEL), lambda i: (0, 0)),
            pl.BlockSpec((tm, D_MODEL), lambda i: (i, 0)),
            pl.BlockSpec((1, D_MODEL), lambda i: (0, 0)),
            pl.BlockSpec((1, D_MODEL), lambda i: (0, 0)),
        ],
        out_specs=pl.BlockSpec((tm, D_MODEL), lambda i: (i, 0)),
        compiler_params=_cp(("arbitrary",)),
    )(y, w, x, g, b)


def _mem_attn_kernel(x_ref, wq_ref, k_ref, v_ref, wo_ref, g_ref, b_ref, o_ref, *, alpha):
    x = x_ref[...]
    q = _dot(x.astype(BF16), wq_ref[...]) * (HEAD_DIM ** -0.5)
    outs = []
    for h in range(MEM_HEADS):
        qh = q[:, h * HEAD_DIM:(h + 1) * HEAD_DIM].astype(BF16)
        s = _dot_nt(qh, k_ref[h])
        m = jnp.max(s, axis=1, keepdims=True)
        e = jnp.exp(s - m)
        p = e * (1.0 / jnp.sum(e, axis=1, keepdims=True))
        outs.append(_dot(p.astype(BF16), v_ref[h]).astype(BF16))
    o = jnp.concatenate(outs, axis=1)
    h_out = _dot(o, wo_ref[...])
    o_ref[...] = _layer_norm(alpha * x + h_out, g_ref[...], b_ref[...])


def mem_attn_ln(x, wq, kv, wo, g, b, *, alpha, seq, mem_len, tm):
    t = x.shape[0]
    per_b = seq // tm
    kv4 = kv.reshape(2 * MEM_HEADS, t // seq, mem_len, HEAD_DIM)
    width = MEM_HEADS * HEAD_DIM
    return pl.pallas_call(
        functools.partial(_mem_attn_kernel, alpha=alpha),
        out_shape=jax.ShapeDtypeStruct((t, D_MODEL), F32),
        grid=(t // tm,),
        in_specs=[
            pl.BlockSpec((tm, D_MODEL), lambda i: (i, 0)),
            pl.BlockSpec((D_MODEL, width), lambda i: (0, 0)),
            pl.BlockSpec((MEM_HEADS, None, mem_len, HEAD_DIM), lambda i: (0, i // per_b, 0, 0)),
            pl.BlockSpec((MEM_HEADS, None, mem_len, HEAD_DIM), lambda i: (1, i // per_b, 0, 0)),
            pl.BlockSpec((width, D_MODEL), lambda i: (0, 0)),
            pl.BlockSpec((1, D_MODEL), lambda i: (0, 0)),
            pl.BlockSpec((1, D_MODEL), lambda i: (0, 0)),
        ],
        out_specs=pl.BlockSpec((tm, D_MODEL), lambda i: (i, 0)),
        compiler_params=_cp(("arbitrary",)),
    )(x, wq, kv4, kv4, wo, g, b)


def _router_kernel(x_ref, w_ref, b_ref, tri_ref, mask_ref, wsel_ref, xp_ref, before_ref, count_ref):
    @pl.when(pl.program_id(0) == 0)
    def _():
        count_ref[...] = jnp.zeros_like(count_ref)

    x = x_ref[...]
    xp_ref[...] = _pack_bf16_pairs(x)
    xh = x.astype(BF16)
    xl = (x - xh.astype(F32)).astype(BF16)
    logits = _dot(xh, w_ref[0]) + _dot(xh, w_ref[1]) + _dot(xl, w_ref[0]) + b_ref[...]
    tm = logits.shape[0]
    lane = lax.broadcasted_iota(jnp.int32, (tm, N_EXPERTS), 1)
    work = logits
    hots, vals = [], []
    for _ in range(TOP_K):
        m = jnp.max(work, axis=1, keepdims=True)
        idx = jnp.min(jnp.where(work == m, lane, N_EXPERTS), axis=1, keepdims=True)
        hot = lane == idx
        hots.append(hot)
        vals.append(m)
        work = jnp.where(hot, -jnp.inf, work)
    es = [jnp.exp(v - vals[0]) for v in vals]
    inv = 1.0 / (es[0] + es[1] + es[2] + es[3])
    mask = jnp.zeros((tm, N_EXPERTS), F32)
    wsel = jnp.zeros((tm, N_EXPERTS), F32)
    for hot, e in zip(hots, es):
        mask = mask + jnp.where(hot, 1.0, 0.0)
        wsel = wsel + jnp.where(hot, e * inv, 0.0)
    mask_ref[...] = mask
    wsel_ref[...] = wsel
    before_ref[...] = count_ref[...] + _dot(tri_ref[...], mask.astype(BF16))
    count_ref[...] = count_ref[...] + jnp.sum(mask, axis=0, keepdims=True)


def router(x, w_hl, b, tri, *, tm):
    t = x.shape[0]
    tok32 = lambda: pl.BlockSpec((tm, N_EXPERTS), lambda i: (i, 0))
    return pl.pallas_call(
        _router_kernel,
        out_shape=(jax.ShapeDtypeStruct((t, N_EXPERTS), F32), jax.ShapeDtypeStruct((t, N_EXPERTS), F32),
                   jax.ShapeDtypeStruct((t, D_MODEL // 2), jnp.uint32),
                   jax.ShapeDtypeStruct((t, N_EXPERTS), F32)),
        grid=(t // tm,),
        in_specs=[
            pl.BlockSpec((tm, D_MODEL), lambda i: (i, 0)),
            pl.BlockSpec((2, D_MODEL, N_EXPERTS), lambda i: (0, 0, 0)),
            pl.BlockSpec((1, N_EXPERTS), lambda i: (0, 0)),
            pl.BlockSpec((tm, tm), lambda i: (0, 0)),
        ],
        out_specs=(tok32(), tok32(), pl.BlockSpec((tm, D_MODEL // 2), lambda i: (i, 0)), tok32()),
        scratch_shapes=[pltpu.VMEM((1, N_EXPERTS), F32)],
        compiler_params=_cp(("arbitrary",)),
    )(x, w_hl, b, tri)


def sc_gather_rows(table, idx, *, chunk):
    n = idx.shape[0]
    d = table.shape[1]
    workers = SC_CORES_V7X * SC_SUBCORES_V7X
    per_w = n // workers
    assert n % (workers * chunk) == 0 and chunk % 8 == 0 and chunk <= 128
    mesh = plsc.VectorSubcoreMesh(core_axis_name="c", subcore_axis_name="s")

    @functools.partial(
        pl.kernel, mesh=mesh,
        out_type=jax.ShapeDtypeStruct((n, d), table.dtype),
        scratch_types=[pltpu.VMEM((chunk,), jnp.int32), pltpu.VMEM((chunk, d), table.dtype),
                       pltpu.SemaphoreType.DMA],
    )
    def gather(table_hbm, idx_hbm, out_hbm, idx_v, rows_v, sem):
        wid = lax.axis_index("s") * SC_CORES_V7X + lax.axis_index("c")
        base = wid * per_w

        @pl.loop(0, per_w // chunk)
        def _(j):
            off = pl.multiple_of(base + j * chunk, 8)
            pltpu.sync_copy(idx_hbm.at[pl.ds(off, chunk)], idx_v)
            pltpu.async_copy(table_hbm.at[idx_v], rows_v, sem).wait()
            pltpu.sync_copy(rows_v, out_hbm.at[pl.ds(off, chunk)])

    return gather(table, idx)


def sc_scatter_rows(rows, idx, n_out, *, copies, chunk):
    t, d = rows.shape
    workers = SC_CORES_V7X * SC_SUBCORES_V7X
    per_w = t // workers
    assert idx.shape == (copies * t,) and t % (workers * chunk) == 0 and chunk % 8 == 0 and chunk <= 128
    mesh = plsc.VectorSubcoreMesh(core_axis_name="c", subcore_axis_name="s")

    @functools.partial(
        pl.kernel, mesh=mesh,
        out_type=jax.ShapeDtypeStruct((n_out, d), rows.dtype),
        scratch_types=[pltpu.VMEM((chunk,), jnp.int32), pltpu.VMEM((chunk, d), rows.dtype)],
    )
    def scatter(rows_hbm, idx_hbm, out_hbm, idx_v, rows_v):
        wid = lax.axis_index("s") * SC_CORES_V7X + lax.axis_index("c")
        base = wid * per_w

        @pl.loop(0, per_w // chunk)
        def _(j):
            off = pl.multiple_of(base + j * chunk, 8)
            pltpu.sync_copy(rows_hbm.at[pl.ds(off, chunk)], rows_v)
            for k in range(copies):
                pltpu.sync_copy(idx_hbm.at[pl.ds(pl.multiple_of(k * t + off, 8), chunk)], idx_v)
                pltpu.sync_copy(rows_v, out_hbm.at[idx_v])

    return scatter(rows, idx)


def _experts_kernel(be_ref, x_ref, wg_ref, bg_ref, wu_ref, bu_ref, wd_ref, bd_ref, y_ref,
                    wgb_ref, wub_ref, wdb_ref):
    i = pl.program_id(0)
    prev = be_ref[jnp.maximum(i - 1, 0)]
    n_used = be_ref[pl.num_programs(0)]

    @pl.when((i < n_used) & ((i == 0) | (be_ref[i] != prev)))
    def _():
        wgb_ref[...] = wg_ref[...].astype(BF16)
        wub_ref[...] = wu_ref[...].astype(BF16)
        wdb_ref[...] = wd_ref[...].astype(BF16)

    @pl.when(i < n_used)
    def _():
        x_lo, x_hi = _unpack_bf16_pairs(x_ref[...])
        xb = jnp.concatenate([x_lo.astype(BF16), x_hi.astype(BF16)], axis=1)
        g = jnp.minimum(_dot(xb, wgb_ref[...]) + bg_ref[...], SWIGLU_LIMIT)
        u = jnp.clip(_dot(xb, wub_ref[...]) + bu_ref[...], -SWIGLU_LIMIT, SWIGLU_LIMIT)
        hdn = (u + 1.0) * (g * jax.nn.sigmoid(SWIGLU_ALPHA * g))
        y_ref[...] = _pack_bf16_pairs(_dot(hdn.astype(BF16), wdb_ref[...]) + bd_ref[...])


def experts(x_rows, blk_e, wg, bg, wu, bu, wd, bd, *, layer, bm):
    n_rows, dp = x_rows.shape
    d, f = wg.shape[2], wg.shape[3]
    w_spec = lambda shape: pl.BlockSpec((None, None) + shape, lambda i, be: (layer, be[i], 0, 0))
    grid_spec = pltpu.PrefetchScalarGridSpec(
        num_scalar_prefetch=1,
        grid=(n_rows // bm,),
        in_specs=[
            pl.BlockSpec((bm, dp), lambda i, be: (i, 0)),
            w_spec((d, f)), w_spec((1, f)), w_spec((d, f)), w_spec((1, f)), w_spec((f, d)), w_spec((1, d)),
        ],
        out_specs=pl.BlockSpec((bm, dp), lambda i, be: (i, 0)),
        scratch_shapes=[pltpu.VMEM((d, f), BF16), pltpu.VMEM((d, f), BF16), pltpu.VMEM((f, d), BF16)],
    )
    return pl.pallas_call(
        _experts_kernel,
        out_shape=jax.ShapeDtypeStruct((n_rows, dp), jnp.uint32),
        grid_spec=grid_spec,
        compiler_params=_cp(("arbitrary",)),
    )(blk_e, x_rows, wg, bg, wu, bu, wd, bd)


def _moe_ln_kernel(x_ref, y_ref, w_ref, g_ref, b_ref, o_ref, *, alpha):
    w = w_ref[...]
    y_lo = y_hi = None
    for k in range(TOP_K):
        lo, hi = _unpack_bf16_pairs(y_ref[k])
        wk = w[:, k:k + 1]
        y_lo = wk * lo if y_lo is None else y_lo + wk * lo
        y_hi = wk * hi if y_hi is None else y_hi + wk * hi
    y = jnp.concatenate([y_lo, y_hi], axis=1)
    o_ref[...] = _layer_norm(alpha * x_ref[...] + y, g_ref[...], b_ref[...])


def moe_ln(x, y4, w4p, g, b, *, alpha, tm):
    t = x.shape[0]
    return pl.pallas_call(
        functools.partial(_moe_ln_kernel, alpha=alpha),
        out_shape=jax.ShapeDtypeStruct((t, D_MODEL), F32),
        grid=(t // tm,),
        in_specs=[
            pl.BlockSpec((tm, D_MODEL), lambda i: (i, 0)),
            pl.BlockSpec((TOP_K, tm, D_MODEL // 2), lambda i: (0, i, 0)),
            pl.BlockSpec((tm, 128), lambda i: (i, 0)),
            pl.BlockSpec((1, D_MODEL), lambda i: (0, 0)),
            pl.BlockSpec((1, D_MODEL), lambda i: (0, 0)),
        ],
        out_specs=pl.BlockSpec((tm, D_MODEL), lambda i: (i, 0)),
        compiler_params=_cp(("arbitrary",)),
    )(x, y4, w4p, g, b)


def _rope_tables(seq):
    inv = np.asarray(ROPE_THETA ** (-np.arange(0, MLA_ROPE, 2) / MLA_ROPE), np.float32)
    ang = jnp.arange(seq, dtype=F32)[:, None] * jnp.asarray(inv)[None, :]
    cos, sin = jnp.cos(ang), jnp.sin(ang)
    zeros = jnp.zeros((seq, 128 - MLA_ROPE), F32)
    return (jnp.concatenate([cos, cos, zeros], axis=1), jnp.concatenate([-sin, sin, zeros], axis=1))


def _nsa_constants(seq, tq, tk):
    nc = seq // NSA_CMP_STRIDE
    qa = np.zeros((NSA_GROUPS, NSA_HG * tq, 128), np.float32)
    for g in range(NSA_GROUPS):
        for hg in range(NSA_HG):
            slope = 2.0 ** (-8.0 * (g * NSA_HG + hg + 1) / NSA_HEADS)
            qa[g, hg * tq:(hg + 1) * tq, 0] = slope * NSA_SEL_LEN
            qa[g, hg * tq:(hg + 1) * tq, 1] = slope
            qa[g, hg * tq:(hg + 1) * tq, 2] = slope * NSA_CMP_STRIDE
            qa[g, hg * tq:(hg + 1) * tq, 3] = slope * (NSA_CMP_LEN - 1) / 2.0
    kpos = np.zeros((seq, 128), np.float32)
    kpos[:, 0] = np.arange(seq) // NSA_SEL_LEN
    kpos[:, 1] = np.arange(seq) % NSA_SEL_LEN
    cpos = np.zeros((nc, 128), np.float32)
    cpos[:, 2] = np.arange(nc)
    cpos[:, 3] = 1.0
    for arr in (qa, kpos, cpos):
        assert np.array_equal(arr.astype(BF16).astype(np.float32), arr)
    n_cmp = (seq - NSA_CMP_LEN) // NSA_CMP_STRIDE + 1
    n_sel = seq // NSA_SEL_LEN
    cs = np.arange(nc) * NSA_CMP_STRIDE
    ss = np.arange(n_sel) * NSA_SEL_LEN
    ov = np.clip(np.minimum(cs[:, None] + NSA_CMP_LEN, ss[None, :] + NSA_SEL_LEN)
                 - np.maximum(cs[:, None], ss[None, :]), 0, None) / NSA_CMP_LEN
    ov[n_cmp:] = 0.0
    e = (np.arange(seq)[None, :] // NSA_SEL_LEN == np.arange(n_sel)[:, None]).astype(np.float32)
    e = e.reshape(n_sel, seq // tk, tk).transpose(1, 0, 2)
    return tuple(jnp.asarray(a, BF16) for a in (ov.T, e, qa, kpos, cpos))


def _pad_cols(w, width):
    return jnp.pad(w, ((0, 0), (0, width - w.shape[1])))


def _swap_halves(w):
    half = w.shape[1] // 2
    return jnp.concatenate([w[:, half:], w[:, :half]], axis=1)


def _layer_weights(w_in, w_q_up, w_kv_up):
    kr = w_in[:, OFF_KR:OFF_NSA_Q]
    gate = w_in[:, OFF_NSA_GATE:OFF_SB]
    per_g = NSA_HG * 3
    w_mla = jnp.concatenate([
        w_in[:, OFF_CQ:OFF_KR],
        _pad_cols(kr, 128), _pad_cols(_swap_halves(kr), 128),
        _pad_cols(gate[:, :per_g], 128), _pad_cols(gate[:, per_g:], 128)], axis=1).astype(BF16)
    wq = w_q_up.reshape(MLA_Q_RANK, MLA_HEADS, MLA_NOPE + MLA_ROPE)
    rope = wq[:, :, MLA_NOPE:]
    rope_sw = jnp.concatenate([rope[:, :, MLA_ROPE // 2:], rope[:, :, :MLA_ROPE // 2]], axis=2)
    pad = ((0, 0), (0, 0), (0, 128 - MLA_ROPE))
    wq3 = jnp.concatenate([
        wq[:, :, :MLA_NOPE].reshape(MLA_Q_RANK, -1),
        jnp.pad(rope, pad).reshape(MLA_Q_RANK, -1),
        jnp.pad(rope_sw, pad).reshape(MLA_Q_RANK, -1)], axis=1).astype(BF16)
    wkv = w_kv_up.reshape(MLA_KV_RANK, MLA_HEADS, 2, 128)
    wkv = jnp.concatenate([wkv[:, :, 0].reshape(MLA_KV_RANK, -1),
                           wkv[:, :, 1].reshape(MLA_KV_RANK, -1)], axis=1).astype(BF16)
    w_heads = jnp.concatenate([w_in[:, OFF_NSA_Q:OFF_NSA_KV], w_in[:, OFF_SB:OFF_MERGE],
                               w_in[:, OFF_NSA_KV:OFF_NSA_GATE]], axis=1).astype(BF16)
    wm = w_in[:, OFF_MERGE:].reshape(D_MODEL, N_BRANCH, D_MODEL).transpose(1, 0, 2).astype(BF16)
    return w_mla, wq3, wkv, w_heads, wm


def _forward(x, mem, w_in, mla_q_norm, mla_w_q_up, mla_kv_norm, mla_w_kv_up,
             nsa_pe_k, nsa_pe_v, nsa_w1_k, nsa_w1_v, nsa_w2_k, nsa_w2_v,
             w_branch, b_merge, w_out, ln_mix_g, ln_mix_b,
             mem_w_q, mem_w_k, mem_w_v, mem_w_o, ln_mem_g, ln_mem_b,
             moe_w_router, moe_b_router, moe_w_gate, moe_b_gate, moe_w_up, moe_b_up,
             moe_w_down, moe_b_down, ln_moe_g, ln_moe_b):
    batch, seq, _ = x.shape
    mem_len = mem.shape[1]
    depth = w_in.shape[0]
    t = batch * seq
    alpha = float((2 * depth) ** 0.25)
    bm = MOE_BLOCK_ROWS
    n_rows = t * TOP_K + N_EXPERTS * bm
    n_blocks = n_rows // bm

    tm_in = min(512, seq)
    tq_mla = min(512, seq)
    tq_sb = 256
    tq_nsa, tk_nsa = 128, 512
    tm_ln = 256
    sc_chunk = 64

    cos128, sin128 = _rope_tables(seq)
    nsa_consts = _nsa_constants(seq, tq_nsa, tk_nsa)
    u_sb = jnp.asarray(np.arange(tq_sb)[:, None] > np.arange(tq_sb)[None, :], BF16)
    tri_router = jnp.asarray(np.arange(tm_in)[:, None] > np.arange(tm_in)[None, :], BF16)
    n_qheads = NSA_HEADS
    n_kvheads = 3 * 2 * NSA_GROUPS
    head_scale = np.ones((1, (n_qheads + n_kvheads + 3 * SB_HEADS) * HEAD_DIM), np.float32)
    head_scale[:, :n_qheads * HEAD_DIM] = HEAD_DIM ** -0.5
    sb0 = n_qheads
    kv0 = sb0 + 3 * SB_HEADS
    head_scale[:, sb0 * HEAD_DIM:(sb0 + SB_HEADS) * HEAD_DIM] = HEAD_DIM ** -0.5
    head_scale = jnp.asarray(head_scale)
    ones_kv = jnp.ones((1, 2 * MEM_HEADS * HEAD_DIM), F32)

    b_gate4 = moe_b_gate.reshape(depth, N_EXPERTS, 1, D_EXPERT)
    b_up4 = moe_b_up.reshape(depth, N_EXPERTS, 1, D_EXPERT)
    b_down4 = moe_b_down.reshape(depth, N_EXPERTS, 1, D_MODEL)

    xf = x.reshape(t, D_MODEL)
    memf = mem.reshape(batch * mem_len, D_MODEL)
    row = lambda v: v.reshape(1, -1)

    for l in range(depth):
        w_mla, wq3, wkv, w_heads, wm = _layer_weights(w_in[l], mla_w_q_up[l], mla_w_kv_up[l])

        q_a, k_a, v_a, gates = mla_in(xf, w_mla, row(mla_q_norm[l]), row(mla_kv_norm[l]), wq3, wkv,
                                      cos128, sin128, seq=seq, tm=tm_ln)
        hm = proj_heads(xf, w_heads, head_scale, tm=tm_in, tn=11 * HEAD_DIM)
        o_a = mla_attn(q_a, k_a, v_a, batch=batch, seq=seq, tq=tq_mla, heads=2)
        w1 = jnp.stack([nsa_w1_k[l], nsa_w1_v[l]]).astype(BF16)
        pe = jnp.stack([nsa_pe_k[l], nsa_pe_v[l]]).reshape(2, 1, -1)
        pe = jnp.broadcast_to(pe, (2, 8, pe.shape[-1])).astype(BF16)
        w2 = jnp.stack([nsa_w2_k[l], nsa_w2_v[l]]).astype(BF16)
        cmp = nsa_compress(hm[kv0:kv0 + 4], w1, pe, w2, batch=batch, seq=seq)
        o_b = nsa_attn(hm, cmp, gates, nsa_consts, q_head0=0, kv_head0=kv0,
                       batch=batch, seq=seq, tq=tq_nsa, tk=tk_nsa)
        o_c = sb_attn(hm, u_sb, head0=sb0, batch=batch, seq=seq, tq=tq_sb, heads=4)
        y = merge_branches(xf, o_a, o_b, o_c, wm, w_branch[l].astype(BF16),
                           b_merge[l].reshape(N_BRANCH, 1, D_MODEL), tm=tm_in, tn=512)
        xf = out_ln(y, w_out[l].astype(BF16), xf, row(ln_mix_g[l]), row(ln_mix_b[l]), alpha=alpha, tm=tm_ln)

        w_kv_mem = jnp.concatenate([mem_w_k[l], mem_w_v[l]], axis=1).astype(BF16)
        kv_mem = proj_heads(memf, w_kv_mem, ones_kv, tm=min(512, batch * mem_len), tn=512)
        xf = mem_attn_ln(xf, mem_w_q[l].astype(BF16), kv_mem, mem_w_o[l].astype(BF16),
                         row(ln_mem_g[l]), row(ln_mem_b[l]), alpha=alpha, seq=seq, mem_len=mem_len, tm=tm_ln)

        wr = moe_w_router[l]
        wr_hi = wr.astype(BF16)
        wr_lo = (wr - wr_hi.astype(F32)).astype(BF16)
        mask, wsel, x_packed, before = router(xf, jnp.stack([wr_hi, wr_lo]), row(moe_b_router[l]),
                                              tri_router, tm=tm_in)
        counts = (before[-1] + mask[-1]).astype(jnp.int32)
        padded = (counts + bm - 1) // bm * bm
        pad_end = jnp.cumsum(padded)
        pad_start = pad_end - padded
        slot = pad_start[None, :] + before.astype(jnp.int32)
        top_e = lax.top_k(mask, TOP_K)[1]
        pos4 = jnp.take_along_axis(slot, top_e, axis=1).astype(jnp.int32)
        w4 = jnp.take_along_axis(wsel, top_e, axis=1)
        blk_row0 = jnp.arange(n_blocks, dtype=jnp.int32)[:, None] * bm
        blk_e = jnp.minimum(jnp.sum((pad_end[None, :] <= blk_row0).astype(jnp.int32), axis=1), N_EXPERTS - 1)
        blk_e = jnp.concatenate([blk_e, pad_end[-1:] // bm]).astype(jnp.int32)
        pos_kmajor = pos4.T.reshape(-1)
        x_rows = sc_scatter_rows(x_packed, pos_kmajor, n_rows, copies=TOP_K, chunk=sc_chunk)
        y_rows = experts(x_rows, blk_e, moe_w_gate, b_gate4, moe_w_up, b_up4, moe_w_down, b_down4,
                         layer=l, bm=bm)
        y4 = sc_gather_rows(y_rows, pos_kmajor, chunk=sc_chunk).reshape(TOP_K, t, D_MODEL // 2)
        xf = moe_ln(xf, y4, _pad_cols(w4, 128), row(ln_moe_g[l]), row(ln_moe_b[l]), alpha=alpha, tm=tm_ln)

    return xf.reshape(batch, seq, D_MODEL)


def kernel(x, mem, w_in, mla_q_norm, mla_w_q_up, mla_kv_norm, mla_w_kv_up, nsa_pe_k, nsa_pe_v, nsa_w1_k, nsa_w1_v, nsa_w2_k, nsa_w2_v, w_branch, b_merge, w_out, ln_mix_g, ln_mix_b, mem_w_q, mem_w_k, mem_w_v, mem_w_o, ln_mem_g, ln_mem_b, moe_w_router, moe_b_router, moe_w_gate, moe_b_gate, moe_w_up, moe_b_up, moe_w_down, moe_b_down, ln_moe_g, ln_moe_b):
    return _forward(x, mem, w_in, mla_q_norm, mla_w_q_up, mla_kv_norm, mla_w_kv_up,
                    nsa_pe_k, nsa_pe_v, nsa_w1_k, nsa_w1_v, nsa_w2_k, nsa_w2_v,
                    w_branch, b_merge, w_out, ln_mix_g, ln_mix_b,
                    mem_w_q, mem_w_k, mem_w_v, mem_w_o, ln_mem_g, ln_mem_b,
                    moe_w_router, moe_b_router, moe_w_gate, moe_b_gate, moe_w_up, moe_b_up,
                    moe_w_down, moe_b_down, ln_moe_g, ln_moe_b)
```

```python
import functools

import numpy as np
import jax
import jax.numpy as jnp
from jax import lax
from jax.experimental import pallas as pl
from jax.experimental.pallas import tpu as pltpu
from jax.experimental.pallas import tpu_sc as plsc

F32 = jnp.float32
BF16 = jnp.bfloat16

D_MODEL = 2048
HEAD_DIM = 128
MLA_HEADS = 8
MLA_Q_RANK = 512
MLA_KV_RANK = 256
MLA_NOPE = 128
MLA_ROPE = 64
ROPE_THETA = 10000.0
NSA_HEADS = 8
NSA_GROUPS = 2
NSA_HG = NSA_HEADS // NSA_GROUPS
NSA_CMP_LEN = 32
NSA_CMP_STRIDE = 16
NSA_SEL_LEN = 64
NSA_TOPK = 16
NSA_WINDOW = 512
SB_HEADS = 8
MEM_HEADS = 4
N_EXPERTS = 32
TOP_K = 4
D_EXPERT = 512
SWIGLU_LIMIT = 7.0
SWIGLU_ALPHA = 1.702
N_BRANCH = 3
BRANCH_WIDTH = 1024
LN_EPS = 1e-5
RMS_EPS = 1e-6
NEG = -1e30
BIG = 1e30
SB_UNDERFLOW_LOG = -100.0

OFF_CQ = 0
OFF_CKV = 512
OFF_KR = 768
OFF_NSA_Q = 832
OFF_NSA_KV = 1856
OFF_NSA_GATE = 3392
OFF_SB = 3416
OFF_MERGE = 6488

VMEM_LIMIT_V7X = 56 * 1024 * 1024
MOE_BLOCK_ROWS = 512
SC_CORES_V7X = 2
SC_SUBCORES_V7X = 16


def _cp(sem, vmem=VMEM_LIMIT_V7X):
    return pltpu.CompilerParams(dimension_semantics=sem, vmem_limit_bytes=vmem)


def _dot(a, b):
    return jnp.dot(a, b, preferred_element_type=F32)


def _dot_nt(a, b):
    return lax.dot_general(a, b, (((1,), (1,)), ((), ())), preferred_element_type=F32)


def _layer_norm(z, g, b):
    mu = jnp.mean(z, axis=-1, keepdims=True)
    zc = z - mu
    var = jnp.mean(zc * zc, axis=-1, keepdims=True)
    return zc * lax.rsqrt(var + LN_EPS) * g + b


def _rms_norm(z, g):
    return z * lax.rsqrt(jnp.mean(z * z, axis=-1, keepdims=True) + RMS_EPS) * g


def _pack_bf16_pairs(z):
    n = z.shape[1] // 2
    bits = pltpu.bitcast(z.astype(BF16).astype(F32), jnp.uint32)
    return lax.shift_right_logical(bits[:, :n], jnp.uint32(16)) | (bits[:, n:] & jnp.uint32(0xFFFF0000))


def _unpack_bf16_pairs(w):
    lo = pltpu.bitcast(lax.shift_left(w, jnp.uint32(16)), F32)
    hi = pltpu.bitcast(w & jnp.uint32(0xFFFF0000), F32)
    return lo, hi


def _proj_heads_kernel(a_ref, w_ref, s_ref, o_ref, abf_ref, *, n_heads_per_tile):
    @pl.when(pl.program_id(1) == 0)
    def _():
        abf_ref[...] = a_ref[...].astype(BF16)

    acc = _dot(abf_ref[...], w_ref[...]) * s_ref[...]
    for c in range(n_heads_per_tile):
        o_ref[c] = acc[:, c * HEAD_DIM:(c + 1) * HEAD_DIM].astype(o_ref.dtype)


def proj_heads(a, w, scale, *, tm, tn):
    m, k = a.shape
    n = w.shape[1]
    hpt = tn // HEAD_DIM
    return pl.pallas_call(
        functools.partial(_proj_heads_kernel, n_heads_per_tile=hpt),
        out_shape=jax.ShapeDtypeStruct((n // HEAD_DIM, m, HEAD_DIM), BF16),
        grid=(m // tm, n // tn),
        in_specs=[
            pl.BlockSpec((tm, k), lambda i, j: (i, 0)),
            pl.BlockSpec((k, tn), lambda i, j: (0, j)),
            pl.BlockSpec((1, tn), lambda i, j: (0, j)),
        ],
        out_specs=pl.BlockSpec((hpt, tm, HEAD_DIM), lambda i, j: (j, i, 0)),
        scratch_shapes=[pltpu.VMEM((tm, k), BF16)],
        compiler_params=_cp(("arbitrary", "arbitrary")),
    )(a, w, scale)


def _mla_in_kernel(x_ref, w_ref, qg_ref, kg_ref, wq_ref, wkv_ref, cos_ref, sin_ref,
                   q_ref, k_ref, v_ref, g_ref):
    xb = x_ref[...].astype(BF16)
    h = _dot(xb, w_ref[...])
    cq = h[:, 0:512]
    ckv = h[:, 512:768]
    kr1 = h[:, 768:896]
    kr2 = h[:, 896:1024]
    g_ref[...] = jax.nn.sigmoid(h[:, 1024:1280])
    cos = cos_ref[...]
    sin = sin_ref[...]
    scale = (MLA_NOPE + MLA_ROPE) ** -0.5
    nq = _rms_norm(cq, qg_ref[...]).astype(BF16)
    q3 = _dot(nq, wq_ref[...])
    for hh in range(MLA_HEADS):
        lo, hi = hh * 128, (hh + 1) * 128
        q_ref[hh, :, 0:128] = (q3[:, lo:hi] * scale).astype(BF16)
        rot = q3[:, 1024 + lo:1024 + hi] * cos + q3[:, 2048 + lo:2048 + hi] * sin
        q_ref[hh, :, 128:256] = (rot * scale).astype(BF16)
    nkv = _rms_norm(ckv, kg_ref[...]).astype(BF16)
    kv = _dot(nkv, wkv_ref[...])
    krot = (kr1 * cos + kr2 * sin).astype(BF16)
    for hh in range(MLA_HEADS):
        lo, hi = hh * 128, (hh + 1) * 128
        k_ref[hh, :, 0:128] = kv[:, lo:hi].astype(BF16)
        k_ref[hh, :, 128:256] = krot
        v_ref[hh] = kv[:, 1024 + lo:1024 + hi].astype(BF16)


def mla_in(x, w_mla, qg, kg, wq3, wkv, cos128, sin128, *, seq, tm):
    t = x.shape[0]
    npos = seq // tm
    full = lambda shape: pl.BlockSpec(shape, lambda i: (0,) * len(shape))
    return pl.pallas_call(
        _mla_in_kernel,
        out_shape=(
            jax.ShapeDtypeStruct((MLA_HEADS, t, 256), BF16),
            jax.ShapeDtypeStruct((MLA_HEADS, t, 256), BF16),
            jax.ShapeDtypeStruct((MLA_HEADS, t, 128), BF16),
            jax.ShapeDtypeStruct((t, 256), F32),
        ),
        grid=(t // tm,),
        in_specs=[
            pl.BlockSpec((tm, D_MODEL), lambda i: (i, 0)),
            full((D_MODEL, 1280)),
            full((1, MLA_Q_RANK)),
            full((1, MLA_KV_RANK)),
            full((MLA_Q_RANK, 3072)),
            full((MLA_KV_RANK, 2048)),
            pl.BlockSpec((tm, 128), lambda i: (i % npos, 0)),
            pl.BlockSpec((tm, 128), lambda i: (i % npos, 0)),
        ],
        out_specs=(
            pl.BlockSpec((MLA_HEADS, tm, 256), lambda i: (0, i, 0)),
            pl.BlockSpec((MLA_HEADS, tm, 256), lambda i: (0, i, 0)),
            pl.BlockSpec((MLA_HEADS, tm, 128), lambda i: (0, i, 0)),
            pl.BlockSpec((tm, 256), lambda i: (i, 0)),
        ),
        compiler_params=_cp(("arbitrary",)),
    )(x, w_mla, qg, kg, wq3, wkv, cos128, sin128)


def _mla_attn_kernel(q_ref, k_ref, v_ref, o_ref, s_ref, *, tq, heads):
    qi = pl.program_id(2)

    def scores(h, kt):
        k0 = pl.multiple_of(kt * tq, tq)
        return _dot_nt(q_ref[h], k_ref[h, pl.ds(k0, tq), :])

    def consume(h, kt, s, carry, diag):
        m, l, acc = carry
        k0 = pl.multiple_of(kt * tq, tq)
        v = v_ref[h, pl.ds(k0, tq), :]
        if diag:
            row = lax.broadcasted_iota(jnp.int32, (tq, tq), 0)
            col = lax.broadcasted_iota(jnp.int32, (tq, tq), 1)
            s = jnp.where(col <= row, s, NEG)
        m_new = jnp.maximum(m, jnp.max(s, axis=1, keepdims=True))
        alpha = jnp.exp(m - m_new)
        p = jnp.exp((s - m_new).astype(BF16))
        l = alpha * l + jnp.sum(p.astype(F32), axis=1, keepdims=True)
        acc = alpha * acc + _dot(p, v)
        return m_new, l, acc

    def fill(slot, kt):
        for h in range(heads):
            s_ref[slot, h] = scores(h, kt)

    def drain(slot, kt, carries, diag):
        return tuple(consume(h, kt, s_ref[slot, h], carries[h], diag) for h in range(heads))

    def body(j, carries):
        kt = 2 * j
        fill(1, kt + 1)
        carries = drain(0, kt, carries, False)
        fill(0, kt + 2)
        return drain(1, kt + 1, carries, False)

    init = (jnp.full((tq, 1), NEG, F32), jnp.zeros((tq, 1), F32), jnp.zeros((tq, 128), F32))
    fill(0, 0)
    carries = lax.fori_loop(0, qi // 2, body, (init,) * heads)

    def even_tail(carries):
        return drain(0, qi, carries, True)

    def odd_tail(carries):
        fill(1, qi)
        return drain(1, qi, drain(0, qi - 1, carries, False), True)

    carries = lax.cond(lax.rem(qi, 2) == 1, odd_tail, even_tail, carries)
    for h, (_, l, acc) in enumerate(carries):
        o_ref[:, h * 128:(h + 1) * 128] = (acc / l).astype(o_ref.dtype)


def mla_attn(q, k, v, *, batch, seq, tq, heads):
    nq = seq // tq
    q4 = q.reshape(MLA_HEADS, batch, seq, 256)
    k4 = k.reshape(MLA_HEADS, batch, seq, 256)
    v4 = v.reshape(MLA_HEADS, batch, seq, 128)
    return pl.pallas_call(
        functools.partial(_mla_attn_kernel, tq=tq, heads=heads),
        out_shape=jax.ShapeDtypeStruct((batch * seq, MLA_HEADS * 128), BF16),
        grid=(MLA_HEADS // heads, batch, nq),
        in_specs=[
            pl.BlockSpec((heads, None, tq, 256), lambda h, b, i: (h, b, i, 0)),
            pl.BlockSpec((heads, None, seq, 256), lambda h, b, i: (h, b, 0, 0)),
            pl.BlockSpec((heads, None, seq, 128), lambda h, b, i: (h, b, 0, 0)),
        ],
        out_specs=pl.BlockSpec((tq, heads * 128), lambda h, b, i: (b * nq + i, h)),
        scratch_shapes=[pltpu.VMEM((2, heads, tq, tq), F32)],
        compiler_params=_cp(("arbitrary", "arbitrary", "arbitrary")),
    )(q4, k4, v4)


def _sb_attn_kernel(q_ref, k_ref, v_ref, u_ref, o_ref, *, tq, heads):
    qi = pl.program_id(2)
    u = u_ref[...]

    def head_step(h, kt, carry, diag):
        run, acc = carry
        k0 = pl.multiple_of(kt * tq, tq)
        k = k_ref[h, pl.ds(k0, tq), :]
        v = v_ref[h, pl.ds(k0, tq), :]
        z = _dot_nt(q_ref[h], k)
        l1m = -(jnp.maximum(z, 0.0) + jnp.log(1.0 + jnp.exp(-jnp.abs(z))))
        if diag:
            row = lax.broadcasted_iota(jnp.int32, (tq, tq), 0)
            col = lax.broadcasted_iota(jnp.int32, (tq, tq), 1)
            strict = col < row
            l1m_m = jnp.where(strict, l1m, 0.0)
        else:
            l1m_m = l1m
        hi = l1m_m.astype(BF16)
        lo = (l1m_m - hi.astype(F32)).astype(BF16)
        between = _dot(hi, u) + _dot(lo, u)
        a = jnp.exp(z + l1m + between + run)
        if diag:
            a = jnp.where(strict, a, 0.0)
        acc = acc + _dot(a.astype(BF16), v)
        run = run + between[:, 0:1] + l1m_m[:, 0:1]
        return run, acc

    def step(kt, carries, diag):
        return tuple(head_step(h, kt, carries[h], diag) for h in range(heads))

    init = (jnp.zeros((tq, 1), F32), jnp.zeros((tq, 128), F32))
    carries = step(qi, (init,) * heads, True)

    def more(c):
        j, carries = c
        top = carries[0][0]
        for run, _ in carries[1:]:
            top = jnp.maximum(top, run)
        return (j < qi) & (jnp.max(top) > SB_UNDERFLOW_LOG)

    def body(c):
        j, carries = c
        return j + 1, step(qi - 1 - j, carries, False)

    _, carries = lax.while_loop(more, body, (jnp.int32(0), carries))
    for h, (_, acc) in enumerate(carries):
        o_ref[:, h * HEAD_DIM:(h + 1) * HEAD_DIM] = acc.astype(o_ref.dtype)


def sb_attn(hm, u, *, head0, batch, seq, tq, heads):
    nq = seq // tq
    assert head0 % heads == 0 and SB_HEADS % heads == 0
    hm4 = hm.reshape(hm.shape[0], batch, seq, HEAD_DIM)
    blk0 = head0 // heads
    per_part = SB_HEADS // heads
    return pl.pallas_call(
        functools.partial(_sb_attn_kernel, tq=tq, heads=heads),
        out_shape=jax.ShapeDtypeStruct((batch * seq, SB_HEADS * HEAD_DIM), BF16),
        grid=(per_part, batch, nq),
        in_specs=[
            pl.BlockSpec((heads, None, tq, HEAD_DIM), lambda h, b, i: (blk0 + h, b, i, 0)),
            pl.BlockSpec((heads, None, seq, HEAD_DIM), lambda h, b, i: (blk0 + per_part + h, b, 0, 0)),
            pl.BlockSpec((heads, None, seq, HEAD_DIM), lambda h, b, i: (blk0 + 2 * per_part + h, b, 0, 0)),
            pl.BlockSpec((tq, tq), lambda h, b, i: (0, 0)),
        ],
        out_specs=pl.BlockSpec((tq, heads * HEAD_DIM), lambda h, b, i: (b * nq + i, h)),
        compiler_params=_cp(("arbitrary", "arbitrary", "arbitrary")),
    )(hm4, hm4, hm4, u)


def _nsa_cmp_kernel(c_ref, w1_ref, pe_ref, w2_ref, o_ref, *, nc):
    c = c_ref[...]
    half = NSA_CMP_STRIDE * HEAD_DIM
    a1 = _dot(c, w1_ref[0:half, :])
    a2 = _dot(c, w1_ref[half:2 * half, :])
    pc = _dot(pe_ref[...], w1_ref[...])[0:1, :]
    pre = a1 + pltpu.roll(a2, nc - 1, 0) + pc
    act = 0.5 * pre * (1.0 + jnp.tanh(0.7978845608028654 * (pre + 0.044715 * (pre * pre * pre))))
    o_ref[...] = _dot(act.astype(BF16), w2_ref[...]).astype(BF16)


def nsa_compress(cmp_heads, w1, pe, w2, *, batch, seq):
    nc = seq // NSA_CMP_STRIDE
    head0 = 0
    hm4 = cmp_heads.reshape(4, batch, nc, NSA_CMP_STRIDE * HEAD_DIM)
    return pl.pallas_call(
        functools.partial(_nsa_cmp_kernel, nc=nc),
        out_shape=jax.ShapeDtypeStruct((4, batch, nc, HEAD_DIM), BF16),
        grid=(4, batch),
        in_specs=[
            pl.BlockSpec((None, None, nc, NSA_CMP_STRIDE * HEAD_DIM), lambda c, b: (head0 + c, b, 0, 0)),
            pl.BlockSpec((None, NSA_CMP_LEN * HEAD_DIM, HEAD_DIM), lambda c, b: (c // 2, 0, 0)),
            pl.BlockSpec((None, 8, NSA_CMP_LEN * HEAD_DIM), lambda c, b: (c // 2, 0, 0)),
            pl.BlockSpec((None, HEAD_DIM, HEAD_DIM), lambda c, b: (c // 2, 0, 0)),
        ],
        out_specs=pl.BlockSpec((None, None, nc, HEAD_DIM), lambda c, b: (c, b, 0, 0)),
        compiler_params=_cp(("arbitrary", "arbitrary")),
    )(hm4, w1, pe, w2)


def _nsa_attn_kernel(q_ref, qa_ref, kc_ref, vc_ref, ks_ref, vs_ref, kw_ref, vw_ref, g_ref, ovt_ref, e_ref,
                     kpos_ref, cpos_ref, o_ref, ss_ref, *, tq, tk, seq, n_sel, n_top):
    qi = pl.program_id(1)
    t0 = qi * tq
    rows = NSA_HG * tq
    nc = seq // NSA_CMP_STRIDE
    groups = range(NSA_GROUPS)

    rid = lax.broadcasted_iota(jnp.int32, (rows, 1), 0)
    trow = t0 + lax.bitwise_and(rid, tq - 1)

    def masked_softmax(s, valid):
        sm = jnp.where(valid, s, NEG)
        m = jnp.max(sm, axis=1, keepdims=True)
        e = jnp.where(valid, jnp.exp(sm - m), 0.0)
        d = jnp.sum(e, axis=1, keepdims=True)
        return e * (1.0 / jnp.where(d > 0.0, d, 1.0))

    n_i = lax.broadcasted_iota(jnp.int32, (1, nc), 1)
    end = n_i * NSA_CMP_STRIDE + (NSA_CMP_LEN - 1)
    cur = lax.shift_right_logical(t0 + lax.broadcasted_iota(jnp.int32, (1, tq), 1),
                                  int(np.log2(NSA_SEL_LEN)))
    blk = lax.broadcasted_iota(jnp.int32, (n_sel, tq), 0)
    forced = (blk == 0) | (blk == cur) | (blk == cur - 1)
    sub = lax.broadcasted_iota(jnp.int32, (8, tq), 0)
    eye = (lax.broadcasted_iota(jnp.int32, (n_sel, n_sel), 0)
           == lax.broadcasted_iota(jnp.int32, (n_sel, n_sel), 1)).astype(F32).astype(BF16)
    ovt = ovt_ref[...]

    def front(g):
        q = jnp.concatenate([q_ref[NSA_HG * g:NSA_HG * (g + 1)].reshape(rows, HEAD_DIM), qa_ref[g]], axis=1)

        s_c = _dot_nt(q, jnp.concatenate([kc_ref[g], cpos_ref[...]], axis=1))
        p_c = masked_softmax(s_c, end <= trow)
        o_c = _dot(p_c.astype(BF16), vc_ref[g])

        psum = p_c[0:tq] + p_c[tq:2 * tq] + p_c[2 * tq:3 * tq] + p_c[3 * tq:4 * tq]
        p_hi = psum.astype(BF16)
        p_lo = (psum - p_hi.astype(F32)).astype(BF16)
        imp = _dot_nt(ovt, p_hi) + _dot_nt(ovt, p_lo)
        key = jnp.where(blk > cur, -BIG, jnp.where(forced, BIG, imp))
        chunks = [key[8 * r:8 * r + 8, :] for r in range(n_sel // 8)]
        ranks = [jnp.zeros((8, tq), F32) for _ in chunks]
        for i in range(n_sel):
            vi = key[i:i + 1, :]
            for r, kc in enumerate(chunks):
                gt = jnp.where(vi > kc, 1.0, 0.0)
                if r < i // 8:
                    win = gt
                else:
                    ge = jnp.where(vi >= kc, 1.0, 0.0)
                    win = ge if r > i // 8 else jnp.where(sub > i % 8, ge, gt)
                ranks[r] = ranks[r] + win
        rank = jnp.concatenate(ranks, axis=0)
        selm_t = jnp.where((rank < float(n_top)) & (blk <= cur), 1.0, 0.0).astype(BF16)
        selm = lax.dot_general(selm_t, eye, (((0,), (0,)), ((), ())), preferred_element_type=F32).astype(BF16)
        return q, selm, o_c

    fronts = [front(g) for g in groups]

    def sel_fill(slot, kt):
        k0 = pl.multiple_of(kt * tk, tk)
        kp = kpos_ref[pl.ds(k0, tk), :]
        for g in groups:
            q, selm, _ = fronts[g]
            kk = jnp.concatenate([ks_ref[g, pl.ds(k0, tk), :], kp], axis=1)
            mex = _dot(selm, e_ref[kt])
            bias = (mex - 1.0) * BIG
            ss_ref[slot, g] = _dot_nt(q, kk) + jnp.concatenate([bias] * NSA_HG, axis=0)

    def sel_drain_group(g, slot, kt, carry, diag):
        m, l, acc = carry
        k0 = pl.multiple_of(kt * tk, tk)
        vv = vs_ref[g, pl.ds(k0, tk), :]
        sm = ss_ref[slot, g]
        if diag:
            spos = k0 + lax.broadcasted_iota(jnp.int32, (1, tk), 1)
            sm = jnp.where(spos <= trow, sm, NEG)
        m_new = jnp.maximum(m, jnp.max(sm, axis=1, keepdims=True))
        alpha = jnp.exp(m - m_new)
        p = jnp.exp((sm - m_new).astype(BF16))
        l = alpha * l + jnp.sum(p.astype(F32), axis=1, keepdims=True)
        acc = alpha * acc + _dot(p, vv)
        return m_new, l, acc

    def sel_drain(slot, kt, carries, diag):
        return tuple(sel_drain_group(g, slot, kt, carries[g], diag) for g in groups)

    def sel_body(j, carries):
        kt = 2 * j
        sel_fill(1, kt + 1)
        carries = sel_drain(0, kt, carries, False)
        sel_fill(0, kt + 2)
        return sel_drain(1, kt + 1, carries, False)

    kt_last = t0 // tk
    init = (jnp.full((rows, 1), NEG, F32), jnp.zeros((rows, 1), F32), jnp.zeros((rows, HEAD_DIM), F32))
    sel_fill(0, 0)
    carries = lax.fori_loop(0, kt_last // 2, sel_body, (init,) * NSA_GROUPS)

    def even_tail(carries):
        return sel_drain(0, kt_last, carries, True)

    def odd_tail(carries):
        sel_fill(1, kt_last)
        return sel_drain(1, kt_last, sel_drain(0, kt_last - 1, carries, False), True)

    carries = lax.cond(lax.rem(kt_last, 2) == 1, odd_tail, even_tail, carries)

    wk = NSA_WINDOW + tq
    ks0 = pl.multiple_of(jnp.maximum(t0 - NSA_WINDOW, 0), tq)
    kpw = kpos_ref[pl.ds(ks0, wk), :]
    wpos = ks0 + lax.broadcasted_iota(jnp.int32, (1, wk), 1)
    dw = trow - wpos
    in_window = pltpu.bitcast(dw, jnp.uint32) < jnp.uint32(NSA_WINDOW)
    gt = g_ref[...]
    for g in groups:
        q, _, o_c = fronts[g]
        _, l_s, acc_s = carries[g]
        o_s = acc_s * (1.0 / l_s)
        kw = jnp.concatenate([kw_ref[g, pl.ds(ks0, wk), :], kpw], axis=1)
        vw = vw_ref[g, pl.ds(ks0, wk), :]
        sm_w = jnp.where(in_window, _dot_nt(q, kw), NEG)
        p_w = jnp.exp((sm_w - jnp.max(sm_w, axis=1, keepdims=True)).astype(BF16))
        o_w = _dot(p_w, vw) * (1.0 / jnp.sum(p_w.astype(F32), axis=1, keepdims=True))
        for hg in range(NSA_HG):
            sl = slice(hg * tq, (hg + 1) * tq)
            c0 = 128 * g + 3 * hg
            o = (gt[:, c0:c0 + 1] * o_c[sl] + gt[:, c0 + 1:c0 + 2] * o_s[sl] + gt[:, c0 + 2:c0 + 3] * o_w[sl])
            h = NSA_HG * g + hg
            o_ref[:, h * HEAD_DIM:(h + 1) * HEAD_DIM] = o.astype(o_ref.dtype)


def nsa_attn(hm, cmp, gates, consts, *, q_head0, kv_head0, batch, seq, tq, tk):
    ov, e, qa, kpos, cpos = consts
    nq = seq // tq
    nc = seq // NSA_CMP_STRIDE
    n_sel = seq // NSA_SEL_LEN
    n_top = min(NSA_TOPK, n_sel)
    assert tk % tq == 0 and seq % tk == 0 and seq >= NSA_WINDOW + tq and NSA_WINDOW % tq == 0
    assert q_head0 % NSA_HEADS == 0 and kv_head0 % NSA_GROUPS == 0
    hm4 = hm.reshape(hm.shape[0], batch, seq, HEAD_DIM)
    kv_spec = lambda off: pl.BlockSpec((NSA_GROUPS, None, seq, HEAD_DIM),
                                       lambda b, i: ((kv_head0 + off) // NSA_GROUPS, b, 0, 0))
    return pl.pallas_call(
        functools.partial(_nsa_attn_kernel, tq=tq, tk=tk, seq=seq, n_sel=n_sel, n_top=n_top),
        out_shape=jax.ShapeDtypeStruct((batch * seq, NSA_HEADS * HEAD_DIM), BF16),
        grid=(batch, nq),
        in_specs=[
            pl.BlockSpec((NSA_HEADS, None, tq, HEAD_DIM), lambda b, i: (q_head0 // NSA_HEADS, b, i, 0)),
            pl.BlockSpec((NSA_GROUPS, NSA_HG * tq, 128), lambda b, i: (0, 0, 0)),
            pl.BlockSpec((NSA_GROUPS, None, nc, HEAD_DIM), lambda b, i: (0, b, 0, 0)),
            pl.BlockSpec((NSA_GROUPS, None, nc, HEAD_DIM), lambda b, i: (1, b, 0, 0)),
            kv_spec(4), kv_spec(6), kv_spec(8), kv_spec(10),
            pl.BlockSpec((tq, NSA_GROUPS * 128), lambda b, i: (b * nq + i, 0)),
            pl.BlockSpec((n_sel, nc), lambda b, i: (0, 0)),
            pl.BlockSpec((seq // tk, n_sel, tk), lambda b, i: (0, 0, 0)),
            pl.BlockSpec((seq, 128), lambda b, i: (0, 0)),
            pl.BlockSpec((nc, 128), lambda b, i: (0, 0)),
        ],
        out_specs=pl.BlockSpec((tq, NSA_HEADS * HEAD_DIM), lambda b, i: (b * nq + i, 0)),
        scratch_shapes=[pltpu.VMEM((2, NSA_GROUPS, NSA_HG * tq, tk), F32)],
        compiler_params=_cp(("arbitrary", "arbitrary")),
    )(hm4, qa, cmp, cmp, hm4, hm4, hm4, hm4, gates, ov, e, kpos, cpos)


def _merge_kernel(x_ref, oa_ref, ob_ref, oc_ref, wm_ref, wb_ref, bm_ref, y_ref, xb_ref):
    @pl.when(pl.program_id(1) == 0)
    def _():
        xb_ref[...] = x_ref[...].astype(BF16)

    xb = xb_ref[...]
    acc = None
    for br, o_ref in enumerate((oa_ref, ob_ref, oc_ref)):
        gate = jax.nn.sigmoid(_dot(xb, wm_ref[br]) + bm_ref[br])
        term = gate * _dot(o_ref[...], wb_ref[br])
        acc = term if acc is None else acc + term
    y_ref[...] = acc.astype(y_ref.dtype)


def merge_branches(x, o_a, o_b, o_c, wm, wb, bm, *, tm, tn):
    t = x.shape[0]
    o_spec = pl.BlockSpec((tm, BRANCH_WIDTH), lambda i, j: (i, 0))
    return pl.pallas_call(
        _merge_kernel,
        out_shape=jax.ShapeDtypeStruct((t, D_MODEL), BF16),
        grid=(t // tm, D_MODEL // tn),
        in_specs=[
            pl.BlockSpec((tm, D_MODEL), lambda i, j: (i, 0)),
            o_spec, o_spec, o_spec,
            pl.BlockSpec((N_BRANCH, D_MODEL, tn), lambda i, j: (0, 0, j)),
            pl.BlockSpec((N_BRANCH, BRANCH_WIDTH, tn), lambda i, j: (0, 0, j)),
            pl.BlockSpec((N_BRANCH, 1, tn), lambda i, j: (0, 0, j)),
        ],
        out_specs=pl.BlockSpec((tm, tn), lambda i, j: (i, j)),
        scratch_shapes=[pltpu.VMEM((tm, D_MODEL), BF16)],
        compiler_params=_cp(("arbitrary", "arbitrary")),
    )(x, o_a, o_b, o_c, wm, wb, bm)


def _out_ln_kernel(y_ref, w_ref, x_ref, g_ref, b_ref, o_ref, *, alpha):
    half = y_ref.shape[0] // 2
    for r in range(2):
        rs = slice(r * half, (r + 1) * half)
        h = _dot(y_ref[rs, :], w_ref[...])
        o_ref[rs, :] = _layer_norm(alpha * x_ref[rs, :] + h, g_ref[...], b_ref[...])


def out_ln(y, w, x, g, b, *, alpha, tm):
    t = x.shape[0]
    return pl.pallas_call(
        functools.partial(_out_ln_kernel, alpha=alpha),
        out_shape=jax.ShapeDtypeStruct((t, D_MODEL), F32),
        grid=(t // tm,),
        in_specs=[
            pl.BlockSpec((tm, D_MODEL), lambda i: (i, 0)),
            pl.BlockSpec((D_MODEL, D_MODEL), lambda i: (0, 0)),
            pl.BlockSpec((tm, D_MODEL), lambda i: (i, 0)),
            pl.BlockSpec((1, D_MODEL), lambda i: (0, 0)),
            pl.BlockSpec((1, D_MODEL), lambda i: (0, 0)),
        ],
        out_specs=pl.BlockSpec((tm, D_MODEL), lambda i: (i, 0)),
        compiler_params=_cp(("arbitrary",)),
    )(y, w, x, g, b)


def _mem_attn_kernel(x_ref, wq_ref, k_ref, v_ref, wo_ref, g_ref, b_ref, o_ref, *, alpha):
    x = x_ref[...]
    q = _dot(x.astype(BF16), wq_ref[...]) * (HEAD_DIM ** -0.5)
    outs = []
    for h in range(MEM_HEADS):
        qh = q[:, h * HEAD_DIM:(h + 1) * HEAD_DIM].astype(BF16)
        s = _dot_nt(qh, k_ref[h])
        m = jnp.max(s, axis=1, keepdims=True)
        e = jnp.exp(s - m)
        p = e * (1.0 / jnp.sum(e, axis=1, keepdims=True))
        outs.append(_dot(p.astype(BF16), v_ref[h]).astype(BF16))
    o = jnp.concatenate(outs, axis=1)
    h_out = _dot(o, wo_ref[...])
    o_ref[...] = _layer_norm(alpha * x + h_out, g_ref[...], b_ref[...])


def mem_attn_ln(x, wq, kv, wo, g, b, *, alpha, seq, mem_len, tm):
    t = x.shape[0]
    per_b = seq // tm
    kv4 = kv.reshape(2 * MEM_HEADS, t // seq, mem_len, HEAD_DIM)
    width = MEM_HEADS * HEAD_DIM
    return pl.pallas_call(
        functools.partial(_mem_attn_kernel, alpha=alpha),
        out_shape=jax.ShapeDtypeStruct((t, D_MODEL), F32),
        grid=(t // tm,),
        in_specs=[
            pl.BlockSpec((tm, D_MODEL), lambda i: (i, 0)),
            pl.BlockSpec((D_MODEL, width), lambda i: (0, 0)),
            pl.BlockSpec((MEM_HEADS, None, mem_len, HEAD_DIM), lambda i: (0, i // per_b, 0, 0)),
            pl.BlockSpec((MEM_HEADS, None, mem_len, HEAD_DIM), lambda i: (1, i // per_b, 0, 0)),
            pl.BlockSpec((width, D_MODEL), lambda i: (0, 0)),
            pl.BlockSpec((1, D_MODEL), lambda i: (0, 0)),
            pl.BlockSpec((1, D_MODEL), lambda i: (0, 0)),
        ],
        out_specs=pl.BlockSpec((tm, D_MODEL), lambda i: (i, 0)),
        compiler_params=_cp(("arbitrary",)),
    )(x, wq, kv4, kv4, wo, g, b)


def _router_kernel(x_ref, w_ref, b_ref, tri_ref, route_ref, xp_ref, count_ref):
    @pl.when(pl.program_id(0) == 0)
    def _():
        count_ref[...] = jnp.zeros_like(count_ref)

    x = x_ref[...]
    xp_ref[...] = _pack_bf16_pairs(x)
    xh = x.astype(BF16)
    xl = (x - xh.astype(F32)).astype(BF16)
    logits = _dot(xh, w_ref[0]) + _dot(xh, w_ref[1]) + _dot(xl, w_ref[0]) + b_ref[...]
    tm = logits.shape[0]
    lane = lax.broadcasted_iota(jnp.int32, (tm, N_EXPERTS), 1)
    work = logits
    hots, vals, idxs = [], [], []
    for _ in range(TOP_K):
        m = jnp.max(work, axis=1, keepdims=True)
        idx = jnp.min(jnp.where(work == m, lane, N_EXPERTS), axis=1, keepdims=True)
        hot = lane == idx
        hots.append(hot)
        vals.append(m)
        idxs.append(idx)
        work = jnp.where(hot, -jnp.inf, work)
    es = [jnp.exp(v - vals[0]) for v in vals]
    inv = 1.0 / (es[0] + es[1] + es[2] + es[3])
    mask = jnp.zeros((tm, N_EXPERTS), F32)
    for hot in hots:
        mask = mask + jnp.where(hot, 1.0, 0.0)
    before = count_ref[...] + _dot(tri_ref[...], mask.astype(BF16))
    count_ref[...] = count_ref[...] + jnp.sum(mask, axis=0, keepdims=True)
    col = lax.broadcasted_iota(jnp.int32, (tm, 128), 1)
    route = jnp.zeros((tm, 128), F32)
    for k in range(TOP_K):
        rank = jnp.sum(jnp.where(hots[k], before, 0.0), axis=1, keepdims=True)
        route = (route + jnp.where(col == k, idxs[k].astype(F32), 0.0)
                 + jnp.where(col == TOP_K + k, es[k] * inv, 0.0) + jnp.where(col == 2 * TOP_K + k, rank, 0.0))
    route_ref[...] = route


def router(x, w_hl, b, tri, *, tm):
    t = x.shape[0]
    return pl.pallas_call(
        _router_kernel,
        out_shape=(jax.ShapeDtypeStruct((t, 128), F32),
                   jax.ShapeDtypeStruct((t, D_MODEL // 2), jnp.uint32),
                   jax.ShapeDtypeStruct((1, N_EXPERTS), F32)),
        grid=(t // tm,),
        in_specs=[
            pl.BlockSpec((tm, D_MODEL), lambda i: (i, 0)),
            pl.BlockSpec((2, D_MODEL, N_EXPERTS), lambda i: (0, 0, 0)),
            pl.BlockSpec((1, N_EXPERTS), lambda i: (0, 0)),
            pl.BlockSpec((tm, tm), lambda i: (0, 0)),
        ],
        out_specs=(pl.BlockSpec((tm, 128), lambda i: (i, 0)),
                   pl.BlockSpec((tm, D_MODEL // 2), lambda i: (i, 0)),
                   pl.BlockSpec((1, N_EXPERTS), lambda i: (0, 0))),
        compiler_params=_cp(("arbitrary",)),
    )(x, w_hl, b, tri)


def sc_gather_rows(table, idx, *, chunk):
    n = idx.shape[0]
    d = table.shape[1]
    workers = SC_CORES_V7X * SC_SUBCORES_V7X
    per_w = n // workers
    assert n % (workers * chunk) == 0 and chunk % 8 == 0 and chunk <= 128
    mesh = plsc.VectorSubcoreMesh(core_axis_name="c", subcore_axis_name="s")

    @functools.partial(
        pl.kernel, mesh=mesh,
        out_type=jax.ShapeDtypeStruct((n, d), table.dtype),
        scratch_types=[pltpu.VMEM((chunk,), jnp.int32), pltpu.VMEM((chunk, d), table.dtype),
                       pltpu.SemaphoreType.DMA],
    )
    def gather(table_hbm, idx_hbm, out_hbm, idx_v, rows_v, sem):
        wid = lax.axis_index("s") * SC_CORES_V7X + lax.axis_index("c")
        base = wid * per_w

        @pl.loop(0, per_w // chunk)
        def _(j):
            off = pl.multiple_of(base + j * chunk, 8)
            pltpu.sync_copy(idx_hbm.at[pl.ds(off, chunk)], idx_v)
            pltpu.async_copy(table_hbm.at[idx_v], rows_v, sem).wait()
            pltpu.sync_copy(rows_v, out_hbm.at[pl.ds(off, chunk)])

    return gather(table, idx)


def sc_scatter_rows(rows, idx, n_out, *, copies, chunk):
    t, d = rows.shape
    workers = SC_CORES_V7X * SC_SUBCORES_V7X
    per_w = t // workers
    assert idx.shape == (copies * t,) and t % (workers * chunk) == 0 and chunk % 8 == 0 and chunk <= 128
    mesh = plsc.VectorSubcoreMesh(core_axis_name="c", subcore_axis_name="s")

    @functools.partial(
        pl.kernel, mesh=mesh,
        out_type=jax.ShapeDtypeStruct((n_out, d), rows.dtype),
        scratch_types=[pltpu.VMEM((chunk,), jnp.int32), pltpu.VMEM((chunk, d), rows.dtype)],
    )
    def scatter(rows_hbm, idx_hbm, out_hbm, idx_v, rows_v):
        wid = lax.axis_index("s") * SC_CORES_V7X + lax.axis_index("c")
        base = wid * per_w

        @pl.loop(0, per_w // chunk)
        def _(j):
            off = pl.multiple_of(base + j * chunk, 8)
            pltpu.sync_copy(rows_hbm.at[pl.ds(off, chunk)], rows_v)
            for k in range(copies):
                pltpu.sync_copy(idx_hbm.at[pl.ds(pl.multiple_of(k * t + off, 8), chunk)], idx_v)
                pltpu.sync_copy(rows_v, out_hbm.at[idx_v])

    return scatter(rows, idx)


def _experts_kernel(be_ref, x_ref, wg_ref, bg_ref, wu_ref, bu_ref, wd_ref, bd_ref, y_ref,
                    wgb_ref, wub_ref, wdb_ref):
    i = pl.program_id(0)
    prev = be_ref[jnp.maximum(i - 1, 0)]
    n_used = be_ref[pl.num_programs(0)]

    @pl.when((i < n_used) & ((i == 0) | (be_ref[i] != prev)))
    def _():
        wgb_ref[...] = wg_ref[...].astype(BF16)
        wub_ref[...] = wu_ref[...].astype(BF16)
        wdb_ref[...] = wd_ref[...].astype(BF16)

    @pl.when(i < n_used)
    def _():
        x_lo, x_hi = _unpack_bf16_pairs(x_ref[...])
        xb = jnp.concatenate([x_lo.astype(BF16), x_hi.astype(BF16)], axis=1)
        g = jnp.minimum(_dot(xb, wgb_ref[...]) + bg_ref[...], SWIGLU_LIMIT)
        u = jnp.clip(_dot(xb, wub_ref[...]) + bu_ref[...], -SWIGLU_LIMIT, SWIGLU_LIMIT)
        hdn = (u + 1.0) * (g * jax.nn.sigmoid(SWIGLU_ALPHA * g))
        y_ref[...] = _pack_bf16_pairs(_dot(hdn.astype(BF16), wdb_ref[...]) + bd_ref[...])


def experts(x_rows, blk_e, wg, bg, wu, bu, wd, bd, *, layer, bm):
    n_rows, dp = x_rows.shape
    d, f = wg.shape[2], wg.shape[3]
    w_spec = lambda shape: pl.BlockSpec((None, None) + shape, lambda i, be: (layer, be[i], 0, 0))
    grid_spec = pltpu.PrefetchScalarGridSpec(
        num_scalar_prefetch=1,
        grid=(n_rows // bm,),
        in_specs=[
            pl.BlockSpec((bm, dp), lambda i, be: (i, 0)),
            w_spec((d, f)), w_spec((1, f)), w_spec((d, f)), w_spec((1, f)), w_spec((f, d)), w_spec((1, d)),
        ],
        out_specs=pl.BlockSpec((bm, dp), lambda i, be: (i, 0)),
        scratch_shapes=[pltpu.VMEM((d, f), BF16), pltpu.VMEM((d, f), BF16), pltpu.VMEM((f, d), BF16)],
    )
    return pl.pallas_call(
        _experts_kernel,
        out_shape=jax.ShapeDtypeStruct((n_rows, dp), jnp.uint32),
        grid_spec=grid_spec,
        compiler_params=_cp(("arbitrary",)),
    )(blk_e, x_rows, wg, bg, wu, bu, wd, bd)


def _moe_ln_kernel(x_ref, y_ref, w_ref, g_ref, b_ref, o_ref, *, alpha):
    w = w_ref[...]
    y_lo = y_hi = None
    for k in range(TOP_K):
        lo, hi = _unpack_bf16_pairs(y_ref[k])
        wk = w[:, TOP_K + k:TOP_K + k + 1]
        y_lo = wk * lo if y_lo is None else y_lo + wk * lo
        y_hi = wk * hi if y_hi is None else y_hi + wk * hi
    y = jnp.concatenate([y_lo, y_hi], axis=1)
    o_ref[...] = _layer_norm(alpha * x_ref[...] + y, g_ref[...], b_ref[...])


def moe_ln(x, y4, w4p, g, b, *, alpha, tm):
    t = x.shape[0]
    return pl.pallas_call(
        functools.partial(_moe_ln_kernel, alpha=alpha),
        out_shape=jax.ShapeDtypeStruct((t, D_MODEL), F32),
        grid=(t // tm,),
        in_specs=[
            pl.BlockSpec((tm, D_MODEL), lambda i: (i, 0)),
            pl.BlockSpec((TOP_K, tm, D_MODEL // 2), lambda i: (0, i, 0)),
            pl.BlockSpec((tm, 128), lambda i: (i, 0)),
            pl.BlockSpec((1, D_MODEL), lambda i: (0, 0)),
            pl.BlockSpec((1, D_MODEL), lambda i: (0, 0)),
        ],
        out_specs=pl.BlockSpec((tm, D_MODEL), lambda i: (i, 0)),
        compiler_params=_cp(("arbitrary",)),
    )(x, y4, w4p, g, b)


def _rope_tables(seq):
    inv = np.asarray(ROPE_THETA ** (-np.arange(0, MLA_ROPE, 2) / MLA_ROPE), np.float32)
    ang = jnp.arange(seq, dtype=F32)[:, None] * jnp.asarray(inv)[None, :]
    cos, sin = jnp.cos(ang), jnp.sin(ang)
    zeros = jnp.zeros((seq, 128 - MLA_ROPE), F32)
    return (jnp.concatenate([cos, cos, zeros], axis=1), jnp.concatenate([-sin, sin, zeros], axis=1))


def _nsa_constants(seq, tq, tk):
    nc = seq // NSA_CMP_STRIDE
    qa = np.zeros((NSA_GROUPS, NSA_HG * tq, 128), np.float32)
    for g in range(NSA_GROUPS):
        for hg in range(NSA_HG):
            slope = 2.0 ** (-8.0 * (g * NSA_HG + hg + 1) / NSA_HEADS)
            qa[g, hg * tq:(hg + 1) * tq, 0] = slope * NSA_SEL_LEN
            qa[g, hg * tq:(hg + 1) * tq, 1] = slope
            qa[g, hg * tq:(hg + 1) * tq, 2] = slope * NSA_CMP_STRIDE
            qa[g, hg * tq:(hg + 1) * tq, 3] = slope * (NSA_CMP_LEN - 1) / 2.0
    kpos = np.zeros((seq, 128), np.float32)
    kpos[:, 0] = np.arange(seq) // NSA_SEL_LEN
    kpos[:, 1] = np.arange(seq) % NSA_SEL_LEN
    cpos = np.zeros((nc, 128), np.float32)
    cpos[:, 2] = np.arange(nc)
    cpos[:, 3] = 1.0
    for arr in (qa, kpos, cpos):
        assert np.array_equal(arr.astype(BF16).astype(np.float32), arr)
    n_cmp = (seq - NSA_CMP_LEN) // NSA_CMP_STRIDE + 1
    n_sel = seq // NSA_SEL_LEN
    cs = np.arange(nc) * NSA_CMP_STRIDE
    ss = np.arange(n_sel) * NSA_SEL_LEN
    ov = np.clip(np.minimum(cs[:, None] + NSA_CMP_LEN, ss[None, :] + NSA_SEL_LEN)
                 - np.maximum(cs[:, None], ss[None, :]), 0, None) / NSA_CMP_LEN
    ov[n_cmp:] = 0.0
    e = (np.arange(seq)[None, :] // NSA_SEL_LEN == np.arange(n_sel)[:, None]).astype(np.float32)
    e = e.reshape(n_sel, seq // tk, tk).transpose(1, 0, 2)
    return tuple(jnp.asarray(a, BF16) for a in (ov.T, e, qa, kpos, cpos))


def _pad_cols(w, width):
    return jnp.pad(w, ((0, 0), (0, width - w.shape[1])))


def _swap_halves(w):
    half = w.shape[1] // 2
    return jnp.concatenate([w[:, half:], w[:, :half]], axis=1)


def _layer_weights(w_in, w_q_up, w_kv_up):
    kr = w_in[:, OFF_KR:OFF_NSA_Q]
    gate = w_in[:, OFF_NSA_GATE:OFF_SB]
    per_g = NSA_HG * 3
    w_mla = jnp.concatenate([
        w_in[:, OFF_CQ:OFF_KR],
        _pad_cols(kr, 128), _pad_cols(_swap_halves(kr), 128),
        _pad_cols(gate[:, :per_g], 128), _pad_cols(gate[:, per_g:], 128)], axis=1).astype(BF16)
    wq = w_q_up.reshape(MLA_Q_RANK, MLA_HEADS, MLA_NOPE + MLA_ROPE)
    rope = wq[:, :, MLA_NOPE:]
    rope_sw = jnp.concatenate([rope[:, :, MLA_ROPE // 2:], rope[:, :, :MLA_ROPE // 2]], axis=2)
    pad = ((0, 0), (0, 0), (0, 128 - MLA_ROPE))
    wq3 = jnp.concatenate([
        wq[:, :, :MLA_NOPE].reshape(MLA_Q_RANK, -1),
        jnp.pad(rope, pad).reshape(MLA_Q_RANK, -1),
        jnp.pad(rope_sw, pad).reshape(MLA_Q_RANK, -1)], axis=1).astype(BF16)
    wkv = w_kv_up.reshape(MLA_KV_RANK, MLA_HEADS, 2, 128)
    wkv = jnp.concatenate([wkv[:, :, 0].reshape(MLA_KV_RANK, -1),
                           wkv[:, :, 1].reshape(MLA_KV_RANK, -1)], axis=1).astype(BF16)
    w_heads = jnp.concatenate([w_in[:, OFF_NSA_Q:OFF_NSA_KV], w_in[:, OFF_SB:OFF_MERGE],
                               w_in[:, OFF_NSA_KV:OFF_NSA_GATE]], axis=1).astype(BF16)
    wm = w_in[:, OFF_MERGE:].reshape(D_MODEL, N_BRANCH, D_MODEL).transpose(1, 0, 2).astype(BF16)
    return w_mla, wq3, wkv, w_heads, wm


def _forward(x, mem, w_in, mla_q_norm, mla_w_q_up, mla_kv_norm, mla_w_kv_up,
             nsa_pe_k, nsa_pe_v, nsa_w1_k, nsa_w1_v, nsa_w2_k, nsa_w2_v,
             w_branch, b_merge, w_out, ln_mix_g, ln_mix_b,
             mem_w_q, mem_w_k, mem_w_v, mem_w_o, ln_mem_g, ln_mem_b,
             moe_w_router, moe_b_router, moe_w_gate, moe_b_gate, moe_w_up, moe_b_up,
             moe_w_down, moe_b_down, ln_moe_g, ln_moe_b):
    batch, seq, _ = x.shape
    mem_len = mem.shape[1]
    depth = w_in.shape[0]
    t = batch * seq
    alpha = float((2 * depth) ** 0.25)
    bm = MOE_BLOCK_ROWS
    n_rows = t * TOP_K + N_EXPERTS * bm
    n_blocks = n_rows // bm

    tm_in = min(512, seq)
    tq_mla = min(512, seq)
    tq_sb = 256
    tq_nsa, tk_nsa = 128, 512
    tm_ln = 256
    sc_chunk = 64

    cos128, sin128 = _rope_tables(seq)
    nsa_consts = _nsa_constants(seq, tq_nsa, tk_nsa)
    u_sb = jnp.asarray(np.arange(tq_sb)[:, None] > np.arange(tq_sb)[None, :], BF16)
    tri_router = jnp.asarray(np.arange(tm_in)[:, None] > np.arange(tm_in)[None, :], BF16)
    n_qheads = NSA_HEADS
    n_kvheads = 3 * 2 * NSA_GROUPS
    head_scale = np.ones((1, (n_qheads + n_kvheads + 3 * SB_HEADS) * HEAD_DIM), np.float32)
    head_scale[:, :n_qheads * HEAD_DIM] = HEAD_DIM ** -0.5
    sb0 = n_qheads
    kv0 = sb0 + 3 * SB_HEADS
    head_scale[:, sb0 * HEAD_DIM:(sb0 + SB_HEADS) * HEAD_DIM] = HEAD_DIM ** -0.5
    head_scale = jnp.asarray(head_scale)
    ones_kv = jnp.ones((1, 2 * MEM_HEADS * HEAD_DIM), F32)

    b_gate4 = moe_b_gate.reshape(depth, N_EXPERTS, 1, D_EXPERT)
    b_up4 = moe_b_up.reshape(depth, N_EXPERTS, 1, D_EXPERT)
    b_down4 = moe_b_down.reshape(depth, N_EXPERTS, 1, D_MODEL)

    xf = x.reshape(t, D_MODEL)
    memf = mem.reshape(batch * mem_len, D_MODEL)
    row = lambda v: v.reshape(1, -1)

    for l in range(depth):
        w_mla, wq3, wkv, w_heads, wm = _layer_weights(w_in[l], mla_w_q_up[l], mla_w_kv_up[l])

        q_a, k_a, v_a, gates = mla_in(xf, w_mla, row(mla_q_norm[l]), row(mla_kv_norm[l]), wq3, wkv,
                                      cos128, sin128, seq=seq, tm=tm_ln)
        hm = proj_heads(xf, w_heads, head_scale, tm=tm_in, tn=11 * HEAD_DIM)
        o_a = mla_attn(q_a, k_a, v_a, batch=batch, seq=seq, tq=tq_mla, heads=2)
        w1 = jnp.stack([nsa_w1_k[l], nsa_w1_v[l]]).astype(BF16)
        pe = jnp.stack([nsa_pe_k[l], nsa_pe_v[l]]).reshape(2, 1, -1)
        pe = jnp.broadcast_to(pe, (2, 8, pe.shape[-1])).astype(BF16)
        w2 = jnp.stack([nsa_w2_k[l], nsa_w2_v[l]]).astype(BF16)
        cmp = nsa_compress(hm[kv0:kv0 + 4], w1, pe, w2, batch=batch, seq=seq)
        o_b = nsa_attn(hm, cmp, gates, nsa_consts, q_head0=0, kv_head0=kv0,
                       batch=batch, seq=seq, tq=tq_nsa, tk=tk_nsa)
        o_c = sb_attn(hm, u_sb, head0=sb0, batch=batch, seq=seq, tq=tq_sb, heads=4)
        y = merge_branches(xf, o_a, o_b, o_c, wm, w_branch[l].astype(BF16),
                           b_merge[l].reshape(N_BRANCH, 1, D_MODEL), tm=tm_in, tn=512)
        xf = out_ln(y, w_out[l].astype(BF16), xf, row(ln_mix_g[l]), row(ln_mix_b[l]), alpha=alpha, tm=tm_in)

        w_kv_mem = jnp.concatenate([mem_w_k[l], mem_w_v[l]], axis=1).astype(BF16)
        kv_mem = proj_heads(memf, w_kv_mem, ones_kv, tm=min(512, batch * mem_len), tn=512)
        xf = mem_attn_ln(xf, mem_w_q[l].astype(BF16), kv_mem, mem_w_o[l].astype(BF16),
                         row(ln_mem_g[l]), row(ln_mem_b[l]), alpha=alpha, seq=seq, mem_len=mem_len, tm=tm_in)

        wr = moe_w_router[l]
        wr_hi = wr.astype(BF16)
        wr_lo = (wr - wr_hi.astype(F32)).astype(BF16)
        route, x_packed, counts = router(xf, jnp.stack([wr_hi, wr_lo]), row(moe_b_router[l]),
                                         tri_router, tm=tm_in)
        padded = (counts[0].astype(jnp.int32) + bm - 1) // bm * bm
        pad_end = jnp.cumsum(padded)
        pad_start = pad_end - padded
        idx4 = route[:, 0:TOP_K].astype(jnp.int32)
        rank4 = route[:, 2 * TOP_K:3 * TOP_K].astype(jnp.int32)
        start4 = jnp.sum(jnp.where(idx4[:, :, None] == jnp.arange(N_EXPERTS, dtype=jnp.int32), pad_start, 0), axis=2)
        pos4 = start4 + rank4
        blk_row0 = jnp.arange(n_blocks, dtype=jnp.int32)[:, None] * bm
        blk_e = jnp.minimum(jnp.sum((pad_end[None, :] <= blk_row0).astype(jnp.int32), axis=1), N_EXPERTS - 1)
        blk_e = jnp.concatenate([blk_e, pad_end[-1:] // bm]).astype(jnp.int32)
        pos_kmajor = pos4.T.reshape(-1)
        x_rows = sc_scatter_rows(x_packed, pos_kmajor, n_rows, copies=TOP_K, chunk=sc_chunk)
        y_rows = experts(x_rows, blk_e, moe_w_gate, b_gate4, moe_w_up, b_up4, moe_w_down, b_down4,
                         layer=l, bm=bm)
        y4 = sc_gather_rows(y_rows, pos_kmajor, chunk=sc_chunk).reshape(TOP_K, t, D_MODEL // 2)
        xf = moe_ln(xf, y4, route, row(ln_moe_g[l]), row(ln_moe_b[l]), alpha=alpha, tm=tm_ln)

    return xf.reshape(batch, seq, D_MODEL)


def kernel(x, mem, w_in, mla_q_norm, mla_w_q_up, mla_kv_norm, mla_w_kv_up, nsa_pe_k, nsa_pe_v, nsa_w1_k, nsa_w1_v, nsa_w2_k, nsa_w2_v, w_branch, b_merge, w_out, ln_mix_g, ln_mix_b, mem_w_q, mem_w_k, mem_w_v, mem_w_o, ln_mem_g, ln_mem_b, moe_w_router, moe_b_router, moe_w_gate, moe_b_gate, moe_w_up, moe_b_up, moe_w_down, moe_b_down, ln_moe_g, ln_moe_b):
    return _forward(x, mem, w_in, mla_q_norm, mla_w_q_up, mla_kv_norm, mla_w_kv_up,
                    nsa_pe_k, nsa_pe_v, nsa_w1_k, nsa_w1_v, nsa_w2_k, nsa_w2_v,
                    w_branch, b_merge, w_out, ln_mix_g, ln_mix_b,
                    mem_w_q, mem_w_k, mem_w_v, mem_w_o, ln_mem_g, ln_mem_b,
                    moe_w_router, moe_b_router, moe_w_gate, moe_b_gate, moe_w_up, moe_b_up,
                    moe_w_down, moe_b_down, ln_moe_g, ln_moe_b)
```

```python
import functools

import numpy as np
import jax
import jax.numpy as jnp
from jax import lax
from jax.experimental import pallas as pl
from jax.experimental.pallas import tpu as pltpu
from jax.experimental.pallas import tpu_sc as plsc

F32 = jnp.float32
BF16 = jnp.bfloat16

D_MODEL = 2048
HEAD_DIM = 128
MLA_HEADS = 8
MLA_Q_RANK = 512
MLA_KV_RANK = 256
MLA_NOPE = 128
MLA_ROPE = 64
ROPE_THETA = 10000.0
NSA_HEADS = 8
NSA_GROUPS = 2
NSA_HG = NSA_HEADS // NSA_GROUPS
NSA_CMP_LEN = 32
NSA_CMP_STRIDE = 16
NSA_SEL_LEN = 64
NSA_TOPK = 16
NSA_WINDOW = 512
SB_HEADS = 8
MEM_HEADS = 4
N_EXPERTS = 32
TOP_K = 4
D_EXPERT = 512
SWIGLU_LIMIT = 7.0
SWIGLU_ALPHA = 1.702
N_BRANCH = 3
BRANCH_WIDTH = 1024
LN_EPS = 1e-5
RMS_EPS = 1e-6
NEG = -1e30
BIG = 1e30
SB_UNDERFLOW_LOG = -100.0

OFF_CQ = 0
OFF_CKV = 512
OFF_KR = 768
OFF_NSA_Q = 832
OFF_NSA_KV = 1856
OFF_NSA_GATE = 3392
OFF_SB = 3416
OFF_MERGE = 6488
MLA_IN_WIDTH = MLA_Q_RANK + MLA_KV_RANK + 2 * 128 + NSA_GROUPS * 128

VMEM_LIMIT_V7X = 56 * 1024 * 1024
MOE_BLOCK_ROWS = 512
SC_CORES_V7X = 2
SC_SUBCORES_V7X = 16


def _cp(sem, vmem=VMEM_LIMIT_V7X):
    return pltpu.CompilerParams(dimension_semantics=sem, vmem_limit_bytes=vmem)


def _dot(a, b):
    return jnp.dot(a, b, preferred_element_type=F32)


def _dot_nt(a, b):
    return lax.dot_general(a, b, (((1,), (1,)), ((), ())), preferred_element_type=F32)


def _layer_norm(z, g, b):
    mu = jnp.mean(z, axis=-1, keepdims=True)
    zc = z - mu
    var = jnp.mean(zc * zc, axis=-1, keepdims=True)
    return zc * lax.rsqrt(var + LN_EPS) * g + b


def _rms_norm(z, g):
    return z * lax.rsqrt(jnp.mean(z * z, axis=-1, keepdims=True) + RMS_EPS) * g


def _pack_bf16_pairs(z):
    n = z.shape[1] // 2
    bits = pltpu.bitcast(z.astype(BF16).astype(F32), jnp.uint32)
    return lax.shift_right_logical(bits[:, :n], jnp.uint32(16)) | (bits[:, n:] & jnp.uint32(0xFFFF0000))


def _unpack_bf16_pairs(w):
    lo = pltpu.bitcast(lax.shift_left(w, jnp.uint32(16)), F32)
    hi = pltpu.bitcast(w & jnp.uint32(0xFFFF0000), F32)
    return lo, hi


def _proj_heads_kernel(a_ref, w_ref, s_ref, o_ref, abf_ref, *, n_heads_per_tile):
    @pl.when(pl.program_id(1) == 0)
    def _():
        abf_ref[...] = a_ref[...].astype(BF16)

    acc = _dot(abf_ref[...], w_ref[...]) * s_ref[...]
    for c in range(n_heads_per_tile):
        o_ref[c] = acc[:, c * HEAD_DIM:(c + 1) * HEAD_DIM].astype(o_ref.dtype)


def proj_heads(a, w, scale, *, tm, tn):
    m, k = a.shape
    n = w.shape[1]
    hpt = tn // HEAD_DIM
    return pl.pallas_call(
        functools.partial(_proj_heads_kernel, n_heads_per_tile=hpt),
        out_shape=jax.ShapeDtypeStruct((n // HEAD_DIM, m, HEAD_DIM), BF16),
        grid=(m // tm, n // tn),
        in_specs=[
            pl.BlockSpec((tm, k), lambda i, j: (i, 0)),
            pl.BlockSpec((k, tn), lambda i, j: (0, j)),
            pl.BlockSpec((1, tn), lambda i, j: (0, j)),
        ],
        out_specs=pl.BlockSpec((hpt, tm, HEAD_DIM), lambda i, j: (j, i, 0)),
        scratch_shapes=[pltpu.VMEM((tm, k), BF16)],
        compiler_params=_cp(("arbitrary", "arbitrary")),
    )(a, w, scale)


def _mla_in_kernel(x_ref, w_ref, qg_ref, kg_ref, wq_ref, wkv_ref, cos_ref, sin_ref,
                   q_ref, k_ref, v_ref, g_ref):
    xb = x_ref[...].astype(BF16)
    h = _dot(xb, w_ref[...])
    c0, c1 = MLA_Q_RANK, MLA_Q_RANK + MLA_KV_RANK
    cq = h[:, 0:c0]
    ckv = h[:, c0:c1]
    kr1 = h[:, c1:c1 + 128]
    kr2 = h[:, c1 + 128:c1 + 256]
    g_ref[...] = jax.nn.sigmoid(h[:, c1 + 256:MLA_IN_WIDTH])
    cos = cos_ref[...]
    sin = sin_ref[...]
    scale = (MLA_NOPE + MLA_ROPE) ** -0.5
    hw = MLA_HEADS * 128
    nq = _rms_norm(cq, qg_ref[...]).astype(BF16)
    q3 = _dot(nq, wq_ref[...])
    for hh in range(MLA_HEADS):
        lo, hi = hh * 128, (hh + 1) * 128
        q_ref[hh, :, 0:128] = (q3[:, lo:hi] * scale).astype(BF16)
        rot = q3[:, hw + lo:hw + hi] * cos + q3[:, 2 * hw + lo:2 * hw + hi] * sin
        q_ref[hh, :, 128:256] = (rot * scale).astype(BF16)
    nkv = _rms_norm(ckv, kg_ref[...]).astype(BF16)
    kv = _dot(nkv, wkv_ref[...])
    krot = (kr1 * cos + kr2 * sin).astype(BF16)
    for hh in range(MLA_HEADS):
        lo, hi = hh * 128, (hh + 1) * 128
        k_ref[hh, :, 0:128] = kv[:, lo:hi].astype(BF16)
        k_ref[hh, :, 128:256] = krot
        v_ref[hh] = kv[:, hw + lo:hw + hi].astype(BF16)


def mla_in(x, w_mla, qg, kg, wq3, wkv, cos128, sin128, *, seq, tm):
    t = x.shape[0]
    npos = seq // tm
    full = lambda shape: pl.BlockSpec(shape, lambda i: (0,) * len(shape))
    return pl.pallas_call(
        _mla_in_kernel,
        out_shape=(
            jax.ShapeDtypeStruct((MLA_HEADS, t, 256), BF16),
            jax.ShapeDtypeStruct((MLA_HEADS, t, 256), BF16),
            jax.ShapeDtypeStruct((MLA_HEADS, t, 128), BF16),
            jax.ShapeDtypeStruct((t, 256), F32),
        ),
        grid=(t // tm,),
        in_specs=[
            pl.BlockSpec((tm, D_MODEL), lambda i: (i, 0)),
            full((D_MODEL, MLA_IN_WIDTH)),
            full((1, MLA_Q_RANK)),
            full((1, MLA_KV_RANK)),
            full((MLA_Q_RANK, 3 * MLA_HEADS * 128)),
            full((MLA_KV_RANK, 2 * MLA_HEADS * 128)),
            pl.BlockSpec((tm, 128), lambda i: (i % npos, 0)),
            pl.BlockSpec((tm, 128), lambda i: (i % npos, 0)),
        ],
        out_specs=(
            pl.BlockSpec((MLA_HEADS, tm, 256), lambda i: (0, i, 0)),
            pl.BlockSpec((MLA_HEADS, tm, 256), lambda i: (0, i, 0)),
            pl.BlockSpec((MLA_HEADS, tm, 128), lambda i: (0, i, 0)),
            pl.BlockSpec((tm, 256), lambda i: (i, 0)),
        ),
        compiler_params=_cp(("arbitrary",)),
    )(x, w_mla, qg, kg, wq3, wkv, cos128, sin128)


def _mla_attn_kernel(q_ref, k_ref, v_ref, o_ref, s_ref, *, tq, heads):
    qi = pl.program_id(2)

    def scores(h, kt):
        k0 = pl.multiple_of(kt * tq, tq)
        return _dot_nt(q_ref[h], k_ref[h, pl.ds(k0, tq), :])

    def consume(h, kt, s, carry, diag):
        m, l, acc = carry
        k0 = pl.multiple_of(kt * tq, tq)
        v = v_ref[h, pl.ds(k0, tq), :]
        if diag:
            row = lax.broadcasted_iota(jnp.int32, (tq, tq), 0)
            col = lax.broadcasted_iota(jnp.int32, (tq, tq), 1)
            s = jnp.where(col <= row, s, NEG)
        m_new = jnp.maximum(m, jnp.max(s, axis=1, keepdims=True))
        alpha = jnp.exp(m - m_new)
        p = jnp.exp((s - m_new).astype(BF16))
        l = alpha * l + jnp.sum(p.astype(F32), axis=1, keepdims=True)
        acc = alpha * acc + _dot(p, v)
        return m_new, l, acc

    def fill(slot, kt):
        for h in range(heads):
            s_ref[slot, h] = scores(h, kt)

    def drain(slot, kt, carries, diag):
        return tuple(consume(h, kt, s_ref[slot, h], carries[h], diag) for h in range(heads))

    def body(j, carries):
        kt = 2 * j
        fill(1, kt + 1)
        carries = drain(0, kt, carries, False)
        fill(0, kt + 2)
        return drain(1, kt + 1, carries, False)

    init = (jnp.full((tq, 1), NEG, F32), jnp.zeros((tq, 1), F32), jnp.zeros((tq, 128), F32))
    fill(0, 0)
    carries = lax.fori_loop(0, qi // 2, body, (init,) * heads)

    def even_tail(carries):
        return drain(0, qi, carries, True)

    def odd_tail(carries):
        fill(1, qi)
        return drain(1, qi, drain(0, qi - 1, carries, False), True)

    carries = lax.cond(lax.rem(qi, 2) == 1, odd_tail, even_tail, carries)
    for h, (_, l, acc) in enumerate(carries):
        o_ref[:, h * 128:(h + 1) * 128] = (acc / l).astype(o_ref.dtype)


def mla_attn(q, k, v, *, batch, seq, tq, heads):
    nq = seq // tq
    q4 = q.reshape(MLA_HEADS, batch, seq, 256)
    k4 = k.reshape(MLA_HEADS, batch, seq, 256)
    v4 = v.reshape(MLA_HEADS, batch, seq, 128)
    return pl.pallas_call(
        functools.partial(_mla_attn_kernel, tq=tq, heads=heads),
        out_shape=jax.ShapeDtypeStruct((batch * seq, MLA_HEADS * 128), BF16),
        grid=(MLA_HEADS // heads, batch, nq),
        in_specs=[
            pl.BlockSpec((heads, None, tq, 256), lambda h, b, i: (h, b, i, 0)),
            pl.BlockSpec((heads, None, seq, 256), lambda h, b, i: (h, b, 0, 0)),
            pl.BlockSpec((heads, None, seq, 128), lambda h, b, i: (h, b, 0, 0)),
        ],
        out_specs=pl.BlockSpec((tq, heads * 128), lambda h, b, i: (b * nq + i, h)),
        scratch_shapes=[pltpu.VMEM((2, heads, tq, tq), F32)],
        compiler_params=_cp(("arbitrary", "arbitrary", "arbitrary")),
    )(q4, k4, v4)


def _sb_attn_kernel(q_ref, k_ref, v_ref, u_ref, o_ref, *, tq, heads):
    qi = pl.program_id(2)
    u = u_ref[...]

    def head_step(h, kt, carry, diag):
        run, acc = carry
        k0 = pl.multiple_of(kt * tq, tq)
        k = k_ref[h, pl.ds(k0, tq), :]
        v = v_ref[h, pl.ds(k0, tq), :]
        z = _dot_nt(q_ref[h], k)
        l1m = -(jnp.maximum(z, 0.0) + jnp.log(1.0 + jnp.exp(-jnp.abs(z))))
        if diag:
            row = lax.broadcasted_iota(jnp.int32, (tq, tq), 0)
            col = lax.broadcasted_iota(jnp.int32, (tq, tq), 1)
            strict = col < row
            l1m_m = jnp.where(strict, l1m, 0.0)
        else:
            l1m_m = l1m
        hi = l1m_m.astype(BF16)
        lo = (l1m_m - hi.astype(F32)).astype(BF16)
        between = _dot(hi, u) + _dot(lo, u)
        a = jnp.exp(z + l1m + between + run)
        if diag:
            a = jnp.where(strict, a, 0.0)
        acc = acc + _dot(a.astype(BF16), v)
        run = run + between[:, 0:1] + l1m_m[:, 0:1]
        return run, acc

    def step(kt, carries, diag):
        return tuple(head_step(h, kt, carries[h], diag) for h in range(heads))

    init = (jnp.zeros((tq, 1), F32), jnp.zeros((tq, 128), F32))
    carries = step(qi, (init,) * heads, True)

    def more(c):
        j, carries = c
        top = carries[0][0]
        for run, _ in carries[1:]:
            top = jnp.maximum(top, run)
        return (j < qi) & (jnp.max(top) > SB_UNDERFLOW_LOG)

    def body(c):
        j, carries = c
        return j + 1, step(qi - 1 - j, carries, False)

    _, carries = lax.while_loop(more, body, (jnp.int32(0), carries))
    for h, (_, acc) in enumerate(carries):
        o_ref[:, h * HEAD_DIM:(h + 1) * HEAD_DIM] = acc.astype(o_ref.dtype)


def sb_attn(hm, u, *, head0, batch, seq, tq, heads):
    nq = seq // tq
    assert head0 % heads == 0 and SB_HEADS % heads == 0
    hm4 = hm.reshape(hm.shape[0], batch, seq, HEAD_DIM)
    blk0 = head0 // heads
    per_part = SB_HEADS // heads
    return pl.pallas_call(
        functools.partial(_sb_attn_kernel, tq=tq, heads=heads),
        out_shape=jax.ShapeDtypeStruct((batch * seq, SB_HEADS * HEAD_DIM), BF16),
        grid=(per_part, batch, nq),
        in_specs=[
            pl.BlockSpec((heads, None, tq, HEAD_DIM), lambda h, b, i: (blk0 + h, b, i, 0)),
            pl.BlockSpec((heads, None, seq, HEAD_DIM), lambda h, b, i: (blk0 + per_part + h, b, 0, 0)),
            pl.BlockSpec((heads, None, seq, HEAD_DIM), lambda h, b, i: (blk0 + 2 * per_part + h, b, 0, 0)),
            pl.BlockSpec((tq, tq), lambda h, b, i: (0, 0)),
        ],
        out_specs=pl.BlockSpec((tq, heads * HEAD_DIM), lambda h, b, i: (b * nq + i, h)),
        compiler_params=_cp(("arbitrary", "arbitrary", "arbitrary")),
    )(hm4, hm4, hm4, u)


def _nsa_cmp_kernel(c_ref, w1_ref, pe_ref, w2_ref, o_ref, *, nc):
    c = c_ref[...]
    half = NSA_CMP_STRIDE * HEAD_DIM
    a1 = _dot(c, w1_ref[0:half, :])
    a2 = _dot(c, w1_ref[half:2 * half, :])
    pc = _dot(pe_ref[...], w1_ref[...])[0:1, :]
    pre = a1 + pltpu.roll(a2, nc - 1, 0) + pc
    act = 0.5 * pre * (1.0 + jnp.tanh(0.7978845608028654 * (pre + 0.044715 * (pre * pre * pre))))
    o_ref[...] = _dot(act.astype(BF16), w2_ref[...]).astype(BF16)


def nsa_compress(cmp_heads, w1, pe, w2, *, batch, seq):
    nc = seq // NSA_CMP_STRIDE
    hm4 = cmp_heads.reshape(4, batch, nc, NSA_CMP_STRIDE * HEAD_DIM)
    return pl.pallas_call(
        functools.partial(_nsa_cmp_kernel, nc=nc),
        out_shape=jax.ShapeDtypeStruct((4, batch, nc, HEAD_DIM), BF16),
        grid=(4, batch),
        in_specs=[
            pl.BlockSpec((None, None, nc, NSA_CMP_STRIDE * HEAD_DIM), lambda c, b: (c, b, 0, 0)),
            pl.BlockSpec((None, NSA_CMP_LEN * HEAD_DIM, HEAD_DIM), lambda c, b: (c // 2, 0, 0)),
            pl.BlockSpec((None, 8, NSA_CMP_LEN * HEAD_DIM), lambda c, b: (c // 2, 0, 0)),
            pl.BlockSpec((None, HEAD_DIM, HEAD_DIM), lambda c, b: (c // 2, 0, 0)),
        ],
        out_specs=pl.BlockSpec((None, None, nc, HEAD_DIM), lambda c, b: (c, b, 0, 0)),
        compiler_params=_cp(("arbitrary", "arbitrary")),
    )(hm4, w1, pe, w2)


def _nsa_attn_kernel(q_ref, qa_ref, kc_ref, vc_ref, ks_ref, vs_ref, kw_ref, vw_ref, g_ref, ovt_ref, e_ref,
                     kpos_ref, cpos_ref, o_ref, ss_ref, *, tq, tk, seq, n_sel, n_top):
    qi = pl.program_id(1)
    t0 = qi * tq
    rows = NSA_HG * tq
    nc = seq // NSA_CMP_STRIDE
    groups = range(NSA_GROUPS)

    rid = lax.broadcasted_iota(jnp.int32, (rows, 1), 0)
    trow = t0 + lax.bitwise_and(rid, tq - 1)

    def masked_softmax(s, valid):
        sm = jnp.where(valid, s, NEG)
        m = jnp.max(sm, axis=1, keepdims=True)
        e = jnp.where(valid, jnp.exp(sm - m), 0.0)
        d = jnp.sum(e, axis=1, keepdims=True)
        return e * (1.0 / jnp.where(d > 0.0, d, 1.0))

    n_i = lax.broadcasted_iota(jnp.int32, (1, nc), 1)
    end = n_i * NSA_CMP_STRIDE + (NSA_CMP_LEN - 1)
    cur = lax.shift_right_logical(t0 + lax.broadcasted_iota(jnp.int32, (1, tq), 1),
                                  int(np.log2(NSA_SEL_LEN)))
    blk = lax.broadcasted_iota(jnp.int32, (n_sel, tq), 0)
    forced = (blk == 0) | (blk == cur) | (blk == cur - 1)
    sub = lax.broadcasted_iota(jnp.int32, (8, tq), 0)
    eye = (lax.broadcasted_iota(jnp.int32, (n_sel, n_sel), 0)
           == lax.broadcasted_iota(jnp.int32, (n_sel, n_sel), 1)).astype(F32).astype(BF16)
    ovt = ovt_ref[...]

    def front(g):
        q = jnp.concatenate([q_ref[NSA_HG * g:NSA_HG * (g + 1)].reshape(rows, HEAD_DIM), qa_ref[g]], axis=1)

        s_c = _dot_nt(q, jnp.concatenate([kc_ref[g], cpos_ref[...]], axis=1))
        p_c = masked_softmax(s_c, end <= trow)
        o_c = _dot(p_c.astype(BF16), vc_ref[g])

        psum = p_c[0:tq] + p_c[tq:2 * tq] + p_c[2 * tq:3 * tq] + p_c[3 * tq:4 * tq]
        p_hi = psum.astype(BF16)
        p_lo = (psum - p_hi.astype(F32)).astype(BF16)
        imp = _dot_nt(ovt, p_hi) + _dot_nt(ovt, p_lo)
        key = jnp.where(blk > cur, -BIG, jnp.where(forced, BIG, imp))
        chunks = [key[8 * r:8 * r + 8, :] for r in range(n_sel // 8)]
        ranks = [jnp.zeros((8, tq), F32) for _ in chunks]
        for i in range(n_sel):
            vi = key[i:i + 1, :]
            for r, kc in enumerate(chunks):
                gt = jnp.where(vi > kc, 1.0, 0.0)
                if r < i // 8:
                    win = gt
                else:
                    ge = jnp.where(vi >= kc, 1.0, 0.0)
                    win = ge if r > i // 8 else jnp.where(sub > i % 8, ge, gt)
                ranks[r] = ranks[r] + win
        rank = jnp.concatenate(ranks, axis=0)
        selm_t = jnp.where((rank < float(n_top)) & (blk <= cur), 1.0, 0.0).astype(BF16)
        selm = lax.dot_general(selm_t, eye, (((0,), (0,)), ((), ())), preferred_element_type=F32).astype(BF16)
        return q, selm, o_c

    fronts = [front(g) for g in groups]

    def sel_fill(slot, kt):
        k0 = pl.multiple_of(kt * tk, tk)
        kp = kpos_ref[pl.ds(k0, tk), :]
        for g in groups:
            q, selm, _ = fronts[g]
            kk = jnp.concatenate([ks_ref[g, pl.ds(k0, tk), :], kp], axis=1)
            mex = _dot(selm, e_ref[kt])
            bias = (mex - 1.0) * BIG
            ss_ref[slot, g] = _dot_nt(q, kk) + jnp.concatenate([bias] * NSA_HG, axis=0)

    def sel_drain_group(g, slot, kt, carry, diag):
        m, l, acc = carry
        k0 = pl.multiple_of(kt * tk, tk)
        vv = vs_ref[g, pl.ds(k0, tk), :]
        sm = ss_ref[slot, g]
        if diag:
            spos = k0 + lax.broadcasted_iota(jnp.int32, (1, tk), 1)
            sm = jnp.where(spos <= trow, sm, NEG)
        m_new = jnp.maximum(m, jnp.max(sm, axis=1, keepdims=True))
        alpha = jnp.exp(m - m_new)
        p = jnp.exp((sm - m_new).astype(BF16))
        l = alpha * l + jnp.sum(p.astype(F32), axis=1, keepdims=True)
        acc = alpha * acc + _dot(p, vv)
        return m_new, l, acc

    def sel_drain(slot, kt, carries, diag):
        return tuple(sel_drain_group(g, slot, kt, carries[g], diag) for g in groups)

    def sel_body(j, carries):
        kt = 2 * j
        sel_fill(1, kt + 1)
        carries = sel_drain(0, kt, carries, False)
        sel_fill(0, kt + 2)
        return sel_drain(1, kt + 1, carries, False)

    kt_last = t0 // tk
    init = (jnp.full((rows, 1), NEG, F32), jnp.zeros((rows, 1), F32), jnp.zeros((rows, HEAD_DIM), F32))
    sel_fill(0, 0)
    carries = lax.fori_loop(0, kt_last // 2, sel_body, (init,) * NSA_GROUPS)

    def even_tail(carries):
        return sel_drain(0, kt_last, carries, True)

    def odd_tail(carries):
        sel_fill(1, kt_last)
        return sel_drain(1, kt_last, sel_drain(0, kt_last - 1, carries, False), True)

    carries = lax.cond(lax.rem(kt_last, 2) == 1, odd_tail, even_tail, carries)

    wk = NSA_WINDOW + tq
    ks0 = pl.multiple_of(jnp.maximum(t0 - NSA_WINDOW, 0), tq)
    kpw = kpos_ref[pl.ds(ks0, wk), :]
    wpos = ks0 + lax.broadcasted_iota(jnp.int32, (1, wk), 1)
    dw = trow - wpos
    in_window = pltpu.bitcast(dw, jnp.uint32) < jnp.uint32(NSA_WINDOW)
    gt = g_ref[...]
    for g in groups:
        q, _, o_c = fronts[g]
        _, l_s, acc_s = carries[g]
        o_s = acc_s * (1.0 / l_s)
        kw = jnp.concatenate([kw_ref[g, pl.ds(ks0, wk), :], kpw], axis=1)
        vw = vw_ref[g, pl.ds(ks0, wk), :]
        sm_w = jnp.where(in_window, _dot_nt(q, kw), NEG)
        p_w = jnp.exp((sm_w - jnp.max(sm_w, axis=1, keepdims=True)).astype(BF16))
        o_w = _dot(p_w, vw) * (1.0 / jnp.sum(p_w.astype(F32), axis=1, keepdims=True))
        for hg in range(NSA_HG):
            sl = slice(hg * tq, (hg + 1) * tq)
            c0 = 128 * g + 3 * hg
            o = (gt[:, c0:c0 + 1] * o_c[sl] + gt[:, c0 + 1:c0 + 2] * o_s[sl] + gt[:, c0 + 2:c0 + 3] * o_w[sl])
            h = NSA_HG * g + hg
            o_ref[:, h * HEAD_DIM:(h + 1) * HEAD_DIM] = o.astype(o_ref.dtype)


def nsa_attn(hm, cmp, gates, consts, *, q_head0, kv_head0, batch, seq, tq, tk):
    ov, e, qa, kpos, cpos = consts
    nq = seq // tq
    nc = seq // NSA_CMP_STRIDE
    n_sel = seq // NSA_SEL_LEN
    n_top = min(NSA_TOPK, n_sel)
    assert tk % tq == 0 and seq % tk == 0 and seq >= NSA_WINDOW + tq and NSA_WINDOW % tq == 0
    assert q_head0 % NSA_HEADS == 0 and kv_head0 % NSA_GROUPS == 0
    hm4 = hm.reshape(hm.shape[0], batch, seq, HEAD_DIM)
    kv_spec = lambda off: pl.BlockSpec((NSA_GROUPS, None, seq, HEAD_DIM),
                                       lambda b, i: ((kv_head0 + off) // NSA_GROUPS, b, 0, 0))
    return pl.pallas_call(
        functools.partial(_nsa_attn_kernel, tq=tq, tk=tk, seq=seq, n_sel=n_sel, n_top=n_top),
        out_shape=jax.ShapeDtypeStruct((batch * seq, NSA_HEADS * HEAD_DIM), BF16),
        grid=(batch, nq),
        in_specs=[
            pl.BlockSpec((NSA_HEADS, None, tq, HEAD_DIM), lambda b, i: (q_head0 // NSA_HEADS, b, i, 0)),
            pl.BlockSpec((NSA_GROUPS, NSA_HG * tq, 128), lambda b, i: (0, 0, 0)),
            pl.BlockSpec((NSA_GROUPS, None, nc, HEAD_DIM), lambda b, i: (0, b, 0, 0)),
            pl.BlockSpec((NSA_GROUPS, None, nc, HEAD_DIM), lambda b, i: (1, b, 0, 0)),
            kv_spec(4), kv_spec(6), kv_spec(8), kv_spec(10),
            pl.BlockSpec((tq, NSA_GROUPS * 128), lambda b, i: (b * nq + i, 0)),
            pl.BlockSpec((n_sel, nc), lambda b, i: (0, 0)),
            pl.BlockSpec((seq // tk, n_sel, tk), lambda b, i: (0, 0, 0)),
            pl.BlockSpec((seq, 128), lambda b, i: (0, 0)),
            pl.BlockSpec((nc, 128), lambda b, i: (0, 0)),
        ],
        out_specs=pl.BlockSpec((tq, NSA_HEADS * HEAD_DIM), lambda b, i: (b * nq + i, 0)),
        scratch_shapes=[pltpu.VMEM((2, NSA_GROUPS, NSA_HG * tq, tk), F32)],
        compiler_params=_cp(("arbitrary", "arbitrary")),
    )(hm4, qa, cmp, cmp, hm4, hm4, hm4, hm4, gates, ov, e, kpos, cpos)


def _merge_kernel(x_ref, oa_ref, ob_ref, oc_ref, wm_ref, wb_ref, bm_ref, y_ref, xb_ref):
    @pl.when(pl.program_id(1) == 0)
    def _():
        xb_ref[...] = x_ref[...].astype(BF16)

    xb = xb_ref[...]
    acc = None
    for br, o_ref in enumerate((oa_ref, ob_ref, oc_ref)):
        gate = jax.nn.sigmoid(_dot(xb, wm_ref[br]) + bm_ref[br])
        term = gate * _dot(o_ref[...], wb_ref[br])
        acc = term if acc is None else acc + term
    y_ref[...] = acc.astype(y_ref.dtype)


def merge_branches(x, o_a, o_b, o_c, wm, wb, bm, *, tm, tn):
    t = x.shape[0]
    o_spec = pl.BlockSpec((tm, BRANCH_WIDTH), lambda i, j: (i, 0))
    return pl.pallas_call(
        _merge_kernel,
        out_shape=jax.ShapeDtypeStruct((t, D_MODEL), BF16),
        grid=(t // tm, D_MODEL // tn),
        in_specs=[
            pl.BlockSpec((tm, D_MODEL), lambda i, j: (i, 0)),
            o_spec, o_spec, o_spec,
            pl.BlockSpec((N_BRANCH, D_MODEL, tn), lambda i, j: (0, 0, j)),
            pl.BlockSpec((N_BRANCH, BRANCH_WIDTH, tn), lambda i, j: (0, 0, j)),
            pl.BlockSpec((N_BRANCH, 1, tn), lambda i, j: (0, 0, j)),
        ],
        out_specs=pl.BlockSpec((tm, tn), lambda i, j: (i, j)),
        scratch_shapes=[pltpu.VMEM((tm, D_MODEL), BF16)],
        compiler_params=_cp(("arbitrary", "arbitrary")),
    )(x, o_a, o_b, o_c, wm, wb, bm)


def _out_ln_kernel(y_ref, w_ref, x_ref, g_ref, b_ref, o_ref, *, alpha):
    half = y_ref.shape[0] // 2
    for r in range(2):
        rs = slice(r * half, (r + 1) * half)
        h = _dot(y_ref[rs, :], w_ref[...])
        o_ref[rs, :] = _layer_norm(alpha * x_ref[rs, :] + h, g_ref[...], b_ref[...])


def out_ln(y, w, x, g, b, *, alpha, tm):
    t = x.shape[0]
    return pl.pallas_call(
        functools.partial(_out_ln_kernel, alpha=alpha),
        out_shape=jax.ShapeDtypeStruct((t, D_MODEL), F32),
        grid=(t // tm,),
        in_specs=[
            pl.BlockSpec((tm, D_MODEL), lambda i: (i, 0)),
            pl.BlockSpec((D_MODEL, D_MODEL), lambda i: (0, 0)),
            pl.BlockSpec((tm, D_MODEL), lambda i: (i, 0)),
            pl.BlockSpec((1, D_MODEL), lambda i: (0, 0)),
            pl.BlockSpec((1, D_MODEL), lambda i: (0, 0)),
        ],
        out_specs=pl.BlockSpec((tm, D_MODEL), lambda i: (i, 0)),
        compiler_params=_cp(("arbitrary",)),
    )(y, w, x, g, b)


def _mem_attn_kernel(x_ref, wq_ref, k_ref, v_ref, wo_ref, g_ref, b_ref, o_ref, *, alpha):
    x = x_ref[...]
    q = _dot(x.astype(BF16), wq_ref[...]) * (HEAD_DIM ** -0.5)
    outs = []
    for h in range(MEM_HEADS):
        qh = q[:, h * HEAD_DIM:(h + 1) * HEAD_DIM].astype(BF16)
        s = _dot_nt(qh, k_ref[h])
        m = jnp.max(s, axis=1, keepdims=True)
        e = jnp.exp(s - m)
        p = e * (1.0 / jnp.sum(e, axis=1, keepdims=True))
        outs.append(_dot(p.astype(BF16), v_ref[h]).astype(BF16))
    o = jnp.concatenate(outs, axis=1)
    h_out = _dot(o, wo_ref[...])
    o_ref[...] = _layer_norm(alpha * x + h_out, g_ref[...], b_ref[...])


def mem_attn_ln(x, wq, kv, wo, g, b, *, alpha, seq, mem_len, tm):
    t = x.shape[0]
    per_b = seq // tm
    kv4 = kv.reshape(2 * MEM_HEADS, t // seq, mem_len, HEAD_DIM)
    width = MEM_HEADS * HEAD_DIM
    return pl.pallas_call(
        functools.partial(_mem_attn_kernel, alpha=alpha),
        out_shape=jax.ShapeDtypeStruct((t, D_MODEL), F32),
        grid=(t // tm,),
        in_specs=[
            pl.BlockSpec((tm, D_MODEL), lambda i: (i, 0)),
            pl.BlockSpec((D_MODEL, width), lambda i: (0, 0)),
            pl.BlockSpec((MEM_HEADS, None, mem_len, HEAD_DIM), lambda i: (0, i // per_b, 0, 0)),
            pl.BlockSpec((MEM_HEADS, None, mem_len, HEAD_DIM), lambda i: (1, i // per_b, 0, 0)),
            pl.BlockSpec((width, D_MODEL), lambda i: (0, 0)),
            pl.BlockSpec((1, D_MODEL), lambda i: (0, 0)),
            pl.BlockSpec((1, D_MODEL), lambda i: (0, 0)),
        ],
        out_specs=pl.BlockSpec((tm, D_MODEL), lambda i: (i, 0)),
        compiler_params=_cp(("arbitrary",)),
    )(x, wq, kv4, kv4, wo, g, b)


def _router_kernel(x_ref, w_ref, b_ref, tri_ref, route_ref, xp_ref, count_ref):
    @pl.when(pl.program_id(0) == 0)
    def _():
        count_ref[...] = jnp.zeros_like(count_ref)

    x = x_ref[...]
    xp_ref[...] = _pack_bf16_pairs(x)
    xh = x.astype(BF16)
    xl = (x - xh.astype(F32)).astype(BF16)
    logits = _dot(xh, w_ref[0]) + _dot(xh, w_ref[1]) + _dot(xl, w_ref[0]) + b_ref[...]
    tm = logits.shape[0]
    lane = lax.broadcasted_iota(jnp.int32, (tm, N_EXPERTS), 1)
    work = logits
    hots, vals, idxs = [], [], []
    for _ in range(TOP_K):
        m = jnp.max(work, axis=1, keepdims=True)
        idx = jnp.min(jnp.where(work == m, lane, N_EXPERTS), axis=1, keepdims=True)
        hot = lane == idx
        hots.append(hot)
        vals.append(m)
        idxs.append(idx)
        work = jnp.where(hot, -jnp.inf, work)
    es = [jnp.exp(v - vals[0]) for v in vals]
    inv = 1.0 / (es[0] + es[1] + es[2] + es[3])
    mask = jnp.zeros((tm, N_EXPERTS), F32)
    for hot in hots:
        mask = mask + jnp.where(hot, 1.0, 0.0)
    before = count_ref[...] + _dot(tri_ref[...], mask.astype(BF16))
    count_ref[...] = count_ref[...] + jnp.sum(mask, axis=0, keepdims=True)
    col = lax.broadcasted_iota(jnp.int32, (tm, 128), 1)
    route = jnp.zeros((tm, 128), F32)
    for k in range(TOP_K):
        rank = jnp.sum(jnp.where(hots[k], before, 0.0), axis=1, keepdims=True)
        route = (route + jnp.where(col == k, idxs[k].astype(F32), 0.0)
                 + jnp.where(col == TOP_K + k, es[k] * inv, 0.0) + jnp.where(col == 2 * TOP_K + k, rank, 0.0))
    route_ref[...] = route


def router(x, w_hl, b, tri, *, tm):
    t = x.shape[0]
    return pl.pallas_call(
        _router_kernel,
        out_shape=(jax.ShapeDtypeStruct((t, 128), F32),
                   jax.ShapeDtypeStruct((t, D_MODEL // 2), jnp.uint32),
                   jax.ShapeDtypeStruct((1, N_EXPERTS), F32)),
        grid=(t // tm,),
        in_specs=[
            pl.BlockSpec((tm, D_MODEL), lambda i: (i, 0)),
            pl.BlockSpec((2, D_MODEL, N_EXPERTS), lambda i: (0, 0, 0)),
            pl.BlockSpec((1, N_EXPERTS), lambda i: (0, 0)),
            pl.BlockSpec((tm, tm), lambda i: (0, 0)),
        ],
        out_specs=(pl.BlockSpec((tm, 128), lambda i: (i, 0)),
                   pl.BlockSpec((tm, D_MODEL // 2), lambda i: (i, 0)),
                   pl.BlockSpec((1, N_EXPERTS), lambda i: (0, 0))),
        compiler_params=_cp(("arbitrary",)),
    )(x, w_hl, b, tri)


def sc_gather_rows(table, idx, *, chunk):
    n = idx.shape[0]
    d = table.shape[1]
    workers = SC_CORES_V7X * SC_SUBCORES_V7X
    per_w = n // workers
    assert n % (workers * chunk) == 0 and chunk % 8 == 0 and chunk <= 128
    mesh = plsc.VectorSubcoreMesh(core_axis_name="c", subcore_axis_name="s")

    @functools.partial(
        pl.kernel, mesh=mesh,
        out_type=jax.ShapeDtypeStruct((n, d), table.dtype),
        scratch_types=[pltpu.VMEM((chunk,), jnp.int32), pltpu.VMEM((chunk, d), table.dtype),
                       pltpu.SemaphoreType.DMA],
    )
    def gather(table_hbm, idx_hbm, out_hbm, idx_v, rows_v, sem):
        wid = lax.axis_index("s") * SC_CORES_V7X + lax.axis_index("c")
        base = wid * per_w

        @pl.loop(0, per_w // chunk)
        def _(j):
            off = pl.multiple_of(base + j * chunk, 8)
            pltpu.sync_copy(idx_hbm.at[pl.ds(off, chunk)], idx_v)
            pltpu.async_copy(table_hbm.at[idx_v], rows_v, sem).wait()
            pltpu.sync_copy(rows_v, out_hbm.at[pl.ds(off, chunk)])

    return gather(table, idx)


def sc_scatter_rows(rows, idx, n_out, *, copies, chunk):
    t, d = rows.shape
    workers = SC_CORES_V7X * SC_SUBCORES_V7X
    per_w = t // workers
    assert idx.shape == (copies * t,) and t % (workers * chunk) == 0 and chunk % 8 == 0 and chunk <= 128
    mesh = plsc.VectorSubcoreMesh(core_axis_name="c", subcore_axis_name="s")

    @functools.partial(
        pl.kernel, mesh=mesh,
        out_type=jax.ShapeDtypeStruct((n_out, d), rows.dtype),
        scratch_types=[pltpu.VMEM((chunk,), jnp.int32), pltpu.VMEM((chunk, d), rows.dtype)],
    )
    def scatter(rows_hbm, idx_hbm, out_hbm, idx_v, rows_v):
        wid = lax.axis_index("s") * SC_CORES_V7X + lax.axis_index("c")
        base = wid * per_w

        @pl.loop(0, per_w // chunk)
        def _(j):
            off = pl.multiple_of(base + j * chunk, 8)
            pltpu.sync_copy(rows_hbm.at[pl.ds(off, chunk)], rows_v)
            for k in range(copies):
                pltpu.sync_copy(idx_hbm.at[pl.ds(pl.multiple_of(k * t + off, 8), chunk)], idx_v)
                pltpu.sync_copy(rows_v, out_hbm.at[idx_v])

    return scatter(rows, idx)


def _experts_kernel(be_ref, x_ref, wg_ref, bg_ref, wu_ref, bu_ref, wd_ref, bd_ref, y_ref,
                    wgb_ref, wub_ref, wdb_ref):
    i = pl.program_id(0)
    prev = be_ref[jnp.maximum(i - 1, 0)]
    n_used = be_ref[pl.num_programs(0)]

    @pl.when((i < n_used) & ((i == 0) | (be_ref[i] != prev)))
    def _():
        wgb_ref[...] = wg_ref[...].astype(BF16)
        wub_ref[...] = wu_ref[...].astype(BF16)
        wdb_ref[...] = wd_ref[...].astype(BF16)

    @pl.when(i < n_used)
    def _():
        x_lo, x_hi = _unpack_bf16_pairs(x_ref[...])
        xb = jnp.concatenate([x_lo.astype(BF16), x_hi.astype(BF16)], axis=1)
        g = jnp.minimum(_dot(xb, wgb_ref[...]) + bg_ref[...], SWIGLU_LIMIT)
        u = jnp.clip(_dot(xb, wub_ref[...]) + bu_ref[...], -SWIGLU_LIMIT, SWIGLU_LIMIT)
        hdn = (u + 1.0) * (g * jax.nn.sigmoid(SWIGLU_ALPHA * g))
        y_ref[...] = _pack_bf16_pairs(_dot(hdn.astype(BF16), wdb_ref[...]) + bd_ref[...])


def experts(x_rows, blk_e, wg, bg, wu, bu, wd, bd, *, layer, bm):
    n_rows, dp = x_rows.shape
    d, f = wg.shape[2], wg.shape[3]
    w_spec = lambda shape: pl.BlockSpec((None, None) + shape, lambda i, be: (layer, be[i], 0, 0))
    grid_spec = pltpu.PrefetchScalarGridSpec(
        num_scalar_prefetch=1,
        grid=(n_rows // bm,),
        in_specs=[
            pl.BlockSpec((bm, dp), lambda i, be: (i, 0)),
            w_spec((d, f)), w_spec((1, f)), w_spec((d, f)), w_spec((1, f)), w_spec((f, d)), w_spec((1, d)),
        ],
        out_specs=pl.BlockSpec((bm, dp), lambda i, be: (i, 0)),
        scratch_shapes=[pltpu.VMEM((d, f), BF16), pltpu.VMEM((d, f), BF16), pltpu.VMEM((f, d), BF16)],
    )
    return pl.pallas_call(
        _experts_kernel,
        out_shape=jax.ShapeDtypeStruct((n_rows, dp), jnp.uint32),
        grid_spec=grid_spec,
        compiler_params=_cp(("arbitrary",)),
    )(blk_e, x_rows, wg, bg, wu, bu, wd, bd)


def _moe_ln_kernel(x_ref, y_ref, w_ref, g_ref, b_ref, o_ref, *, alpha):
    w = w_ref[...]
    y_lo = y_hi = None
    for k in range(TOP_K):
        lo, hi = _unpack_bf16_pairs(y_ref[k])
        wk = w[:, TOP_K + k:TOP_K + k + 1]
        y_lo = wk * lo if y_lo is None else y_lo + wk * lo
        y_hi = wk * hi if y_hi is None else y_hi + wk * hi
    y = jnp.concatenate([y_lo, y_hi], axis=1)
    o_ref[...] = _layer_norm(alpha * x_ref[...] + y, g_ref[...], b_ref[...])


def moe_ln(x, y4, w4p, g, b, *, alpha, tm):
    t = x.shape[0]
    return pl.pallas_call(
        functools.partial(_moe_ln_kernel, alpha=alpha),
        out_shape=jax.ShapeDtypeStruct((t, D_MODEL), F32),
        grid=(t // tm,),
        in_specs=[
            pl.BlockSpec((tm, D_MODEL), lambda i: (i, 0)),
            pl.BlockSpec((TOP_K, tm, D_MODEL // 2), lambda i: (0, i, 0)),
            pl.BlockSpec((tm, 128), lambda i: (i, 0)),
            pl.BlockSpec((1, D_MODEL), lambda i: (0, 0)),
            pl.BlockSpec((1, D_MODEL), lambda i: (0, 0)),
        ],
        out_specs=pl.BlockSpec((tm, D_MODEL), lambda i: (i, 0)),
        compiler_params=_cp(("arbitrary",)),
    )(x, y4, w4p, g, b)


def _rope_tables(seq):
    inv = np.asarray(ROPE_THETA ** (-np.arange(0, MLA_ROPE, 2) / MLA_ROPE), np.float32)
    ang = jnp.arange(seq, dtype=F32)[:, None] * jnp.asarray(inv)[None, :]
    cos, sin = jnp.cos(ang), jnp.sin(ang)
    zeros = jnp.zeros((seq, 128 - MLA_ROPE), F32)
    return (jnp.concatenate([cos, cos, zeros], axis=1), jnp.concatenate([-sin, sin, zeros], axis=1))


def _nsa_constants(seq, tq, tk):
    nc = seq // NSA_CMP_STRIDE
    qa = np.zeros((NSA_GROUPS, NSA_HG * tq, 128), np.float32)
    for g in range(NSA_GROUPS):
        for hg in range(NSA_HG):
            slope = 2.0 ** (-8.0 * (g * NSA_HG + hg + 1) / NSA_HEADS)
            qa[g, hg * tq:(hg + 1) * tq, 0] = slope * NSA_SEL_LEN
            qa[g, hg * tq:(hg + 1) * tq, 1] = slope
            qa[g, hg * tq:(hg + 1) * tq, 2] = slope * NSA_CMP_STRIDE
            qa[g, hg * tq:(hg + 1) * tq, 3] = slope * (NSA_CMP_LEN - 1) / 2.0
    kpos = np.zeros((seq, 128), np.float32)
    kpos[:, 0] = np.arange(seq) // NSA_SEL_LEN
    kpos[:, 1] = np.arange(seq) % NSA_SEL_LEN
    cpos = np.zeros((nc, 128), np.float32)
    cpos[:, 2] = np.arange(nc)
    cpos[:, 3] = 1.0
    for arr in (qa, kpos, cpos):
        assert np.array_equal(arr.astype(BF16).astype(np.float32), arr)
    n_cmp = (seq - NSA_CMP_LEN) // NSA_CMP_STRIDE + 1
    n_sel = seq // NSA_SEL_LEN
    cs = np.arange(nc) * NSA_CMP_STRIDE
    ss = np.arange(n_sel) * NSA_SEL_LEN
    ov = np.clip(np.minimum(cs[:, None] + NSA_CMP_LEN, ss[None, :] + NSA_SEL_LEN)
                 - np.maximum(cs[:, None], ss[None, :]), 0, None) / NSA_CMP_LEN
    ov[n_cmp:] = 0.0
    e = (np.arange(seq)[None, :] // NSA_SEL_LEN == np.arange(n_sel)[:, None]).astype(np.float32)
    e = e.reshape(n_sel, seq // tk, tk).transpose(1, 0, 2)
    return tuple(jnp.asarray(a, BF16) for a in (ov.T, e, qa, kpos, cpos))


def _pad_cols(w, width):
    return jnp.pad(w, ((0, 0), (0, width - w.shape[1])))


def _swap_halves(w):
    half = w.shape[1] // 2
    return jnp.concatenate([w[:, half:], w[:, :half]], axis=1)


def _layer_weights(w_in, w_q_up, w_kv_up):
    kr = w_in[:, OFF_KR:OFF_NSA_Q]
    gate = w_in[:, OFF_NSA_GATE:OFF_SB]
    per_g = NSA_HG * 3
    w_mla = jnp.concatenate([
        w_in[:, OFF_CQ:OFF_KR],
        _pad_cols(kr, 128), _pad_cols(_swap_halves(kr), 128),
        _pad_cols(gate[:, :per_g], 128), _pad_cols(gate[:, per_g:], 128)], axis=1).astype(BF16)
    wq = w_q_up.reshape(MLA_Q_RANK, MLA_HEADS, MLA_NOPE + MLA_ROPE)
    rope = wq[:, :, MLA_NOPE:]
    rope_sw = jnp.concatenate([rope[:, :, MLA_ROPE // 2:], rope[:, :, :MLA_ROPE // 2]], axis=2)
    pad = ((0, 0), (0, 0), (0, 128 - MLA_ROPE))
    wq3 = jnp.concatenate([
        wq[:, :, :MLA_NOPE].reshape(MLA_Q_RANK, -1),
        jnp.pad(rope, pad).reshape(MLA_Q_RANK, -1),
        jnp.pad(rope_sw, pad).reshape(MLA_Q_RANK, -1)], axis=1).astype(BF16)
    wkv = w_kv_up.reshape(MLA_KV_RANK, MLA_HEADS, 2, 128)
    wkv = jnp.concatenate([wkv[:, :, 0].reshape(MLA_KV_RANK, -1),
                           wkv[:, :, 1].reshape(MLA_KV_RANK, -1)], axis=1).astype(BF16)
    w_heads = jnp.concatenate([w_in[:, OFF_NSA_Q:OFF_NSA_KV], w_in[:, OFF_SB:OFF_MERGE],
                               w_in[:, OFF_NSA_KV:OFF_NSA_GATE]], axis=1).astype(BF16)
    wm = w_in[:, OFF_MERGE:].reshape(D_MODEL, N_BRANCH, D_MODEL).transpose(1, 0, 2).astype(BF16)
    return w_mla, wq3, wkv, w_heads, wm


def _forward(x, mem, w_in, mla_q_norm, mla_w_q_up, mla_kv_norm, mla_w_kv_up,
             nsa_pe_k, nsa_pe_v, nsa_w1_k, nsa_w1_v, nsa_w2_k, nsa_w2_v,
             w_branch, b_merge, w_out, ln_mix_g, ln_mix_b,
             mem_w_q, mem_w_k, mem_w_v, mem_w_o, ln_mem_g, ln_mem_b,
             moe_w_router, moe_b_router, moe_w_gate, moe_b_gate, moe_w_up, moe_b_up,
             moe_w_down, moe_b_down, ln_moe_g, ln_moe_b):
    batch, seq, _ = x.shape
    mem_len = mem.shape[1]
    depth = w_in.shape[0]
    t = batch * seq
    alpha = float((2 * depth) ** 0.25)
    bm = MOE_BLOCK_ROWS
    n_rows = t * TOP_K + N_EXPERTS * bm
    n_blocks = n_rows // bm

    tm_in = min(512, seq)
    tq_mla = min(512, seq)
    tq_sb = 256
    tq_nsa, tk_nsa = 256, 512
    tm_ln = 256
    sc_chunk = 64

    cos128, sin128 = _rope_tables(seq)
    nsa_consts = _nsa_constants(seq, tq_nsa, tk_nsa)
    u_sb = jnp.asarray(np.arange(tq_sb)[:, None] > np.arange(tq_sb)[None, :], BF16)
    tri_router = jnp.asarray(np.arange(tm_in)[:, None] > np.arange(tm_in)[None, :], BF16)
    n_qheads = NSA_HEADS
    n_kvheads = 3 * 2 * NSA_GROUPS
    head_scale = np.ones((1, (n_qheads + n_kvheads + 3 * SB_HEADS) * HEAD_DIM), np.float32)
    head_scale[:, :n_qheads * HEAD_DIM] = HEAD_DIM ** -0.5
    sb0 = n_qheads
    kv0 = sb0 + 3 * SB_HEADS
    tn_heads = (n_qheads + n_kvheads + 3 * SB_HEADS) // 4 * HEAD_DIM
    head_scale[:, sb0 * HEAD_DIM:(sb0 + SB_HEADS) * HEAD_DIM] = HEAD_DIM ** -0.5
    head_scale = jnp.asarray(head_scale)
    ones_kv = jnp.ones((1, 2 * MEM_HEADS * HEAD_DIM), F32)

    b_gate4 = moe_b_gate.reshape(depth, N_EXPERTS, 1, D_EXPERT)
    b_up4 = moe_b_up.reshape(depth, N_EXPERTS, 1, D_EXPERT)
    b_down4 = moe_b_down.reshape(depth, N_EXPERTS, 1, D_MODEL)

    xf = x.reshape(t, D_MODEL)
    memf = mem.reshape(batch * mem_len, D_MODEL)
    row = lambda v: v.reshape(1, -1)

    for l in range(depth):
        w_mla, wq3, wkv, w_heads, wm = _layer_weights(w_in[l], mla_w_q_up[l], mla_w_kv_up[l])

        q_a, k_a, v_a, gates = mla_in(xf, w_mla, row(mla_q_norm[l]), row(mla_kv_norm[l]), wq3, wkv,
                                      cos128, sin128, seq=seq, tm=tm_ln)
        hm = proj_heads(xf, w_heads, head_scale, tm=tm_in, tn=tn_heads)
        o_a = mla_attn(q_a, k_a, v_a, batch=batch, seq=seq, tq=tq_mla, heads=2)
        w1 = jnp.stack([nsa_w1_k[l], nsa_w1_v[l]]).astype(BF16)
        pe = jnp.stack([nsa_pe_k[l], nsa_pe_v[l]]).reshape(2, 1, -1)
        pe = jnp.broadcast_to(pe, (2, 8, pe.shape[-1])).astype(BF16)
        w2 = jnp.stack([nsa_w2_k[l], nsa_w2_v[l]]).astype(BF16)
        cmp = nsa_compress(hm[kv0:kv0 + 4], w1, pe, w2, batch=batch, seq=seq)
        o_b = nsa_attn(hm, cmp, gates, nsa_consts, q_head0=0, kv_head0=kv0,
                       batch=batch, seq=seq, tq=tq_nsa, tk=tk_nsa)
        o_c = sb_attn(hm, u_sb, head0=sb0, batch=batch, seq=seq, tq=tq_sb, heads=4)
        y = merge_branches(xf, o_a, o_b, o_c, wm, w_branch[l].astype(BF16),
                           b_merge[l].reshape(N_BRANCH, 1, D_MODEL), tm=tm_in, tn=512)
        xf = out_ln(y, w_out[l].astype(BF16), xf, row(ln_mix_g[l]), row(ln_mix_b[l]), alpha=alpha, tm=tm_in)

        w_kv_mem = jnp.concatenate([mem_w_k[l], mem_w_v[l]], axis=1).astype(BF16)
        kv_mem = proj_heads(memf, w_kv_mem, ones_kv, tm=min(512, batch * mem_len), tn=512)
        xf = mem_attn_ln(xf, mem_w_q[l].astype(BF16), kv_mem, mem_w_o[l].astype(BF16),
                         row(ln_mem_g[l]), row(ln_mem_b[l]), alpha=alpha, seq=seq, mem_len=mem_len, tm=tm_in)

        wr = moe_w_router[l]
        wr_hi = wr.astype(BF16)
        wr_lo = (wr - wr_hi.astype(F32)).astype(BF16)
        route, x_packed, counts = router(xf, jnp.stack([wr_hi, wr_lo]), row(moe_b_router[l]),
                                         tri_router, tm=tm_in)
        padded = (counts[0].astype(jnp.int32) + bm - 1) // bm * bm
        pad_end = jnp.cumsum(padded)
        pad_start = pad_end - padded
        idx4 = route[:, 0:TOP_K].astype(jnp.int32)
        rank4 = route[:, 2 * TOP_K:3 * TOP_K].astype(jnp.int32)
        start4 = jnp.sum(jnp.where(idx4[:, :, None] == jnp.arange(N_EXPERTS, dtype=jnp.int32), pad_start, 0), axis=2)
        pos4 = start4 + rank4
        blk_row0 = jnp.arange(n_blocks, dtype=jnp.int32)[:, None] * bm
        blk_e = jnp.minimum(jnp.sum((pad_end[None, :] <= blk_row0).astype(jnp.int32), axis=1), N_EXPERTS - 1)
        blk_e = jnp.concatenate([blk_e, pad_end[-1:] // bm]).astype(jnp.int32)
        pos_kmajor = pos4.T.reshape(-1)
        x_rows = sc_scatter_rows(x_packed, pos_kmajor, n_rows, copies=TOP_K, chunk=sc_chunk)
        y_rows = experts(x_rows, blk_e, moe_w_gate, b_gate4, moe_w_up, b_up4, moe_w_down, b_down4,
                         layer=l, bm=bm)
        y4 = sc_gather_rows(y_rows, pos_kmajor, chunk=sc_chunk).reshape(TOP_K, t, D_MODEL // 2)
        xf = moe_ln(xf, y4, route, row(ln_moe_g[l]), row(ln_moe_b[l]), alpha=alpha, tm=tm_ln)

    return xf.reshape(batch, seq, D_MODEL)


def kernel(x, mem, w_in, mla_q_norm, mla_w_q_up, mla_kv_norm, mla_w_kv_up, nsa_pe_k, nsa_pe_v, nsa_w1_k, nsa_w1_v, nsa_w2_k, nsa_w2_v, w_branch, b_merge, w_out, ln_mix_g, ln_mix_b, mem_w_q, mem_w_k, mem_w_v, mem_w_o, ln_mem_g, ln_mem_b, moe_w_router, moe_b_router, moe_w_gate, moe_b_gate, moe_w_up, moe_b_up, moe_w_down, moe_b_down, ln_moe_g, ln_moe_b):
    return _forward(x, mem, w_in, mla_q_norm, mla_w_q_up, mla_kv_norm, mla_w_kv_up,
                    nsa_pe_k, nsa_pe_v, nsa_w1_k, nsa_w1_v, nsa_w2_k, nsa_w2_v,
                    w_branch, b_merge, w_out, ln_mix_g, ln_mix_b,
                    mem_w_q, mem_w_k, mem_w_v, mem_w_o, ln_mem_g, ln_mem_b,
                    moe_w_router, moe_b_router, moe_w_gate, moe_b_gate, moe_w_up, moe_b_up,
                    moe_w_down, moe_b_down, ln_moe_g, ln_moe_b)
```

```python
import functools

import numpy as np
import jax
import jax.numpy as jnp
from jax import lax
from jax.experimental import pallas as pl
from jax.experimental.pallas import tpu as pltpu
from jax.experimental.pallas import tpu_sc as plsc

F32 = jnp.float32
BF16 = jnp.bfloat16

D_MODEL = 2048
HEAD_DIM = 128
MLA_HEADS = 8
MLA_Q_RANK = 512
MLA_KV_RANK = 256
MLA_NOPE = 128
MLA_ROPE = 64
ROPE_THETA = 10000.0
NSA_HEADS = 8
NSA_GROUPS = 2
NSA_HG = NSA_HEADS // NSA_GROUPS
NSA_CMP_LEN = 32
NSA_CMP_STRIDE = 16
NSA_SEL_LEN = 64
NSA_TOPK = 16
NSA_WINDOW = 512
SB_HEADS = 8
MEM_HEADS = 4
N_EXPERTS = 32
TOP_K = 4
D_EXPERT = 512
SWIGLU_LIMIT = 7.0
SWIGLU_ALPHA = 1.702
N_BRANCH = 3
BRANCH_WIDTH = 1024
LN_EPS = 1e-5
RMS_EPS = 1e-6
NEG = -1e30
BIG = 1e30
SB_UNDERFLOW_LOG = -100.0

OFF_CQ = 0
OFF_CKV = 512
OFF_KR = 768
OFF_NSA_Q = 832
OFF_NSA_KV = 1856
OFF_NSA_GATE = 3392
OFF_SB = 3416
OFF_MERGE = 6488
MLA_IN_WIDTH = MLA_Q_RANK + MLA_KV_RANK + 2 * 128 + NSA_GROUPS * 128

VMEM_LIMIT_V7X = 56 * 1024 * 1024
MOE_BLOCK_ROWS = 512
SC_CORES_V7X = 2
SC_SUBCORES_V7X = 16


def _cp(sem, vmem=VMEM_LIMIT_V7X):
    return pltpu.CompilerParams(dimension_semantics=sem, vmem_limit_bytes=vmem)


def _dot(a, b):
    return jnp.dot(a, b, preferred_element_type=F32)


def _dot_nt(a, b):
    return lax.dot_general(a, b, (((1,), (1,)), ((), ())), preferred_element_type=F32)


def _layer_norm(z, g, b):
    mu = jnp.mean(z, axis=-1, keepdims=True)
    zc = z - mu
    var = jnp.mean(zc * zc, axis=-1, keepdims=True)
    return zc * lax.rsqrt(var + LN_EPS) * g + b


def _rms_norm(z, g):
    return z * lax.rsqrt(jnp.mean(z * z, axis=-1, keepdims=True) + RMS_EPS) * g


def _pack_bf16_pairs(z):
    n = z.shape[1] // 2
    bits = pltpu.bitcast(z.astype(BF16).astype(F32), jnp.uint32)
    return lax.shift_right_logical(bits[:, :n], jnp.uint32(16)) | (bits[:, n:] & jnp.uint32(0xFFFF0000))


def _unpack_bf16_pairs(w):
    lo = pltpu.bitcast(lax.shift_left(w, jnp.uint32(16)), F32)
    hi = pltpu.bitcast(w & jnp.uint32(0xFFFF0000), F32)
    return lo, hi


def _proj_heads_kernel(a_ref, w_ref, s_ref, o_ref, abf_ref, *, n_heads_per_tile):
    @pl.when(pl.program_id(1) == 0)
    def _():
        abf_ref[...] = a_ref[...].astype(BF16)

    acc = _dot(abf_ref[...], w_ref[...]) * s_ref[...]
    for c in range(n_heads_per_tile):
        o_ref[c] = acc[:, c * HEAD_DIM:(c + 1) * HEAD_DIM].astype(o_ref.dtype)


def proj_heads(a, w, scale, *, tm, tn):
    m, k = a.shape
    n = w.shape[1]
    hpt = tn // HEAD_DIM
    return pl.pallas_call(
        functools.partial(_proj_heads_kernel, n_heads_per_tile=hpt),
        out_shape=jax.ShapeDtypeStruct((n // HEAD_DIM, m, HEAD_DIM), BF16),
        grid=(m // tm, n // tn),
        in_specs=[
            pl.BlockSpec((tm, k), lambda i, j: (i, 0)),
            pl.BlockSpec((k, tn), lambda i, j: (0, j)),
            pl.BlockSpec((1, tn), lambda i, j: (0, j)),
        ],
        out_specs=pl.BlockSpec((hpt, tm, HEAD_DIM), lambda i, j: (j, i, 0)),
        scratch_shapes=[pltpu.VMEM((tm, k), BF16)],
        compiler_params=_cp(("arbitrary", "arbitrary")),
    )(a, w, scale)


def _mla_in_kernel(x_ref, w_ref, qg_ref, kg_ref, wq_ref, wkv_ref, cos_ref, sin_ref,
                   q_ref, k_ref, v_ref, g_ref):
    xb = x_ref[...].astype(BF16)
    h = _dot(xb, w_ref[...])
    c0, c1 = MLA_Q_RANK, MLA_Q_RANK + MLA_KV_RANK
    cq = h[:, 0:c0]
    ckv = h[:, c0:c1]
    kr1 = h[:, c1:c1 + 128]
    kr2 = h[:, c1 + 128:c1 + 256]
    g_ref[...] = jax.nn.sigmoid(h[:, c1 + 256:MLA_IN_WIDTH])
    cos = cos_ref[...]
    sin = sin_ref[...]
    scale = (MLA_NOPE + MLA_ROPE) ** -0.5
    hw = MLA_HEADS * 128
    nq = _rms_norm(cq, qg_ref[...]).astype(BF16)
    q3 = _dot(nq, wq_ref[...])
    for hh in range(MLA_HEADS):
        lo, hi = hh * 128, (hh + 1) * 128
        q_ref[hh, :, 0:128] = (q3[:, lo:hi] * scale).astype(BF16)
        rot = q3[:, hw + lo:hw + hi] * cos + q3[:, 2 * hw + lo:2 * hw + hi] * sin
        q_ref[hh, :, 128:256] = (rot * scale).astype(BF16)
    nkv = _rms_norm(ckv, kg_ref[...]).astype(BF16)
    kv = _dot(nkv, wkv_ref[...])
    krot = (kr1 * cos + kr2 * sin).astype(BF16)
    for hh in range(MLA_HEADS):
        lo, hi = hh * 128, (hh + 1) * 128
        k_ref[hh, :, 0:128] = kv[:, lo:hi].astype(BF16)
        k_ref[hh, :, 128:256] = krot
        v_ref[hh] = kv[:, hw + lo:hw + hi].astype(BF16)


def mla_in(x, w_mla, qg, kg, wq3, wkv, cos128, sin128, *, seq, tm):
    t = x.shape[0]
    npos = seq // tm
    full = lambda shape: pl.BlockSpec(shape, lambda i: (0,) * len(shape))
    return pl.pallas_call(
        _mla_in_kernel,
        out_shape=(
            jax.ShapeDtypeStruct((MLA_HEADS, t, 256), BF16),
            jax.ShapeDtypeStruct((MLA_HEADS, t, 256), BF16),
            jax.ShapeDtypeStruct((MLA_HEADS, t, 128), BF16),
            jax.ShapeDtypeStruct((t, 256), F32),
        ),
        grid=(t // tm,),
        in_specs=[
            pl.BlockSpec((tm, D_MODEL), lambda i: (i, 0)),
            full((D_MODEL, MLA_IN_WIDTH)),
            full((1, MLA_Q_RANK)),
            full((1, MLA_KV_RANK)),
            full((MLA_Q_RANK, 3 * MLA_HEADS * 128)),
            full((MLA_KV_RANK, 2 * MLA_HEADS * 128)),
            pl.BlockSpec((tm, 128), lambda i: (i % npos, 0)),
            pl.BlockSpec((tm, 128), lambda i: (i % npos, 0)),
        ],
        out_specs=(
            pl.BlockSpec((MLA_HEADS, tm, 256), lambda i: (0, i, 0)),
            pl.BlockSpec((MLA_HEADS, tm, 256), lambda i: (0, i, 0)),
            pl.BlockSpec((MLA_HEADS, tm, 128), lambda i: (0, i, 0)),
            pl.BlockSpec((tm, 256), lambda i: (i, 0)),
        ),
        compiler_params=_cp(("arbitrary",)),
    )(x, w_mla, qg, kg, wq3, wkv, cos128, sin128)


def _mla_attn_kernel(q_ref, k_ref, v_ref, o_ref, s_ref, *, tq, heads):
    qi = pl.program_id(2)

    def scores(h, kt):
        k0 = pl.multiple_of(kt * tq, tq)
        return _dot_nt(q_ref[h], k_ref[h, pl.ds(k0, tq), :])

    def consume(h, kt, s, carry, diag):
        m, l, acc = carry
        k0 = pl.multiple_of(kt * tq, tq)
        v = v_ref[h, pl.ds(k0, tq), :]
        if diag:
            row = lax.broadcasted_iota(jnp.int32, (tq, tq), 0)
            col = lax.broadcasted_iota(jnp.int32, (tq, tq), 1)
            s = jnp.where(col <= row, s, NEG)
        m_new = jnp.maximum(m, jnp.max(s, axis=1, keepdims=True))
        alpha = jnp.exp(m - m_new)
        p = jnp.exp((s - m_new).astype(BF16))
        l = alpha * l + jnp.sum(p.astype(F32), axis=1, keepdims=True)
        acc = alpha * acc + _dot(p, v)
        return m_new, l, acc

    def fill(slot, kt):
        for h in range(heads):
            s_ref[slot, h] = scores(h, kt)

    def drain(slot, kt, carries, diag):
        return tuple(consume(h, kt, s_ref[slot, h], carries[h], diag) for h in range(heads))

    def body(j, carries):
        kt = 2 * j
        fill(1, kt + 1)
        carries = drain(0, kt, carries, False)
        fill(0, kt + 2)
        return drain(1, kt + 1, carries, False)

    init = (jnp.full((tq, 1), NEG, F32), jnp.zeros((tq, 1), F32), jnp.zeros((tq, 128), F32))
    fill(0, 0)
    carries = lax.fori_loop(0, qi // 2, body, (init,) * heads)

    def even_tail(carries):
        return drain(0, qi, carries, True)

    def odd_tail(carries):
        fill(1, qi)
        return drain(1, qi, drain(0, qi - 1, carries, False), True)

    carries = lax.cond(lax.rem(qi, 2) == 1, odd_tail, even_tail, carries)
    for h, (_, l, acc) in enumerate(carries):
        o_ref[:, h * 128:(h + 1) * 128] = (acc / l).astype(o_ref.dtype)


def mla_attn(q, k, v, *, batch, seq, tq, heads):
    nq = seq // tq
    q4 = q.reshape(MLA_HEADS, batch, seq, 256)
    k4 = k.reshape(MLA_HEADS, batch, seq, 256)
    v4 = v.reshape(MLA_HEADS, batch, seq, 128)
    return pl.pallas_call(
        functools.partial(_mla_attn_kernel, tq=tq, heads=heads),
        out_shape=jax.ShapeDtypeStruct((batch * seq, MLA_HEADS * 128), BF16),
        grid=(MLA_HEADS // heads, batch, nq),
        in_specs=[
            pl.BlockSpec((heads, None, tq, 256), lambda h, b, i: (h, b, i, 0)),
            pl.BlockSpec((heads, None, seq, 256), lambda h, b, i: (h, b, 0, 0)),
            pl.BlockSpec((heads, None, seq, 128), lambda h, b, i: (h, b, 0, 0)),
        ],
        out_specs=pl.BlockSpec((tq, heads * 128), lambda h, b, i: (b * nq + i, h)),
        scratch_shapes=[pltpu.VMEM((2, heads, tq, tq), F32)],
        compiler_params=_cp(("arbitrary", "arbitrary", "arbitrary")),
    )(q4, k4, v4)


def _sb_attn_kernel(q_ref, k_ref, v_ref, u_ref, o_ref, *, tq, heads):
    qi = pl.program_id(2)
    u = u_ref[...]

    def head_step(h, kt, carry, diag):
        run, acc = carry
        k0 = pl.multiple_of(kt * tq, tq)
        k = k_ref[h, pl.ds(k0, tq), :]
        v = v_ref[h, pl.ds(k0, tq), :]
        z = _dot_nt(q_ref[h], k)
        l1m = -(jnp.maximum(z, 0.0) + jnp.log(1.0 + jnp.exp(-jnp.abs(z))))
        if diag:
            row = lax.broadcasted_iota(jnp.int32, (tq, tq), 0)
            col = lax.broadcasted_iota(jnp.int32, (tq, tq), 1)
            strict = col < row
            l1m_m = jnp.where(strict, l1m, 0.0)
        else:
            l1m_m = l1m
        hi = l1m_m.astype(BF16)
        lo = (l1m_m - hi.astype(F32)).astype(BF16)
        between = _dot(hi, u) + _dot(lo, u)
        a = jnp.exp(z + l1m + between + run)
        if diag:
            a = jnp.where(strict, a, 0.0)
        acc = acc + _dot(a.astype(BF16), v)
        run = run + between[:, 0:1] + l1m_m[:, 0:1]
        return run, acc

    def step(kt, carries, diag):
        return tuple(head_step(h, kt, carries[h], diag) for h in range(heads))

    init = (jnp.zeros((tq, 1), F32), jnp.zeros((tq, 128), F32))
    carries = step(qi, (init,) * heads, True)

    def more(c):
        j, carries = c
        top = carries[0][0]
        for run, _ in carries[1:]:
            top = jnp.maximum(top, run)
        return (j < qi) & (jnp.max(top) > SB_UNDERFLOW_LOG)

    def body(c):
        j, carries = c
        return j + 1, step(qi - 1 - j, carries, False)

    _, carries = lax.while_loop(more, body, (jnp.int32(0), carries))
    for h, (_, acc) in enumerate(carries):
        o_ref[:, h * HEAD_DIM:(h + 1) * HEAD_DIM] = acc.astype(o_ref.dtype)


def sb_attn(hm, u, *, head0, batch, seq, tq, heads):
    nq = seq // tq
    assert head0 % heads == 0 and SB_HEADS % heads == 0
    hm4 = hm.reshape(hm.shape[0], batch, seq, HEAD_DIM)
    blk0 = head0 // heads
    per_part = SB_HEADS // heads
    return pl.pallas_call(
        functools.partial(_sb_attn_kernel, tq=tq, heads=heads),
        out_shape=jax.ShapeDtypeStruct((batch * seq, SB_HEADS * HEAD_DIM), BF16),
        grid=(per_part, batch, nq),
        in_specs=[
            pl.BlockSpec((heads, None, tq, HEAD_DIM), lambda h, b, i: (blk0 + h, b, i, 0)),
            pl.BlockSpec((heads, None, seq, HEAD_DIM), lambda h, b, i: (blk0 + per_part + h, b, 0, 0)),
            pl.BlockSpec((heads, None, seq, HEAD_DIM), lambda h, b, i: (blk0 + 2 * per_part + h, b, 0, 0)),
            pl.BlockSpec((tq, tq), lambda h, b, i: (0, 0)),
        ],
        out_specs=pl.BlockSpec((tq, heads * HEAD_DIM), lambda h, b, i: (b * nq + i, h)),
        compiler_params=_cp(("arbitrary", "arbitrary", "arbitrary")),
    )(hm4, hm4, hm4, u)


def _nsa_cmp_kernel(c_ref, w1_ref, pe_ref, w2_ref, o_ref, *, nc):
    c = c_ref[...]
    half = NSA_CMP_STRIDE * HEAD_DIM
    a1 = _dot(c, w1_ref[0:half, :])
    a2 = _dot(c, w1_ref[half:2 * half, :])
    pc = _dot(pe_ref[...], w1_ref[...])[0:1, :]
    pre = a1 + pltpu.roll(a2, nc - 1, 0) + pc
    act = 0.5 * pre * (1.0 + jnp.tanh(0.7978845608028654 * (pre + 0.044715 * (pre * pre * pre))))
    o_ref[...] = _dot(act.astype(BF16), w2_ref[...]).astype(BF16)


def nsa_compress(cmp_heads, w1, pe, w2, *, batch, seq):
    nc = seq // NSA_CMP_STRIDE
    hm4 = cmp_heads.reshape(4, batch, nc, NSA_CMP_STRIDE * HEAD_DIM)
    return pl.pallas_call(
        functools.partial(_nsa_cmp_kernel, nc=nc),
        out_shape=jax.ShapeDtypeStruct((4, batch, nc, HEAD_DIM), BF16),
        grid=(4, batch),
        in_specs=[
            pl.BlockSpec((None, None, nc, NSA_CMP_STRIDE * HEAD_DIM), lambda c, b: (c, b, 0, 0)),
            pl.BlockSpec((None, NSA_CMP_LEN * HEAD_DIM, HEAD_DIM), lambda c, b: (c // 2, 0, 0)),
            pl.BlockSpec((None, 8, NSA_CMP_LEN * HEAD_DIM), lambda c, b: (c // 2, 0, 0)),
            pl.BlockSpec((None, HEAD_DIM, HEAD_DIM), lambda c, b: (c // 2, 0, 0)),
        ],
        out_specs=pl.BlockSpec((None, None, nc, HEAD_DIM), lambda c, b: (c, b, 0, 0)),
        compiler_params=_cp(("arbitrary", "arbitrary")),
    )(hm4, w1, pe, w2)


def _nsa_attn_kernel(q_ref, qa_ref, kc_ref, vc_ref, ks_ref, vs_ref, kw_ref, vw_ref, g_ref, ovt_ref, e_ref,
                     kpos_ref, cpos_ref, o_ref, ss_ref, *, tq, tk, seq, n_sel, n_top):
    qi = pl.program_id(1)
    t0 = qi * tq
    rows = NSA_HG * tq
    nc = seq // NSA_CMP_STRIDE
    groups = range(NSA_GROUPS)

    rid = lax.broadcasted_iota(jnp.int32, (rows, 1), 0)
    trow = t0 + lax.bitwise_and(rid, tq - 1)

    def masked_softmax(s, valid):
        sm = jnp.where(valid, s, NEG)
        m = jnp.max(sm, axis=1, keepdims=True)
        e = jnp.where(valid, jnp.exp(sm - m), 0.0)
        d = jnp.sum(e, axis=1, keepdims=True)
        return e * (1.0 / jnp.where(d > 0.0, d, 1.0))

    n_i = lax.broadcasted_iota(jnp.int32, (1, nc), 1)
    end = n_i * NSA_CMP_STRIDE + (NSA_CMP_LEN - 1)
    cur = lax.shift_right_logical(t0 + lax.broadcasted_iota(jnp.int32, (1, tq), 1),
                                  int(np.log2(NSA_SEL_LEN)))
    blk = lax.broadcasted_iota(jnp.int32, (n_sel, tq), 0)
    forced = (blk == 0) | (blk == cur) | (blk == cur - 1)
    sub = lax.broadcasted_iota(jnp.int32, (8, tq), 0)
    eye = (lax.broadcasted_iota(jnp.int32, (n_sel, n_sel), 0)
           == lax.broadcasted_iota(jnp.int32, (n_sel, n_sel), 1)).astype(F32).astype(BF16)
    ovt = ovt_ref[...]

    def front(g):
        q = jnp.concatenate([q_ref[NSA_HG * g:NSA_HG * (g + 1)].reshape(rows, HEAD_DIM), qa_ref[g]], axis=1)

        s_c = _dot_nt(q, jnp.concatenate([kc_ref[g], cpos_ref[...]], axis=1))
        p_c = masked_softmax(s_c, end <= trow)
        o_c = _dot(p_c.astype(BF16), vc_ref[g])

        psum = p_c[0:tq] + p_c[tq:2 * tq] + p_c[2 * tq:3 * tq] + p_c[3 * tq:4 * tq]
        p_hi = psum.astype(BF16)
        p_lo = (psum - p_hi.astype(F32)).astype(BF16)
        imp = _dot_nt(ovt, p_hi) + _dot_nt(ovt, p_lo)
        key = jnp.where(blk > cur, -BIG, jnp.where(forced, BIG, imp))
        chunks = [key[8 * r:8 * r + 8, :] for r in range(n_sel // 8)]
        ranks = [jnp.zeros((8, tq), F32) for _ in chunks]
        for i in range(n_sel):
            vi = key[i:i + 1, :]
            for r, kc in enumerate(chunks):
                gt = jnp.where(vi > kc, 1.0, 0.0)
                if r < i // 8:
                    win = gt
                else:
                    ge = jnp.where(vi >= kc, 1.0, 0.0)
                    win = ge if r > i // 8 else jnp.where(sub > i % 8, ge, gt)
                ranks[r] = ranks[r] + win
        rank = jnp.concatenate(ranks, axis=0)
        selm_t = jnp.where((rank < float(n_top)) & (blk <= cur), 1.0, 0.0).astype(BF16)
        selm = lax.dot_general(selm_t, eye, (((0,), (0,)), ((), ())), preferred_element_type=F32).astype(BF16)
        return q, selm, o_c

    fronts = [front(g) for g in groups]

    def sel_fill(slot, kt):
        k0 = pl.multiple_of(kt * tk, tk)
        kp = kpos_ref[pl.ds(k0, tk), :]
        for g in groups:
            q, selm, _ = fronts[g]
            kk = jnp.concatenate([ks_ref[g, pl.ds(k0, tk), :], kp], axis=1)
            mex = _dot(selm, e_ref[kt])
            bias = (mex - 1.0) * BIG
            ss_ref[slot, g] = _dot_nt(q, kk) + jnp.concatenate([bias] * NSA_HG, axis=0)

    def sel_drain_group(g, slot, kt, carry, diag):
        m, l, acc = carry
        k0 = pl.multiple_of(kt * tk, tk)
        vv = vs_ref[g, pl.ds(k0, tk), :]
        sm = ss_ref[slot, g]
        if diag:
            spos = k0 + lax.broadcasted_iota(jnp.int32, (1, tk), 1)
            sm = jnp.where(spos <= trow, sm, NEG)
        m_new = jnp.maximum(m, jnp.max(sm, axis=1, keepdims=True))
        alpha = jnp.exp(m - m_new)
        p = jnp.exp((sm - m_new).astype(BF16))
        l = alpha * l + jnp.sum(p.astype(F32), axis=1, keepdims=True)
        acc = alpha * acc + _dot(p, vv)
        return m_new, l, acc

    def sel_drain(slot, kt, carries, diag):
        return tuple(sel_drain_group(g, slot, kt, carries[g], diag) for g in groups)

    def sel_body(j, carries):
        kt = 2 * j
        sel_fill(1, kt + 1)
        carries = sel_drain(0, kt, carries, False)
        sel_fill(0, kt + 2)
        return sel_drain(1, kt + 1, carries, False)

    kt_last = t0 // tk
    init = (jnp.full((rows, 1), NEG, F32), jnp.zeros((rows, 1), F32), jnp.zeros((rows, HEAD_DIM), F32))
    sel_fill(0, 0)
    carries = lax.fori_loop(0, kt_last // 2, sel_body, (init,) * NSA_GROUPS)

    def even_tail(carries):
        return sel_drain(0, kt_last, carries, True)

    def odd_tail(carries):
        sel_fill(1, kt_last)
        return sel_drain(1, kt_last, sel_drain(0, kt_last - 1, carries, False), True)

    carries = lax.cond(lax.rem(kt_last, 2) == 1, odd_tail, even_tail, carries)

    wk = NSA_WINDOW + tq
    ks0 = pl.multiple_of(jnp.maximum(t0 - NSA_WINDOW, 0), tq)
    kpw = kpos_ref[pl.ds(ks0, wk), :]
    wpos = ks0 + lax.broadcasted_iota(jnp.int32, (1, wk), 1)
    dw = trow - wpos
    in_window = pltpu.bitcast(dw, jnp.uint32) < jnp.uint32(NSA_WINDOW)
    gt = g_ref[...]
    for g in groups:
        q, _, o_c = fronts[g]
        _, l_s, acc_s = carries[g]
        o_s = acc_s * (1.0 / l_s)
        kw = jnp.concatenate([kw_ref[g, pl.ds(ks0, wk), :], kpw], axis=1)
        vw = vw_ref[g, pl.ds(ks0, wk), :]
        sm_w = jnp.where(in_window, _dot_nt(q, kw), NEG)
        p_w = jnp.exp((sm_w - jnp.max(sm_w, axis=1, keepdims=True)).astype(BF16))
        o_w = _dot(p_w, vw) * (1.0 / jnp.sum(p_w.astype(F32), axis=1, keepdims=True))
        for hg in range(NSA_HG):
            sl = slice(hg * tq, (hg + 1) * tq)
            c0 = 128 * g + 3 * hg
            o = (gt[:, c0:c0 + 1] * o_c[sl] + gt[:, c0 + 1:c0 + 2] * o_s[sl] + gt[:, c0 + 2:c0 + 3] * o_w[sl])
            h = NSA_HG * g + hg
            o_ref[:, h * HEAD_DIM:(h + 1) * HEAD_DIM] = o.astype(o_ref.dtype)


def nsa_attn(hm, cmp, gates, consts, *, q_head0, kv_head0, batch, seq, tq, tk):
    ov, e, qa, kpos, cpos = consts
    nq = seq // tq
    nc = seq // NSA_CMP_STRIDE
    n_sel = seq // NSA_SEL_LEN
    n_top = min(NSA_TOPK, n_sel)
    assert tk % tq == 0 and seq % tk == 0 and seq >= NSA_WINDOW + tq and NSA_WINDOW % tq == 0
    assert q_head0 % NSA_HEADS == 0 and kv_head0 % NSA_GROUPS == 0
    hm4 = hm.reshape(hm.shape[0], batch, seq, HEAD_DIM)
    kv_spec = lambda off: pl.BlockSpec((NSA_GROUPS, None, seq, HEAD_DIM),
                                       lambda b, i: ((kv_head0 + off) // NSA_GROUPS, b, 0, 0))
    return pl.pallas_call(
        functools.partial(_nsa_attn_kernel, tq=tq, tk=tk, seq=seq, n_sel=n_sel, n_top=n_top),
        out_shape=jax.ShapeDtypeStruct((batch * seq, NSA_HEADS * HEAD_DIM), BF16),
        grid=(batch, nq),
        in_specs=[
            pl.BlockSpec((NSA_HEADS, None, tq, HEAD_DIM), lambda b, i: (q_head0 // NSA_HEADS, b, i, 0)),
            pl.BlockSpec((NSA_GROUPS, NSA_HG * tq, 128), lambda b, i: (0, 0, 0)),
            pl.BlockSpec((NSA_GROUPS, None, nc, HEAD_DIM), lambda b, i: (0, b, 0, 0)),
            pl.BlockSpec((NSA_GROUPS, None, nc, HEAD_DIM), lambda b, i: (1, b, 0, 0)),
            kv_spec(4), kv_spec(6), kv_spec(8), kv_spec(10),
            pl.BlockSpec((tq, NSA_GROUPS * 128), lambda b, i: (b * nq + i, 0)),
            pl.BlockSpec((n_sel, nc), lambda b, i: (0, 0)),
            pl.BlockSpec((seq // tk, n_sel, tk), lambda b, i: (0, 0, 0)),
            pl.BlockSpec((seq, 128), lambda b, i: (0, 0)),
            pl.BlockSpec((nc, 128), lambda b, i: (0, 0)),
        ],
        out_specs=pl.BlockSpec((tq, NSA_HEADS * HEAD_DIM), lambda b, i: (b * nq + i, 0)),
        scratch_shapes=[pltpu.VMEM((2, NSA_GROUPS, NSA_HG * tq, tk), F32)],
        compiler_params=_cp(("arbitrary", "arbitrary")),
    )(hm4, qa, cmp, cmp, hm4, hm4, hm4, hm4, gates, ov, e, kpos, cpos)


def _merge_kernel(x_ref, oa_ref, ob_ref, oc_ref, wm_ref, wb_ref, bm_ref, y_ref, xb_ref):
    @pl.when(pl.program_id(1) == 0)
    def _():
        xb_ref[...] = x_ref[...].astype(BF16)

    xb = xb_ref[...]
    acc = None
    for br, o_ref in enumerate((oa_ref, ob_ref, oc_ref)):
        gate = jax.nn.sigmoid(_dot(xb, wm_ref[br]) + bm_ref[br])
        term = gate * _dot(o_ref[...], wb_ref[br])
        acc = term if acc is None else acc + term
    y_ref[...] = acc.astype(y_ref.dtype)


def merge_branches(x, o_a, o_b, o_c, wm, wb, bm, *, tm, tn):
    t = x.shape[0]
    o_spec = pl.BlockSpec((tm, BRANCH_WIDTH), lambda i, j: (i, 0))
    return pl.pallas_call(
        _merge_kernel,
        out_shape=jax.ShapeDtypeStruct((t, D_MODEL), BF16),
        grid=(t // tm, D_MODEL // tn),
        in_specs=[
            pl.BlockSpec((tm, D_MODEL), lambda i, j: (i, 0)),
            o_spec, o_spec, o_spec,
            pl.BlockSpec((N_BRANCH, D_MODEL, tn), lambda i, j: (0, 0, j)),
            pl.BlockSpec((N_BRANCH, BRANCH_WIDTH, tn), lambda i, j: (0, 0, j)),
            pl.BlockSpec((N_BRANCH, 1, tn), lambda i, j: (0, 0, j)),
        ],
        out_specs=pl.BlockSpec((tm, tn), lambda i, j: (i, j)),
        scratch_shapes=[pltpu.VMEM((tm, D_MODEL), BF16)],
        compiler_params=_cp(("arbitrary", "arbitrary")),
    )(x, o_a, o_b, o_c, wm, wb, bm)


def _out_ln_kernel(y_ref, w_ref, x_ref, g_ref, b_ref, o_ref, *, alpha):
    half = y_ref.shape[0] // 2
    for r in range(2):
        rs = slice(r * half, (r + 1) * half)
        h = _dot(y_ref[rs, :], w_ref[...])
        o_ref[rs, :] = _layer_norm(alpha * x_ref[rs, :] + h, g_ref[...], b_ref[...])


def out_ln(y, w, x, g, b, *, alpha, tm):
    t = x.shape[0]
    return pl.pallas_call(
        functools.partial(_out_ln_kernel, alpha=alpha),
        out_shape=jax.ShapeDtypeStruct((t, D_MODEL), F32),
        grid=(t // tm,),
        in_specs=[
            pl.BlockSpec((tm, D_MODEL), lambda i: (i, 0)),
            pl.BlockSpec((D_MODEL, D_MODEL), lambda i: (0, 0)),
            pl.BlockSpec((tm, D_MODEL), lambda i: (i, 0)),
            pl.BlockSpec((1, D_MODEL), lambda i: (0, 0)),
            pl.BlockSpec((1, D_MODEL), lambda i: (0, 0)),
        ],
        out_specs=pl.BlockSpec((tm, D_MODEL), lambda i: (i, 0)),
        compiler_params=_cp(("arbitrary",)),
    )(y, w, x, g, b)


def _mem_attn_kernel(x_ref, wq_ref, k_ref, v_ref, wo_ref, g_ref, b_ref, o_ref, *, alpha):
    x = x_ref[...]
    q = _dot(x.astype(BF16), wq_ref[...]) * (HEAD_DIM ** -0.5)
    outs = []
    for h in range(MEM_HEADS):
        qh = q[:, h * HEAD_DIM:(h + 1) * HEAD_DIM].astype(BF16)
        s = _dot_nt(qh, k_ref[h])
        m = jnp.max(s, axis=1, keepdims=True)
        e = jnp.exp(s - m)
        p = e * (1.0 / jnp.sum(e, axis=1, keepdims=True))
        outs.append(_dot(p.astype(BF16), v_ref[h]).astype(BF16))
    o = jnp.concatenate(outs, axis=1)
    h_out = _dot(o, wo_ref[...])
    o_ref[...] = _layer_norm(alpha * x + h_out, g_ref[...], b_ref[...])


def mem_attn_ln(x, wq, kv, wo, g, b, *, alpha, seq, mem_len, tm):
    t = x.shape[0]
    per_b = seq // tm
    kv4 = kv.reshape(2 * MEM_HEADS, t // seq, mem_len, HEAD_DIM)
    width = MEM_HEADS * HEAD_DIM
    return pl.pallas_call(
        functools.partial(_mem_attn_kernel, alpha=alpha),
        out_shape=jax.ShapeDtypeStruct((t, D_MODEL), F32),
        grid=(t // tm,),
        in_specs=[
            pl.BlockSpec((tm, D_MODEL), lambda i: (i, 0)),
            pl.BlockSpec((D_MODEL, width), lambda i: (0, 0)),
            pl.BlockSpec((MEM_HEADS, None, mem_len, HEAD_DIM), lambda i: (0, i // per_b, 0, 0)),
            pl.BlockSpec((MEM_HEADS, None, mem_len, HEAD_DIM), lambda i: (1, i // per_b, 0, 0)),
            pl.BlockSpec((width, D_MODEL), lambda i: (0, 0)),
            pl.BlockSpec((1, D_MODEL), lambda i: (0, 0)),
            pl.BlockSpec((1, D_MODEL), lambda i: (0, 0)),
        ],
        out_specs=pl.BlockSpec((tm, D_MODEL), lambda i: (i, 0)),
        compiler_params=_cp(("arbitrary",)),
    )(x, wq, kv4, kv4, wo, g, b)


def _router_kernel(x_ref, w_ref, b_ref, tri_ref, route_ref, xp_ref, count_ref):
    @pl.when(pl.program_id(0) == 0)
    def _():
        count_ref[...] = jnp.zeros_like(count_ref)

    x = x_ref[...]
    xp_ref[...] = _pack_bf16_pairs(x)
    xh = x.astype(BF16)
    xl = (x - xh.astype(F32)).astype(BF16)
    logits = _dot(xh, w_ref[0]) + _dot(xh, w_ref[1]) + _dot(xl, w_ref[0]) + b_ref[...]
    tm = logits.shape[0]
    lane = lax.broadcasted_iota(jnp.int32, (tm, N_EXPERTS), 1)
    work = logits
    hots, vals, idxs = [], [], []
    for _ in range(TOP_K):
        m = jnp.max(work, axis=1, keepdims=True)
        idx = jnp.min(jnp.where(work == m, lane, N_EXPERTS), axis=1, keepdims=True)
        hot = lane == idx
        hots.append(hot)
        vals.append(m)
        idxs.append(idx)
        work = jnp.where(hot, -jnp.inf, work)
    es = [jnp.exp(v - vals[0]) for v in vals]
    inv = 1.0 / (es[0] + es[1] + es[2] + es[3])
    mask = jnp.zeros((tm, N_EXPERTS), F32)
    for hot in hots:
        mask = mask + jnp.where(hot, 1.0, 0.0)
    before = count_ref[...] + _dot(tri_ref[...], mask.astype(BF16))
    count_ref[...] = count_ref[...] + jnp.sum(mask, axis=0, keepdims=True)
    col = lax.broadcasted_iota(jnp.int32, (tm, 128), 1)
    route = jnp.zeros((tm, 128), F32)
    for k in range(TOP_K):
        rank = jnp.sum(jnp.where(hots[k], before, 0.0), axis=1, keepdims=True)
        route = (route + jnp.where(col == k, idxs[k].astype(F32), 0.0)
                 + jnp.where(col == TOP_K + k, es[k] * inv, 0.0) + jnp.where(col == 2 * TOP_K + k, rank, 0.0))
    route_ref[...] = route


def router(x, w_hl, b, tri, *, tm):
    t = x.shape[0]
    return pl.pallas_call(
        _router_kernel,
        out_shape=(jax.ShapeDtypeStruct((t, 128), F32),
                   jax.ShapeDtypeStruct((t, D_MODEL // 2), jnp.uint32),
                   jax.ShapeDtypeStruct((1, N_EXPERTS), F32)),
        grid=(t // tm,),
        in_specs=[
            pl.BlockSpec((tm, D_MODEL), lambda i: (i, 0)),
            pl.BlockSpec((2, D_MODEL, N_EXPERTS), lambda i: (0, 0, 0)),
            pl.BlockSpec((1, N_EXPERTS), lambda i: (0, 0)),
            pl.BlockSpec((tm, tm), lambda i: (0, 0)),
        ],
        out_specs=(pl.BlockSpec((tm, 128), lambda i: (i, 0)),
                   pl.BlockSpec((tm, D_MODEL // 2), lambda i: (i, 0)),
                   pl.BlockSpec((1, N_EXPERTS), lambda i: (0, 0))),
        compiler_params=_cp(("arbitrary",)),
    )(x, w_hl, b, tri)


def sc_gather_rows(table, idx, *, chunk):
    n = idx.shape[0]
    d = table.shape[1]
    workers = SC_CORES_V7X * SC_SUBCORES_V7X
    per_w = n // workers
    assert n % (workers * chunk) == 0 and chunk % 8 == 0 and chunk <= 128
    mesh = plsc.VectorSubcoreMesh(core_axis_name="c", subcore_axis_name="s")

    @functools.partial(
        pl.kernel, mesh=mesh,
        out_type=jax.ShapeDtypeStruct((n, d), table.dtype),
        scratch_types=[pltpu.VMEM((chunk,), jnp.int32), pltpu.VMEM((chunk, d), table.dtype),
                       pltpu.SemaphoreType.DMA],
    )
    def gather(table_hbm, idx_hbm, out_hbm, idx_v, rows_v, sem):
        wid = lax.axis_index("s") * SC_CORES_V7X + lax.axis_index("c")
        base = wid * per_w

        @pl.loop(0, per_w // chunk)
        def _(j):
            off = pl.multiple_of(base + j * chunk, 8)
            pltpu.sync_copy(idx_hbm.at[pl.ds(off, chunk)], idx_v)
            pltpu.async_copy(table_hbm.at[idx_v], rows_v, sem).wait()
            pltpu.sync_copy(rows_v, out_hbm.at[pl.ds(off, chunk)])

    return gather(table, idx)


def sc_scatter_rows(rows, idx, n_out, *, copies, chunk):
    t, d = rows.shape
    workers = SC_CORES_V7X * SC_SUBCORES_V7X
    per_w = t // workers
    assert idx.shape == (copies * t,) and t % (workers * chunk) == 0 and chunk % 8 == 0 and chunk <= 128
    mesh = plsc.VectorSubcoreMesh(core_axis_name="c", subcore_axis_name="s")

    @functools.partial(
        pl.kernel, mesh=mesh,
        out_type=jax.ShapeDtypeStruct((n_out, d), rows.dtype),
        scratch_types=([pltpu.VMEM((chunk,), jnp.int32)] * copies + [pltpu.VMEM((chunk, d), rows.dtype)]
                       + [pltpu.SemaphoreType.DMA] * 3),
    )
    def scatter(rows_hbm, idx_hbm, out_hbm, *scratch):
        idx_vs, rows_v = scratch[:copies], scratch[copies]
        sem_rows, sem_idx, sem_out = scratch[copies + 1:]
        wid = lax.axis_index("s") * SC_CORES_V7X + lax.axis_index("c")
        base = wid * per_w

        @pl.loop(0, per_w // chunk)
        def _(j):
            off = pl.multiple_of(base + j * chunk, 8)
            loads = [pltpu.async_copy(rows_hbm.at[pl.ds(off, chunk)], rows_v, sem_rows)]
            for k in range(copies):
                src = idx_hbm.at[pl.ds(pl.multiple_of(k * t + off, 8), chunk)]
                loads.append(pltpu.async_copy(src, idx_vs[k], sem_idx))
            for cp in loads:
                cp.wait()
            stores = [pltpu.async_copy(rows_v, out_hbm.at[idx_vs[k]], sem_out) for k in range(copies)]
            for cp in stores:
                cp.wait()

    return scatter(rows, idx)


def _experts_kernel(be_ref, x_ref, wg_ref, bg_ref, wu_ref, bu_ref, wd_ref, bd_ref, y_ref,
                    wgb_ref, wub_ref, wdb_ref):
    i = pl.program_id(0)
    prev = be_ref[jnp.maximum(i - 1, 0)]
    n_used = be_ref[pl.num_programs(0)]

    @pl.when((i < n_used) & ((i == 0) | (be_ref[i] != prev)))
    def _():
        wgb_ref[...] = wg_ref[...].astype(BF16)
        wub_ref[...] = wu_ref[...].astype(BF16)
        wdb_ref[...] = wd_ref[...].astype(BF16)

    @pl.when(i < n_used)
    def _():
        x_lo, x_hi = _unpack_bf16_pairs(x_ref[...])
        xb = jnp.concatenate([x_lo.astype(BF16), x_hi.astype(BF16)], axis=1)
        g = jnp.minimum(_dot(xb, wgb_ref[...]) + bg_ref[...], SWIGLU_LIMIT)
        u = jnp.clip(_dot(xb, wub_ref[...]) + bu_ref[...], -SWIGLU_LIMIT, SWIGLU_LIMIT)
        hdn = (u + 1.0) * (g * jax.nn.sigmoid(SWIGLU_ALPHA * g))
        y_ref[...] = _pack_bf16_pairs(_dot(hdn.astype(BF16), wdb_ref[...]) + bd_ref[...])


def experts(x_rows, blk_e, wg, bg, wu, bu, wd, bd, *, layer, bm):
    n_rows, dp = x_rows.shape
    d, f = wg.shape[2], wg.shape[3]
    w_spec = lambda shape: pl.BlockSpec((None, None) + shape, lambda i, be: (layer, be[i], 0, 0))
    grid_spec = pltpu.PrefetchScalarGridSpec(
        num_scalar_prefetch=1,
        grid=(n_rows // bm,),
        in_specs=[
            pl.BlockSpec((bm, dp), lambda i, be: (i, 0)),
            w_spec((d, f)), w_spec((1, f)), w_spec((d, f)), w_spec((1, f)), w_spec((f, d)), w_spec((1, d)),
        ],
        out_specs=pl.BlockSpec((bm, dp), lambda i, be: (i, 0)),
        scratch_shapes=[pltpu.VMEM((d, f), BF16), pltpu.VMEM((d, f), BF16), pltpu.VMEM((f, d), BF16)],
    )
    return pl.pallas_call(
        _experts_kernel,
        out_shape=jax.ShapeDtypeStruct((n_rows, dp), jnp.uint32),
        grid_spec=grid_spec,
        compiler_params=_cp(("arbitrary",)),
    )(blk_e, x_rows, wg, bg, wu, bu, wd, bd)


def _moe_ln_kernel(x_ref, y_ref, w_ref, g_ref, b_ref, o_ref, *, alpha):
    w = w_ref[...]
    y_lo = y_hi = None
    for k in range(TOP_K):
        lo, hi = _unpack_bf16_pairs(y_ref[k])
        wk = w[:, TOP_K + k:TOP_K + k + 1]
        y_lo = wk * lo if y_lo is None else y_lo + wk * lo
        y_hi = wk * hi if y_hi is None else y_hi + wk * hi
    y = jnp.concatenate([y_lo, y_hi], axis=1)
    o_ref[...] = _layer_norm(alpha * x_ref[...] + y, g_ref[...], b_ref[...])


def moe_ln(x, y4, w4p, g, b, *, alpha, tm):
    t = x.shape[0]
    return pl.pallas_call(
        functools.partial(_moe_ln_kernel, alpha=alpha),
        out_shape=jax.ShapeDtypeStruct((t, D_MODEL), F32),
        grid=(t // tm,),
        in_specs=[
            pl.BlockSpec((tm, D_MODEL), lambda i: (i, 0)),
            pl.BlockSpec((TOP_K, tm, D_MODEL // 2), lambda i: (0, i, 0)),
            pl.BlockSpec((tm, 128), lambda i: (i, 0)),
            pl.BlockSpec((1, D_MODEL), lambda i: (0, 0)),
            pl.BlockSpec((1, D_MODEL), lambda i: (0, 0)),
        ],
        out_specs=pl.BlockSpec((tm, D_MODEL), lambda i: (i, 0)),
        compiler_params=_cp(("arbitrary",)),
    )(x, y4, w4p, g, b)


def _rope_tables(seq):
    inv = np.asarray(ROPE_THETA ** (-np.arange(0, MLA_ROPE, 2) / MLA_ROPE), np.float32)
    ang = jnp.arange(seq, dtype=F32)[:, None] * jnp.asarray(inv)[None, :]
    cos, sin = jnp.cos(ang), jnp.sin(ang)
    zeros = jnp.zeros((seq, 128 - MLA_ROPE), F32)
    return (jnp.concatenate([cos, cos, zeros], axis=1), jnp.concatenate([-sin, sin, zeros], axis=1))


def _nsa_constants(seq, tq, tk):
    nc = seq // NSA_CMP_STRIDE
    qa = np.zeros((NSA_GROUPS, NSA_HG * tq, 128), np.float32)
    for g in range(NSA_GROUPS):
        for hg in range(NSA_HG):
            slope = 2.0 ** (-8.0 * (g * NSA_HG + hg + 1) / NSA_HEADS)
            qa[g, hg * tq:(hg + 1) * tq, 0] = slope * NSA_SEL_LEN
            qa[g, hg * tq:(hg + 1) * tq, 1] = slope
            qa[g, hg * tq:(hg + 1) * tq, 2] = slope * NSA_CMP_STRIDE
            qa[g, hg * tq:(hg + 1) * tq, 3] = slope * (NSA_CMP_LEN - 1) / 2.0
    kpos = np.zeros((seq, 128), np.float32)
    kpos[:, 0] = np.arange(seq) // NSA_SEL_LEN
    kpos[:, 1] = np.arange(seq) % NSA_SEL_LEN
    cpos = np.zeros((nc, 128), np.float32)
    cpos[:, 2] = np.arange(nc)
    cpos[:, 3] = 1.0
    for arr in (qa, kpos, cpos):
        assert np.array_equal(arr.astype(BF16).astype(np.float32), arr)
    n_cmp = (seq - NSA_CMP_LEN) // NSA_CMP_STRIDE + 1
    n_sel = seq // NSA_SEL_LEN
    cs = np.arange(nc) * NSA_CMP_STRIDE
    ss = np.arange(n_sel) * NSA_SEL_LEN
    ov = np.clip(np.minimum(cs[:, None] + NSA_CMP_LEN, ss[None, :] + NSA_SEL_LEN)
                 - np.maximum(cs[:, None], ss[None, :]), 0, None) / NSA_CMP_LEN
    ov[n_cmp:] = 0.0
    e = (np.arange(seq)[None, :] // NSA_SEL_LEN == np.arange(n_sel)[:, None]).astype(np.float32)
    e = e.reshape(n_sel, seq // tk, tk).transpose(1, 0, 2)
    return tuple(jnp.asarray(a, BF16) for a in (ov.T, e, qa, kpos, cpos))


def _pad_cols(w, width):
    return jnp.pad(w, ((0, 0), (0, width - w.shape[1])))


def _swap_halves(w):
    half = w.shape[1] // 2
    return jnp.concatenate([w[:, half:], w[:, :half]], axis=1)


def _layer_weights(w_in, w_q_up, w_kv_up):
    kr = w_in[:, OFF_KR:OFF_NSA_Q]
    gate = w_in[:, OFF_NSA_GATE:OFF_SB]
    per_g = NSA_HG * 3
    w_mla = jnp.concatenate([
        w_in[:, OFF_CQ:OFF_KR],
        _pad_cols(kr, 128), _pad_cols(_swap_halves(kr), 128),
        _pad_cols(gate[:, :per_g], 128), _pad_cols(gate[:, per_g:], 128)], axis=1).astype(BF16)
    wq = w_q_up.reshape(MLA_Q_RANK, MLA_HEADS, MLA_NOPE + MLA_ROPE)
    rope = wq[:, :, MLA_NOPE:]
    rope_sw = jnp.concatenate([rope[:, :, MLA_ROPE // 2:], rope[:, :, :MLA_ROPE // 2]], axis=2)
    pad = ((0, 0), (0, 0), (0, 128 - MLA_ROPE))
    wq3 = jnp.concatenate([
        wq[:, :, :MLA_NOPE].reshape(MLA_Q_RANK, -1),
        jnp.pad(rope, pad).reshape(MLA_Q_RANK, -1),
        jnp.pad(rope_sw, pad).reshape(MLA_Q_RANK, -1)], axis=1).astype(BF16)
    wkv = w_kv_up.reshape(MLA_KV_RANK, MLA_HEADS, 2, 128)
    wkv = jnp.concatenate([wkv[:, :, 0].reshape(MLA_KV_RANK, -1),
                           wkv[:, :, 1].reshape(MLA_KV_RANK, -1)], axis=1).astype(BF16)
    w_heads = jnp.concatenate([w_in[:, OFF_NSA_Q:OFF_NSA_KV], w_in[:, OFF_SB:OFF_MERGE],
                               w_in[:, OFF_NSA_KV:OFF_NSA_GATE]], axis=1).astype(BF16)
    wm = w_in[:, OFF_MERGE:].reshape(D_MODEL, N_BRANCH, D_MODEL).transpose(1, 0, 2).astype(BF16)
    return w_mla, wq3, wkv, w_heads, wm


def _forward(x, mem, w_in, mla_q_norm, mla_w_q_up, mla_kv_norm, mla_w_kv_up,
             nsa_pe_k, nsa_pe_v, nsa_w1_k, nsa_w1_v, nsa_w2_k, nsa_w2_v,
             w_branch, b_merge, w_out, ln_mix_g, ln_mix_b,
             mem_w_q, mem_w_k, mem_w_v, mem_w_o, ln_mem_g, ln_mem_b,
             moe_w_router, moe_b_router, moe_w_gate, moe_b_gate, moe_w_up, moe_b_up,
             moe_w_down, moe_b_down, ln_moe_g, ln_moe_b):
    batch, seq, _ = x.shape
    mem_len = mem.shape[1]
    depth = w_in.shape[0]
    t = batch * seq
    alpha = float((2 * depth) ** 0.25)
    bm = MOE_BLOCK_ROWS
    n_rows = t * TOP_K + N_EXPERTS * bm
    n_blocks = n_rows // bm

    tm_in = min(512, seq)
    tq_mla = min(512, seq)
    tq_sb = 256
    tq_nsa, tk_nsa = 256, 512
    tm_ln = 256
    sc_chunk = 64

    cos128, sin128 = _rope_tables(seq)
    nsa_consts = _nsa_constants(seq, tq_nsa, tk_nsa)
    u_sb = jnp.asarray(np.arange(tq_sb)[:, None] > np.arange(tq_sb)[None, :], BF16)
    tri_router = jnp.asarray(np.arange(tm_in)[:, None] > np.arange(tm_in)[None, :], BF16)
    n_qheads = NSA_HEADS
    n_kvheads = 3 * 2 * NSA_GROUPS
    head_scale = np.ones((1, (n_qheads + n_kvheads + 3 * SB_HEADS) * HEAD_DIM), np.float32)
    head_scale[:, :n_qheads * HEAD_DIM] = HEAD_DIM ** -0.5
    sb0 = n_qheads
    kv0 = sb0 + 3 * SB_HEADS
    tn_heads = (n_qheads + n_kvheads + 3 * SB_HEADS) // 4 * HEAD_DIM
    head_scale[:, sb0 * HEAD_DIM:(sb0 + SB_HEADS) * HEAD_DIM] = HEAD_DIM ** -0.5
    head_scale = jnp.asarray(head_scale)
    ones_kv = jnp.ones((1, 2 * MEM_HEADS * HEAD_DIM), F32)

    b_gate4 = moe_b_gate.reshape(depth, N_EXPERTS, 1, D_EXPERT)
    b_up4 = moe_b_up.reshape(depth, N_EXPERTS, 1, D_EXPERT)
    b_down4 = moe_b_down.reshape(depth, N_EXPERTS, 1, D_MODEL)

    xf = x.reshape(t, D_MODEL)
    memf = mem.reshape(batch * mem_len, D_MODEL)
    row = lambda v: v.reshape(1, -1)

    for l in range(depth):
        w_mla, wq3, wkv, w_heads, wm = _layer_weights(w_in[l], mla_w_q_up[l], mla_w_kv_up[l])

        q_a, k_a, v_a, gates = mla_in(xf, w_mla, row(mla_q_norm[l]), row(mla_kv_norm[l]), wq3, wkv,
                                      cos128, sin128, seq=seq, tm=tm_ln)
        hm = proj_heads(xf, w_heads, head_scale, tm=tm_in, tn=tn_heads)
        o_a = mla_attn(q_a, k_a, v_a, batch=batch, seq=seq, tq=tq_mla, heads=2)
        w1 = jnp.stack([nsa_w1_k[l], nsa_w1_v[l]]).astype(BF16)
        pe = jnp.stack([nsa_pe_k[l], nsa_pe_v[l]]).reshape(2, 1, -1)
        pe = jnp.broadcast_to(pe, (2, 8, pe.shape[-1])).astype(BF16)
        w2 = jnp.stack([nsa_w2_k[l], nsa_w2_v[l]]).astype(BF16)
        cmp = nsa_compress(hm[kv0:kv0 + 4], w1, pe, w2, batch=batch, seq=seq)
        o_b = nsa_attn(hm, cmp, gates, nsa_consts, q_head0=0, kv_head0=kv0,
                       batch=batch, seq=seq, tq=tq_nsa, tk=tk_nsa)
        o_c = sb_attn(hm, u_sb, head0=sb0, batch=batch, seq=seq, tq=tq_sb, heads=4)
        y = merge_branches(xf, o_a, o_b, o_c, wm, w_branch[l].astype(BF16),
                           b_merge[l].reshape(N_BRANCH, 1, D_MODEL), tm=tm_in, tn=512)
        xf = out_ln(y, w_out[l].astype(BF16), xf, row(ln_mix_g[l]), row(ln_mix_b[l]), alpha=alpha, tm=tm_in)

        w_kv_mem = jnp.concatenate([mem_w_k[l], mem_w_v[l]], axis=1).astype(BF16)
        kv_mem = proj_heads(memf, w_kv_mem, ones_kv, tm=min(512, batch * mem_len), tn=512)
        xf = mem_attn_ln(xf, mem_w_q[l].astype(BF16), kv_mem, mem_w_o[l].astype(BF16),
                         row(ln_mem_g[l]), row(ln_mem_b[l]), alpha=alpha, seq=seq, mem_len=mem_len, tm=tm_in)

        wr = moe_w_router[l]
        wr_hi = wr.astype(BF16)
        wr_lo = (wr - wr_hi.astype(F32)).astype(BF16)
        route, x_packed, counts = router(xf, jnp.stack([wr_hi, wr_lo]), row(moe_b_router[l]),
                                         tri_router, tm=tm_in)
        padded = (counts[0].astype(jnp.int32) + bm - 1) // bm * bm
        pad_end = jnp.cumsum(padded)
        pad_start = pad_end - padded
        idx4 = route[:, 0:TOP_K].astype(jnp.int32)
        rank4 = route[:, 2 * TOP_K:3 * TOP_K].astype(jnp.int32)
        start4 = jnp.sum(jnp.where(idx4[:, :, None] == jnp.arange(N_EXPERTS, dtype=jnp.int32), pad_start, 0), axis=2)
        pos4 = start4 + rank4
        blk_row0 = jnp.arange(n_blocks, dtype=jnp.int32)[:, None] * bm
        blk_e = jnp.minimum(jnp.sum((pad_end[None, :] <= blk_row0).astype(jnp.int32), axis=1), N_EXPERTS - 1)
        blk_e = jnp.concatenate([blk_e, pad_end[-1:] // bm]).astype(jnp.int32)
        pos_kmajor = pos4.T.reshape(-1)
        x_rows = sc_scatter_rows(x_packed, pos_kmajor, n_rows, copies=TOP_K, chunk=sc_chunk)
        y_rows = experts(x_rows, blk_e, moe_w_gate, b_gate4, moe_w_up, b_up4, moe_w_down, b_down4,
                         layer=l, bm=bm)
        y4 = sc_gather_rows(y_rows, pos_kmajor, chunk=sc_chunk).reshape(TOP_K, t, D_MODEL // 2)
        xf = moe_ln(xf, y4, route, row(ln_moe_g[l]), row(ln_moe_b[l]), alpha=alpha, tm=tm_ln)

    return xf.reshape(batch, seq, D_MODEL)


def kernel(x, mem, w_in, mla_q_norm, mla_w_q_up, mla_kv_norm, mla_w_kv_up, nsa_pe_k, nsa_pe_v, nsa_w1_k, nsa_w1_v, nsa_w2_k, nsa_w2_v, w_branch, b_merge, w_out, ln_mix_g, ln_mix_b, mem_w_q, mem_w_k, mem_w_v, mem_w_o, ln_mem_g, ln_mem_b, moe_w_router, moe_b_router, moe_w_gate, moe_b_gate, moe_w_up, moe_b_up, moe_w_down, moe_b_down, ln_moe_g, ln_moe_b):
    return _forward(x, mem, w_in, mla_q_norm, mla_w_q_up, mla_kv_norm, mla_w_kv_up,
                    nsa_pe_k, nsa_pe_v, nsa_w1_k, nsa_w1_v, nsa_w2_k, nsa_w2_v,
                    w_branch, b_merge, w_out, ln_mix_g, ln_mix_b,
                    mem_w_q, mem_w_k, mem_w_v, mem_w_o, ln_mem_g, ln_mem_b,
                    moe_w_router, moe_b_router, moe_w_gate, moe_b_gate, moe_w_up, moe_b_up,
                    moe_w_down, moe_b_down, ln_moe_g, ln_moe_b)
```

```python
import functools

import numpy as np
import jax
import jax.numpy as jnp
from jax import lax
from jax.experimental import pallas as pl
from jax.experimental.pallas import tpu as pltpu
from jax.experimental.pallas import tpu_sc as plsc

F32 = jnp.float32
BF16 = jnp.bfloat16

D_MODEL = 2048
HEAD_DIM = 128
MLA_HEADS = 8
MLA_Q_RANK = 512
MLA_KV_RANK = 256
MLA_NOPE = 128
MLA_ROPE = 64
ROPE_THETA = 10000.0
NSA_HEADS = 8
NSA_GROUPS = 2
NSA_HG = NSA_HEADS // NSA_GROUPS
NSA_CMP_LEN = 32
NSA_CMP_STRIDE = 16
NSA_SEL_LEN = 64
NSA_TOPK = 16
NSA_WINDOW = 512
SB_HEADS = 8
MEM_HEADS = 4
N_EXPERTS = 32
TOP_K = 4
D_EXPERT = 512
SWIGLU_LIMIT = 7.0
SWIGLU_ALPHA = 1.702
N_BRANCH = 3
BRANCH_WIDTH = 1024
LN_EPS = 1e-5
RMS_EPS = 1e-6
NEG = -1e30
BIG = 1e30
SB_UNDERFLOW_LOG = -100.0

OFF_CQ = 0
OFF_CKV = 512
OFF_KR = 768
OFF_NSA_Q = 832
OFF_NSA_KV = 1856
OFF_NSA_GATE = 3392
OFF_SB = 3416
OFF_MERGE = 6488
MLA_IN_WIDTH = MLA_Q_RANK + MLA_KV_RANK + 2 * 128 + NSA_GROUPS * 128

VMEM_LIMIT_V7X = 56 * 1024 * 1024
MOE_BLOCK_ROWS = 512
SC_CORES_V7X = 2
SC_SUBCORES_V7X = 16


def _cp(sem, vmem=VMEM_LIMIT_V7X):
    return pltpu.CompilerParams(dimension_semantics=sem, vmem_limit_bytes=vmem)


def _dot(a, b):
    return jnp.dot(a, b, preferred_element_type=F32)


def _dot_nt(a, b):
    return lax.dot_general(a, b, (((1,), (1,)), ((), ())), preferred_element_type=F32)


def _layer_norm(z, g, b):
    mu = jnp.mean(z, axis=-1, keepdims=True)
    zc = z - mu
    var = jnp.mean(zc * zc, axis=-1, keepdims=True)
    return zc * lax.rsqrt(var + LN_EPS) * g + b


def _rms_norm(z, g):
    return z * lax.rsqrt(jnp.mean(z * z, axis=-1, keepdims=True) + RMS_EPS) * g


def _pack_bf16_pairs(z):
    n = z.shape[1] // 2
    bits = pltpu.bitcast(z.astype(BF16).astype(F32), jnp.uint32)
    return lax.shift_right_logical(bits[:, :n], jnp.uint32(16)) | (bits[:, n:] & jnp.uint32(0xFFFF0000))


def _unpack_bf16_pairs(w):
    lo = pltpu.bitcast(lax.shift_left(w, jnp.uint32(16)), F32)
    hi = pltpu.bitcast(w & jnp.uint32(0xFFFF0000), F32)
    return lo, hi


def _proj_heads_kernel(a_ref, w_ref, s_ref, o_ref, abf_ref, *, n_heads_per_tile):
    @pl.when(pl.program_id(1) == 0)
    def _():
        abf_ref[...] = a_ref[...].astype(BF16)

    acc = _dot(abf_ref[...], w_ref[...]) * s_ref[...]
    for c in range(n_heads_per_tile):
        o_ref[c] = acc[:, c * HEAD_DIM:(c + 1) * HEAD_DIM].astype(o_ref.dtype)


def proj_heads(a, w, scale, *, tm, tn):
    m, k = a.shape
    n = w.shape[1]
    hpt = tn // HEAD_DIM
    return pl.pallas_call(
        functools.partial(_proj_heads_kernel, n_heads_per_tile=hpt),
        out_shape=jax.ShapeDtypeStruct((n // HEAD_DIM, m, HEAD_DIM), BF16),
        grid=(m // tm, n // tn),
        in_specs=[
            pl.BlockSpec((tm, k), lambda i, j: (i, 0)),
            pl.BlockSpec((k, tn), lambda i, j: (0, j)),
            pl.BlockSpec((1, tn), lambda i, j: (0, j)),
        ],
        out_specs=pl.BlockSpec((hpt, tm, HEAD_DIM), lambda i, j: (j, i, 0)),
        scratch_shapes=[pltpu.VMEM((tm, k), BF16)],
        compiler_params=_cp(("arbitrary", "arbitrary")),
    )(a, w, scale)


def _mla_in_kernel(x_ref, w_ref, qg_ref, kg_ref, wq_ref, wkv_ref, cos_ref, sin_ref,
                   q_ref, k_ref, v_ref, g_ref):
    xb = x_ref[...].astype(BF16)
    h = _dot(xb, w_ref[...])
    c0, c1 = MLA_Q_RANK, MLA_Q_RANK + MLA_KV_RANK
    cq = h[:, 0:c0]
    ckv = h[:, c0:c1]
    kr1 = h[:, c1:c1 + 128]
    kr2 = h[:, c1 + 128:c1 + 256]
    g_ref[...] = jax.nn.sigmoid(h[:, c1 + 256:MLA_IN_WIDTH])
    cos = cos_ref[...]
    sin = sin_ref[...]
    scale = (MLA_NOPE + MLA_ROPE) ** -0.5
    hw = MLA_HEADS * 128
    nq = _rms_norm(cq, qg_ref[...]).astype(BF16)
    q3 = _dot(nq, wq_ref[...])
    for hh in range(MLA_HEADS):
        lo, hi = hh * 128, (hh + 1) * 128
        q_ref[hh, :, 0:128] = (q3[:, lo:hi] * scale).astype(BF16)
        rot = q3[:, hw + lo:hw + hi] * cos + q3[:, 2 * hw + lo:2 * hw + hi] * sin
        q_ref[hh, :, 128:256] = (rot * scale).astype(BF16)
    nkv = _rms_norm(ckv, kg_ref[...]).astype(BF16)
    kv = _dot(nkv, wkv_ref[...])
    krot = (kr1 * cos + kr2 * sin).astype(BF16)
    for hh in range(MLA_HEADS):
        lo, hi = hh * 128, (hh + 1) * 128
        k_ref[hh, :, 0:128] = kv[:, lo:hi].astype(BF16)
        k_ref[hh, :, 128:256] = krot
        v_ref[hh] = kv[:, hw + lo:hw + hi].astype(BF16)


def mla_in(x, w_mla, qg, kg, wq3, wkv, cos128, sin128, *, seq, tm):
    t = x.shape[0]
    npos = seq // tm
    full = lambda shape: pl.BlockSpec(shape, lambda i: (0,) * len(shape))
    return pl.pallas_call(
        _mla_in_kernel,
        out_shape=(
            jax.ShapeDtypeStruct((MLA_HEADS, t, 256), BF16),
            jax.ShapeDtypeStruct((MLA_HEADS, t, 256), BF16),
            jax.ShapeDtypeStruct((MLA_HEADS, t, 128), BF16),
            jax.ShapeDtypeStruct((t, 256), F32),
        ),
        grid=(t // tm,),
        in_specs=[
            pl.BlockSpec((tm, D_MODEL), lambda i: (i, 0)),
            full((D_MODEL, MLA_IN_WIDTH)),
            full((1, MLA_Q_RANK)),
            full((1, MLA_KV_RANK)),
            full((MLA_Q_RANK, 3 * MLA_HEADS * 128)),
            full((MLA_KV_RANK, 2 * MLA_HEADS * 128)),
            pl.BlockSpec((tm, 128), lambda i: (i % npos, 0)),
            pl.BlockSpec((tm, 128), lambda i: (i % npos, 0)),
        ],
        out_specs=(
            pl.BlockSpec((MLA_HEADS, tm, 256), lambda i: (0, i, 0)),
            pl.BlockSpec((MLA_HEADS, tm, 256), lambda i: (0, i, 0)),
            pl.BlockSpec((MLA_HEADS, tm, 128), lambda i: (0, i, 0)),
            pl.BlockSpec((tm, 256), lambda i: (i, 0)),
        ),
        compiler_params=_cp(("arbitrary",)),
    )(x, w_mla, qg, kg, wq3, wkv, cos128, sin128)


def _mla_attn_kernel(q_ref, k_ref, v_ref, o_ref, s_ref, *, tq, heads):
    qi = pl.program_id(2)

    def scores(h, kt):
        k0 = pl.multiple_of(kt * tq, tq)
        return _dot_nt(q_ref[h], k_ref[h, pl.ds(k0, tq), :])

    def consume(h, kt, s, carry, diag):
        m, l, acc = carry
        k0 = pl.multiple_of(kt * tq, tq)
        v = v_ref[h, pl.ds(k0, tq), :]
        if diag:
            row = lax.broadcasted_iota(jnp.int32, (tq, tq), 0)
            col = lax.broadcasted_iota(jnp.int32, (tq, tq), 1)
            s = jnp.where(col <= row, s, NEG)
        m_new = jnp.maximum(m, jnp.max(s, axis=1, keepdims=True))
        alpha = jnp.exp(m - m_new)
        p = jnp.exp((s - m_new).astype(BF16))
        l = alpha * l + jnp.sum(p.astype(F32), axis=1, keepdims=True)
        acc = alpha * acc + _dot(p, v)
        return m_new, l, acc

    def fill(slot, kt):
        for h in range(heads):
            s_ref[slot, h] = scores(h, kt)

    def drain(slot, kt, carries, diag):
        return tuple(consume(h, kt, s_ref[slot, h], carries[h], diag) for h in range(heads))

    def body(j, carries):
        kt = 2 * j
        fill(1, kt + 1)
        carries = drain(0, kt, carries, False)
        fill(0, kt + 2)
        return drain(1, kt + 1, carries, False)

    init = (jnp.full((tq, 1), NEG, F32), jnp.zeros((tq, 1), F32), jnp.zeros((tq, 128), F32))
    fill(0, 0)
    carries = lax.fori_loop(0, qi // 2, body, (init,) * heads)

    def even_tail(carries):
        return drain(0, qi, carries, True)

    def odd_tail(carries):
        fill(1, qi)
        return drain(1, qi, drain(0, qi - 1, carries, False), True)

    carries = lax.cond(lax.rem(qi, 2) == 1, odd_tail, even_tail, carries)
    for h, (_, l, acc) in enumerate(carries):
        o_ref[:, h * 128:(h + 1) * 128] = (acc / l).astype(o_ref.dtype)


def mla_attn(q, k, v, *, batch, seq, tq, heads):
    nq = seq // tq
    q4 = q.reshape(MLA_HEADS, batch, seq, 256)
    k4 = k.reshape(MLA_HEADS, batch, seq, 256)
    v4 = v.reshape(MLA_HEADS, batch, seq, 128)
    return pl.pallas_call(
        functools.partial(_mla_attn_kernel, tq=tq, heads=heads),
        out_shape=jax.ShapeDtypeStruct((batch * seq, MLA_HEADS * 128), BF16),
        grid=(MLA_HEADS // heads, batch, nq),
        in_specs=[
            pl.BlockSpec((heads, None, tq, 256), lambda h, b, i: (h, b, i, 0)),
            pl.BlockSpec((heads, None, seq, 256), lambda h, b, i: (h, b, 0, 0)),
            pl.BlockSpec((heads, None, seq, 128), lambda h, b, i: (h, b, 0, 0)),
        ],
        out_specs=pl.BlockSpec((tq, heads * 128), lambda h, b, i: (b * nq + i, h)),
        scratch_shapes=[pltpu.VMEM((2, heads, tq, tq), F32)],
        compiler_params=_cp(("arbitrary", "arbitrary", "arbitrary")),
    )(q4, k4, v4)


def _sb_attn_kernel(q_ref, k_ref, v_ref, u_ref, o_ref, *, tq, heads):
    qi = pl.program_id(2)
    u = u_ref[...]

    def head_step(h, kt, carry, diag):
        run, acc = carry
        k0 = pl.multiple_of(kt * tq, tq)
        k = k_ref[h, pl.ds(k0, tq), :]
        v = v_ref[h, pl.ds(k0, tq), :]
        z = _dot_nt(q_ref[h], k)
        l1m = -(jnp.maximum(z, 0.0) + jnp.log(1.0 + jnp.exp(-jnp.abs(z))))
        if diag:
            row = lax.broadcasted_iota(jnp.int32, (tq, tq), 0)
            col = lax.broadcasted_iota(jnp.int32, (tq, tq), 1)
            strict = col < row
            l1m_m = jnp.where(strict, l1m, 0.0)
        else:
            l1m_m = l1m
        hi = l1m_m.astype(BF16)
        lo = (l1m_m - hi.astype(F32)).astype(BF16)
        between = _dot(hi, u) + _dot(lo, u)
        a = jnp.exp(z + l1m + between + run)
        if diag:
            a = jnp.where(strict, a, 0.0)
        acc = acc + _dot(a.astype(BF16), v)
        run = run + between[:, 0:1] + l1m_m[:, 0:1]
        return run, acc

    def step(kt, carries, diag):
        return tuple(head_step(h, kt, carries[h], diag) for h in range(heads))

    init = (jnp.zeros((tq, 1), F32), jnp.zeros((tq, 128), F32))
    carries = step(qi, (init,) * heads, True)

    def more(c):
        j, carries = c
        top = carries[0][0]
        for run, _ in carries[1:]:
            top = jnp.maximum(top, run)
        return (j < qi) & (jnp.max(top) > SB_UNDERFLOW_LOG)

    def body(c):
        j, carries = c
        return j + 1, step(qi - 1 - j, carries, False)

    _, carries = lax.while_loop(more, body, (jnp.int32(0), carries))
    for h, (_, acc) in enumerate(carries):
        o_ref[:, h * HEAD_DIM:(h + 1) * HEAD_DIM] = acc.astype(o_ref.dtype)


def sb_attn(hm, u, *, head0, batch, seq, tq, heads):
    nq = seq // tq
    assert head0 % heads == 0 and SB_HEADS % heads == 0
    hm4 = hm.reshape(hm.shape[0], batch, seq, HEAD_DIM)
    blk0 = head0 // heads
    per_part = SB_HEADS // heads
    return pl.pallas_call(
        functools.partial(_sb_attn_kernel, tq=tq, heads=heads),
        out_shape=jax.ShapeDtypeStruct((batch * seq, SB_HEADS * HEAD_DIM), BF16),
        grid=(per_part, batch, nq),
        in_specs=[
            pl.BlockSpec((heads, None, tq, HEAD_DIM), lambda h, b, i: (blk0 + h, b, i, 0)),
            pl.BlockSpec((heads, None, seq, HEAD_DIM), lambda h, b, i: (blk0 + per_part + h, b, 0, 0)),
            pl.BlockSpec((heads, None, seq, HEAD_DIM), lambda h, b, i: (blk0 + 2 * per_part + h, b, 0, 0)),
            pl.BlockSpec((tq, tq), lambda h, b, i: (0, 0)),
        ],
        out_specs=pl.BlockSpec((tq, heads * HEAD_DIM), lambda h, b, i: (b * nq + i, h)),
        compiler_params=_cp(("arbitrary", "arbitrary", "arbitrary")),
    )(hm4, hm4, hm4, u)


def _nsa_cmp_kernel(c_ref, w1_ref, pe_ref, w2_ref, o_ref, *, nc):
    c = c_ref[...]
    half = NSA_CMP_STRIDE * HEAD_DIM
    a1 = _dot(c, w1_ref[0:half, :])
    a2 = _dot(c, w1_ref[half:2 * half, :])
    pc = _dot(pe_ref[...], w1_ref[...])[0:1, :]
    pre = a1 + pltpu.roll(a2, nc - 1, 0) + pc
    act = 0.5 * pre * (1.0 + jnp.tanh(0.7978845608028654 * (pre + 0.044715 * (pre * pre * pre))))
    o_ref[...] = _dot(act.astype(BF16), w2_ref[...]).astype(BF16)


def nsa_compress(cmp_heads, w1, pe, w2, *, batch, seq):
    nc = seq // NSA_CMP_STRIDE
    hm4 = cmp_heads.reshape(4, batch, nc, NSA_CMP_STRIDE * HEAD_DIM)
    return pl.pallas_call(
        functools.partial(_nsa_cmp_kernel, nc=nc),
        out_shape=jax.ShapeDtypeStruct((4, batch, nc, HEAD_DIM), BF16),
        grid=(4, batch),
        in_specs=[
            pl.BlockSpec((None, None, nc, NSA_CMP_STRIDE * HEAD_DIM), lambda c, b: (c, b, 0, 0)),
            pl.BlockSpec((None, NSA_CMP_LEN * HEAD_DIM, HEAD_DIM), lambda c, b: (c // 2, 0, 0)),
            pl.BlockSpec((None, 8, NSA_CMP_LEN * HEAD_DIM), lambda c, b: (c // 2, 0, 0)),
            pl.BlockSpec((None, HEAD_DIM, HEAD_DIM), lambda c, b: (c // 2, 0, 0)),
        ],
        out_specs=pl.BlockSpec((None, None, nc, HEAD_DIM), lambda c, b: (c, b, 0, 0)),
        compiler_params=_cp(("arbitrary", "arbitrary")),
    )(hm4, w1, pe, w2)


def _nsa_attn_kernel(q_ref, qa_ref, kc_ref, vc_ref, ks_ref, vs_ref, kw_ref, vw_ref, g_ref, ovt_ref, e_ref,
                     kpos_ref, cpos_ref, o_ref, ss_ref, *, tq, tk, seq, n_sel, n_top):
    qi = pl.program_id(1)
    t0 = qi * tq
    rows = NSA_HG * tq
    nc = seq // NSA_CMP_STRIDE
    groups = range(NSA_GROUPS)

    rid = lax.broadcasted_iota(jnp.int32, (rows, 1), 0)
    trow = t0 + lax.bitwise_and(rid, tq - 1)

    def masked_softmax(s, valid):
        sm = jnp.where(valid, s, NEG)
        m = jnp.max(sm, axis=1, keepdims=True)
        e = jnp.where(valid, jnp.exp(sm - m), 0.0)
        d = jnp.sum(e, axis=1, keepdims=True)
        return e * (1.0 / jnp.where(d > 0.0, d, 1.0))

    n_i = lax.broadcasted_iota(jnp.int32, (1, nc), 1)
    end = n_i * NSA_CMP_STRIDE + (NSA_CMP_LEN - 1)
    cur = lax.shift_right_logical(t0 + lax.broadcasted_iota(jnp.int32, (1, tq), 1),
                                  int(np.log2(NSA_SEL_LEN)))
    blk = lax.broadcasted_iota(jnp.int32, (n_sel, tq), 0)
    forced = (blk == 0) | (blk == cur) | (blk == cur - 1)
    sub = lax.broadcasted_iota(jnp.int32, (8, tq), 0)
    eye = (lax.broadcasted_iota(jnp.int32, (n_sel, n_sel), 0)
           == lax.broadcasted_iota(jnp.int32, (n_sel, n_sel), 1)).astype(F32).astype(BF16)
    ovt = ovt_ref[...]

    def front(g):
        q = jnp.concatenate([q_ref[NSA_HG * g:NSA_HG * (g + 1)].reshape(rows, HEAD_DIM), qa_ref[g]], axis=1)

        s_c = _dot_nt(q, jnp.concatenate([kc_ref[g], cpos_ref[...]], axis=1))
        p_c = masked_softmax(s_c, end <= trow)
        o_c = _dot(p_c.astype(BF16), vc_ref[g])

        psum = p_c[0:tq] + p_c[tq:2 * tq] + p_c[2 * tq:3 * tq] + p_c[3 * tq:4 * tq]
        p_hi = psum.astype(BF16)
        p_lo = (psum - p_hi.astype(F32)).astype(BF16)
        imp = _dot_nt(ovt, p_hi) + _dot_nt(ovt, p_lo)
        key = jnp.where(blk > cur, -BIG, jnp.where(forced, BIG, imp))
        chunks = [key[8 * r:8 * r + 8, :] for r in range(n_sel // 8)]
        ranks = [jnp.zeros((8, tq), F32) for _ in chunks]
        for i in range(n_sel):
            vi = key[i:i + 1, :]
            for r, kc in enumerate(chunks):
                gt = jnp.where(vi > kc, 1.0, 0.0)
                if r < i // 8:
                    win = gt
                else:
                    ge = jnp.where(vi >= kc, 1.0, 0.0)
                    win = ge if r > i // 8 else jnp.where(sub > i % 8, ge, gt)
                ranks[r] = ranks[r] + win
        rank = jnp.concatenate(ranks, axis=0)
        selm_t = jnp.where((rank < float(n_top)) & (blk <= cur), 1.0, 0.0).astype(BF16)
        selm = lax.dot_general(selm_t, eye, (((0,), (0,)), ((), ())), preferred_element_type=F32).astype(BF16)
        return q, selm, o_c

    fronts = [front(g) for g in groups]

    def sel_fill(slot, kt):
        k0 = pl.multiple_of(kt * tk, tk)
        kp = kpos_ref[pl.ds(k0, tk), :]
        for g in groups:
            q, selm, _ = fronts[g]
            kk = jnp.concatenate([ks_ref[g, pl.ds(k0, tk), :], kp], axis=1)
            mex = _dot(selm, e_ref[kt])
            bias = (mex - 1.0) * BIG
            ss_ref[slot, g] = _dot_nt(q, kk) + jnp.concatenate([bias] * NSA_HG, axis=0)

    def sel_drain_group(g, slot, kt, carry, diag):
        m, l, acc = carry
        k0 = pl.multiple_of(kt * tk, tk)
        vv = vs_ref[g, pl.ds(k0, tk), :]
        sm = ss_ref[slot, g]
        if diag:
            spos = k0 + lax.broadcasted_iota(jnp.int32, (1, tk), 1)
            sm = jnp.where(spos <= trow, sm, NEG)
        m_new = jnp.maximum(m, jnp.max(sm, axis=1, keepdims=True))
        alpha = jnp.exp(m - m_new)
        p = jnp.exp((sm - m_new).astype(BF16))
        l = alpha * l + jnp.sum(p.astype(F32), axis=1, keepdims=True)
        acc = alpha * acc + _dot(p, vv)
        return m_new, l, acc

    def sel_drain(slot, kt, carries, diag):
        return tuple(sel_drain_group(g, slot, kt, carries[g], diag) for g in groups)

    def sel_body(j, carries):
        kt = 2 * j
        sel_fill(1, kt + 1)
        carries = sel_drain(0, kt, carries, False)
        sel_fill(0, kt + 2)
        return sel_drain(1, kt + 1, carries, False)

    kt_last = t0 // tk
    init = (jnp.full((rows, 1), NEG, F32), jnp.zeros((rows, 1), F32), jnp.zeros((rows, HEAD_DIM), F32))
    sel_fill(0, 0)
    carries = lax.fori_loop(0, kt_last // 2, sel_body, (init,) * NSA_GROUPS)

    def even_tail(carries):
        return sel_drain(0, kt_last, carries, True)

    def odd_tail(carries):
        sel_fill(1, kt_last)
        return sel_drain(1, kt_last, sel_drain(0, kt_last - 1, carries, False), True)

    carries = lax.cond(lax.rem(kt_last, 2) == 1, odd_tail, even_tail, carries)

    wk = NSA_WINDOW + tq
    ks0 = pl.multiple_of(jnp.maximum(t0 - NSA_WINDOW, 0), tq)
    kpw = kpos_ref[pl.ds(ks0, wk), :]
    wpos = ks0 + lax.broadcasted_iota(jnp.int32, (1, wk), 1)
    dw = trow - wpos
    in_window = pltpu.bitcast(dw, jnp.uint32) < jnp.uint32(NSA_WINDOW)
    gt = g_ref[...]
    for g in groups:
        q, _, o_c = fronts[g]
        _, l_s, acc_s = carries[g]
        o_s = acc_s * (1.0 / l_s)
        kw = jnp.concatenate([kw_ref[g, pl.ds(ks0, wk), :], kpw], axis=1)
        vw = vw_ref[g, pl.ds(ks0, wk), :]
        sm_w = jnp.where(in_window, _dot_nt(q, kw), NEG)
        p_w = jnp.exp((sm_w - jnp.max(sm_w, axis=1, keepdims=True)).astype(BF16))
        o_w = _dot(p_w, vw) * (1.0 / jnp.sum(p_w.astype(F32), axis=1, keepdims=True))
        for hg in range(NSA_HG):
            sl = slice(hg * tq, (hg + 1) * tq)
            c0 = 128 * g + 3 * hg
            o = (gt[:, c0:c0 + 1] * o_c[sl] + gt[:, c0 + 1:c0 + 2] * o_s[sl] + gt[:, c0 + 2:c0 + 3] * o_w[sl])
            h = NSA_HG * g + hg
            o_ref[:, h * HEAD_DIM:(h + 1) * HEAD_DIM] = o.astype(o_ref.dtype)


def nsa_attn(hm, cmp, gates, consts, *, q_head0, kv_head0, batch, seq, tq, tk):
    ov, e, qa, kpos, cpos = consts
    nq = seq // tq
    nc = seq // NSA_CMP_STRIDE
    n_sel = seq // NSA_SEL_LEN
    n_top = min(NSA_TOPK, n_sel)
    assert tk % tq == 0 and seq % tk == 0 and seq >= NSA_WINDOW + tq and NSA_WINDOW % tq == 0
    assert q_head0 % NSA_HEADS == 0 and kv_head0 % NSA_GROUPS == 0
    hm4 = hm.reshape(hm.shape[0], batch, seq, HEAD_DIM)
    kv_spec = lambda off: pl.BlockSpec((NSA_GROUPS, None, seq, HEAD_DIM),
                                       lambda b, i: ((kv_head0 + off) // NSA_GROUPS, b, 0, 0))
    return pl.pallas_call(
        functools.partial(_nsa_attn_kernel, tq=tq, tk=tk, seq=seq, n_sel=n_sel, n_top=n_top),
        out_shape=jax.ShapeDtypeStruct((batch * seq, NSA_HEADS * HEAD_DIM), BF16),
        grid=(batch, nq),
        in_specs=[
            pl.BlockSpec((NSA_HEADS, None, tq, HEAD_DIM), lambda b, i: (q_head0 // NSA_HEADS, b, i, 0)),
            pl.BlockSpec((NSA_GROUPS, NSA_HG * tq, 128), lambda b, i: (0, 0, 0)),
            pl.BlockSpec((NSA_GROUPS, None, nc, HEAD_DIM), lambda b, i: (0, b, 0, 0)),
            pl.BlockSpec((NSA_GROUPS, None, nc, HEAD_DIM), lambda b, i: (1, b, 0, 0)),
            kv_spec(4), kv_spec(6), kv_spec(8), kv_spec(10),
            pl.BlockSpec((tq, NSA_GROUPS * 128), lambda b, i: (b * nq + i, 0)),
            pl.BlockSpec((n_sel, nc), lambda b, i: (0, 0)),
            pl.BlockSpec((seq // tk, n_sel, tk), lambda b, i: (0, 0, 0)),
            pl.BlockSpec((seq, 128), lambda b, i: (0, 0)),
            pl.BlockSpec((nc, 128), lambda b, i: (0, 0)),
        ],
        out_specs=pl.BlockSpec((tq, NSA_HEADS * HEAD_DIM), lambda b, i: (b * nq + i, 0)),
        scratch_shapes=[pltpu.VMEM((2, NSA_GROUPS, NSA_HG * tq, tk), F32)],
        compiler_params=_cp(("arbitrary", "arbitrary")),
    )(hm4, qa, cmp, cmp, hm4, hm4, hm4, hm4, gates, ov, e, kpos, cpos)


def _merge_kernel(x_ref, oa_ref, ob_ref, oc_ref, wm_ref, wb_ref, bm_ref, y_ref, xb_ref):
    @pl.when(pl.program_id(1) == 0)
    def _():
        xb_ref[...] = x_ref[...].astype(BF16)

    xb = xb_ref[...]
    acc = None
    for br, o_ref in enumerate((oa_ref, ob_ref, oc_ref)):
        gate = jax.nn.sigmoid(_dot(xb, wm_ref[br]) + bm_ref[br])
        term = gate * _dot(o_ref[...], wb_ref[br])
        acc = term if acc is None else acc + term
    y_ref[...] = acc.astype(y_ref.dtype)


def merge_branches(x, o_a, o_b, o_c, wm, wb, bm, *, tm, tn):
    t = x.shape[0]
    o_spec = pl.BlockSpec((tm, BRANCH_WIDTH), lambda i, j: (i, 0))
    return pl.pallas_call(
        _merge_kernel,
        out_shape=jax.ShapeDtypeStruct((t, D_MODEL), BF16),
        grid=(t // tm, D_MODEL // tn),
        in_specs=[
            pl.BlockSpec((tm, D_MODEL), lambda i, j: (i, 0)),
            o_spec, o_spec, o_spec,
            pl.BlockSpec((N_BRANCH, D_MODEL, tn), lambda i, j: (0, 0, j)),
            pl.BlockSpec((N_BRANCH, BRANCH_WIDTH, tn), lambda i, j: (0, 0, j)),
            pl.BlockSpec((N_BRANCH, 1, tn), lambda i, j: (0, 0, j)),
        ],
        out_specs=pl.BlockSpec((tm, tn), lambda i, j: (i, j)),
        scratch_shapes=[pltpu.VMEM((tm, D_MODEL), BF16)],
        compiler_params=_cp(("arbitrary", "arbitrary")),
    )(x, o_a, o_b, o_c, wm, wb, bm)


def _mem_attention(x, wq_ref, k_ref, v_ref, wo_ref):
    q = _dot(x.astype(BF16), wq_ref[...]) * (HEAD_DIM ** -0.5)
    outs = []
    for h in range(MEM_HEADS):
        qh = q[:, h * HEAD_DIM:(h + 1) * HEAD_DIM].astype(BF16)
        s = _dot_nt(qh, k_ref[h])
        m = jnp.max(s, axis=1, keepdims=True)
        e = jnp.exp(s - m)
        p = e * (1.0 / jnp.sum(e, axis=1, keepdims=True))
        outs.append(_dot(p.astype(BF16), v_ref[h]).astype(BF16))
    return _dot(jnp.concatenate(outs, axis=1), wo_ref[...])


def _route_tokens(x, w_ref, b_ref, tri_ref, count_ref):
    xh = x.astype(BF16)
    xl = (x - xh.astype(F32)).astype(BF16)
    logits = _dot(xh, w_ref[0]) + _dot(xh, w_ref[1]) + _dot(xl, w_ref[0]) + b_ref[...]
    tm = logits.shape[0]
    lane = lax.broadcasted_iota(jnp.int32, (tm, N_EXPERTS), 1)
    work = logits
    hots, vals, idxs = [], [], []
    for _ in range(TOP_K):
        m = jnp.max(work, axis=1, keepdims=True)
        idx = jnp.min(jnp.where(work == m, lane, N_EXPERTS), axis=1, keepdims=True)
        hot = lane == idx
        hots.append(hot)
        vals.append(m)
        idxs.append(idx)
        work = jnp.where(hot, -jnp.inf, work)
    es = [jnp.exp(v - vals[0]) for v in vals]
    inv = 1.0 / (es[0] + es[1] + es[2] + es[3])
    mask = jnp.zeros((tm, N_EXPERTS), F32)
    for hot in hots:
        mask = mask + jnp.where(hot, 1.0, 0.0)
    before = count_ref[...] + _dot(tri_ref[...], mask.astype(BF16))
    count_ref[...] = count_ref[...] + jnp.sum(mask, axis=0, keepdims=True)
    col = lax.broadcasted_iota(jnp.int32, (tm, 128), 1)
    route = jnp.zeros((tm, 128), F32)
    for k in range(TOP_K):
        rank = jnp.sum(jnp.where(hots[k], before, 0.0), axis=1, keepdims=True)
        route = (route + jnp.where(col == k, idxs[k].astype(F32), 0.0)
                 + jnp.where(col == TOP_K + k, es[k] * inv, 0.0) + jnp.where(col == 2 * TOP_K + k, rank, 0.0))
    return route


def _post_mix_kernel(y_ref, wout_ref, x_ref, g1_ref, b1_ref, wq_ref, k_ref, v_ref, wo_ref, g2_ref, b2_ref,
                     wr_ref, br_ref, tri_ref, o_ref, route_ref, xp_ref, count_ref, *, alpha):
    @pl.when(pl.program_id(0) == 0)
    def _():
        count_ref[...] = jnp.zeros_like(count_ref)

    x1 = _layer_norm(alpha * x_ref[...] + _dot(y_ref[...], wout_ref[...]), g1_ref[...], b1_ref[...])
    x2 = _layer_norm(alpha * x1 + _mem_attention(x1, wq_ref, k_ref, v_ref, wo_ref), g2_ref[...], b2_ref[...])
    o_ref[...] = x2
    xp_ref[...] = _pack_bf16_pairs(x2)
    route_ref[...] = _route_tokens(x2, wr_ref, br_ref, tri_ref, count_ref)


def post_mix(y, w_out, x, ln1, wq, kv, wo, ln2, w_router_hl, b_router, tri, *, alpha, seq, mem_len, tm):
    t = x.shape[0]
    per_b = seq // tm
    kv4 = kv.reshape(2 * MEM_HEADS, t // seq, mem_len, HEAD_DIM)
    width = MEM_HEADS * HEAD_DIM
    rows = lambda w: pl.BlockSpec((tm, w), lambda i: (i, 0))
    full = lambda shape: pl.BlockSpec(shape, lambda i: (0,) * len(shape))
    kv_spec = lambda part: pl.BlockSpec((MEM_HEADS, None, mem_len, HEAD_DIM), lambda i: (part, i // per_b, 0, 0))
    return pl.pallas_call(
        functools.partial(_post_mix_kernel, alpha=alpha),
        out_shape=(jax.ShapeDtypeStruct((t, D_MODEL), F32),
                   jax.ShapeDtypeStruct((t, 128), F32),
                   jax.ShapeDtypeStruct((t, D_MODEL // 2), jnp.uint32),
                   jax.ShapeDtypeStruct((1, N_EXPERTS), F32)),
        grid=(t // tm,),
        in_specs=[
            rows(D_MODEL), full((D_MODEL, D_MODEL)), rows(D_MODEL), full((1, D_MODEL)), full((1, D_MODEL)),
            full((D_MODEL, width)), kv_spec(0), kv_spec(1), full((width, D_MODEL)),
            full((1, D_MODEL)), full((1, D_MODEL)),
            full((2, D_MODEL, N_EXPERTS)), full((1, N_EXPERTS)), full((tm, tm)),
        ],
        out_specs=(rows(D_MODEL), rows(128), rows(D_MODEL // 2), full((1, N_EXPERTS))),
        compiler_params=_cp(("arbitrary",)),
    )(y, w_out, x, ln1[0], ln1[1], wq, kv4, kv4, wo, ln2[0], ln2[1], w_router_hl, b_router, tri)


def sc_gather_rows(table, idx, *, chunk):
    n = idx.shape[0]
    d = table.shape[1]
    workers = SC_CORES_V7X * SC_SUBCORES_V7X
    per_w = n // workers
    assert n % (workers * chunk) == 0 and chunk % 8 == 0 and chunk <= 128
    mesh = plsc.VectorSubcoreMesh(core_axis_name="c", subcore_axis_name="s")

    @functools.partial(
        pl.kernel, mesh=mesh,
        out_type=jax.ShapeDtypeStruct((n, d), table.dtype),
        scratch_types=[pltpu.VMEM((chunk,), jnp.int32), pltpu.VMEM((chunk, d), table.dtype),
                       pltpu.SemaphoreType.DMA],
    )
    def gather(table_hbm, idx_hbm, out_hbm, idx_v, rows_v, sem):
        wid = lax.axis_index("s") * SC_CORES_V7X + lax.axis_index("c")
        base = wid * per_w

        @pl.loop(0, per_w // chunk)
        def _(j):
            off = pl.multiple_of(base + j * chunk, 8)
            pltpu.sync_copy(idx_hbm.at[pl.ds(off, chunk)], idx_v)
            pltpu.async_copy(table_hbm.at[idx_v], rows_v, sem).wait()
            pltpu.sync_copy(rows_v, out_hbm.at[pl.ds(off, chunk)])

    return gather(table, idx)


def sc_scatter_rows(rows, idx, n_out, *, copies, chunk):
    t, d = rows.shape
    workers = SC_CORES_V7X * SC_SUBCORES_V7X
    per_w = t // workers
    assert idx.shape == (copies * t,) and t % (workers * chunk) == 0 and chunk % 8 == 0 and chunk <= 128
    mesh = plsc.VectorSubcoreMesh(core_axis_name="c", subcore_axis_name="s")

    @functools.partial(
        pl.kernel, mesh=mesh,
        out_type=jax.ShapeDtypeStruct((n_out, d), rows.dtype),
        scratch_types=[pltpu.VMEM((chunk,), jnp.int32), pltpu.VMEM((chunk, d), rows.dtype)],
    )
    def scatter(rows_hbm, idx_hbm, out_hbm, idx_v, rows_v):
        wid = lax.axis_index("s") * SC_CORES_V7X + lax.axis_index("c")
        base = wid * per_w

        @pl.loop(0, per_w // chunk)
        def _(j):
            off = pl.multiple_of(base + j * chunk, 8)
            pltpu.sync_copy(rows_hbm.at[pl.ds(off, chunk)], rows_v)
            for k in range(copies):
                pltpu.sync_copy(idx_hbm.at[pl.ds(pl.multiple_of(k * t + off, 8), chunk)], idx_v)
                pltpu.sync_copy(rows_v, out_hbm.at[idx_v])

    return scatter(rows, idx)


def _experts_kernel(be_ref, x_ref, wg_ref, bg_ref, wu_ref, bu_ref, wd_ref, bd_ref, y_ref,
                    wgb_ref, wub_ref, wdb_ref):
    i = pl.program_id(0)
    prev = be_ref[jnp.maximum(i - 1, 0)]
    n_used = be_ref[pl.num_programs(0)]

    @pl.when((i < n_used) & ((i == 0) | (be_ref[i] != prev)))
    def _():
        wgb_ref[...] = wg_ref[...].astype(BF16)
        wub_ref[...] = wu_ref[...].astype(BF16)
        wdb_ref[...] = wd_ref[...].astype(BF16)

    @pl.when(i < n_used)
    def _():
        x_lo, x_hi = _unpack_bf16_pairs(x_ref[...])
        xb = jnp.concatenate([x_lo.astype(BF16), x_hi.astype(BF16)], axis=1)
        g = jnp.minimum(_dot(xb, wgb_ref[...]) + bg_ref[...], SWIGLU_LIMIT)
        u = jnp.clip(_dot(xb, wub_ref[...]) + bu_ref[...], -SWIGLU_LIMIT, SWIGLU_LIMIT)
        hdn = (u + 1.0) * (g * jax.nn.sigmoid(SWIGLU_ALPHA * g))
        y_ref[...] = _pack_bf16_pairs(_dot(hdn.astype(BF16), wdb_ref[...]) + bd_ref[...])


def experts(x_rows, blk_e, wg, bg, wu, bu, wd, bd, *, layer, bm):
    n_rows, dp = x_rows.shape
    d, f = wg.shape[2], wg.shape[3]
    w_spec = lambda shape: pl.BlockSpec((None, None) + shape, lambda i, be: (layer, be[i], 0, 0))
    grid_spec = pltpu.PrefetchScalarGridSpec(
        num_scalar_prefetch=1,
        grid=(n_rows // bm,),
        in_specs=[
            pl.BlockSpec((bm, dp), lambda i, be: (i, 0)),
            w_spec((d, f)), w_spec((1, f)), w_spec((d, f)), w_spec((1, f)), w_spec((f, d)), w_spec((1, d)),
        ],
        out_specs=pl.BlockSpec((bm, dp), lambda i, be: (i, 0)),
        scratch_shapes=[pltpu.VMEM((d, f), BF16), pltpu.VMEM((d, f), BF16), pltpu.VMEM((f, d), BF16)],
    )
    return pl.pallas_call(
        _experts_kernel,
        out_shape=jax.ShapeDtypeStruct((n_rows, dp), jnp.uint32),
        grid_spec=grid_spec,
        compiler_params=_cp(("arbitrary",)),
    )(blk_e, x_rows, wg, bg, wu, bu, wd, bd)


def _moe_ln_kernel(x_ref, y_ref, w_ref, g_ref, b_ref, o_ref, *, alpha):
    w = w_ref[...]
    y_lo = y_hi = None
    for k in range(TOP_K):
        lo, hi = _unpack_bf16_pairs(y_ref[k])
        wk = w[:, TOP_K + k:TOP_K + k + 1]
        y_lo = wk * lo if y_lo is None else y_lo + wk * lo
        y_hi = wk * hi if y_hi is None else y_hi + wk * hi
    y = jnp.concatenate([y_lo, y_hi], axis=1)
    o_ref[...] = _layer_norm(alpha * x_ref[...] + y, g_ref[...], b_ref[...])


def moe_ln(x, y4, w4p, g, b, *, alpha, tm):
    t = x.shape[0]
    return pl.pallas_call(
        functools.partial(_moe_ln_kernel, alpha=alpha),
        out_shape=jax.ShapeDtypeStruct((t, D_MODEL), F32),
        grid=(t // tm,),
        in_specs=[
            pl.BlockSpec((tm, D_MODEL), lambda i: (i, 0)),
            pl.BlockSpec((TOP_K, tm, D_MODEL // 2), lambda i: (0, i, 0)),
            pl.BlockSpec((tm, 128), lambda i: (i, 0)),
            pl.BlockSpec((1, D_MODEL), lambda i: (0, 0)),
            pl.BlockSpec((1, D_MODEL), lambda i: (0, 0)),
        ],
        out_specs=pl.BlockSpec((tm, D_MODEL), lambda i: (i, 0)),
        compiler_params=_cp(("arbitrary",)),
    )(x, y4, w4p, g, b)


def _rope_tables(seq):
    inv = np.asarray(ROPE_THETA ** (-np.arange(0, MLA_ROPE, 2) / MLA_ROPE), np.float32)
    ang = jnp.arange(seq, dtype=F32)[:, None] * jnp.asarray(inv)[None, :]
    cos, sin = jnp.cos(ang), jnp.sin(ang)
    zeros = jnp.zeros((seq, 128 - MLA_ROPE), F32)
    return (jnp.concatenate([cos, cos, zeros], axis=1), jnp.concatenate([-sin, sin, zeros], axis=1))


def _nsa_constants(seq, tq, tk):
    nc = seq // NSA_CMP_STRIDE
    qa = np.zeros((NSA_GROUPS, NSA_HG * tq, 128), np.float32)
    for g in range(NSA_GROUPS):
        for hg in range(NSA_HG):
            slope = 2.0 ** (-8.0 * (g * NSA_HG + hg + 1) / NSA_HEADS)
            qa[g, hg * tq:(hg + 1) * tq, 0] = slope * NSA_SEL_LEN
            qa[g, hg * tq:(hg + 1) * tq, 1] = slope
            qa[g, hg * tq:(hg + 1) * tq, 2] = slope * NSA_CMP_STRIDE
            qa[g, hg * tq:(hg + 1) * tq, 3] = slope * (NSA_CMP_LEN - 1) / 2.0
    kpos = np.zeros((seq, 128), np.float32)
    kpos[:, 0] = np.arange(seq) // NSA_SEL_LEN
    kpos[:, 1] = np.arange(seq) % NSA_SEL_LEN
    cpos = np.zeros((nc, 128), np.float32)
    cpos[:, 2] = np.arange(nc)
    cpos[:, 3] = 1.0
    for arr in (qa, kpos, cpos):
        assert np.array_equal(arr.astype(BF16).astype(np.float32), arr)
    n_cmp = (seq - NSA_CMP_LEN) // NSA_CMP_STRIDE + 1
    n_sel = seq // NSA_SEL_LEN
    cs = np.arange(nc) * NSA_CMP_STRIDE
    ss = np.arange(n_sel) * NSA_SEL_LEN
    ov = np.clip(np.minimum(cs[:, None] + NSA_CMP_LEN, ss[None, :] + NSA_SEL_LEN)
                 - np.maximum(cs[:, None], ss[None, :]), 0, None) / NSA_CMP_LEN
    ov[n_cmp:] = 0.0
    e = (np.arange(seq)[None, :] // NSA_SEL_LEN == np.arange(n_sel)[:, None]).astype(np.float32)
    e = e.reshape(n_sel, seq // tk, tk).transpose(1, 0, 2)
    return tuple(jnp.asarray(a, BF16) for a in (ov.T, e, qa, kpos, cpos))


def _pad_cols(w, width):
    return jnp.pad(w, ((0, 0), (0, width - w.shape[1])))


def _swap_halves(w):
    half = w.shape[1] // 2
    return jnp.concatenate([w[:, half:], w[:, :half]], axis=1)


def _layer_weights(w_in, w_q_up, w_kv_up):
    kr = w_in[:, OFF_KR:OFF_NSA_Q]
    gate = w_in[:, OFF_NSA_GATE:OFF_SB]
    per_g = NSA_HG * 3
    w_mla = jnp.concatenate([
        w_in[:, OFF_CQ:OFF_KR],
        _pad_cols(kr, 128), _pad_cols(_swap_halves(kr), 128),
        _pad_cols(gate[:, :per_g], 128), _pad_cols(gate[:, per_g:], 128)], axis=1).astype(BF16)
    wq = w_q_up.reshape(MLA_Q_RANK, MLA_HEADS, MLA_NOPE + MLA_ROPE)
    rope = wq[:, :, MLA_NOPE:]
    rope_sw = jnp.concatenate([rope[:, :, MLA_ROPE // 2:], rope[:, :, :MLA_ROPE // 2]], axis=2)
    pad = ((0, 0), (0, 0), (0, 128 - MLA_ROPE))
    wq3 = jnp.concatenate([
        wq[:, :, :MLA_NOPE].reshape(MLA_Q_RANK, -1),
        jnp.pad(rope, pad).reshape(MLA_Q_RANK, -1),
        jnp.pad(rope_sw, pad).reshape(MLA_Q_RANK, -1)], axis=1).astype(BF16)
    wkv = w_kv_up.reshape(MLA_KV_RANK, MLA_HEADS, 2, 128)
    wkv = jnp.concatenate([wkv[:, :, 0].reshape(MLA_KV_RANK, -1),
                           wkv[:, :, 1].reshape(MLA_KV_RANK, -1)], axis=1).astype(BF16)
    w_heads = jnp.concatenate([w_in[:, OFF_NSA_Q:OFF_NSA_KV], w_in[:, OFF_SB:OFF_MERGE],
                               w_in[:, OFF_NSA_KV:OFF_NSA_GATE]], axis=1).astype(BF16)
    wm = w_in[:, OFF_MERGE:].reshape(D_MODEL, N_BRANCH, D_MODEL).transpose(1, 0, 2).astype(BF16)
    return w_mla, wq3, wkv, w_heads, wm


def _forward(x, mem, w_in, mla_q_norm, mla_w_q_up, mla_kv_norm, mla_w_kv_up,
             nsa_pe_k, nsa_pe_v, nsa_w1_k, nsa_w1_v, nsa_w2_k, nsa_w2_v,
             w_branch, b_merge, w_out, ln_mix_g, ln_mix_b,
             mem_w_q, mem_w_k, mem_w_v, mem_w_o, ln_mem_g, ln_mem_b,
             moe_w_router, moe_b_router, moe_w_gate, moe_b_gate, moe_w_up, moe_b_up,
             moe_w_down, moe_b_down, ln_moe_g, ln_moe_b):
    batch, seq, _ = x.shape
    mem_len = mem.shape[1]
    depth = w_in.shape[0]
    t = batch * seq
    alpha = float((2 * depth) ** 0.25)
    bm = MOE_BLOCK_ROWS
    n_rows = t * TOP_K + N_EXPERTS * bm
    n_blocks = n_rows // bm

    tm_in = min(512, seq)
    tq_mla = min(512, seq)
    tq_sb = 256
    tq_nsa, tk_nsa = 256, 512
    tm_ln = 256
    sc_chunk = 64

    cos128, sin128 = _rope_tables(seq)
    nsa_consts = _nsa_constants(seq, tq_nsa, tk_nsa)
    u_sb = jnp.asarray(np.arange(tq_sb)[:, None] > np.arange(tq_sb)[None, :], BF16)
    tri_router = jnp.asarray(np.arange(tm_ln)[:, None] > np.arange(tm_ln)[None, :], BF16)
    n_qheads = NSA_HEADS
    n_kvheads = 3 * 2 * NSA_GROUPS
    head_scale = np.ones((1, (n_qheads + n_kvheads + 3 * SB_HEADS) * HEAD_DIM), np.float32)
    head_scale[:, :n_qheads * HEAD_DIM] = HEAD_DIM ** -0.5
    sb0 = n_qheads
    kv0 = sb0 + 3 * SB_HEADS
    tn_heads = (n_qheads + n_kvheads + 3 * SB_HEADS) // 4 * HEAD_DIM
    head_scale[:, sb0 * HEAD_DIM:(sb0 + SB_HEADS) * HEAD_DIM] = HEAD_DIM ** -0.5
    head_scale = jnp.asarray(head_scale)
    ones_kv = jnp.ones((1, 2 * MEM_HEADS * HEAD_DIM), F32)

    b_gate4 = moe_b_gate.reshape(depth, N_EXPERTS, 1, D_EXPERT)
    b_up4 = moe_b_up.reshape(depth, N_EXPERTS, 1, D_EXPERT)
    b_down4 = moe_b_down.reshape(depth, N_EXPERTS, 1, D_MODEL)

    xf = x.reshape(t, D_MODEL)
    memf = mem.reshape(batch * mem_len, D_MODEL)
    row = lambda v: v.reshape(1, -1)

    for l in range(depth):
        w_mla, wq3, wkv, w_heads, wm = _layer_weights(w_in[l], mla_w_q_up[l], mla_w_kv_up[l])

        q_a, k_a, v_a, gates = mla_in(xf, w_mla, row(mla_q_norm[l]), row(mla_kv_norm[l]), wq3, wkv,
                                      cos128, sin128, seq=seq, tm=tm_ln)
        hm = proj_heads(xf, w_heads, head_scale, tm=tm_in, tn=tn_heads)
        o_a = mla_attn(q_a, k_a, v_a, batch=batch, seq=seq, tq=tq_mla, heads=2)
        w1 = jnp.stack([nsa_w1_k[l], nsa_w1_v[l]]).astype(BF16)
        pe = jnp.stack([nsa_pe_k[l], nsa_pe_v[l]]).reshape(2, 1, -1)
        pe = jnp.broadcast_to(pe, (2, 8, pe.shape[-1])).astype(BF16)
        w2 = jnp.stack([nsa_w2_k[l], nsa_w2_v[l]]).astype(BF16)
        cmp = nsa_compress(hm[kv0:kv0 + 4], w1, pe, w2, batch=batch, seq=seq)
        o_b = nsa_attn(hm, cmp, gates, nsa_consts, q_head0=0, kv_head0=kv0,
                       batch=batch, seq=seq, tq=tq_nsa, tk=tk_nsa)
        o_c = sb_attn(hm, u_sb, head0=sb0, batch=batch, seq=seq, tq=tq_sb, heads=4)
        y = merge_branches(xf, o_a, o_b, o_c, wm, w_branch[l].astype(BF16),
                           b_merge[l].reshape(N_BRANCH, 1, D_MODEL), tm=tm_in, tn=512)
        w_kv_mem = jnp.concatenate([mem_w_k[l], mem_w_v[l]], axis=1).astype(BF16)
        kv_mem = proj_heads(memf, w_kv_mem, ones_kv, tm=min(512, batch * mem_len), tn=512)
        wr = moe_w_router[l]
        wr_hi = wr.astype(BF16)
        wr_lo = (wr - wr_hi.astype(F32)).astype(BF16)
        xf, route, x_packed, counts = post_mix(
            y, w_out[l].astype(BF16), xf, (row(ln_mix_g[l]), row(ln_mix_b[l])),
            mem_w_q[l].astype(BF16), kv_mem, mem_w_o[l].astype(BF16), (row(ln_mem_g[l]), row(ln_mem_b[l])),
            jnp.stack([wr_hi, wr_lo]), row(moe_b_router[l]), tri_router,
            alpha=alpha, seq=seq, mem_len=mem_len, tm=tm_ln)

        padded = (counts[0].astype(jnp.int32) + bm - 1) // bm * bm
        pad_end = jnp.cumsum(padded)
        pad_start = pad_end - padded
        idx4 = route[:, 0:TOP_K].astype(jnp.int32)
        rank4 = route[:, 2 * TOP_K:3 * TOP_K].astype(jnp.int32)
        start4 = jnp.sum(jnp.where(idx4[:, :, None] == jnp.arange(N_EXPERTS, dtype=jnp.int32), pad_start, 0), axis=2)
        pos4 = start4 + rank4
        blk_row0 = jnp.arange(n_blocks, dtype=jnp.int32)[:, None] * bm
        blk_e = jnp.minimum(jnp.sum((pad_end[None, :] <= blk_row0).astype(jnp.int32), axis=1), N_EXPERTS - 1)
        blk_e = jnp.concatenate([blk_e, pad_end[-1:] // bm]).astype(jnp.int32)
        pos_kmajor = pos4.T.reshape(-1)
        x_rows = sc_scatter_rows(x_packed, pos_kmajor, n_rows, copies=TOP_K, chunk=sc_chunk)
        y_rows = experts(x_rows, blk_e, moe_w_gate, b_gate4, moe_w_up, b_up4, moe_w_down, b_down4,
                         layer=l, bm=bm)
        y4 = sc_gather_rows(y_rows, pos_kmajor, chunk=sc_chunk).reshape(TOP_K, t, D_MODEL // 2)
        xf = moe_ln(xf, y4, route, row(ln_moe_g[l]), row(ln_moe_b[l]), alpha=alpha, tm=tm_ln)

    return xf.reshape(batch, seq, D_MODEL)


def kernel(x, mem, w_in, mla_q_norm, mla_w_q_up, mla_kv_norm, mla_w_kv_up, nsa_pe_k, nsa_pe_v, nsa_w1_k, nsa_w1_v, nsa_w2_k, nsa_w2_v, w_branch, b_merge, w_out, ln_mix_g, ln_mix_b, mem_w_q, mem_w_k, mem_w_v, mem_w_o, ln_mem_g, ln_mem_b, moe_w_router, moe_b_router, moe_w_gate, moe_b_gate, moe_w_up, moe_b_up, moe_w_down, moe_b_down, ln_moe_g, ln_moe_b):
    return _forward(x, mem, w_in, mla_q_norm, mla_w_q_up, mla_kv_norm, mla_w_kv_up,
                    nsa_pe_k, nsa_pe_v, nsa_w1_k, nsa_w1_v, nsa_w2_k, nsa_w2_v,
                    w_branch, b_merge, w_out, ln_mix_g, ln_mix_b,
                    mem_w_q, mem_w_k, mem_w_v, mem_w_o, ln_mem_g, ln_mem_b,
                    moe_w_router, moe_b_router, moe_w_gate, moe_b_gate, moe_w_up, moe_b_up,
                    moe_w_down, moe_b_down, ln_moe_g, ln_moe_b)
```

```python
import functools

import numpy as np
import jax
import jax.numpy as jnp
from jax import lax
from jax.experimental import pallas as pl
from jax.experimental.pallas import tpu as pltpu
from jax.experimental.pallas import tpu_sc as plsc

F32 = jnp.float32
BF16 = jnp.bfloat16

D_MODEL = 2048
HEAD_DIM = 128
MLA_HEADS = 8
MLA_Q_RANK = 512
MLA_KV_RANK = 256
MLA_NOPE = 128
MLA_ROPE = 64
ROPE_THETA = 10000.0
NSA_HEADS = 8
NSA_GROUPS = 2
NSA_HG = NSA_HEADS // NSA_GROUPS
NSA_CMP_LEN = 32
NSA_CMP_STRIDE = 16
NSA_SEL_LEN = 64
NSA_TOPK = 16
NSA_WINDOW = 512
SB_HEADS = 8
MEM_HEADS = 4
N_EXPERTS = 32
TOP_K = 4
D_EXPERT = 512
SWIGLU_LIMIT = 7.0
SWIGLU_ALPHA = 1.702
N_BRANCH = 3
BRANCH_WIDTH = 1024
LN_EPS = 1e-5
RMS_EPS = 1e-6
NEG = -1e30
BIG = 1e30
SB_UNDERFLOW_LOG = -100.0

OFF_CQ = 0
OFF_CKV = 512
OFF_KR = 768
OFF_NSA_Q = 832
OFF_NSA_KV = 1856
OFF_NSA_GATE = 3392
OFF_SB = 3416
OFF_MERGE = 6488
MLA_IN_WIDTH = MLA_Q_RANK + MLA_KV_RANK + 2 * 128 + NSA_GROUPS * 128

VMEM_LIMIT_V7X = 56 * 1024 * 1024
MOE_BLOCK_ROWS = 512
SC_CORES_V7X = 2
SC_SUBCORES_V7X = 16


def _cp(sem, vmem=VMEM_LIMIT_V7X):
    return pltpu.CompilerParams(dimension_semantics=sem, vmem_limit_bytes=vmem)


def _dot(a, b):
    return jnp.dot(a, b, preferred_element_type=F32)


def _dot_nt(a, b):
    return lax.dot_general(a, b, (((1,), (1,)), ((), ())), preferred_element_type=F32)


def _layer_norm(z, g, b):
    mu = jnp.mean(z, axis=-1, keepdims=True)
    zc = z - mu
    var = jnp.mean(zc * zc, axis=-1, keepdims=True)
    return zc * lax.rsqrt(var + LN_EPS) * g + b


def _rms_norm(z, g):
    return z * lax.rsqrt(jnp.mean(z * z, axis=-1, keepdims=True) + RMS_EPS) * g


def _pack_bf16_pairs(z):
    n = z.shape[1] // 2
    bits = pltpu.bitcast(z.astype(BF16).astype(F32), jnp.uint32)
    return lax.shift_right_logical(bits[:, :n], jnp.uint32(16)) | (bits[:, n:] & jnp.uint32(0xFFFF0000))


def _unpack_bf16_pairs(w):
    lo = pltpu.bitcast(lax.shift_left(w, jnp.uint32(16)), F32)
    hi = pltpu.bitcast(w & jnp.uint32(0xFFFF0000), F32)
    return lo, hi


def _proj_heads_kernel(a_ref, w_ref, s_ref, o_ref, abf_ref, *, n_heads_per_tile):
    @pl.when(pl.program_id(1) == 0)
    def _():
        abf_ref[...] = a_ref[...].astype(BF16)

    acc = _dot(abf_ref[...], w_ref[...]) * s_ref[...]
    for c in range(n_heads_per_tile):
        o_ref[c] = acc[:, c * HEAD_DIM:(c + 1) * HEAD_DIM].astype(o_ref.dtype)


def proj_heads(a, w, scale, *, tm, tn):
    m, k = a.shape
    n = w.shape[1]
    hpt = tn // HEAD_DIM
    return pl.pallas_call(
        functools.partial(_proj_heads_kernel, n_heads_per_tile=hpt),
        out_shape=jax.ShapeDtypeStruct((n // HEAD_DIM, m, HEAD_DIM), BF16),
        grid=(m // tm, n // tn),
        in_specs=[
            pl.BlockSpec((tm, k), lambda i, j: (i, 0)),
            pl.BlockSpec((k, tn), lambda i, j: (0, j)),
            pl.BlockSpec((1, tn), lambda i, j: (0, j)),
        ],
        out_specs=pl.BlockSpec((hpt, tm, HEAD_DIM), lambda i, j: (j, i, 0)),
        scratch_shapes=[pltpu.VMEM((tm, k), BF16)],
        compiler_params=_cp(("arbitrary", "arbitrary")),
    )(a, w, scale)


def _mla_in_kernel(x_ref, w_ref, qg_ref, kg_ref, wq_ref, wkv_ref, cos_ref, sin_ref,
                   q_ref, k_ref, v_ref, g_ref):
    _mla_in_body(x_ref[...], w_ref, qg_ref, kg_ref, wq_ref, wkv_ref, cos_ref, sin_ref, q_ref, k_ref, v_ref, g_ref)


def _mla_in_body(x, w_ref, qg_ref, kg_ref, wq_ref, wkv_ref, cos_ref, sin_ref, q_ref, k_ref, v_ref, g_ref):
    xb = x.astype(BF16)
    h = _dot(xb, w_ref[...])
    c0, c1 = MLA_Q_RANK, MLA_Q_RANK + MLA_KV_RANK
    cq = h[:, 0:c0]
    ckv = h[:, c0:c1]
    kr1 = h[:, c1:c1 + 128]
    kr2 = h[:, c1 + 128:c1 + 256]
    g_ref[...] = jax.nn.sigmoid(h[:, c1 + 256:MLA_IN_WIDTH])
    cos = cos_ref[...]
    sin = sin_ref[...]
    scale = (MLA_NOPE + MLA_ROPE) ** -0.5
    hw = MLA_HEADS * 128
    nq = _rms_norm(cq, qg_ref[...]).astype(BF16)
    q3 = _dot(nq, wq_ref[...])
    for hh in range(MLA_HEADS):
        lo, hi = hh * 128, (hh + 1) * 128
        q_ref[hh, :, 0:128] = (q3[:, lo:hi] * scale).astype(BF16)
        rot = q3[:, hw + lo:hw + hi] * cos + q3[:, 2 * hw + lo:2 * hw + hi] * sin
        q_ref[hh, :, 128:256] = (rot * scale).astype(BF16)
    nkv = _rms_norm(ckv, kg_ref[...]).astype(BF16)
    kv = _dot(nkv, wkv_ref[...])
    krot = (kr1 * cos + kr2 * sin).astype(BF16)
    for hh in range(MLA_HEADS):
        lo, hi = hh * 128, (hh + 1) * 128
        k_ref[hh, :, 0:128] = kv[:, lo:hi].astype(BF16)
        k_ref[hh, :, 128:256] = krot
        v_ref[hh] = kv[:, hw + lo:hw + hi].astype(BF16)


def mla_in(x, w_mla, qg, kg, wq3, wkv, cos128, sin128, *, seq, tm):
    t = x.shape[0]
    npos = seq // tm
    full = lambda shape: pl.BlockSpec(shape, lambda i: (0,) * len(shape))
    return pl.pallas_call(
        _mla_in_kernel,
        out_shape=(
            jax.ShapeDtypeStruct((MLA_HEADS, t, 256), BF16),
            jax.ShapeDtypeStruct((MLA_HEADS, t, 256), BF16),
            jax.ShapeDtypeStruct((MLA_HEADS, t, 128), BF16),
            jax.ShapeDtypeStruct((t, 256), F32),
        ),
        grid=(t // tm,),
        in_specs=[
            pl.BlockSpec((tm, D_MODEL), lambda i: (i, 0)),
            full((D_MODEL, MLA_IN_WIDTH)),
            full((1, MLA_Q_RANK)),
            full((1, MLA_KV_RANK)),
            full((MLA_Q_RANK, 3 * MLA_HEADS * 128)),
            full((MLA_KV_RANK, 2 * MLA_HEADS * 128)),
            pl.BlockSpec((tm, 128), lambda i: (i % npos, 0)),
            pl.BlockSpec((tm, 128), lambda i: (i % npos, 0)),
        ],
        out_specs=(
            pl.BlockSpec((MLA_HEADS, tm, 256), lambda i: (0, i, 0)),
            pl.BlockSpec((MLA_HEADS, tm, 256), lambda i: (0, i, 0)),
            pl.BlockSpec((MLA_HEADS, tm, 128), lambda i: (0, i, 0)),
            pl.BlockSpec((tm, 256), lambda i: (i, 0)),
        ),
        compiler_params=_cp(("arbitrary",)),
    )(x, w_mla, qg, kg, wq3, wkv, cos128, sin128)


def _mla_attn_kernel(q_ref, k_ref, v_ref, o_ref, s_ref, *, tq, heads):
    qi = pl.program_id(2)

    def scores(h, kt):
        k0 = pl.multiple_of(kt * tq, tq)
        return _dot_nt(q_ref[h], k_ref[h, pl.ds(k0, tq), :])

    def consume(h, kt, s, carry, diag):
        m, l, acc = carry
        k0 = pl.multiple_of(kt * tq, tq)
        v = v_ref[h, pl.ds(k0, tq), :]
        if diag:
            row = lax.broadcasted_iota(jnp.int32, (tq, tq), 0)
            col = lax.broadcasted_iota(jnp.int32, (tq, tq), 1)
            s = jnp.where(col <= row, s, NEG)
        m_new = jnp.maximum(m, jnp.max(s, axis=1, keepdims=True))
        alpha = jnp.exp(m - m_new)
        p = jnp.exp((s - m_new).astype(BF16))
        l = alpha * l + jnp.sum(p.astype(F32), axis=1, keepdims=True)
        acc = alpha * acc + _dot(p, v)
        return m_new, l, acc

    def fill(slot, kt):
        for h in range(heads):
            s_ref[slot, h] = scores(h, kt)

    def drain(slot, kt, carries, diag):
        return tuple(consume(h, kt, s_ref[slot, h], carries[h], diag) for h in range(heads))

    def body(j, carries):
        kt = 2 * j
        fill(1, kt + 1)
        carries = drain(0, kt, carries, False)
        fill(0, kt + 2)
        return drain(1, kt + 1, carries, False)

    init = (jnp.full((tq, 1), NEG, F32), jnp.zeros((tq, 1), F32), jnp.zeros((tq, 128), F32))
    fill(0, 0)
    carries = lax.fori_loop(0, qi // 2, body, (init,) * heads)

    def even_tail(carries):
        return drain(0, qi, carries, True)

    def odd_tail(carries):
        fill(1, qi)
        return drain(1, qi, drain(0, qi - 1, carries, False), True)

    carries = lax.cond(lax.rem(qi, 2) == 1, odd_tail, even_tail, carries)
    for h, (_, l, acc) in enumerate(carries):
        o_ref[:, h * 128:(h + 1) * 128] = (acc / l).astype(o_ref.dtype)


def mla_attn(q, k, v, *, batch, seq, tq, heads):
    nq = seq // tq
    q4 = q.reshape(MLA_HEADS, batch, seq, 256)
    k4 = k.reshape(MLA_HEADS, batch, seq, 256)
    v4 = v.reshape(MLA_HEADS, batch, seq, 128)
    return pl.pallas_call(
        functools.partial(_mla_attn_kernel, tq=tq, heads=heads),
        out_shape=jax.ShapeDtypeStruct((batch * seq, MLA_HEADS * 128), BF16),
        grid=(MLA_HEADS // heads, batch, nq),
        in_specs=[
            pl.BlockSpec((heads, None, tq, 256), lambda h, b, i: (h, b, i, 0)),
            pl.BlockSpec((heads, None, seq, 256), lambda h, b, i: (h, b, 0, 0)),
            pl.BlockSpec((heads, None, seq, 128), lambda h, b, i: (h, b, 0, 0)),
        ],
        out_specs=pl.BlockSpec((tq, heads * 128), lambda h, b, i: (b * nq + i, h)),
        scratch_shapes=[pltpu.VMEM((2, heads, tq, tq), F32)],
        compiler_params=_cp(("arbitrary", "arbitrary", "arbitrary")),
    )(q4, k4, v4)


def _sb_attn_kernel(q_ref, k_ref, v_ref, u_ref, o_ref, *, tq, heads):
    qi = pl.program_id(2)
    u = u_ref[...]

    def head_step(h, kt, carry, diag):
        run, acc = carry
        k0 = pl.multiple_of(kt * tq, tq)
        k = k_ref[h, pl.ds(k0, tq), :]
        v = v_ref[h, pl.ds(k0, tq), :]
        z = _dot_nt(q_ref[h], k)
        l1m = -(jnp.maximum(z, 0.0) + jnp.log(1.0 + jnp.exp(-jnp.abs(z))))
        if diag:
            row = lax.broadcasted_iota(jnp.int32, (tq, tq), 0)
            col = lax.broadcasted_iota(jnp.int32, (tq, tq), 1)
            strict = col < row
            l1m_m = jnp.where(strict, l1m, 0.0)
        else:
            l1m_m = l1m
        hi = l1m_m.astype(BF16)
        lo = (l1m_m - hi.astype(F32)).astype(BF16)
        between = _dot(hi, u) + _dot(lo, u)
        a = jnp.exp(z + l1m + between + run)
        if diag:
            a = jnp.where(strict, a, 0.0)
        acc = acc + _dot(a.astype(BF16), v)
        run = run + between[:, 0:1] + l1m_m[:, 0:1]
        return run, acc

    def step(kt, carries, diag):
        return tuple(head_step(h, kt, carries[h], diag) for h in range(heads))

    init = (jnp.zeros((tq, 1), F32), jnp.zeros((tq, 128), F32))
    carries = step(qi, (init,) * heads, True)

    def more(c):
        j, carries = c
        top = carries[0][0]
        for run, _ in carries[1:]:
            top = jnp.maximum(top, run)
        return (j < qi) & (jnp.max(top) > SB_UNDERFLOW_LOG)

    def body(c):
        j, carries = c
        return j + 1, step(qi - 1 - j, carries, False)

    _, carries = lax.while_loop(more, body, (jnp.int32(0), carries))
    for h, (_, acc) in enumerate(carries):
        o_ref[:, h * HEAD_DIM:(h + 1) * HEAD_DIM] = acc.astype(o_ref.dtype)


def sb_attn(hm, u, *, head0, batch, seq, tq, heads):
    nq = seq // tq
    assert head0 % heads == 0 and SB_HEADS % heads == 0
    hm4 = hm.reshape(hm.shape[0], batch, seq, HEAD_DIM)
    blk0 = head0 // heads
    per_part = SB_HEADS // heads
    return pl.pallas_call(
        functools.partial(_sb_attn_kernel, tq=tq, heads=heads),
        out_shape=jax.ShapeDtypeStruct((batch * seq, SB_HEADS * HEAD_DIM), BF16),
        grid=(per_part, batch, nq),
        in_specs=[
            pl.BlockSpec((heads, None, tq, HEAD_DIM), lambda h, b, i: (blk0 + h, b, i, 0)),
            pl.BlockSpec((heads, None, seq, HEAD_DIM), lambda h, b, i: (blk0 + per_part + h, b, 0, 0)),
            pl.BlockSpec((heads, None, seq, HEAD_DIM), lambda h, b, i: (blk0 + 2 * per_part + h, b, 0, 0)),
            pl.BlockSpec((tq, tq), lambda h, b, i: (0, 0)),
        ],
        out_specs=pl.BlockSpec((tq, heads * HEAD_DIM), lambda h, b, i: (b * nq + i, h)),
        compiler_params=_cp(("arbitrary", "arbitrary", "arbitrary")),
    )(hm4, hm4, hm4, u)


def _nsa_cmp_kernel(c_ref, w1_ref, pe_ref, w2_ref, o_ref, *, nc):
    c = c_ref[...]
    half = NSA_CMP_STRIDE * HEAD_DIM
    a1 = _dot(c, w1_ref[0:half, :])
    a2 = _dot(c, w1_ref[half:2 * half, :])
    pc = _dot(pe_ref[...], w1_ref[...])[0:1, :]
    pre = a1 + pltpu.roll(a2, nc - 1, 0) + pc
    act = 0.5 * pre * (1.0 + jnp.tanh(0.7978845608028654 * (pre + 0.044715 * (pre * pre * pre))))
    o_ref[...] = _dot(act.astype(BF16), w2_ref[...]).astype(BF16)


def nsa_compress(cmp_heads, w1, pe, w2, *, batch, seq):
    nc = seq // NSA_CMP_STRIDE
    hm4 = cmp_heads.reshape(4, batch, nc, NSA_CMP_STRIDE * HEAD_DIM)
    return pl.pallas_call(
        functools.partial(_nsa_cmp_kernel, nc=nc),
        out_shape=jax.ShapeDtypeStruct((4, batch, nc, HEAD_DIM), BF16),
        grid=(4, batch),
        in_specs=[
            pl.BlockSpec((None, None, nc, NSA_CMP_STRIDE * HEAD_DIM), lambda c, b: (c, b, 0, 0)),
            pl.BlockSpec((None, NSA_CMP_LEN * HEAD_DIM, HEAD_DIM), lambda c, b: (c // 2, 0, 0)),
            pl.BlockSpec((None, 8, NSA_CMP_LEN * HEAD_DIM), lambda c, b: (c // 2, 0, 0)),
            pl.BlockSpec((None, HEAD_DIM, HEAD_DIM), lambda c, b: (c // 2, 0, 0)),
        ],
        out_specs=pl.BlockSpec((None, None, nc, HEAD_DIM), lambda c, b: (c, b, 0, 0)),
        compiler_params=_cp(("arbitrary", "arbitrary")),
    )(hm4, w1, pe, w2)


def _nsa_attn_kernel(q_ref, qa_ref, kc_ref, vc_ref, ks_ref, vs_ref, kw_ref, vw_ref, g_ref, ovt_ref, e_ref,
                     kpos_ref, cpos_ref, o_ref, ss_ref, *, tq, tk, seq, n_sel, n_top):
    qi = pl.program_id(1)
    t0 = qi * tq
    rows = NSA_HG * tq
    nc = seq // NSA_CMP_STRIDE
    groups = range(NSA_GROUPS)

    rid = lax.broadcasted_iota(jnp.int32, (rows, 1), 0)
    trow = t0 + lax.bitwise_and(rid, tq - 1)

    def masked_softmax(s, valid):
        sm = jnp.where(valid, s, NEG)
        m = jnp.max(sm, axis=1, keepdims=True)
        e = jnp.where(valid, jnp.exp(sm - m), 0.0)
        d = jnp.sum(e, axis=1, keepdims=True)
        return e * (1.0 / jnp.where(d > 0.0, d, 1.0))

    n_i = lax.broadcasted_iota(jnp.int32, (1, nc), 1)
    end = n_i * NSA_CMP_STRIDE + (NSA_CMP_LEN - 1)
    cur = lax.shift_right_logical(t0 + lax.broadcasted_iota(jnp.int32, (1, tq), 1),
                                  int(np.log2(NSA_SEL_LEN)))
    blk = lax.broadcasted_iota(jnp.int32, (n_sel, tq), 0)
    forced = (blk == 0) | (blk == cur) | (blk == cur - 1)
    sub = lax.broadcasted_iota(jnp.int32, (8, tq), 0)
    eye = (lax.broadcasted_iota(jnp.int32, (n_sel, n_sel), 0)
           == lax.broadcasted_iota(jnp.int32, (n_sel, n_sel), 1)).astype(F32).astype(BF16)
    ovt = ovt_ref[...]

    def front(g):
        q = jnp.concatenate([q_ref[NSA_HG * g:NSA_HG * (g + 1)].reshape(rows, HEAD_DIM), qa_ref[g]], axis=1)

        s_c = _dot_nt(q, jnp.concatenate([kc_ref[g], cpos_ref[...]], axis=1))
        p_c = masked_softmax(s_c, end <= trow)
        o_c = _dot(p_c.astype(BF16), vc_ref[g])

        psum = p_c[0:tq] + p_c[tq:2 * tq] + p_c[2 * tq:3 * tq] + p_c[3 * tq:4 * tq]
        p_hi = psum.astype(BF16)
        p_lo = (psum - p_hi.astype(F32)).astype(BF16)
        imp = _dot_nt(ovt, p_hi) + _dot_nt(ovt, p_lo)
        key = jnp.where(blk > cur, -BIG, jnp.where(forced, BIG, imp))
        chunks = [key[8 * r:8 * r + 8, :] for r in range(n_sel // 8)]
        ranks = [jnp.zeros((8, tq), F32) for _ in chunks]
        for i in range(n_sel):
            vi = key[i:i + 1, :]
            for r, kc in enumerate(chunks):
                gt = jnp.where(vi > kc, 1.0, 0.0)
                if r < i // 8:
                    win = gt
                else:
                    ge = jnp.where(vi >= kc, 1.0, 0.0)
                    win = ge if r > i // 8 else jnp.where(sub > i % 8, ge, gt)
                ranks[r] = ranks[r] + win
        rank = jnp.concatenate(ranks, axis=0)
        selm_t = jnp.where((rank < float(n_top)) & (blk <= cur), 1.0, 0.0).astype(BF16)
        selm = lax.dot_general(selm_t, eye, (((0,), (0,)), ((), ())), preferred_element_type=F32).astype(BF16)
        return q, selm, o_c

    fronts = [front(g) for g in groups]

    def sel_fill(slot, kt):
        k0 = pl.multiple_of(kt * tk, tk)
        kp = kpos_ref[pl.ds(k0, tk), :]
        for g in groups:
            q, selm, _ = fronts[g]
            kk = jnp.concatenate([ks_ref[g, pl.ds(k0, tk), :], kp], axis=1)
            mex = _dot(selm, e_ref[kt])
            bias = (mex - 1.0) * BIG
            ss_ref[slot, g] = _dot_nt(q, kk) + jnp.concatenate([bias] * NSA_HG, axis=0)

    def sel_drain_group(g, slot, kt, carry, diag):
        m, l, acc = carry
        k0 = pl.multiple_of(kt * tk, tk)
        vv = vs_ref[g, pl.ds(k0, tk), :]
        sm = ss_ref[slot, g]
        if diag:
            spos = k0 + lax.broadcasted_iota(jnp.int32, (1, tk), 1)
            sm = jnp.where(spos <= trow, sm, NEG)
        m_new = jnp.maximum(m, jnp.max(sm, axis=1, keepdims=True))
        alpha = jnp.exp(m - m_new)
        p = jnp.exp((sm - m_new).astype(BF16))
        l = alpha * l + jnp.sum(p.astype(F32), axis=1, keepdims=True)
        acc = alpha * acc + _dot(p, vv)
        return m_new, l, acc

    def sel_drain(slot, kt, carries, diag):
        return tuple(sel_drain_group(g, slot, kt, carries[g], diag) for g in groups)

    def sel_body(j, carries):
        kt = 2 * j
        sel_fill(1, kt + 1)
        carries = sel_drain(0, kt, carries, False)
        sel_fill(0, kt + 2)
        return sel_drain(1, kt + 1, carries, False)

    kt_last = t0 // tk
    init = (jnp.full((rows, 1), NEG, F32), jnp.zeros((rows, 1), F32), jnp.zeros((rows, HEAD_DIM), F32))
    sel_fill(0, 0)
    carries = lax.fori_loop(0, kt_last // 2, sel_body, (init,) * NSA_GROUPS)

    def even_tail(carries):
        return sel_drain(0, kt_last, carries, True)

    def odd_tail(carries):
        sel_fill(1, kt_last)
        return sel_drain(1, kt_last, sel_drain(0, kt_last - 1, carries, False), True)

    carries = lax.cond(lax.rem(kt_last, 2) == 1, odd_tail, even_tail, carries)

    wk = NSA_WINDOW + tq
    ks0 = pl.multiple_of(jnp.maximum(t0 - NSA_WINDOW, 0), tq)
    kpw = kpos_ref[pl.ds(ks0, wk), :]
    wpos = ks0 + lax.broadcasted_iota(jnp.int32, (1, wk), 1)
    dw = trow - wpos
    in_window = pltpu.bitcast(dw, jnp.uint32) < jnp.uint32(NSA_WINDOW)
    gt = g_ref[...]
    for g in groups:
        q, _, o_c = fronts[g]
        _, l_s, acc_s = carries[g]
        o_s = acc_s * (1.0 / l_s)
        kw = jnp.concatenate([kw_ref[g, pl.ds(ks0, wk), :], kpw], axis=1)
        vw = vw_ref[g, pl.ds(ks0, wk), :]
        sm_w = jnp.where(in_window, _dot_nt(q, kw), NEG)
        p_w = jnp.exp((sm_w - jnp.max(sm_w, axis=1, keepdims=True)).astype(BF16))
        o_w = _dot(p_w, vw) * (1.0 / jnp.sum(p_w.astype(F32), axis=1, keepdims=True))
        for hg in range(NSA_HG):
            sl = slice(hg * tq, (hg + 1) * tq)
            c0 = 128 * g + 3 * hg
            o = (gt[:, c0:c0 + 1] * o_c[sl] + gt[:, c0 + 1:c0 + 2] * o_s[sl] + gt[:, c0 + 2:c0 + 3] * o_w[sl])
            h = NSA_HG * g + hg
            o_ref[:, h * HEAD_DIM:(h + 1) * HEAD_DIM] = o.astype(o_ref.dtype)


def nsa_attn(hm, cmp, gates, consts, *, q_head0, kv_head0, batch, seq, tq, tk):
    ov, e, qa, kpos, cpos = consts
    nq = seq // tq
    nc = seq // NSA_CMP_STRIDE
    n_sel = seq // NSA_SEL_LEN
    n_top = min(NSA_TOPK, n_sel)
    assert tk % tq == 0 and seq % tk == 0 and seq >= NSA_WINDOW + tq and NSA_WINDOW % tq == 0
    assert q_head0 % NSA_HEADS == 0 and kv_head0 % NSA_GROUPS == 0
    hm4 = hm.reshape(hm.shape[0], batch, seq, HEAD_DIM)
    kv_spec = lambda off: pl.BlockSpec((NSA_GROUPS, None, seq, HEAD_DIM),
                                       lambda b, i: ((kv_head0 + off) // NSA_GROUPS, b, 0, 0))
    return pl.pallas_call(
        functools.partial(_nsa_attn_kernel, tq=tq, tk=tk, seq=seq, n_sel=n_sel, n_top=n_top),
        out_shape=jax.ShapeDtypeStruct((batch * seq, NSA_HEADS * HEAD_DIM), BF16),
        grid=(batch, nq),
        in_specs=[
            pl.BlockSpec((NSA_HEADS, None, tq, HEAD_DIM), lambda b, i: (q_head0 // NSA_HEADS, b, i, 0)),
            pl.BlockSpec((NSA_GROUPS, NSA_HG * tq, 128), lambda b, i: (0, 0, 0)),
            pl.BlockSpec((NSA_GROUPS, None, nc, HEAD_DIM), lambda b, i: (0, b, 0, 0)),
            pl.BlockSpec((NSA_GROUPS, None, nc, HEAD_DIM), lambda b, i: (1, b, 0, 0)),
            kv_spec(4), kv_spec(6), kv_spec(8), kv_spec(10),
            pl.BlockSpec((tq, NSA_GROUPS * 128), lambda b, i: (b * nq + i, 0)),
            pl.BlockSpec((n_sel, nc), lambda b, i: (0, 0)),
            pl.BlockSpec((seq // tk, n_sel, tk), lambda b, i: (0, 0, 0)),
            pl.BlockSpec((seq, 128), lambda b, i: (0, 0)),
            pl.BlockSpec((nc, 128), lambda b, i: (0, 0)),
        ],
        out_specs=pl.BlockSpec((tq, NSA_HEADS * HEAD_DIM), lambda b, i: (b * nq + i, 0)),
        scratch_shapes=[pltpu.VMEM((2, NSA_GROUPS, NSA_HG * tq, tk), F32)],
        compiler_params=_cp(("arbitrary", "arbitrary")),
    )(hm4, qa, cmp, cmp, hm4, hm4, hm4, hm4, gates, ov, e, kpos, cpos)


def _merge_kernel(x_ref, oa_ref, ob_ref, oc_ref, wm_ref, wb_ref, bm_ref, y_ref, xb_ref):
    @pl.when(pl.program_id(1) == 0)
    def _():
        xb_ref[...] = x_ref[...].astype(BF16)

    xb = xb_ref[...]
    acc = None
    for br, o_ref in enumerate((oa_ref, ob_ref, oc_ref)):
        gate = jax.nn.sigmoid(_dot(xb, wm_ref[br]) + bm_ref[br])
        term = gate * _dot(o_ref[...], wb_ref[br])
        acc = term if acc is None else acc + term
    y_ref[...] = acc.astype(y_ref.dtype)


def merge_branches(x, o_a, o_b, o_c, wm, wb, bm, *, tm, tn):
    t = x.shape[0]
    o_spec = pl.BlockSpec((tm, BRANCH_WIDTH), lambda i, j: (i, 0))
    return pl.pallas_call(
        _merge_kernel,
        out_shape=jax.ShapeDtypeStruct((t, D_MODEL), BF16),
        grid=(t // tm, D_MODEL // tn),
        in_specs=[
            pl.BlockSpec((tm, D_MODEL), lambda i, j: (i, 0)),
            o_spec, o_spec, o_spec,
            pl.BlockSpec((N_BRANCH, D_MODEL, tn), lambda i, j: (0, 0, j)),
            pl.BlockSpec((N_BRANCH, BRANCH_WIDTH, tn), lambda i, j: (0, 0, j)),
            pl.BlockSpec((N_BRANCH, 1, tn), lambda i, j: (0, 0, j)),
        ],
        out_specs=pl.BlockSpec((tm, tn), lambda i, j: (i, j)),
        scratch_shapes=[pltpu.VMEM((tm, D_MODEL), BF16)],
        compiler_params=_cp(("arbitrary", "arbitrary")),
    )(x, o_a, o_b, o_c, wm, wb, bm)


def _out_ln_kernel(y_ref, w_ref, x_ref, g_ref, b_ref, o_ref, *, alpha):
    half = y_ref.shape[0] // 2
    for r in range(2):
        rs = slice(r * half, (r + 1) * half)
        h = _dot(y_ref[rs, :], w_ref[...])
        o_ref[rs, :] = _layer_norm(alpha * x_ref[rs, :] + h, g_ref[...], b_ref[...])


def out_ln(y, w, x, g, b, *, alpha, tm):
    t = x.shape[0]
    return pl.pallas_call(
        functools.partial(_out_ln_kernel, alpha=alpha),
        out_shape=jax.ShapeDtypeStruct((t, D_MODEL), F32),
        grid=(t // tm,),
        in_specs=[
            pl.BlockSpec((tm, D_MODEL), lambda i: (i, 0)),
            pl.BlockSpec((D_MODEL, D_MODEL), lambda i: (0, 0)),
            pl.BlockSpec((tm, D_MODEL), lambda i: (i, 0)),
            pl.BlockSpec((1, D_MODEL), lambda i: (0, 0)),
            pl.BlockSpec((1, D_MODEL), lambda i: (0, 0)),
        ],
        out_specs=pl.BlockSpec((tm, D_MODEL), lambda i: (i, 0)),
        compiler_params=_cp(("arbitrary",)),
    )(y, w, x, g, b)


def _mem_attn_kernel(x_ref, wq_ref, k_ref, v_ref, wo_ref, g_ref, b_ref, o_ref, *, alpha):
    x = x_ref[...]
    q = _dot(x.astype(BF16), wq_ref[...]) * (HEAD_DIM ** -0.5)
    outs = []
    for h in range(MEM_HEADS):
        qh = q[:, h * HEAD_DIM:(h + 1) * HEAD_DIM].astype(BF16)
        s = _dot_nt(qh, k_ref[h])
        m = jnp.max(s, axis=1, keepdims=True)
        e = jnp.exp(s - m)
        p = e * (1.0 / jnp.sum(e, axis=1, keepdims=True))
        outs.append(_dot(p.astype(BF16), v_ref[h]).astype(BF16))
    o = jnp.concatenate(outs, axis=1)
    h_out = _dot(o, wo_ref[...])
    o_ref[...] = _layer_norm(alpha * x + h_out, g_ref[...], b_ref[...])


def mem_attn_ln(x, wq, kv, wo, g, b, *, alpha, seq, mem_len, tm):
    t = x.shape[0]
    per_b = seq // tm
    kv4 = kv.reshape(2 * MEM_HEADS, t // seq, mem_len, HEAD_DIM)
    width = MEM_HEADS * HEAD_DIM
    return pl.pallas_call(
        functools.partial(_mem_attn_kernel, alpha=alpha),
        out_shape=jax.ShapeDtypeStruct((t, D_MODEL), F32),
        grid=(t // tm,),
        in_specs=[
            pl.BlockSpec((tm, D_MODEL), lambda i: (i, 0)),
            pl.BlockSpec((D_MODEL, width), lambda i: (0, 0)),
            pl.BlockSpec((MEM_HEADS, None, mem_len, HEAD_DIM), lambda i: (0, i // per_b, 0, 0)),
            pl.BlockSpec((MEM_HEADS, None, mem_len, HEAD_DIM), lambda i: (1, i // per_b, 0, 0)),
            pl.BlockSpec((width, D_MODEL), lambda i: (0, 0)),
            pl.BlockSpec((1, D_MODEL), lambda i: (0, 0)),
            pl.BlockSpec((1, D_MODEL), lambda i: (0, 0)),
        ],
        out_specs=pl.BlockSpec((tm, D_MODEL), lambda i: (i, 0)),
        compiler_params=_cp(("arbitrary",)),
    )(x, wq, kv4, kv4, wo, g, b)


def _router_kernel(x_ref, w_ref, b_ref, tri_ref, route_ref, xp_ref, count_ref):
    @pl.when(pl.program_id(0) == 0)
    def _():
        count_ref[...] = jnp.zeros_like(count_ref)

    x = x_ref[...]
    xp_ref[...] = _pack_bf16_pairs(x)
    xh = x.astype(BF16)
    xl = (x - xh.astype(F32)).astype(BF16)
    logits = _dot(xh, w_ref[0]) + _dot(xh, w_ref[1]) + _dot(xl, w_ref[0]) + b_ref[...]
    tm = logits.shape[0]
    lane = lax.broadcasted_iota(jnp.int32, (tm, N_EXPERTS), 1)
    work = logits
    hots, vals, idxs = [], [], []
    for _ in range(TOP_K):
        m = jnp.max(work, axis=1, keepdims=True)
        idx = jnp.min(jnp.where(work == m, lane, N_EXPERTS), axis=1, keepdims=True)
        hot = lane == idx
        hots.append(hot)
        vals.append(m)
        idxs.append(idx)
        work = jnp.where(hot, -jnp.inf, work)
    es = [jnp.exp(v - vals[0]) for v in vals]
    inv = 1.0 / (es[0] + es[1] + es[2] + es[3])
    mask = jnp.zeros((tm, N_EXPERTS), F32)
    for hot in hots:
        mask = mask + jnp.where(hot, 1.0, 0.0)
    before = count_ref[...] + _dot(tri_ref[...], mask.astype(BF16))
    count_ref[...] = count_ref[...] + jnp.sum(mask, axis=0, keepdims=True)
    col = lax.broadcasted_iota(jnp.int32, (tm, 128), 1)
    route = jnp.zeros((tm, 128), F32)
    for k in range(TOP_K):
        rank = jnp.sum(jnp.where(hots[k], before, 0.0), axis=1, keepdims=True)
        route = (route + jnp.where(col == k, idxs[k].astype(F32), 0.0)
                 + jnp.where(col == TOP_K + k, es[k] * inv, 0.0) + jnp.where(col == 2 * TOP_K + k, rank, 0.0))
    route_ref[...] = route


def router(x, w_hl, b, tri, *, tm):
    t = x.shape[0]
    return pl.pallas_call(
        _router_kernel,
        out_shape=(jax.ShapeDtypeStruct((t, 128), F32),
                   jax.ShapeDtypeStruct((t, D_MODEL // 2), jnp.uint32),
                   jax.ShapeDtypeStruct((1, N_EXPERTS), F32)),
        grid=(t // tm,),
        in_specs=[
            pl.BlockSpec((tm, D_MODEL), lambda i: (i, 0)),
            pl.BlockSpec((2, D_MODEL, N_EXPERTS), lambda i: (0, 0, 0)),
            pl.BlockSpec((1, N_EXPERTS), lambda i: (0, 0)),
            pl.BlockSpec((tm, tm), lambda i: (0, 0)),
        ],
        out_specs=(pl.BlockSpec((tm, 128), lambda i: (i, 0)),
                   pl.BlockSpec((tm, D_MODEL // 2), lambda i: (i, 0)),
                   pl.BlockSpec((1, N_EXPERTS), lambda i: (0, 0))),
        compiler_params=_cp(("arbitrary",)),
    )(x, w_hl, b, tri)


def sc_gather_rows(table, idx, *, chunk):
    n = idx.shape[0]
    d = table.shape[1]
    workers = SC_CORES_V7X * SC_SUBCORES_V7X
    per_w = n // workers
    assert n % (workers * chunk) == 0 and chunk % 8 == 0 and chunk <= 128
    mesh = plsc.VectorSubcoreMesh(core_axis_name="c", subcore_axis_name="s")

    @functools.partial(
        pl.kernel, mesh=mesh,
        out_type=jax.ShapeDtypeStruct((n, d), table.dtype),
        scratch_types=[pltpu.VMEM((chunk,), jnp.int32), pltpu.VMEM((chunk, d), table.dtype),
                       pltpu.SemaphoreType.DMA],
    )
    def gather(table_hbm, idx_hbm, out_hbm, idx_v, rows_v, sem):
        wid = lax.axis_index("s") * SC_CORES_V7X + lax.axis_index("c")
        base = wid * per_w

        @pl.loop(0, per_w // chunk)
        def _(j):
            off = pl.multiple_of(base + j * chunk, 8)
            pltpu.sync_copy(idx_hbm.at[pl.ds(off, chunk)], idx_v)
            pltpu.async_copy(table_hbm.at[idx_v], rows_v, sem).wait()
            pltpu.sync_copy(rows_v, out_hbm.at[pl.ds(off, chunk)])

    return gather(table, idx)


def sc_scatter_rows(rows, idx, n_out, *, copies, chunk):
    t, d = rows.shape
    workers = SC_CORES_V7X * SC_SUBCORES_V7X
    per_w = t // workers
    assert idx.shape == (copies * t,) and t % (workers * chunk) == 0 and chunk % 8 == 0 and chunk <= 128
    mesh = plsc.VectorSubcoreMesh(core_axis_name="c", subcore_axis_name="s")

    @functools.partial(
        pl.kernel, mesh=mesh,
        out_type=jax.ShapeDtypeStruct((n_out, d), rows.dtype),
        scratch_types=[pltpu.VMEM((chunk,), jnp.int32), pltpu.VMEM((chunk, d), rows.dtype)],
    )
    def scatter(rows_hbm, idx_hbm, out_hbm, idx_v, rows_v):
        wid = lax.axis_index("s") * SC_CORES_V7X + lax.axis_index("c")
        base = wid * per_w

        @pl.loop(0, per_w // chunk)
        def _(j):
            off = pl.multiple_of(base + j * chunk, 8)
            pltpu.sync_copy(rows_hbm.at[pl.ds(off, chunk)], rows_v)
            for k in range(copies):
                pltpu.sync_copy(idx_hbm.at[pl.ds(pl.multiple_of(k * t + off, 8), chunk)], idx_v)
                pltpu.sync_copy(rows_v, out_hbm.at[idx_v])

    return scatter(rows, idx)


def _experts_kernel(be_ref, x_ref, wg_ref, bg_ref, wu_ref, bu_ref, wd_ref, bd_ref, y_ref,
                    wgb_ref, wub_ref, wdb_ref):
    i = pl.program_id(0)
    prev = be_ref[jnp.maximum(i - 1, 0)]
    n_used = be_ref[pl.num_programs(0)]

    @pl.when((i < n_used) & ((i == 0) | (be_ref[i] != prev)))
    def _():
        wgb_ref[...] = wg_ref[...].astype(BF16)
        wub_ref[...] = wu_ref[...].astype(BF16)
        wdb_ref[...] = wd_ref[...].astype(BF16)

    @pl.when(i < n_used)
    def _():
        x_lo, x_hi = _unpack_bf16_pairs(x_ref[...])
        xb = jnp.concatenate([x_lo.astype(BF16), x_hi.astype(BF16)], axis=1)
        g = jnp.minimum(_dot(xb, wgb_ref[...]) + bg_ref[...], SWIGLU_LIMIT)
        u = jnp.clip(_dot(xb, wub_ref[...]) + bu_ref[...], -SWIGLU_LIMIT, SWIGLU_LIMIT)
        hdn = (u + 1.0) * (g * jax.nn.sigmoid(SWIGLU_ALPHA * g))
        y_ref[...] = _pack_bf16_pairs(_dot(hdn.astype(BF16), wdb_ref[...]) + bd_ref[...])


def experts(x_rows, blk_e, wg, bg, wu, bu, wd, bd, *, layer, bm):
    n_rows, dp = x_rows.shape
    d, f = wg.shape[2], wg.shape[3]
    w_spec = lambda shape: pl.BlockSpec((None, None) + shape, lambda i, be: (layer, be[i], 0, 0))
    grid_spec = pltpu.PrefetchScalarGridSpec(
        num_scalar_prefetch=1,
        grid=(n_rows // bm,),
        in_specs=[
            pl.BlockSpec((bm, dp), lambda i, be: (i, 0)),
            w_spec((d, f)), w_spec((1, f)), w_spec((d, f)), w_spec((1, f)), w_spec((f, d)), w_spec((1, d)),
        ],
        out_specs=pl.BlockSpec((bm, dp), lambda i, be: (i, 0)),
        scratch_shapes=[pltpu.VMEM((d, f), BF16), pltpu.VMEM((d, f), BF16), pltpu.VMEM((f, d), BF16)],
    )
    return pl.pallas_call(
        _experts_kernel,
        out_shape=jax.ShapeDtypeStruct((n_rows, dp), jnp.uint32),
        grid_spec=grid_spec,
        compiler_params=_cp(("arbitrary",)),
    )(blk_e, x_rows, wg, bg, wu, bu, wd, bd)


def _moe_ln_kernel(x_ref, y_ref, w_ref, g_ref, b_ref, o_ref, *, alpha):
    o_ref[...] = _moe_combine(x_ref, y_ref, w_ref, g_ref, b_ref, alpha)


def _moe_ln_mla_in_kernel(x_ref, y_ref, r_ref, lg_ref, lb_ref, w_ref, qg_ref, kg_ref, wq_ref, wkv_ref, cos_ref, sin_ref,
                          xo_ref, q_ref, k_ref, v_ref, g_ref, *, alpha):
    x_new = _moe_combine(x_ref, y_ref, r_ref, lg_ref, lb_ref, alpha)
    xo_ref[...] = x_new
    _mla_in_body(x_new, w_ref, qg_ref, kg_ref, wq_ref, wkv_ref, cos_ref, sin_ref, q_ref, k_ref, v_ref, g_ref)


def _moe_combine(x_ref, y_ref, w_ref, g_ref, b_ref, alpha):
    w = w_ref[...]
    y_lo = y_hi = None
    for k in range(TOP_K):
        lo, hi = _unpack_bf16_pairs(y_ref[k])
        wk = w[:, TOP_K + k:TOP_K + k + 1]
        y_lo = wk * lo if y_lo is None else y_lo + wk * lo
        y_hi = wk * hi if y_hi is None else y_hi + wk * hi
    y = jnp.concatenate([y_lo, y_hi], axis=1)
    return _layer_norm(alpha * x_ref[...] + y, g_ref[...], b_ref[...])


def moe_ln_mla_in(x, y4, route, g, b, w_mla, qg, kg, wq3, wkv, cos128, sin128, *, alpha, seq, tm):
    t = x.shape[0]
    npos = seq // tm
    rows = lambda w: pl.BlockSpec((tm, w), lambda i: (i, 0))
    full = lambda shape: pl.BlockSpec(shape, lambda i: (0,) * len(shape))
    heads = lambda w: pl.BlockSpec((MLA_HEADS, tm, w), lambda i: (0, i, 0))
    pos = pl.BlockSpec((tm, 128), lambda i: (i % npos, 0))
    return pl.pallas_call(
        functools.partial(_moe_ln_mla_in_kernel, alpha=alpha),
        out_shape=(
            jax.ShapeDtypeStruct((t, D_MODEL), F32),
            jax.ShapeDtypeStruct((MLA_HEADS, t, 256), BF16),
            jax.ShapeDtypeStruct((MLA_HEADS, t, 256), BF16),
            jax.ShapeDtypeStruct((MLA_HEADS, t, 128), BF16),
            jax.ShapeDtypeStruct((t, 256), F32),
        ),
        grid=(t // tm,),
        in_specs=[
            rows(D_MODEL), pl.BlockSpec((TOP_K, tm, D_MODEL // 2), lambda i: (0, i, 0)), rows(128),
            full((1, D_MODEL)), full((1, D_MODEL)),
            full((D_MODEL, MLA_IN_WIDTH)), full((1, MLA_Q_RANK)), full((1, MLA_KV_RANK)),
            full((MLA_Q_RANK, 3 * MLA_HEADS * 128)), full((MLA_KV_RANK, 2 * MLA_HEADS * 128)), pos, pos,
        ],
        out_specs=(rows(D_MODEL), heads(256), heads(256), heads(128), rows(256)),
        compiler_params=_cp(("arbitrary",)),
    )(x, y4, route, g, b, w_mla, qg, kg, wq3, wkv, cos128, sin128)


def moe_ln(x, y4, w4p, g, b, *, alpha, tm):
    t = x.shape[0]
    return pl.pallas_call(
        functools.partial(_moe_ln_kernel, alpha=alpha),
        out_shape=jax.ShapeDtypeStruct((t, D_MODEL), F32),
        grid=(t // tm,),
        in_specs=[
            pl.BlockSpec((tm, D_MODEL), lambda i: (i, 0)),
            pl.BlockSpec((TOP_K, tm, D_MODEL // 2), lambda i: (0, i, 0)),
            pl.BlockSpec((tm, 128), lambda i: (i, 0)),
            pl.BlockSpec((1, D_MODEL), lambda i: (0, 0)),
            pl.BlockSpec((1, D_MODEL), lambda i: (0, 0)),
        ],
        out_specs=pl.BlockSpec((tm, D_MODEL), lambda i: (i, 0)),
        compiler_params=_cp(("arbitrary",)),
    )(x, y4, w4p, g, b)


def _rope_tables(seq):
    inv = np.asarray(ROPE_THETA ** (-np.arange(0, MLA_ROPE, 2) / MLA_ROPE), np.float32)
    ang = jnp.arange(seq, dtype=F32)[:, None] * jnp.asarray(inv)[None, :]
    cos, sin = jnp.cos(ang), jnp.sin(ang)
    zeros = jnp.zeros((seq, 128 - MLA_ROPE), F32)
    return (jnp.concatenate([cos, cos, zeros], axis=1), jnp.concatenate([-sin, sin, zeros], axis=1))


def _nsa_constants(seq, tq, tk):
    nc = seq // NSA_CMP_STRIDE
    qa = np.zeros((NSA_GROUPS, NSA_HG * tq, 128), np.float32)
    for g in range(NSA_GROUPS):
        for hg in range(NSA_HG):
            slope = 2.0 ** (-8.0 * (g * NSA_HG + hg + 1) / NSA_HEADS)
            qa[g, hg * tq:(hg + 1) * tq, 0] = slope * NSA_SEL_LEN
            qa[g, hg * tq:(hg + 1) * tq, 1] = slope
            qa[g, hg * tq:(hg + 1) * tq, 2] = slope * NSA_CMP_STRIDE
            qa[g, hg * tq:(hg + 1) * tq, 3] = slope * (NSA_CMP_LEN - 1) / 2.0
    kpos = np.zeros((seq, 128), np.float32)
    kpos[:, 0] = np.arange(seq) // NSA_SEL_LEN
    kpos[:, 1] = np.arange(seq) % NSA_SEL_LEN
    cpos = np.zeros((nc, 128), np.float32)
    cpos[:, 2] = np.arange(nc)
    cpos[:, 3] = 1.0
    for arr in (qa, kpos, cpos):
        assert np.array_equal(arr.astype(BF16).astype(np.float32), arr)
    n_cmp = (seq - NSA_CMP_LEN) // NSA_CMP_STRIDE + 1
    n_sel = seq // NSA_SEL_LEN
    cs = np.arange(nc) * NSA_CMP_STRIDE
    ss = np.arange(n_sel) * NSA_SEL_LEN
    ov = np.clip(np.minimum(cs[:, None] + NSA_CMP_LEN, ss[None, :] + NSA_SEL_LEN)
                 - np.maximum(cs[:, None], ss[None, :]), 0, None) / NSA_CMP_LEN
    ov[n_cmp:] = 0.0
    e = (np.arange(seq)[None, :] // NSA_SEL_LEN == np.arange(n_sel)[:, None]).astype(np.float32)
    e = e.reshape(n_sel, seq // tk, tk).transpose(1, 0, 2)
    return tuple(jnp.asarray(a, BF16) for a in (ov.T, e, qa, kpos, cpos))


def _pad_cols(w, width):
    return jnp.pad(w, ((0, 0), (0, width - w.shape[1])))


def _swap_halves(w):
    half = w.shape[1] // 2
    return jnp.concatenate([w[:, half:], w[:, :half]], axis=1)


def _layer_weights(w_in, w_q_up, w_kv_up):
    kr = w_in[:, OFF_KR:OFF_NSA_Q]
    gate = w_in[:, OFF_NSA_GATE:OFF_SB]
    per_g = NSA_HG * 3
    w_mla = jnp.concatenate([
        w_in[:, OFF_CQ:OFF_KR],
        _pad_cols(kr, 128), _pad_cols(_swap_halves(kr), 128),
        _pad_cols(gate[:, :per_g], 128), _pad_cols(gate[:, per_g:], 128)], axis=1).astype(BF16)
    wq = w_q_up.reshape(MLA_Q_RANK, MLA_HEADS, MLA_NOPE + MLA_ROPE)
    rope = wq[:, :, MLA_NOPE:]
    rope_sw = jnp.concatenate([rope[:, :, MLA_ROPE // 2:], rope[:, :, :MLA_ROPE // 2]], axis=2)
    pad = ((0, 0), (0, 0), (0, 128 - MLA_ROPE))
    wq3 = jnp.concatenate([
        wq[:, :, :MLA_NOPE].reshape(MLA_Q_RANK, -1),
        jnp.pad(rope, pad).reshape(MLA_Q_RANK, -1),
        jnp.pad(rope_sw, pad).reshape(MLA_Q_RANK, -1)], axis=1).astype(BF16)
    wkv = w_kv_up.reshape(MLA_KV_RANK, MLA_HEADS, 2, 128)
    wkv = jnp.concatenate([wkv[:, :, 0].reshape(MLA_KV_RANK, -1),
                           wkv[:, :, 1].reshape(MLA_KV_RANK, -1)], axis=1).astype(BF16)
    w_heads = jnp.concatenate([w_in[:, OFF_NSA_Q:OFF_NSA_KV], w_in[:, OFF_SB:OFF_MERGE],
                               w_in[:, OFF_NSA_KV:OFF_NSA_GATE]], axis=1).astype(BF16)
    wm = w_in[:, OFF_MERGE:].reshape(D_MODEL, N_BRANCH, D_MODEL).transpose(1, 0, 2).astype(BF16)
    return w_mla, wq3, wkv, w_heads, wm


def _forward(x, mem, w_in, mla_q_norm, mla_w_q_up, mla_kv_norm, mla_w_kv_up,
             nsa_pe_k, nsa_pe_v, nsa_w1_k, nsa_w1_v, nsa_w2_k, nsa_w2_v,
             w_branch, b_merge, w_out, ln_mix_g, ln_mix_b,
             mem_w_q, mem_w_k, mem_w_v, mem_w_o, ln_mem_g, ln_mem_b,
             moe_w_router, moe_b_router, moe_w_gate, moe_b_gate, moe_w_up, moe_b_up,
             moe_w_down, moe_b_down, ln_moe_g, ln_moe_b):
    batch, seq, _ = x.shape
    mem_len = mem.shape[1]
    depth = w_in.shape[0]
    t = batch * seq
    alpha = float((2 * depth) ** 0.25)
    bm = MOE_BLOCK_ROWS
    n_rows = t * TOP_K + N_EXPERTS * bm
    n_blocks = n_rows // bm

    tm_in = min(512, seq)
    tq_mla = min(512, seq)
    tq_sb = 256
    tq_nsa, tk_nsa = 256, 512
    tm_ln = 256
    sc_chunk = 64

    cos128, sin128 = _rope_tables(seq)
    nsa_consts = _nsa_constants(seq, tq_nsa, tk_nsa)
    u_sb = jnp.asarray(np.arange(tq_sb)[:, None] > np.arange(tq_sb)[None, :], BF16)
    tri_router = jnp.asarray(np.arange(tm_in)[:, None] > np.arange(tm_in)[None, :], BF16)
    n_qheads = NSA_HEADS
    n_kvheads = 3 * 2 * NSA_GROUPS
    head_scale = np.ones((1, (n_qheads + n_kvheads + 3 * SB_HEADS) * HEAD_DIM), np.float32)
    head_scale[:, :n_qheads * HEAD_DIM] = HEAD_DIM ** -0.5
    sb0 = n_qheads
    kv0 = sb0 + 3 * SB_HEADS
    tn_heads = (n_qheads + n_kvheads + 3 * SB_HEADS) // 4 * HEAD_DIM
    head_scale[:, sb0 * HEAD_DIM:(sb0 + SB_HEADS) * HEAD_DIM] = HEAD_DIM ** -0.5
    head_scale = jnp.asarray(head_scale)
    ones_kv = jnp.ones((1, 2 * MEM_HEADS * HEAD_DIM), F32)

    b_gate4 = moe_b_gate.reshape(depth, N_EXPERTS, 1, D_EXPERT)
    b_up4 = moe_b_up.reshape(depth, N_EXPERTS, 1, D_EXPERT)
    b_down4 = moe_b_down.reshape(depth, N_EXPERTS, 1, D_MODEL)

    xf = x.reshape(t, D_MODEL)
    memf = mem.reshape(batch * mem_len, D_MODEL)
    row = lambda v: v.reshape(1, -1)

    layer_w = [_layer_weights(w_in[l], mla_w_q_up[l], mla_w_kv_up[l]) for l in range(depth)]
    mla_parts = None
    for l in range(depth):
        w_mla, wq3, wkv, w_heads, wm = layer_w[l]

        if mla_parts is None:
            mla_parts = mla_in(xf, w_mla, row(mla_q_norm[l]), row(mla_kv_norm[l]), wq3, wkv,
                               cos128, sin128, seq=seq, tm=tm_ln)
        q_a, k_a, v_a, gates = mla_parts
        hm = proj_heads(xf, w_heads, head_scale, tm=tm_in, tn=tn_heads)
        o_a = mla_attn(q_a, k_a, v_a, batch=batch, seq=seq, tq=tq_mla, heads=2)
        w1 = jnp.stack([nsa_w1_k[l], nsa_w1_v[l]]).astype(BF16)
        pe = jnp.stack([nsa_pe_k[l], nsa_pe_v[l]]).reshape(2, 1, -1)
        pe = jnp.broadcast_to(pe, (2, 8, pe.shape[-1])).astype(BF16)
        w2 = jnp.stack([nsa_w2_k[l], nsa_w2_v[l]]).astype(BF16)
        cmp = nsa_compress(hm[kv0:kv0 + 4], w1, pe, w2, batch=batch, seq=seq)
        o_b = nsa_attn(hm, cmp, gates, nsa_consts, q_head0=0, kv_head0=kv0,
                       batch=batch, seq=seq, tq=tq_nsa, tk=tk_nsa)
        o_c = sb_attn(hm, u_sb, head0=sb0, batch=batch, seq=seq, tq=tq_sb, heads=4)
        y = merge_branches(xf, o_a, o_b, o_c, wm, w_branch[l].astype(BF16),
                           b_merge[l].reshape(N_BRANCH, 1, D_MODEL), tm=tm_in, tn=512)
        xf = out_ln(y, w_out[l].astype(BF16), xf, row(ln_mix_g[l]), row(ln_mix_b[l]), alpha=alpha, tm=tm_in)

        w_kv_mem = jnp.concatenate([mem_w_k[l], mem_w_v[l]], axis=1).astype(BF16)
        kv_mem = proj_heads(memf, w_kv_mem, ones_kv, tm=min(512, batch * mem_len), tn=512)
        xf = mem_attn_ln(xf, mem_w_q[l].astype(BF16), kv_mem, mem_w_o[l].astype(BF16),
                         row(ln_mem_g[l]), row(ln_mem_b[l]), alpha=alpha, seq=seq, mem_len=mem_len, tm=tm_in)

        wr = moe_w_router[l]
        wr_hi = wr.astype(BF16)
        wr_lo = (wr - wr_hi.astype(F32)).astype(BF16)
        route, x_packed, counts = router(xf, jnp.stack([wr_hi, wr_lo]), row(moe_b_router[l]),
                                         tri_router, tm=tm_in)
        padded = (counts[0].astype(jnp.int32) + bm - 1) // bm * bm
        pad_end = jnp.cumsum(padded)
        pad_start = pad_end - padded
        idx4 = route[:, 0:TOP_K].astype(jnp.int32)
        rank4 = route[:, 2 * TOP_K:3 * TOP_K].astype(jnp.int32)
        start4 = jnp.sum(jnp.where(idx4[:, :, None] == jnp.arange(N_EXPERTS, dtype=jnp.int32), pad_start, 0), axis=2)
        pos4 = start4 + rank4
        blk_row0 = jnp.arange(n_blocks, dtype=jnp.int32)[:, None] * bm
        blk_e = jnp.minimum(jnp.sum((pad_end[None, :] <= blk_row0).astype(jnp.int32), axis=1), N_EXPERTS - 1)
        blk_e = jnp.concatenate([blk_e, pad_end[-1:] // bm]).astype(jnp.int32)
        pos_kmajor = pos4.T.reshape(-1)
        x_rows = sc_scatter_rows(x_packed, pos_kmajor, n_rows, copies=TOP_K, chunk=sc_chunk)
        y_rows = experts(x_rows, blk_e, moe_w_gate, b_gate4, moe_w_up, b_up4, moe_w_down, b_down4,
                         layer=l, bm=bm)
        y4 = sc_gather_rows(y_rows, pos_kmajor, chunk=sc_chunk).reshape(TOP_K, t, D_MODEL // 2)
        if l + 1 < depth:
            nw_mla, nwq3, nwkv, _, _ = layer_w[l + 1]
            xf, *mla_parts = moe_ln_mla_in(
                xf, y4, route, row(ln_moe_g[l]), row(ln_moe_b[l]), nw_mla, row(mla_q_norm[l + 1]),
                row(mla_kv_norm[l + 1]), nwq3, nwkv, cos128, sin128, alpha=alpha, seq=seq, tm=tm_ln)
        else:
            xf = moe_ln(xf, y4, route, row(ln_moe_g[l]), row(ln_moe_b[l]), alpha=alpha, tm=tm_ln)

    return xf.reshape(batch, seq, D_MODEL)


def kernel(x, mem, w_in, mla_q_norm, mla_w_q_up, mla_kv_norm, mla_w_kv_up, nsa_pe_k, nsa_pe_v, nsa_w1_k, nsa_w1_v, nsa_w2_k, nsa_w2_v, w_branch, b_merge, w_out, ln_mix_g, ln_mix_b, mem_w_q, mem_w_k, mem_w_v, mem_w_o, ln_mem_g, ln_mem_b, moe_w_router, moe_b_router, moe_w_gate, moe_b_gate, moe_w_up, moe_b_up, moe_w_down, moe_b_down, ln_moe_g, ln_moe_b):
    return _forward(x, mem, w_in, mla_q_norm, mla_w_q_up, mla_kv_norm, mla_w_kv_up,
                    nsa_pe_k, nsa_pe_v, nsa_w1_k, nsa_w1_v, nsa_w2_k, nsa_w2_v,
                    w_branch, b_merge, w_out, ln_mix_g, ln_mix_b,
                    mem_w_q, mem_w_k, mem_w_v, mem_w_o, ln_mem_g, ln_mem_b,
                    moe_w_router, moe_b_router, moe_w_gate, moe_b_gate, moe_w_up, moe_b_up,
                    moe_w_down, moe_b_down, ln_moe_g, ln_moe_b)
```

```python
import functools

import numpy as np
import jax
import jax.numpy as jnp
from jax import lax
from jax.experimental import pallas as pl
from jax.experimental.pallas import tpu as pltpu
from jax.experimental.pallas import tpu_sc as plsc

F32 = jnp.float32
BF16 = jnp.bfloat16

D_MODEL = 2048
HEAD_DIM = 128
MLA_HEADS = 8
MLA_Q_RANK = 512
MLA_KV_RANK = 256
MLA_NOPE = 128
MLA_ROPE = 64
ROPE_THETA = 10000.0
NSA_HEADS = 8
NSA_GROUPS = 2
NSA_HG = NSA_HEADS // NSA_GROUPS
NSA_CMP_LEN = 32
NSA_CMP_STRIDE = 16
NSA_SEL_LEN = 64
NSA_TOPK = 16
NSA_WINDOW = 512
SB_HEADS = 8
MEM_HEADS = 4
N_EXPERTS = 32
TOP_K = 4
D_EXPERT = 512
SWIGLU_LIMIT = 7.0
SWIGLU_ALPHA = 1.702
N_BRANCH = 3
BRANCH_WIDTH = 1024
LN_EPS = 1e-5
RMS_EPS = 1e-6
NEG = -1e30
BIG = 1e30
SB_UNDERFLOW_LOG = -100.0

OFF_CQ = 0
OFF_CKV = 512
OFF_KR = 768
OFF_NSA_Q = 832
OFF_NSA_KV = 1856
OFF_NSA_GATE = 3392
OFF_SB = 3416
OFF_MERGE = 6488
MLA_IN_WIDTH = MLA_Q_RANK + MLA_KV_RANK + 2 * 128 + NSA_GROUPS * 128

VMEM_LIMIT_V7X = 56 * 1024 * 1024
MOE_BLOCK_ROWS = 512
SC_CORES_V7X = 2
SC_SUBCORES_V7X = 16


def _cp(sem, vmem=VMEM_LIMIT_V7X):
    return pltpu.CompilerParams(dimension_semantics=sem, vmem_limit_bytes=vmem)


def _dot(a, b):
    return jnp.dot(a, b, preferred_element_type=F32)


def _dot_nt(a, b):
    return lax.dot_general(a, b, (((1,), (1,)), ((), ())), preferred_element_type=F32)


def _layer_norm(z, g, b):
    mu = jnp.mean(z, axis=-1, keepdims=True)
    zc = z - mu
    var = jnp.mean(zc * zc, axis=-1, keepdims=True)
    return zc * lax.rsqrt(var + LN_EPS) * g + b


def _rms_norm(z, g):
    return z * lax.rsqrt(jnp.mean(z * z, axis=-1, keepdims=True) + RMS_EPS) * g


def _pack_bf16_pairs(z):
    n = z.shape[1] // 2
    bits = pltpu.bitcast(z.astype(BF16).astype(F32), jnp.uint32)
    return lax.shift_right_logical(bits[:, :n], jnp.uint32(16)) | (bits[:, n:] & jnp.uint32(0xFFFF0000))


def _unpack_bf16_pairs(w):
    lo = pltpu.bitcast(lax.shift_left(w, jnp.uint32(16)), F32)
    hi = pltpu.bitcast(w & jnp.uint32(0xFFFF0000), F32)
    return lo, hi


def _proj_heads_kernel(a_ref, w_ref, s_ref, o_ref, abf_ref, *, n_heads_per_tile):
    @pl.when(pl.program_id(1) == 0)
    def _():
        abf_ref[...] = a_ref[...].astype(BF16)

    acc = _dot(abf_ref[...], w_ref[...]) * s_ref[...]
    for c in range(n_heads_per_tile):
        o_ref[c] = acc[:, c * HEAD_DIM:(c + 1) * HEAD_DIM].astype(o_ref.dtype)


def proj_heads(a, w, scale, *, tm, tn):
    m, k = a.shape
    n = w.shape[1]
    hpt = tn // HEAD_DIM
    return pl.pallas_call(
        functools.partial(_proj_heads_kernel, n_heads_per_tile=hpt),
        out_shape=jax.ShapeDtypeStruct((n // HEAD_DIM, m, HEAD_DIM), BF16),
        grid=(m // tm, n // tn),
        in_specs=[
            pl.BlockSpec((tm, k), lambda i, j: (i, 0)),
            pl.BlockSpec((k, tn), lambda i, j: (0, j)),
            pl.BlockSpec((1, tn), lambda i, j: (0, j)),
        ],
        out_specs=pl.BlockSpec((hpt, tm, HEAD_DIM), lambda i, j: (j, i, 0)),
        scratch_shapes=[pltpu.VMEM((tm, k), BF16)],
        compiler_params=_cp(("arbitrary", "arbitrary")),
    )(a, w, scale)


def _mla_in_kernel(x_ref, w_ref, qg_ref, kg_ref, wq_ref, wkv_ref, cos_ref, sin_ref,
                   q_ref, k_ref, v_ref, g_ref):
    _mla_in_body(x_ref[...], w_ref, qg_ref, kg_ref, wq_ref, wkv_ref, cos_ref, sin_ref, q_ref, k_ref, v_ref, g_ref)


def _mla_in_body(x, w_ref, qg_ref, kg_ref, wq_ref, wkv_ref, cos_ref, sin_ref, q_ref, k_ref, v_ref, g_ref):
    xb = x.astype(BF16)
    h = _dot(xb, w_ref[...])
    c0, c1 = MLA_Q_RANK, MLA_Q_RANK + MLA_KV_RANK
    cq = h[:, 0:c0]
    ckv = h[:, c0:c1]
    kr1 = h[:, c1:c1 + 128]
    kr2 = h[:, c1 + 128:c1 + 256]
    g_ref[...] = jax.nn.sigmoid(h[:, c1 + 256:MLA_IN_WIDTH])
    cos = cos_ref[...]
    sin = sin_ref[...]
    scale = (MLA_NOPE + MLA_ROPE) ** -0.5
    hw = MLA_HEADS * 128
    nq = _rms_norm(cq, qg_ref[...]).astype(BF16)
    q3 = _dot(nq, wq_ref[...])
    for hh in range(MLA_HEADS):
        lo, hi = hh * 128, (hh + 1) * 128
        q_ref[hh, :, 0:128] = (q3[:, lo:hi] * scale).astype(BF16)
        rot = q3[:, hw + lo:hw + hi] * cos + q3[:, 2 * hw + lo:2 * hw + hi] * sin
        q_ref[hh, :, 128:256] = (rot * scale).astype(BF16)
    nkv = _rms_norm(ckv, kg_ref[...]).astype(BF16)
    kv = _dot(nkv, wkv_ref[...])
    krot = (kr1 * cos + kr2 * sin).astype(BF16)
    for hh in range(MLA_HEADS):
        lo, hi = hh * 128, (hh + 1) * 128
        k_ref[hh, :, 0:128] = kv[:, lo:hi].astype(BF16)
        k_ref[hh, :, 128:256] = krot
        v_ref[hh] = kv[:, hw + lo:hw + hi].astype(BF16)


def mla_in(x, w_mla, qg, kg, wq3, wkv, cos128, sin128, *, seq, tm):
    t = x.shape[0]
    npos = seq // tm
    full = lambda shape: pl.BlockSpec(shape, lambda i: (0,) * len(shape))
    return pl.pallas_call(
        _mla_in_kernel,
        out_shape=(
            jax.ShapeDtypeStruct((MLA_HEADS, t, 256), BF16),
            jax.ShapeDtypeStruct((MLA_HEADS, t, 256), BF16),
            jax.ShapeDtypeStruct((MLA_HEADS, t, 128), BF16),
            jax.ShapeDtypeStruct((t, 256), F32),
        ),
        grid=(t // tm,),
        in_specs=[
            pl.BlockSpec((tm, D_MODEL), lambda i: (i, 0)),
            full((D_MODEL, MLA_IN_WIDTH)),
            full((1, MLA_Q_RANK)),
            full((1, MLA_KV_RANK)),
            full((MLA_Q_RANK, 3 * MLA_HEADS * 128)),
            full((MLA_KV_RANK, 2 * MLA_HEADS * 128)),
            pl.BlockSpec((tm, 128), lambda i: (i % npos, 0)),
            pl.BlockSpec((tm, 128), lambda i: (i % npos, 0)),
        ],
        out_specs=(
            pl.BlockSpec((MLA_HEADS, tm, 256), lambda i: (0, i, 0)),
            pl.BlockSpec((MLA_HEADS, tm, 256), lambda i: (0, i, 0)),
            pl.BlockSpec((MLA_HEADS, tm, 128), lambda i: (0, i, 0)),
            pl.BlockSpec((tm, 256), lambda i: (i, 0)),
        ),
        compiler_params=_cp(("arbitrary",)),
    )(x, w_mla, qg, kg, wq3, wkv, cos128, sin128)


def _mla_attn_kernel(q_ref, k_ref, v_ref, o_ref, s_ref, *, tq, heads):
    qi = pl.program_id(2)

    def scores(h, kt):
        k0 = pl.multiple_of(kt * tq, tq)
        return _dot_nt(q_ref[h], k_ref[h, pl.ds(k0, tq), :])

    def consume(h, kt, s, carry, diag):
        m, l, acc = carry
        k0 = pl.multiple_of(kt * tq, tq)
        v = v_ref[h, pl.ds(k0, tq), :]
        if diag:
            row = lax.broadcasted_iota(jnp.int32, (tq, tq), 0)
            col = lax.broadcasted_iota(jnp.int32, (tq, tq), 1)
            s = jnp.where(col <= row, s, NEG)
        m_new = jnp.maximum(m, jnp.max(s, axis=1, keepdims=True))
        alpha = jnp.exp(m - m_new)
        p = jnp.exp((s - m_new).astype(BF16))
        l = alpha * l + jnp.sum(p.astype(F32), axis=1, keepdims=True)
        acc = alpha * acc + _dot(p, v)
        return m_new, l, acc

    def fill(slot, kt):
        for h in range(heads):
            s_ref[slot, h] = scores(h, kt)

    def drain(slot, kt, carries, diag):
        return tuple(consume(h, kt, s_ref[slot, h], carries[h], diag) for h in range(heads))

    def body(j, carries):
        kt = 2 * j
        fill(1, kt + 1)
        carries = drain(0, kt, carries, False)
        fill(0, kt + 2)
        return drain(1, kt + 1, carries, False)

    init = (jnp.full((tq, 1), NEG, F32), jnp.zeros((tq, 1), F32), jnp.zeros((tq, 128), F32))
    fill(0, 0)
    carries = lax.fori_loop(0, qi // 2, body, (init,) * heads)

    def even_tail(carries):
        return drain(0, qi, carries, True)

    def odd_tail(carries):
        fill(1, qi)
        return drain(1, qi, drain(0, qi - 1, carries, False), True)

    carries = lax.cond(lax.rem(qi, 2) == 1, odd_tail, even_tail, carries)
    for h, (_, l, acc) in enumerate(carries):
        o_ref[:, h * 128:(h + 1) * 128] = (acc / l).astype(o_ref.dtype)


def mla_attn(q, k, v, *, batch, seq, tq, heads):
    nq = seq // tq
    q4 = q.reshape(MLA_HEADS, batch, seq, 256)
    k4 = k.reshape(MLA_HEADS, batch, seq, 256)
    v4 = v.reshape(MLA_HEADS, batch, seq, 128)
    return pl.pallas_call(
        functools.partial(_mla_attn_kernel, tq=tq, heads=heads),
        out_shape=jax.ShapeDtypeStruct((batch * seq, MLA_HEADS * 128), BF16),
        grid=(MLA_HEADS // heads, batch, nq),
        in_specs=[
            pl.BlockSpec((heads, None, tq, 256), lambda h, b, i: (h, b, i, 0)),
            pl.BlockSpec((heads, None, seq, 256), lambda h, b, i: (h, b, 0, 0)),
            pl.BlockSpec((heads, None, seq, 128), lambda h, b, i: (h, b, 0, 0)),
        ],
        out_specs=pl.BlockSpec((tq, heads * 128), lambda h, b, i: (b * nq + i, h)),
        scratch_shapes=[pltpu.VMEM((2, heads, tq, tq), F32)],
        compiler_params=_cp(("arbitrary", "arbitrary", "arbitrary")),
    )(q4, k4, v4)


def _sb_attn_kernel(q_ref, k_ref, v_ref, u_ref, o_ref, *, tq, heads):
    qi = pl.program_id(2)
    u = u_ref[...]

    def head_step(h, kt, carry, diag):
        run, acc = carry
        k0 = pl.multiple_of(kt * tq, tq)
        k = k_ref[h, pl.ds(k0, tq), :]
        v = v_ref[h, pl.ds(k0, tq), :]
        z = _dot_nt(q_ref[h], k)
        l1m = -(jnp.maximum(z, 0.0) + jnp.log(1.0 + jnp.exp(-jnp.abs(z))))
        if diag:
            row = lax.broadcasted_iota(jnp.int32, (tq, tq), 0)
            col = lax.broadcasted_iota(jnp.int32, (tq, tq), 1)
            strict = col < row
            l1m_m = jnp.where(strict, l1m, 0.0)
        else:
            l1m_m = l1m
        hi = l1m_m.astype(BF16)
        lo = (l1m_m - hi.astype(F32)).astype(BF16)
        between = _dot(hi, u) + _dot(lo, u)
        a = jnp.exp(z + l1m + between + run)
        if diag:
            a = jnp.where(strict, a, 0.0)
        acc = acc + _dot(a.astype(BF16), v)
        run = run + between[:, 0:1] + l1m_m[:, 0:1]
        return run, acc

    def step(kt, carries, diag):
        return tuple(head_step(h, kt, carries[h], diag) for h in range(heads))

    init = (jnp.zeros((tq, 1), F32), jnp.zeros((tq, 128), F32))
    carries = step(qi, (init,) * heads, True)

    def more(c):
        j, carries = c
        top = carries[0][0]
        for run, _ in carries[1:]:
            top = jnp.maximum(top, run)
        return (j < qi) & (jnp.max(top) > SB_UNDERFLOW_LOG)

    def body(c):
        j, carries = c
        return j + 1, step(qi - 1 - j, carries, False)

    _, carries = lax.while_loop(more, body, (jnp.int32(0), carries))
    for h, (_, acc) in enumerate(carries):
        o_ref[:, h * HEAD_DIM:(h + 1) * HEAD_DIM] = acc.astype(o_ref.dtype)


def sb_attn(hm, u, *, head0, batch, seq, tq, heads):
    nq = seq // tq
    assert head0 % heads == 0 and SB_HEADS % heads == 0
    hm4 = hm.reshape(hm.shape[0], batch, seq, HEAD_DIM)
    blk0 = head0 // heads
    per_part = SB_HEADS // heads
    return pl.pallas_call(
        functools.partial(_sb_attn_kernel, tq=tq, heads=heads),
        out_shape=jax.ShapeDtypeStruct((batch * seq, SB_HEADS * HEAD_DIM), BF16),
        grid=(per_part, batch, nq),
        in_specs=[
            pl.BlockSpec((heads, None, tq, HEAD_DIM), lambda h, b, i: (blk0 + h, b, i, 0)),
            pl.BlockSpec((heads, None, seq, HEAD_DIM), lambda h, b, i: (blk0 + per_part + h, b, 0, 0)),
            pl.BlockSpec((heads, None, seq, HEAD_DIM), lambda h, b, i: (blk0 + 2 * per_part + h, b, 0, 0)),
            pl.BlockSpec((tq, tq), lambda h, b, i: (0, 0)),
        ],
        out_specs=pl.BlockSpec((tq, heads * HEAD_DIM), lambda h, b, i: (b * nq + i, h)),
        compiler_params=_cp(("arbitrary", "arbitrary", "arbitrary")),
    )(hm4, hm4, hm4, u)


def _nsa_cmp_kernel(c_ref, w1_ref, pe_ref, w2_ref, o_ref, *, nc):
    c = c_ref[...]
    half = NSA_CMP_STRIDE * HEAD_DIM
    a1 = _dot(c, w1_ref[0:half, :])
    a2 = _dot(c, w1_ref[half:2 * half, :])
    pc = _dot(pe_ref[...], w1_ref[...])[0:1, :]
    pre = a1 + pltpu.roll(a2, nc - 1, 0) + pc
    act = 0.5 * pre * (1.0 + jnp.tanh(0.7978845608028654 * (pre + 0.044715 * (pre * pre * pre))))
    o_ref[...] = _dot(act.astype(BF16), w2_ref[...]).astype(BF16)


def nsa_compress(cmp_heads, w1, pe, w2, *, batch, seq):
    nc = seq // NSA_CMP_STRIDE
    hm4 = cmp_heads.reshape(4, batch, nc, NSA_CMP_STRIDE * HEAD_DIM)
    return pl.pallas_call(
        functools.partial(_nsa_cmp_kernel, nc=nc),
        out_shape=jax.ShapeDtypeStruct((4, batch, nc, HEAD_DIM), BF16),
        grid=(4, batch),
        in_specs=[
            pl.BlockSpec((None, None, nc, NSA_CMP_STRIDE * HEAD_DIM), lambda c, b: (c, b, 0, 0)),
            pl.BlockSpec((None, NSA_CMP_LEN * HEAD_DIM, HEAD_DIM), lambda c, b: (c // 2, 0, 0)),
            pl.BlockSpec((None, 8, NSA_CMP_LEN * HEAD_DIM), lambda c, b: (c // 2, 0, 0)),
            pl.BlockSpec((None, HEAD_DIM, HEAD_DIM), lambda c, b: (c // 2, 0, 0)),
        ],
        out_specs=pl.BlockSpec((None, None, nc, HEAD_DIM), lambda c, b: (c, b, 0, 0)),
        compiler_params=_cp(("arbitrary", "arbitrary")),
    )(hm4, w1, pe, w2)


def _nsa_attn_kernel(q_ref, qa_ref, kc_ref, vc_ref, ks_ref, vs_ref, kw_ref, vw_ref, g_ref, ovt_ref, e_ref,
                     kpos_ref, cpos_ref, o_ref, ss_ref, *, tq, tk, seq, n_sel, n_top):
    qi = pl.program_id(1)
    t0 = qi * tq
    rows = NSA_HG * tq
    nc = seq // NSA_CMP_STRIDE
    groups = range(NSA_GROUPS)

    rid = lax.broadcasted_iota(jnp.int32, (rows, 1), 0)
    trow = t0 + lax.bitwise_and(rid, tq - 1)

    def masked_softmax(s, valid):
        sm = jnp.where(valid, s, NEG)
        m = jnp.max(sm, axis=1, keepdims=True)
        e = jnp.where(valid, jnp.exp(sm - m), 0.0)
        d = jnp.sum(e, axis=1, keepdims=True)
        return e * (1.0 / jnp.where(d > 0.0, d, 1.0))

    n_i = lax.broadcasted_iota(jnp.int32, (1, nc), 1)
    end = n_i * NSA_CMP_STRIDE + (NSA_CMP_LEN - 1)
    cur = lax.shift_right_logical(t0 + lax.broadcasted_iota(jnp.int32, (1, tq), 1),
                                  int(np.log2(NSA_SEL_LEN)))
    blk = lax.broadcasted_iota(jnp.int32, (n_sel, tq), 0)
    forced = (blk == 0) | (blk == cur) | (blk == cur - 1)
    sub = lax.broadcasted_iota(jnp.int32, (8, tq), 0)
    eye = (lax.broadcasted_iota(jnp.int32, (n_sel, n_sel), 0)
           == lax.broadcasted_iota(jnp.int32, (n_sel, n_sel), 1)).astype(F32).astype(BF16)
    ovt = ovt_ref[...]

    def front(g):
        q = jnp.concatenate([q_ref[NSA_HG * g:NSA_HG * (g + 1)].reshape(rows, HEAD_DIM), qa_ref[g]], axis=1)

        s_c = _dot_nt(q, jnp.concatenate([kc_ref[g], cpos_ref[...]], axis=1))
        p_c = masked_softmax(s_c, end <= trow)
        o_c = _dot(p_c.astype(BF16), vc_ref[g])

        psum = p_c[0:tq] + p_c[tq:2 * tq] + p_c[2 * tq:3 * tq] + p_c[3 * tq:4 * tq]
        p_hi = psum.astype(BF16)
        p_lo = (psum - p_hi.astype(F32)).astype(BF16)
        imp = _dot_nt(ovt, p_hi) + _dot_nt(ovt, p_lo)
        key = jnp.where(blk > cur, -BIG, jnp.where(forced, BIG, imp))
        chunks = [key[8 * r:8 * r + 8, :] for r in range(n_sel // 8)]
        ranks = [jnp.zeros((8, tq), F32) for _ in chunks]
        for i in range(n_sel):
            vi = key[i:i + 1, :]
            for r, kc in enumerate(chunks):
                gt = jnp.where(vi > kc, 1.0, 0.0)
                if r < i // 8:
                    win = gt
                else:
                    ge = jnp.where(vi >= kc, 1.0, 0.0)
                    win = ge if r > i // 8 else jnp.where(sub > i % 8, ge, gt)
                ranks[r] = ranks[r] + win
        rank = jnp.concatenate(ranks, axis=0)
        selm_t = jnp.where((rank < float(n_top)) & (blk <= cur), 1.0, 0.0).astype(BF16)
        selm = lax.dot_general(selm_t, eye, (((0,), (0,)), ((), ())), preferred_element_type=F32).astype(BF16)
        return q, selm, o_c

    fronts = [front(g) for g in groups]

    def sel_fill(slot, kt):
        k0 = pl.multiple_of(kt * tk, tk)
        kp = kpos_ref[pl.ds(k0, tk), :]
        for g in groups:
            q, selm, _ = fronts[g]
            kk = jnp.concatenate([ks_ref[g, pl.ds(k0, tk), :], kp], axis=1)
            mex = _dot(selm, e_ref[kt])
            bias = (mex - 1.0) * BIG
            ss_ref[slot, g] = _dot_nt(q, kk) + jnp.concatenate([bias] * NSA_HG, axis=0)

    def sel_drain_group(g, slot, kt, carry, diag):
        m, l, acc = carry
        k0 = pl.multiple_of(kt * tk, tk)
        vv = vs_ref[g, pl.ds(k0, tk), :]
        sm = ss_ref[slot, g]
        if diag:
            spos = k0 + lax.broadcasted_iota(jnp.int32, (1, tk), 1)
            sm = jnp.where(spos <= trow, sm, NEG)
        m_new = jnp.maximum(m, jnp.max(sm, axis=1, keepdims=True))
        alpha = jnp.exp(m - m_new)
        p = jnp.exp((sm - m_new).astype(BF16))
        l = alpha * l + jnp.sum(p.astype(F32), axis=1, keepdims=True)
        acc = alpha * acc + _dot(p, vv)
        return m_new, l, acc

    def sel_drain(slot, kt, carries, diag):
        return tuple(sel_drain_group(g, slot, kt, carries[g], diag) for g in groups)

    def sel_body(j, carries):
        kt = 2 * j
        sel_fill(1, kt + 1)
        carries = sel_drain(0, kt, carries, False)
        sel_fill(0, kt + 2)
        return sel_drain(1, kt + 1, carries, False)

    kt_last = t0 // tk
    init = (jnp.full((rows, 1), NEG, F32), jnp.zeros((rows, 1), F32), jnp.zeros((rows, HEAD_DIM), F32))
    sel_fill(0, 0)
    carries = lax.fori_loop(0, kt_last // 2, sel_body, (init,) * NSA_GROUPS)

    def even_tail(carries):
        return sel_drain(0, kt_last, carries, True)

    def odd_tail(carries):
        sel_fill(1, kt_last)
        return sel_drain(1, kt_last, sel_drain(0, kt_last - 1, carries, False), True)

    carries = lax.cond(lax.rem(kt_last, 2) == 1, odd_tail, even_tail, carries)

    wk = NSA_WINDOW + tq
    ks0 = pl.multiple_of(jnp.maximum(t0 - NSA_WINDOW, 0), tq)
    kpw = kpos_ref[pl.ds(ks0, wk), :]
    wpos = ks0 + lax.broadcasted_iota(jnp.int32, (1, wk), 1)
    dw = trow - wpos
    in_window = pltpu.bitcast(dw, jnp.uint32) < jnp.uint32(NSA_WINDOW)
    gt = g_ref[...]
    for g in groups:
        q, _, o_c = fronts[g]
        _, l_s, acc_s = carries[g]
        o_s = acc_s * (1.0 / l_s)
        kw = jnp.concatenate([kw_ref[g, pl.ds(ks0, wk), :], kpw], axis=1)
        vw = vw_ref[g, pl.ds(ks0, wk), :]
        sm_w = jnp.where(in_window, _dot_nt(q, kw), NEG)
        p_w = jnp.exp((sm_w - jnp.max(sm_w, axis=1, keepdims=True)).astype(BF16))
        o_w = _dot(p_w, vw) * (1.0 / jnp.sum(p_w.astype(F32), axis=1, keepdims=True))
        for hg in range(NSA_HG):
            sl = slice(hg * tq, (hg + 1) * tq)
            c0 = 128 * g + 3 * hg
            o = (gt[:, c0:c0 + 1] * o_c[sl] + gt[:, c0 + 1:c0 + 2] * o_s[sl] + gt[:, c0 + 2:c0 + 3] * o_w[sl])
            h = NSA_HG * g + hg
            o_ref[:, h * HEAD_DIM:(h + 1) * HEAD_DIM] = o.astype(o_ref.dtype)


def nsa_attn(hm, cmp, gates, consts, *, q_head0, kv_head0, batch, seq, tq, tk):
    ov, e, qa, kpos, cpos = consts
    nq = seq // tq
    nc = seq // NSA_CMP_STRIDE
    n_sel = seq // NSA_SEL_LEN
    n_top = min(NSA_TOPK, n_sel)
    assert tk % tq == 0 and seq % tk == 0 and seq >= NSA_WINDOW + tq and NSA_WINDOW % tq == 0
    assert q_head0 % NSA_HEADS == 0 and kv_head0 % NSA_GROUPS == 0
    hm4 = hm.reshape(hm.shape[0], batch, seq, HEAD_DIM)
    kv_spec = lambda off: pl.BlockSpec((NSA_GROUPS, None, seq, HEAD_DIM),
                                       lambda b, i: ((kv_head0 + off) // NSA_GROUPS, b, 0, 0))
    return pl.pallas_call(
        functools.partial(_nsa_attn_kernel, tq=tq, tk=tk, seq=seq, n_sel=n_sel, n_top=n_top),
        out_shape=jax.ShapeDtypeStruct((batch * seq, NSA_HEADS * HEAD_DIM), BF16),
        grid=(batch, nq),
        in_specs=[
            pl.BlockSpec((NSA_HEADS, None, tq, HEAD_DIM), lambda b, i: (q_head0 // NSA_HEADS, b, i, 0)),
            pl.BlockSpec((NSA_GROUPS, NSA_HG * tq, 128), lambda b, i: (0, 0, 0)),
            pl.BlockSpec((NSA_GROUPS, None, nc, HEAD_DIM), lambda b, i: (0, b, 0, 0)),
            pl.BlockSpec((NSA_GROUPS, None, nc, HEAD_DIM), lambda b, i: (1, b, 0, 0)),
            kv_spec(4), kv_spec(6), kv_spec(8), kv_spec(10),
            pl.BlockSpec((tq, NSA_GROUPS * 128), lambda b, i: (b * nq + i, 0)),
            pl.BlockSpec((n_sel, nc), lambda b, i: (0, 0)),
            pl.BlockSpec((seq // tk, n_sel, tk), lambda b, i: (0, 0, 0)),
            pl.BlockSpec((seq, 128), lambda b, i: (0, 0)),
            pl.BlockSpec((nc, 128), lambda b, i: (0, 0)),
        ],
        out_specs=pl.BlockSpec((tq, NSA_HEADS * HEAD_DIM), lambda b, i: (b * nq + i, 0)),
        scratch_shapes=[pltpu.VMEM((2, NSA_GROUPS, NSA_HG * tq, tk), F32)],
        compiler_params=_cp(("arbitrary", "arbitrary")),
    )(hm4, qa, cmp, cmp, hm4, hm4, hm4, hm4, gates, ov, e, kpos, cpos)


def _merge_kernel(x_ref, oa_ref, ob_ref, oc_ref, wm_ref, wb_ref, bm_ref, y_ref, xb_ref):
    @pl.when(pl.program_id(1) == 0)
    def _():
        xb_ref[...] = x_ref[...].astype(BF16)

    xb = xb_ref[...]
    acc = None
    for br, o_ref in enumerate((oa_ref, ob_ref, oc_ref)):
        gate = jax.nn.sigmoid(_dot(xb, wm_ref[br]) + bm_ref[br])
        term = gate * _dot(o_ref[...], wb_ref[br])
        acc = term if acc is None else acc + term
    y_ref[...] = acc.astype(y_ref.dtype)


def merge_branches(x, o_a, o_b, o_c, wm, wb, bm, *, tm, tn):
    t = x.shape[0]
    o_spec = pl.BlockSpec((tm, BRANCH_WIDTH), lambda i, j: (i, 0))
    return pl.pallas_call(
        _merge_kernel,
        out_shape=jax.ShapeDtypeStruct((t, D_MODEL), BF16),
        grid=(t // tm, D_MODEL // tn),
        in_specs=[
            pl.BlockSpec((tm, D_MODEL), lambda i, j: (i, 0)),
            o_spec, o_spec, o_spec,
            pl.BlockSpec((N_BRANCH, D_MODEL, tn), lambda i, j: (0, 0, j)),
            pl.BlockSpec((N_BRANCH, BRANCH_WIDTH, tn), lambda i, j: (0, 0, j)),
            pl.BlockSpec((N_BRANCH, 1, tn), lambda i, j: (0, 0, j)),
        ],
        out_specs=pl.BlockSpec((tm, tn), lambda i, j: (i, j)),
        scratch_shapes=[pltpu.VMEM((tm, D_MODEL), BF16)],
        compiler_params=_cp(("arbitrary", "arbitrary")),
    )(x, o_a, o_b, o_c, wm, wb, bm)


def _out_ln_kernel(y_ref, w_ref, x_ref, g_ref, b_ref, o_ref, *, alpha):
    half = y_ref.shape[0] // 2
    for r in range(2):
        rs = slice(r * half, (r + 1) * half)
        h = _dot(y_ref[rs, :], w_ref[...])
        o_ref[rs, :] = _layer_norm(alpha * x_ref[rs, :] + h, g_ref[...], b_ref[...])


def out_ln(y, w, x, g, b, *, alpha, tm):
    t = x.shape[0]
    return pl.pallas_call(
        functools.partial(_out_ln_kernel, alpha=alpha),
        out_shape=jax.ShapeDtypeStruct((t, D_MODEL), F32),
        grid=(t // tm,),
        in_specs=[
            pl.BlockSpec((tm, D_MODEL), lambda i: (i, 0)),
            pl.BlockSpec((D_MODEL, D_MODEL), lambda i: (0, 0)),
            pl.BlockSpec((tm, D_MODEL), lambda i: (i, 0)),
            pl.BlockSpec((1, D_MODEL), lambda i: (0, 0)),
            pl.BlockSpec((1, D_MODEL), lambda i: (0, 0)),
        ],
        out_specs=pl.BlockSpec((tm, D_MODEL), lambda i: (i, 0)),
        compiler_params=_cp(("arbitrary",)),
    )(y, w, x, g, b)


def _mem_attn_kernel(x_ref, wq_ref, k_ref, v_ref, wo_ref, g_ref, b_ref, wr_ref, br_ref, tri_ref,
                     o_ref, route_ref, xp_ref, count_ref, *, alpha):
    x = x_ref[...]
    q = _dot(x.astype(BF16), wq_ref[...]) * (HEAD_DIM ** -0.5)
    outs = []
    for h in range(MEM_HEADS):
        qh = q[:, h * HEAD_DIM:(h + 1) * HEAD_DIM].astype(BF16)
        s = _dot_nt(qh, k_ref[h])
        m = jnp.max(s, axis=1, keepdims=True)
        e = jnp.exp(s - m)
        p = e * (1.0 / jnp.sum(e, axis=1, keepdims=True))
        outs.append(_dot(p.astype(BF16), v_ref[h]).astype(BF16))
    o = jnp.concatenate(outs, axis=1)
    h_out = _dot(o, wo_ref[...])
    x_new = _layer_norm(alpha * x + h_out, g_ref[...], b_ref[...])
    o_ref[...] = x_new
    _route_rows(x_new, wr_ref, br_ref, tri_ref, route_ref, xp_ref, count_ref)


def mem_attn_ln(x, wq, kv, wo, g, b, w_router_hl, b_router, tri, *, alpha, seq, mem_len, tm):
    t = x.shape[0]
    per_b = seq // tm
    kv4 = kv.reshape(2 * MEM_HEADS, t // seq, mem_len, HEAD_DIM)
    width = MEM_HEADS * HEAD_DIM
    return pl.pallas_call(
        functools.partial(_mem_attn_kernel, alpha=alpha),
        out_shape=(jax.ShapeDtypeStruct((t, D_MODEL), F32),
                   jax.ShapeDtypeStruct((t, 128), F32),
                   jax.ShapeDtypeStruct((t, D_MODEL // 2), jnp.uint32),
                   jax.ShapeDtypeStruct((1, N_EXPERTS), F32)),
        grid=(t // tm,),
        in_specs=[
            pl.BlockSpec((tm, D_MODEL), lambda i: (i, 0)),
            pl.BlockSpec((D_MODEL, width), lambda i: (0, 0)),
            pl.BlockSpec((MEM_HEADS, None, mem_len, HEAD_DIM), lambda i: (0, i // per_b, 0, 0)),
            pl.BlockSpec((MEM_HEADS, None, mem_len, HEAD_DIM), lambda i: (1, i // per_b, 0, 0)),
            pl.BlockSpec((width, D_MODEL), lambda i: (0, 0)),
            pl.BlockSpec((1, D_MODEL), lambda i: (0, 0)),
            pl.BlockSpec((1, D_MODEL), lambda i: (0, 0)),
            pl.BlockSpec((2, D_MODEL, N_EXPERTS), lambda i: (0, 0, 0)),
            pl.BlockSpec((1, N_EXPERTS), lambda i: (0, 0)),
            pl.BlockSpec((tm, tm), lambda i: (0, 0)),
        ],
        out_specs=(pl.BlockSpec((tm, D_MODEL), lambda i: (i, 0)),
                   pl.BlockSpec((tm, 128), lambda i: (i, 0)),
                   pl.BlockSpec((tm, D_MODEL // 2), lambda i: (i, 0)),
                   pl.BlockSpec((1, N_EXPERTS), lambda i: (0, 0))),
        compiler_params=_cp(("arbitrary",)),
    )(x, wq, kv4, kv4, wo, g, b, w_router_hl, b_router, tri)


def _route_rows(x, w_ref, b_ref, tri_ref, route_ref, xp_ref, count_ref):
    @pl.when(pl.program_id(0) == 0)
    def _():
        count_ref[...] = jnp.zeros_like(count_ref)

    xp_ref[...] = _pack_bf16_pairs(x)
    xh = x.astype(BF16)
    xl = (x - xh.astype(F32)).astype(BF16)
    logits = _dot(xh, w_ref[0]) + _dot(xh, w_ref[1]) + _dot(xl, w_ref[0]) + b_ref[...]
    tm = logits.shape[0]
    lane = lax.broadcasted_iota(jnp.int32, (tm, N_EXPERTS), 1)
    work = logits
    hots, vals, idxs = [], [], []
    for _ in range(TOP_K):
        m = jnp.max(work, axis=1, keepdims=True)
        idx = jnp.min(jnp.where(work == m, lane, N_EXPERTS), axis=1, keepdims=True)
        hot = lane == idx
        hots.append(hot)
        vals.append(m)
        idxs.append(idx)
        work = jnp.where(hot, -jnp.inf, work)
    es = [jnp.exp(v - vals[0]) for v in vals]
    inv = 1.0 / (es[0] + es[1] + es[2] + es[3])
    mask = jnp.zeros((tm, N_EXPERTS), F32)
    for hot in hots:
        mask = mask + jnp.where(hot, 1.0, 0.0)
    before = count_ref[...] + _dot(tri_ref[...], mask.astype(BF16))
    count_ref[...] = count_ref[...] + jnp.sum(mask, axis=0, keepdims=True)
    col = lax.broadcasted_iota(jnp.int32, (tm, 128), 1)
    route = jnp.zeros((tm, 128), F32)
    for k in range(TOP_K):
        rank = jnp.sum(jnp.where(hots[k], before, 0.0), axis=1, keepdims=True)
        route = (route + jnp.where(col == k, idxs[k].astype(F32), 0.0)
                 + jnp.where(col == TOP_K + k, es[k] * inv, 0.0) + jnp.where(col == 2 * TOP_K + k, rank, 0.0))
    route_ref[...] = route


def sc_gather_rows(table, idx, *, chunk):
    n = idx.shape[0]
    d = table.shape[1]
    workers = SC_CORES_V7X * SC_SUBCORES_V7X
    per_w = n // workers
    assert n % (workers * chunk) == 0 and chunk % 8 == 0 and chunk <= 128
    mesh = plsc.VectorSubcoreMesh(core_axis_name="c", subcore_axis_name="s")

    @functools.partial(
        pl.kernel, mesh=mesh,
        out_type=jax.ShapeDtypeStruct((n, d), table.dtype),
        scratch_types=[pltpu.VMEM((chunk,), jnp.int32), pltpu.VMEM((chunk, d), table.dtype),
                       pltpu.SemaphoreType.DMA],
    )
    def gather(table_hbm, idx_hbm, out_hbm, idx_v, rows_v, sem):
        wid = lax.axis_index("s") * SC_CORES_V7X + lax.axis_index("c")
        base = wid * per_w

        @pl.loop(0, per_w // chunk)
        def _(j):
            off = pl.multiple_of(base + j * chunk, 8)
            pltpu.sync_copy(idx_hbm.at[pl.ds(off, chunk)], idx_v)
            pltpu.async_copy(table_hbm.at[idx_v], rows_v, sem).wait()
            pltpu.sync_copy(rows_v, out_hbm.at[pl.ds(off, chunk)])

    return gather(table, idx)


def sc_scatter_rows(rows, idx, n_out, *, copies, chunk):
    t, d = rows.shape
    workers = SC_CORES_V7X * SC_SUBCORES_V7X
    per_w = t // workers
    assert idx.shape == (copies * t,) and t % (workers * chunk) == 0 and chunk % 8 == 0 and chunk <= 128
    mesh = plsc.VectorSubcoreMesh(core_axis_name="c", subcore_axis_name="s")

    @functools.partial(
        pl.kernel, mesh=mesh,
        out_type=jax.ShapeDtypeStruct((n_out, d), rows.dtype),
        scratch_types=[pltpu.VMEM((chunk,), jnp.int32), pltpu.VMEM((chunk, d), rows.dtype)],
    )
    def scatter(rows_hbm, idx_hbm, out_hbm, idx_v, rows_v):
        wid = lax.axis_index("s") * SC_CORES_V7X + lax.axis_index("c")
        base = wid * per_w

        @pl.loop(0, per_w // chunk)
        def _(j):
            off = pl.multiple_of(base + j * chunk, 8)
            pltpu.sync_copy(rows_hbm.at[pl.ds(off, chunk)], rows_v)
            for k in range(copies):
                pltpu.sync_copy(idx_hbm.at[pl.ds(pl.multiple_of(k * t + off, 8), chunk)], idx_v)
                pltpu.sync_copy(rows_v, out_hbm.at[idx_v])

    return scatter(rows, idx)


def _experts_kernel(be_ref, x_ref, wg_ref, bg_ref, wu_ref, bu_ref, wd_ref, bd_ref, y_ref,
                    wgb_ref, wub_ref, wdb_ref):
    i = pl.program_id(0)
    prev = be_ref[jnp.maximum(i - 1, 0)]
    n_used = be_ref[pl.num_programs(0)]

    @pl.when((i < n_used) & ((i == 0) | (be_ref[i] != prev)))
    def _():
        wgb_ref[...] = wg_ref[...].astype(BF16)
        wub_ref[...] = wu_ref[...].astype(BF16)
        wdb_ref[...] = wd_ref[...].astype(BF16)

    @pl.when(i < n_used)
    def _():
        x_lo, x_hi = _unpack_bf16_pairs(x_ref[...])
        xb = jnp.concatenate([x_lo.astype(BF16), x_hi.astype(BF16)], axis=1)
        g = jnp.minimum(_dot(xb, wgb_ref[...]) + bg_ref[...], SWIGLU_LIMIT)
        u = jnp.clip(_dot(xb, wub_ref[...]) + bu_ref[...], -SWIGLU_LIMIT, SWIGLU_LIMIT)
        hdn = (u + 1.0) * (g * jax.nn.sigmoid(SWIGLU_ALPHA * g))
        y_ref[...] = _pack_bf16_pairs(_dot(hdn.astype(BF16), wdb_ref[...]) + bd_ref[...])


def experts(x_rows, blk_e, wg, bg, wu, bu, wd, bd, *, layer, bm):
    n_rows, dp = x_rows.shape
    d, f = wg.shape[2], wg.shape[3]
    w_spec = lambda shape: pl.BlockSpec((None, None) + shape, lambda i, be: (layer, be[i], 0, 0))
    grid_spec = pltpu.PrefetchScalarGridSpec(
        num_scalar_prefetch=1,
        grid=(n_rows // bm,),
        in_specs=[
            pl.BlockSpec((bm, dp), lambda i, be: (i, 0)),
            w_spec((d, f)), w_spec((1, f)), w_spec((d, f)), w_spec((1, f)), w_spec((f, d)), w_spec((1, d)),
        ],
        out_specs=pl.BlockSpec((bm, dp), lambda i, be: (i, 0)),
        scratch_shapes=[pltpu.VMEM((d, f), BF16), pltpu.VMEM((d, f), BF16), pltpu.VMEM((f, d), BF16)],
    )
    return pl.pallas_call(
        _experts_kernel,
        out_shape=jax.ShapeDtypeStruct((n_rows, dp), jnp.uint32),
        grid_spec=grid_spec,
        compiler_params=_cp(("arbitrary",)),
    )(blk_e, x_rows, wg, bg, wu, bu, wd, bd)


def _moe_ln_kernel(x_ref, y_ref, w_ref, g_ref, b_ref, o_ref, *, alpha):
    o_ref[...] = _moe_combine(x_ref, y_ref, w_ref, g_ref, b_ref, alpha)


def _moe_ln_mla_in_kernel(x_ref, y_ref, r_ref, lg_ref, lb_ref, w_ref, qg_ref, kg_ref, wq_ref, wkv_ref, cos_ref, sin_ref,
                          xo_ref, q_ref, k_ref, v_ref, g_ref, *, alpha):
    x_new = _moe_combine(x_ref, y_ref, r_ref, lg_ref, lb_ref, alpha)
    xo_ref[...] = x_new
    _mla_in_body(x_new, w_ref, qg_ref, kg_ref, wq_ref, wkv_ref, cos_ref, sin_ref, q_ref, k_ref, v_ref, g_ref)


def _moe_combine(x_ref, y_ref, w_ref, g_ref, b_ref, alpha):
    w = w_ref[...]
    y_lo = y_hi = None
    for k in range(TOP_K):
        lo, hi = _unpack_bf16_pairs(y_ref[k])
        wk = w[:, TOP_K + k:TOP_K + k + 1]
        y_lo = wk * lo if y_lo is None else y_lo + wk * lo
        y_hi = wk * hi if y_hi is None else y_hi + wk * hi
    y = jnp.concatenate([y_lo, y_hi], axis=1)
    return _layer_norm(alpha * x_ref[...] + y, g_ref[...], b_ref[...])


def moe_ln_mla_in(x, y4, route, g, b, w_mla, qg, kg, wq3, wkv, cos128, sin128, *, alpha, seq, tm):
    t = x.shape[0]
    npos = seq // tm
    rows = lambda w: pl.BlockSpec((tm, w), lambda i: (i, 0))
    full = lambda shape: pl.BlockSpec(shape, lambda i: (0,) * len(shape))
    heads = lambda w: pl.BlockSpec((MLA_HEADS, tm, w), lambda i: (0, i, 0))
    pos = pl.BlockSpec((tm, 128), lambda i: (i % npos, 0))
    return pl.pallas_call(
        functools.partial(_moe_ln_mla_in_kernel, alpha=alpha),
        out_shape=(
            jax.ShapeDtypeStruct((t, D_MODEL), F32),
            jax.ShapeDtypeStruct((MLA_HEADS, t, 256), BF16),
            jax.ShapeDtypeStruct((MLA_HEADS, t, 256), BF16),
            jax.ShapeDtypeStruct((MLA_HEADS, t, 128), BF16),
            jax.ShapeDtypeStruct((t, 256), F32),
        ),
        grid=(t // tm,),
        in_specs=[
            rows(D_MODEL), pl.BlockSpec((TOP_K, tm, D_MODEL // 2), lambda i: (0, i, 0)), rows(128),
            full((1, D_MODEL)), full((1, D_MODEL)),
            full((D_MODEL, MLA_IN_WIDTH)), full((1, MLA_Q_RANK)), full((1, MLA_KV_RANK)),
            full((MLA_Q_RANK, 3 * MLA_HEADS * 128)), full((MLA_KV_RANK, 2 * MLA_HEADS * 128)), pos, pos,
        ],
        out_specs=(rows(D_MODEL), heads(256), heads(256), heads(128), rows(256)),
        compiler_params=_cp(("arbitrary",)),
    )(x, y4, route, g, b, w_mla, qg, kg, wq3, wkv, cos128, sin128)


def moe_ln(x, y4, w4p, g, b, *, alpha, tm):
    t = x.shape[0]
    return pl.pallas_call(
        functools.partial(_moe_ln_kernel, alpha=alpha),
        out_shape=jax.ShapeDtypeStruct((t, D_MODEL), F32),
        grid=(t // tm,),
        in_specs=[
            pl.BlockSpec((tm, D_MODEL), lambda i: (i, 0)),
            pl.BlockSpec((TOP_K, tm, D_MODEL // 2), lambda i: (0, i, 0)),
            pl.BlockSpec((tm, 128), lambda i: (i, 0)),
            pl.BlockSpec((1, D_MODEL), lambda i: (0, 0)),
            pl.BlockSpec((1, D_MODEL), lambda i: (0, 0)),
        ],
        out_specs=pl.BlockSpec((tm, D_MODEL), lambda i: (i, 0)),
        compiler_params=_cp(("arbitrary",)),
    )(x, y4, w4p, g, b)


def _rope_tables(seq):
    inv = np.asarray(ROPE_THETA ** (-np.arange(0, MLA_ROPE, 2) / MLA_ROPE), np.float32)
    ang = jnp.arange(seq, dtype=F32)[:, None] * jnp.asarray(inv)[None, :]
    cos, sin = jnp.cos(ang), jnp.sin(ang)
    zeros = jnp.zeros((seq, 128 - MLA_ROPE), F32)
    return (jnp.concatenate([cos, cos, zeros], axis=1), jnp.concatenate([-sin, sin, zeros], axis=1))


def _nsa_constants(seq, tq, tk):
    nc = seq // NSA_CMP_STRIDE
    qa = np.zeros((NSA_GROUPS, NSA_HG * tq, 128), np.float32)
    for g in range(NSA_GROUPS):
        for hg in range(NSA_HG):
            slope = 2.0 ** (-8.0 * (g * NSA_HG + hg + 1) / NSA_HEADS)
            qa[g, hg * tq:(hg + 1) * tq, 0] = slope * NSA_SEL_LEN
            qa[g, hg * tq:(hg + 1) * tq, 1] = slope
            qa[g, hg * tq:(hg + 1) * tq, 2] = slope * NSA_CMP_STRIDE
            qa[g, hg * tq:(hg + 1) * tq, 3] = slope * (NSA_CMP_LEN - 1) / 2.0
    kpos = np.zeros((seq, 128), np.float32)
    kpos[:, 0] = np.arange(seq) // NSA_SEL_LEN
    kpos[:, 1] = np.arange(seq) % NSA_SEL_LEN
    cpos = np.zeros((nc, 128), np.float32)
    cpos[:, 2] = np.arange(nc)
    cpos[:, 3] = 1.0
    for arr in (qa, kpos, cpos):
        assert np.array_equal(arr.astype(BF16).astype(np.float32), arr)
    n_cmp = (seq - NSA_CMP_LEN) // NSA_CMP_STRIDE + 1
    n_sel = seq // NSA_SEL_LEN
    cs = np.arange(nc) * NSA_CMP_STRIDE
    ss = np.arange(n_sel) * NSA_SEL_LEN
    ov = np.clip(np.minimum(cs[:, None] + NSA_CMP_LEN, ss[None, :] + NSA_SEL_LEN)
                 - np.maximum(cs[:, None], ss[None, :]), 0, None) / NSA_CMP_LEN
    ov[n_cmp:] = 0.0
    e = (np.arange(seq)[None, :] // NSA_SEL_LEN == np.arange(n_sel)[:, None]).astype(np.float32)
    e = e.reshape(n_sel, seq // tk, tk).transpose(1, 0, 2)
    return tuple(jnp.asarray(a, BF16) for a in (ov.T, e, qa, kpos, cpos))


def _pad_cols(w, width):
    return jnp.pad(w, ((0, 0), (0, width - w.shape[1])))


def _swap_halves(w):
    half = w.shape[1] // 2
    return jnp.concatenate([w[:, half:], w[:, :half]], axis=1)


def _layer_weights(w_in, w_q_up, w_kv_up):
    kr = w_in[:, OFF_KR:OFF_NSA_Q]
    gate = w_in[:, OFF_NSA_GATE:OFF_SB]
    per_g = NSA_HG * 3
    w_mla = jnp.concatenate([
        w_in[:, OFF_CQ:OFF_KR],
        _pad_cols(kr, 128), _pad_cols(_swap_halves(kr), 128),
        _pad_cols(gate[:, :per_g], 128), _pad_cols(gate[:, per_g:], 128)], axis=1).astype(BF16)
    wq = w_q_up.reshape(MLA_Q_RANK, MLA_HEADS, MLA_NOPE + MLA_ROPE)
    rope = wq[:, :, MLA_NOPE:]
    rope_sw = jnp.concatenate([rope[:, :, MLA_ROPE // 2:], rope[:, :, :MLA_ROPE // 2]], axis=2)
    pad = ((0, 0), (0, 0), (0, 128 - MLA_ROPE))
    wq3 = jnp.concatenate([
        wq[:, :, :MLA_NOPE].reshape(MLA_Q_RANK, -1),
        jnp.pad(rope, pad).reshape(MLA_Q_RANK, -1),
        jnp.pad(rope_sw, pad).reshape(MLA_Q_RANK, -1)], axis=1).astype(BF16)
    wkv = w_kv_up.reshape(MLA_KV_RANK, MLA_HEADS, 2, 128)
    wkv = jnp.concatenate([wkv[:, :, 0].reshape(MLA_KV_RANK, -1),
                           wkv[:, :, 1].reshape(MLA_KV_RANK, -1)], axis=1).astype(BF16)
    w_heads = jnp.concatenate([w_in[:, OFF_NSA_Q:OFF_NSA_KV], w_in[:, OFF_SB:OFF_MERGE],
                               w_in[:, OFF_NSA_KV:OFF_NSA_GATE]], axis=1).astype(BF16)
    wm = w_in[:, OFF_MERGE:].reshape(D_MODEL, N_BRANCH, D_MODEL).transpose(1, 0, 2).astype(BF16)
    return w_mla, wq3, wkv, w_heads, wm


def _forward(x, mem, w_in, mla_q_norm, mla_w_q_up, mla_kv_norm, mla_w_kv_up,
             nsa_pe_k, nsa_pe_v, nsa_w1_k, nsa_w1_v, nsa_w2_k, nsa_w2_v,
             w_branch, b_merge, w_out, ln_mix_g, ln_mix_b,
             mem_w_q, mem_w_k, mem_w_v, mem_w_o, ln_mem_g, ln_mem_b,
             moe_w_router, moe_b_router, moe_w_gate, moe_b_gate, moe_w_up, moe_b_up,
             moe_w_down, moe_b_down, ln_moe_g, ln_moe_b):
    batch, seq, _ = x.shape
    mem_len = mem.shape[1]
    depth = w_in.shape[0]
    t = batch * seq
    alpha = float((2 * depth) ** 0.25)
    bm = MOE_BLOCK_ROWS
    n_rows = t * TOP_K + N_EXPERTS * bm
    n_blocks = n_rows // bm

    tm_in = min(512, seq)
    tq_mla = min(512, seq)
    tq_sb = 256
    tq_nsa, tk_nsa = 256, 512
    tm_ln = 256
    sc_chunk = 64

    cos128, sin128 = _rope_tables(seq)
    nsa_consts = _nsa_constants(seq, tq_nsa, tk_nsa)
    u_sb = jnp.asarray(np.arange(tq_sb)[:, None] > np.arange(tq_sb)[None, :], BF16)
    tri_router = jnp.asarray(np.arange(tm_in)[:, None] > np.arange(tm_in)[None, :], BF16)
    n_qheads = NSA_HEADS
    n_kvheads = 3 * 2 * NSA_GROUPS
    head_scale = np.ones((1, (n_qheads + n_kvheads + 3 * SB_HEADS) * HEAD_DIM), np.float32)
    head_scale[:, :n_qheads * HEAD_DIM] = HEAD_DIM ** -0.5
    sb0 = n_qheads
    kv0 = sb0 + 3 * SB_HEADS
    tn_heads = (n_qheads + n_kvheads + 3 * SB_HEADS) // 4 * HEAD_DIM
    head_scale[:, sb0 * HEAD_DIM:(sb0 + SB_HEADS) * HEAD_DIM] = HEAD_DIM ** -0.5
    head_scale = jnp.asarray(head_scale)
    ones_kv = jnp.ones((1, 2 * MEM_HEADS * HEAD_DIM), F32)

    b_gate4 = moe_b_gate.reshape(depth, N_EXPERTS, 1, D_EXPERT)
    b_up4 = moe_b_up.reshape(depth, N_EXPERTS, 1, D_EXPERT)
    b_down4 = moe_b_down.reshape(depth, N_EXPERTS, 1, D_MODEL)

    xf = x.reshape(t, D_MODEL)
    memf = mem.reshape(batch * mem_len, D_MODEL)
    row = lambda v: v.reshape(1, -1)

    layer_w = [_layer_weights(w_in[l], mla_w_q_up[l], mla_w_kv_up[l]) for l in range(depth)]
    mla_parts = None
    for l in range(depth):
        w_mla, wq3, wkv, w_heads, wm = layer_w[l]

        if mla_parts is None:
            mla_parts = mla_in(xf, w_mla, row(mla_q_norm[l]), row(mla_kv_norm[l]), wq3, wkv,
                               cos128, sin128, seq=seq, tm=tm_ln)
        q_a, k_a, v_a, gates = mla_parts
        hm = proj_heads(xf, w_heads, head_scale, tm=tm_in, tn=tn_heads)
        o_a = mla_attn(q_a, k_a, v_a, batch=batch, seq=seq, tq=tq_mla, heads=2)
        w1 = jnp.stack([nsa_w1_k[l], nsa_w1_v[l]]).astype(BF16)
        pe = jnp.stack([nsa_pe_k[l], nsa_pe_v[l]]).reshape(2, 1, -1)
        pe = jnp.broadcast_to(pe, (2, 8, pe.shape[-1])).astype(BF16)
        w2 = jnp.stack([nsa_w2_k[l], nsa_w2_v[l]]).astype(BF16)
        cmp = nsa_compress(hm[kv0:kv0 + 4], w1, pe, w2, batch=batch, seq=seq)
        o_b = nsa_attn(hm, cmp, gates, nsa_consts, q_head0=0, kv_head0=kv0,
                       batch=batch, seq=seq, tq=tq_nsa, tk=tk_nsa)
        o_c = sb_attn(hm, u_sb, head0=sb0, batch=batch, seq=seq, tq=tq_sb, heads=4)
        y = merge_branches(xf, o_a, o_b, o_c, wm, w_branch[l].astype(BF16),
                           b_merge[l].reshape(N_BRANCH, 1, D_MODEL), tm=tm_in, tn=512)
        xf = out_ln(y, w_out[l].astype(BF16), xf, row(ln_mix_g[l]), row(ln_mix_b[l]), alpha=alpha, tm=tm_in)

        w_kv_mem = jnp.concatenate([mem_w_k[l], mem_w_v[l]], axis=1).astype(BF16)
        kv_mem = proj_heads(memf, w_kv_mem, ones_kv, tm=min(512, batch * mem_len), tn=512)
        wr = moe_w_router[l]
        wr_hi = wr.astype(BF16)
        wr_lo = (wr - wr_hi.astype(F32)).astype(BF16)
        xf, route, x_packed, counts = mem_attn_ln(
            xf, mem_w_q[l].astype(BF16), kv_mem, mem_w_o[l].astype(BF16), row(ln_mem_g[l]), row(ln_mem_b[l]),
            jnp.stack([wr_hi, wr_lo]), row(moe_b_router[l]), tri_router,
            alpha=alpha, seq=seq, mem_len=mem_len, tm=tm_in)

        padded = (counts[0].astype(jnp.int32) + bm - 1) // bm * bm
        pad_end = jnp.cumsum(padded)
        pad_start = pad_end - padded
        idx4 = route[:, 0:TOP_K].astype(jnp.int32)
        rank4 = route[:, 2 * TOP_K:3 * TOP_K].astype(jnp.int32)
        start4 = jnp.sum(jnp.where(idx4[:, :, None] == jnp.arange(N_EXPERTS, dtype=jnp.int32), pad_start, 0), axis=2)
        pos4 = start4 + rank4
        blk_row0 = jnp.arange(n_blocks, dtype=jnp.int32)[:, None] * bm
        blk_e = jnp.minimum(jnp.sum((pad_end[None, :] <= blk_row0).astype(jnp.int32), axis=1), N_EXPERTS - 1)
        blk_e = jnp.concatenate([blk_e, pad_end[-1:] // bm]).astype(jnp.int32)
        pos_kmajor = pos4.T.reshape(-1)
        x_rows = sc_scatter_rows(x_packed, pos_kmajor, n_rows, copies=TOP_K, chunk=sc_chunk)
        y_rows = experts(x_rows, blk_e, moe_w_gate, b_gate4, moe_w_up, b_up4, moe_w_down, b_down4,
                         layer=l, bm=bm)
        y4 = sc_gather_rows(y_rows, pos_kmajor, chunk=sc_chunk).reshape(TOP_K, t, D_MODEL // 2)
        if l + 1 < depth:
            nw_mla, nwq3, nwkv, _, _ = layer_w[l + 1]
            xf, *mla_parts = moe_ln_mla_in(
                xf, y4, route, row(ln_moe_g[l]), row(ln_moe_b[l]), nw_mla, row(mla_q_norm[l + 1]),
                row(mla_kv_norm[l + 1]), nwq3, nwkv, cos128, sin128, alpha=alpha, seq=seq, tm=tm_ln)
        else:
            xf = moe_ln(xf, y4, route, row(ln_moe_g[l]), row(ln_moe_b[l]), alpha=alpha, tm=tm_ln)

    return xf.reshape(batch, seq, D_MODEL)


def kernel(x, mem, w_in, mla_q_norm, mla_w_q_up, mla_kv_norm, mla_w_kv_up, nsa_pe_k, nsa_pe_v, nsa_w1_k, nsa_w1_v, nsa_w2_k, nsa_w2_v, w_branch, b_merge, w_out, ln_mix_g, ln_mix_b, mem_w_q, mem_w_k, mem_w_v, mem_w_o, ln_mem_g, ln_mem_b, moe_w_router, moe_b_router, moe_w_gate, moe_b_gate, moe_w_up, moe_b_up, moe_w_down, moe_b_down, ln_moe_g, ln_moe_b):
    return _forward(x, mem, w_in, mla_q_norm, mla_w_q_up, mla_kv_norm, mla_w_kv_up,
                    nsa_pe_k, nsa_pe_v, nsa_w1_k, nsa_w1_v, nsa_w2_k, nsa_w2_v,
                    w_branch, b_merge, w_out, ln_mix_g, ln_mix_b,
                    mem_w_q, mem_w_k, mem_w_v, mem_w_o, ln_mem_g, ln_mem_b,
                    moe_w_router, moe_b_router, moe_w_gate, moe_b_gate, moe_w_up, moe_b_up,
                    moe_w_down, moe_b_down, ln_moe_g, ln_moe_b)
```
